```python
import math
import jax, jax.numpy as jnp
from jax import lax
import numpy as np

D_MODEL = 1024
BATCH = 8
SEQ = 4096
DEPTH = 2

N_META = 16
S5_GROUP = 16
S5_GROUPS = D_MODEL // S5_GROUP
S5_STATE = 64
N_HEADS = 16
HEAD_DIM = D_MODEL // N_HEADS
D_FF = ((8 * D_MODEL + 3 * 256 - 1) // (3 * 256)) * 256
Q_BLOCK = 128
N_A_LAYERS = DEPTH // 2
N_B_LAYERS = DEPTH - N_A_LAYERS
RMS_EPS = 1e-6
DT_MIN = 1e-3
DT_MAX = 1e-1

kernel_name = "s5_yoco_stickbreaking_hybrid"


def rmsnorm(x, gain):
    x32 = x.astype(jnp.float32)
    y = x32 * lax.rsqrt(jnp.mean(x32 * x32, axis=-1, keepdims=True) + RMS_EPS)
    return (y * gain.astype(jnp.float32)).astype(x.dtype)


def s5_mixer(u, a_re, a_im, log_dt, b_re, b_im, c_re, c_im, d_skip, w_glu):
    bsz, L, _ = u.shape
    f32 = jnp.float32
    u32 = u.astype(f32).reshape(bsz, L, S5_GROUPS, S5_GROUP)
    a_re = a_re.astype(f32)
    a_im = a_im.astype(f32)
    dt = jnp.exp(log_dt.astype(f32))[:, None]
    mag = jnp.exp(dt * a_re)
    ang = dt * a_im
    abar_re = mag * jnp.cos(ang)
    abar_im = mag * jnp.sin(ang)
    den = a_re * a_re + a_im * a_im
    coef_re = ((abar_re - 1.0) * a_re + abar_im * a_im) / den
    coef_im = (abar_im * a_re - (abar_re - 1.0) * a_im) / den
    b_re = b_re.astype(f32)
    b_im = b_im.astype(f32)
    bbar_re = coef_re[..., None] * b_re - coef_im[..., None] * b_im
    bbar_im = coef_re[..., None] * b_im + coef_im[..., None] * b_re
    bu_re = jnp.einsum('blgc,gpc->blgp', u32, bbar_re)
    bu_im = jnp.einsum('blgc,gpc->blgp', u32, bbar_im)
    a_seq_re = jnp.broadcast_to(abar_re, (1, L, S5_GROUPS, S5_STATE))
    a_seq_im = jnp.broadcast_to(abar_im, (1, L, S5_GROUPS, S5_STATE))

    def combine(e1, e2):
        a1r, a1i, b1r, b1i = e1
        a2r, a2i, b2r, b2i = e2
        return (a2r * a1r - a2i * a1i,
                a2r * a1i + a2i * a1r,
                a2r * b1r - a2i * b1i + b2r,
                a2r * b1i + a2i * b1r + b2i)

    _, _, x_re, x_im = lax.associative_scan(combine, (a_seq_re, a_seq_im, bu_re, bu_im), axis=1)
    y = (jnp.einsum('blgp,gcp->blgc', x_re, c_re.astype(f32))
         - jnp.einsum('blgp,gcp->blgc', x_im, c_im.astype(f32)))
    y = (y + d_skip.astype(f32).reshape(S5_GROUPS, S5_GROUP) * u32).reshape(bsz, L, D_MODEL)
    z = jax.nn.gelu(y)
    vg = jnp.einsum('bld,de->ble', z, w_glu.astype(f32))
    val, gate = jnp.split(vg, 2, axis=-1)
    return (val * jax.nn.sigmoid(gate)).astype(u.dtype)


def stick_breaking_attention(q, k, v):
    f32 = jnp.float32
    q = q.astype(f32)
    k = k.astype(f32)
    v = v.astype(f32)
    L = q.shape[1]
    scale = 1.0 / math.sqrt(HEAD_DIM)
    n_real_blocks = (L - N_META) // Q_BLOCK
    bounds = [(0, N_META)] + [(N_META + i * Q_BLOCK, N_META + (i + 1) * Q_BLOCK) for i in range(n_real_blocks)]
    outs = []
    for q0, q1 in bounds:
        qb = q[:, q0:q1]
        kb = k[:, :q1]
        vb = v[:, :q1]
        z = jnp.einsum('bqhd,bkhd->bhqk', qb, kb) * scale
        t_idx = jnp.arange(q0, q1)[:, None]
        s_idx = jnp.arange(q1)[None, :]
        strict = s_idx < t_idx
        log_beta = jax.nn.log_sigmoid(z)
        log_1m = jnp.where(strict, jax.nn.log_sigmoid(-z), 0.0)
        log_remain = lax.cumsum(log_1m, axis=3, reverse=True) - log_1m
        w = jnp.where(strict, jnp.exp(log_beta + log_remain), 0.0)
        outs.append(jnp.einsum('bhqk,bkhd->bqhd', w, vb))
    return jnp.concatenate(outs, axis=1)


def swiglu_ffn(h, w_in, w_out):
    gu = jnp.einsum('bld,df->blf', h, w_in)
    g, u = jnp.split(gu, 2, axis=-1)
    return jnp.einsum('blf,fd->bld', jax.nn.silu(g) * u, w_out)


def _fwd_setup_inputs(seed: int = 0) -> dict:
    key = jax.random.key(seed)
    ks = jax.random.split(key, 24)
    f32 = jnp.float32
    G, P, C = S5_GROUPS, S5_STATE, S5_GROUP
    HD = N_HEADS * HEAD_DIM
    x = jax.random.normal(ks[0], (BATCH, SEQ, D_MODEL), f32)
    meta_tokens = jax.random.normal(ks[1], (N_META, D_MODEL), f32)
    norm_mix = 1.0 + 0.02 * jax.random.normal(ks[2], (DEPTH, D_MODEL), f32)
    norm_ffn = 1.0 + 0.02 * jax.random.normal(ks[3], (DEPTH, D_MODEL), f32)
    n_idx = jnp.arange(P, dtype=f32)
    s5_a_re = -0.5 + 0.01 * jax.random.normal(ks[4], (N_A_LAYERS, G, P), f32)
    s5_a_im = math.pi * n_idx + 0.01 * jax.random.normal(ks[5], (N_A_LAYERS, G, P), f32)
    s5_log_dt = jax.random.uniform(ks[6], (N_A_LAYERS, G), f32, math.log(DT_MIN), math.log(DT_MAX))
    s5_b_re = jax.random.normal(ks[7], (N_A_LAYERS, G, P, C), f32) * (2 * C) ** -0.5
    s5_b_im = jax.random.normal(ks[8], (N_A_LAYERS, G, P, C), f32) * (2 * C) ** -0.5
    s5_c_re = jax.random.normal(ks[9], (N_A_LAYERS, G, C, P), f32) * P ** -0.5
    s5_c_im = jax.random.normal(ks[10], (N_A_LAYERS, G, C, P), f32) * P ** -0.5
    s5_d = jax.random.normal(ks[11], (N_A_LAYERS, D_MODEL), f32)
    s5_w_glu = jax.random.normal(ks[12], (N_A_LAYERS, D_MODEL, 2 * D_MODEL), f32) * D_MODEL ** -0.5
    norm_kv = 1.0 + 0.02 * jax.random.normal(ks[13], (D_MODEL,), f32)
    w_kv = jax.random.normal(ks[14], (D_MODEL, 2 * HD), f32) * D_MODEL ** -0.5
    w_q = jax.random.normal(ks[15], (N_B_LAYERS, D_MODEL, HD), f32) * D_MODEL ** -0.5
    w_o = jax.random.normal(ks[16], (N_B_LAYERS, HD, D_MODEL), f32) * HD ** -0.5
    w_ffn_in = jax.random.normal(ks[17], (DEPTH, D_MODEL, 2 * D_FF), f32) * D_MODEL ** -0.5
    w_ffn_out = jax.random.normal(ks[18], (DEPTH, D_FF, D_MODEL), f32) * D_FF ** -0.5
    norm_final = 1.0 + 0.02 * jax.random.normal(ks[19], (D_MODEL,), f32)
    return {"x": x, "meta_tokens": meta_tokens, "norm_mix": norm_mix, "norm_ffn": norm_ffn,
            "s5_a_re": s5_a_re, "s5_a_im": s5_a_im, "s5_log_dt": s5_log_dt,
            "s5_b_re": s5_b_re, "s5_b_im": s5_b_im, "s5_c_re": s5_c_re, "s5_c_im": s5_c_im,
            "s5_d": s5_d, "s5_w_glu": s5_w_glu, "norm_kv": norm_kv, "w_kv": w_kv,
            "w_q": w_q, "w_o": w_o, "w_ffn_in": w_ffn_in, "w_ffn_out": w_ffn_out,
            "norm_final": norm_final}


def _fwd_reference(x, meta_tokens, norm_mix, norm_ffn, s5_a_re, s5_a_im, s5_log_dt,
              s5_b_re, s5_b_im, s5_c_re, s5_c_im, s5_d, s5_w_glu, norm_kv, w_kv,
              w_q, w_o, w_ffn_in, w_ffn_out, norm_final):
    bsz = x.shape[0]
    meta = jnp.broadcast_to(meta_tokens.astype(x.dtype)[None], (bsz, N_META, D_MODEL))
    h = jnp.concatenate([meta, x], axis=1)
    L = h.shape[1]
    k_shared = None
    v_shared = None
    for i in range(DEPTH):
        if i < N_A_LAYERS:
            a = i
            h = h + s5_mixer(rmsnorm(h, norm_mix[i]), s5_a_re[a], s5_a_im[a], s5_log_dt[a],
                             s5_b_re[a], s5_b_im[a], s5_c_re[a], s5_c_im[a], s5_d[a], s5_w_glu[a])
        else:
            j = i - N_A_LAYERS
            q = jnp.einsum('bld,de->ble', rmsnorm(h, norm_mix[i]), w_q[j]).reshape(bsz, L, N_HEADS, HEAD_DIM)
            o = stick_breaking_attention(q, k_shared, v_shared).astype(h.dtype)
            h = h + jnp.einsum('ble,ed->bld', o.reshape(bsz, L, N_HEADS * HEAD_DIM), w_o[j])
        h = h + swiglu_ffn(rmsnorm(h, norm_ffn[i]), w_ffn_in[i], w_ffn_out[i])
        if i == N_A_LAYERS - 1:
            kv = jnp.einsum('bld,de->ble', rmsnorm(h, norm_kv), w_kv).reshape(bsz, L, 2, N_HEADS, HEAD_DIM)
            k_shared = kv[:, :, 0]
            v_shared = kv[:, :, 1]
    out = rmsnorm(h, norm_final)
    return out[:, N_META:]


import jax as _jax
import jax.numpy as _jnp

TWIN_FORMAT = 'train_step'
FWD_PARAMS = ['x', 'meta_tokens', 'norm_mix', 'norm_ffn', 's5_a_re', 's5_a_im', 's5_log_dt', 's5_b_re', 's5_b_im', 's5_c_re', 's5_c_im', 's5_d', 's5_w_glu', 'norm_kv', 'w_kv', 'w_q', 'w_o', 'w_ffn_in', 'w_ffn_out', 'norm_final']
TWIN_WEIGHTS = ['meta_tokens', 'norm_mix', 'norm_ffn', 's5_a_re', 's5_a_im', 's5_log_dt', 's5_b_re', 's5_b_im', 's5_c_re', 's5_c_im', 's5_d', 's5_w_glu', 'norm_kv', 'w_kv', 'w_q', 'w_o', 'w_ffn_in', 'w_ffn_out', 'norm_final']
TWIN_DIFF_INPUT = 'x'
TWIN_INPUTS = ['x', 'meta_tokens', 'norm_mix', 'norm_ffn', 's5_a_re', 's5_a_im', 's5_log_dt', 's5_b_re', 's5_b_im', 's5_c_re', 's5_c_im', 's5_d', 's5_w_glu', 'norm_kv', 'w_kv', 'w_q', 'w_o', 'w_ffn_in', 'w_ffn_out', 'norm_final', 'loss_target', 'm_meta_tokens', 'm_norm_mix', 'm_norm_ffn', 'm_s5_a_re', 'm_s5_a_im', 'm_s5_log_dt', 'm_s5_b_re', 'm_s5_b_im', 'm_s5_c_re', 'm_s5_c_im', 'm_s5_d', 'm_s5_w_glu', 'm_norm_kv', 'm_w_kv', 'm_w_q', 'm_w_o', 'm_w_ffn_in', 'm_w_ffn_out', 'm_norm_final', 'v_meta_tokens', 'v_norm_mix', 'v_norm_ffn', 'v_s5_a_re', 'v_s5_a_im', 'v_s5_log_dt', 'v_s5_b_re', 'v_s5_b_im', 'v_s5_c_re', 'v_s5_c_im', 'v_s5_d', 'v_s5_w_glu', 'v_norm_kv', 'v_w_kv', 'v_w_q', 'v_w_o', 'v_w_ffn_in', 'v_w_ffn_out', 'v_norm_final']
TWIN_OUTPUTS = ['loss', 'grad_x', 'grad_meta_tokens', 'grad_norm_mix', 'grad_norm_ffn', 'grad_s5_a_re', 'grad_s5_a_im', 'grad_s5_log_dt', 'grad_s5_b_re', 'grad_s5_b_im', 'grad_s5_c_re', 'grad_s5_c_im', 'grad_s5_d', 'grad_s5_w_glu', 'grad_norm_kv', 'grad_w_kv', 'grad_w_q', 'grad_w_o', 'grad_w_ffn_in', 'grad_w_ffn_out', 'grad_norm_final', 'delta_meta_tokens', 'delta_norm_mix', 'delta_norm_ffn', 'delta_s5_a_re', 'delta_s5_a_im', 'delta_s5_log_dt', 'delta_s5_b_re', 'delta_s5_b_im', 'delta_s5_c_re', 'delta_s5_c_im', 'delta_s5_d', 'delta_s5_w_glu', 'delta_norm_kv', 'delta_w_kv', 'delta_w_q', 'delta_w_o', 'delta_w_ffn_in', 'delta_w_ffn_out', 'delta_norm_final', 'new_m_meta_tokens', 'new_m_norm_mix', 'new_m_norm_ffn', 'new_m_s5_a_re', 'new_m_s5_a_im', 'new_m_s5_log_dt', 'new_m_s5_b_re', 'new_m_s5_b_im', 'new_m_s5_c_re', 'new_m_s5_c_im', 'new_m_s5_d', 'new_m_s5_w_glu', 'new_m_norm_kv', 'new_m_w_kv', 'new_m_w_q', 'new_m_w_o', 'new_m_w_ffn_in', 'new_m_w_ffn_out', 'new_m_norm_final', 'new_v_meta_tokens', 'new_v_norm_mix', 'new_v_norm_ffn', 'new_v_s5_a_re', 'new_v_s5_a_im', 'new_v_s5_log_dt', 'new_v_s5_b_re', 'new_v_s5_b_im', 'new_v_s5_c_re', 'new_v_s5_c_im', 'new_v_s5_d', 'new_v_s5_w_glu', 'new_v_norm_kv', 'new_v_w_kv', 'new_v_w_q', 'new_v_w_o', 'new_v_w_ffn_in', 'new_v_w_ffn_out', 'new_v_norm_final']
TWIN_LEAF_KINDS = {'loss': 'loss', 'grad_x': 'grad_x', 'grad_meta_tokens': 'grad_w', 'grad_norm_mix': 'grad_w', 'grad_norm_ffn': 'grad_w', 'grad_s5_a_re': 'grad_w', 'grad_s5_a_im': 'grad_w', 'grad_s5_log_dt': 'grad_w', 'grad_s5_b_re': 'grad_w', 'grad_s5_b_im': 'grad_w', 'grad_s5_c_re': 'grad_w', 'grad_s5_c_im': 'grad_w', 'grad_s5_d': 'grad_w', 'grad_s5_w_glu': 'grad_w', 'grad_norm_kv': 'grad_w', 'grad_w_kv': 'grad_w', 'grad_w_q': 'grad_w', 'grad_w_o': 'grad_w', 'grad_w_ffn_in': 'grad_w', 'grad_w_ffn_out': 'grad_w', 'grad_norm_final': 'grad_w', 'delta_meta_tokens': 'delta_w', 'delta_norm_mix': 'delta_w', 'delta_norm_ffn': 'delta_w', 'delta_s5_a_re': 'delta_w', 'delta_s5_a_im': 'delta_w', 'delta_s5_log_dt': 'delta_w', 'delta_s5_b_re': 'delta_w', 'delta_s5_b_im': 'delta_w', 'delta_s5_c_re': 'delta_w', 'delta_s5_c_im': 'delta_w', 'delta_s5_d': 'delta_w', 'delta_s5_w_glu': 'delta_w', 'delta_norm_kv': 'delta_w', 'delta_w_kv': 'delta_w', 'delta_w_q': 'delta_w', 'delta_w_o': 'delta_w', 'delta_w_ffn_in': 'delta_w', 'delta_w_ffn_out': 'delta_w', 'delta_norm_final': 'delta_w', 'new_m_meta_tokens': 'new_m', 'new_m_norm_mix': 'new_m', 'new_m_norm_ffn': 'new_m', 'new_m_s5_a_re': 'new_m', 'new_m_s5_a_im': 'new_m', 'new_m_s5_log_dt': 'new_m', 'new_m_s5_b_re': 'new_m', 'new_m_s5_b_im': 'new_m', 'new_m_s5_c_re': 'new_m', 'new_m_s5_c_im': 'new_m', 'new_m_s5_d': 'new_m', 'new_m_s5_w_glu': 'new_m', 'new_m_norm_kv': 'new_m', 'new_m_w_kv': 'new_m', 'new_m_w_q': 'new_m', 'new_m_w_o': 'new_m', 'new_m_w_ffn_in': 'new_m', 'new_m_w_ffn_out': 'new_m', 'new_m_norm_final': 'new_m', 'new_v_meta_tokens': 'new_v', 'new_v_norm_mix': 'new_v', 'new_v_norm_ffn': 'new_v', 'new_v_s5_a_re': 'new_v', 'new_v_s5_a_im': 'new_v', 'new_v_s5_log_dt': 'new_v', 'new_v_s5_b_re': 'new_v', 'new_v_s5_b_im': 'new_v', 'new_v_s5_c_re': 'new_v', 'new_v_s5_c_im': 'new_v', 'new_v_s5_d': 'new_v', 'new_v_s5_w_glu': 'new_v', 'new_v_norm_kv': 'new_v', 'new_v_w_kv': 'new_v', 'new_v_w_q': 'new_v', 'new_v_w_o': 'new_v', 'new_v_w_ffn_in': 'new_v', 'new_v_w_ffn_out': 'new_v', 'new_v_norm_final': 'new_v'}


def _forward(args):
    return _fwd_reference(*[args[k] for k in FWD_PARAMS])


def _output_shape():
    out = _jax.eval_shape(lambda: _forward(_fwd_setup_inputs(0)))
    return out.shape, out.dtype

N_MICROBATCH = 1
ADAM_LR = 0.001
ADAM_B1 = 0.9
ADAM_B2 = 0.999
ADAM_EPS = 1e-08
ADAM_WD = 0.01
ADAM_STEP = 10
PER_EXAMPLE_BATCH_AXIS = {'x': 0, 'loss_target': 0}
SHARED_INPUTS = []
_WEIGHT_DTYPES = {'meta_tokens': _jnp.float32, 'norm_mix': _jnp.float32, 'norm_ffn': _jnp.float32, 's5_a_re': _jnp.float32, 's5_a_im': _jnp.float32, 's5_log_dt': _jnp.float32, 's5_b_re': _jnp.float32, 's5_b_im': _jnp.float32, 's5_c_re': _jnp.float32, 's5_c_im': _jnp.float32, 's5_d': _jnp.float32, 's5_w_glu': _jnp.float32, 'norm_kv': _jnp.float32, 'w_kv': _jnp.float32, 'w_q': _jnp.float32, 'w_o': _jnp.float32, 'w_ffn_in': _jnp.float32, 'w_ffn_out': _jnp.float32, 'norm_final': _jnp.float32}
MOMENT_SCALE = {'meta_tokens': 2.111437e-03, 'norm_mix': 7.566879e-02, 'norm_ffn': 1.239801e-01, 's5_a_re': 5.267875e-03, 's5_a_im': 5.488730e-03, 's5_log_dt': 4.016008e+00, 's5_b_re': 3.632105e-03, 's5_b_im': 3.595035e-03, 's5_c_re': 5.296095e-03, 's5_c_im': 5.091471e-03, 's5_d': 7.933504e-02, 's5_w_glu': 5.520817e-02, 'norm_kv': 1.015353e-01, 'w_kv': 7.059651e-02, 'w_q': 4.137331e-02, 'w_o': 9.156745e-02, 'w_ffn_in': 5.227377e-02, 'w_ffn_out': 8.516257e-02, 'norm_final': 3.195592e+01}


def _to_microbatches(a, axis):
    t = _jnp.moveaxis(a, axis, 0)
    t = t.reshape((N_MICROBATCH, t.shape[0] // N_MICROBATCH) + t.shape[1:])
    return _jnp.moveaxis(t, 1, axis + 1)


def setup_inputs(seed: int = 0) -> dict:
    inp = _fwd_setup_inputs(seed)
    key = _jax.random.fold_in(_jax.random.key(seed), 7919)
    shape, _ = _output_shape()
    out = dict(inp)
    out["loss_target"] = _jax.random.normal(_jax.random.fold_in(key, 0), shape, _jnp.float32)
    for i, name in enumerate(TWIN_WEIGHTS):
        w = inp[name].astype(_jnp.float32)
        if MOMENT_SCALE is None:
            s = _jnp.sqrt(_jnp.mean(_jnp.square(w)) + 1e-30)
        else:
            s = MOMENT_SCALE[name]
        km, kv = _jax.random.split(_jax.random.fold_in(key, i + 1))
        out[name] = w
        out["m_" + name] = s * _jax.random.normal(km, w.shape, _jnp.float32)
        out["v_" + name] = (s * s) * _jax.random.uniform(kv, w.shape, _jnp.float32, 0.5, 1.5)
    if N_MICROBATCH > 1:
        for name, axis in PER_EXAMPLE_BATCH_AXIS.items():
            out[name] = _to_microbatches(out[name], axis)
    return {'x': out['x'], 'meta_tokens': out['meta_tokens'], 'norm_mix': out['norm_mix'], 'norm_ffn': out['norm_ffn'], 's5_a_re': out['s5_a_re'], 's5_a_im': out['s5_a_im'], 's5_log_dt': out['s5_log_dt'], 's5_b_re': out['s5_b_re'], 's5_b_im': out['s5_b_im'], 's5_c_re': out['s5_c_re'], 's5_c_im': out['s5_c_im'], 's5_d': out['s5_d'], 's5_w_glu': out['s5_w_glu'], 'norm_kv': out['norm_kv'], 'w_kv': out['w_kv'], 'w_q': out['w_q'], 'w_o': out['w_o'], 'w_ffn_in': out['w_ffn_in'], 'w_ffn_out': out['w_ffn_out'], 'norm_final': out['norm_final'], 'loss_target': out['loss_target'], 'm_meta_tokens': out['m_meta_tokens'], 'm_norm_mix': out['m_norm_mix'], 'm_norm_ffn': out['m_norm_ffn'], 'm_s5_a_re': out['m_s5_a_re'], 'm_s5_a_im': out['m_s5_a_im'], 'm_s5_log_dt': out['m_s5_log_dt'], 'm_s5_b_re': out['m_s5_b_re'], 'm_s5_b_im': out['m_s5_b_im'], 'm_s5_c_re': out['m_s5_c_re'], 'm_s5_c_im': out['m_s5_c_im'], 'm_s5_d': out['m_s5_d'], 'm_s5_w_glu': out['m_s5_w_glu'], 'm_norm_kv': out['m_norm_kv'], 'm_w_kv': out['m_w_kv'], 'm_w_q': out['m_w_q'], 'm_w_o': out['m_w_o'], 'm_w_ffn_in': out['m_w_ffn_in'], 'm_w_ffn_out': out['m_w_ffn_out'], 'm_norm_final': out['m_norm_final'], 'v_meta_tokens': out['v_meta_tokens'], 'v_norm_mix': out['v_norm_mix'], 'v_norm_ffn': out['v_norm_ffn'], 'v_s5_a_re': out['v_s5_a_re'], 'v_s5_a_im': out['v_s5_a_im'], 'v_s5_log_dt': out['v_s5_log_dt'], 'v_s5_b_re': out['v_s5_b_re'], 'v_s5_b_im': out['v_s5_b_im'], 'v_s5_c_re': out['v_s5_c_re'], 'v_s5_c_im': out['v_s5_c_im'], 'v_s5_d': out['v_s5_d'], 'v_s5_w_glu': out['v_s5_w_glu'], 'v_norm_kv': out['v_norm_kv'], 'v_w_kv': out['v_w_kv'], 'v_w_q': out['v_w_q'], 'v_w_o': out['v_w_o'], 'v_w_ffn_in': out['v_w_ffn_in'], 'v_w_ffn_out': out['v_w_ffn_out'], 'v_norm_final': out['v_norm_final']}


def _loss(weights, diff, rest, loss_target):
    with _jax.named_scope("forward"):
        args = {**rest, TWIN_DIFF_INPUT: diff, **{k: w.astype(_WEIGHT_DTYPES[k]) for k, w in weights.items()}}
        y = _forward(args)
    with _jax.named_scope("loss_head"):
        err = _jnp.square(y.astype(_jnp.float32) - loss_target)
        return 0.5 * _jnp.sum(_jnp.mean(err, axis=-1)) if err.ndim else 0.5 * err


def _adamw(w, g, m, v):
    m = ADAM_B1 * m + (1.0 - ADAM_B1) * g
    v = ADAM_B2 * v + (1.0 - ADAM_B2) * _jnp.square(g)
    m_hat = m / (1.0 - ADAM_B1 ** ADAM_STEP)
    v_hat = v / (1.0 - ADAM_B2 ** ADAM_STEP)
    delta = -ADAM_LR * (m_hat / (_jnp.sqrt(v_hat) + ADAM_EPS) + ADAM_WD * w)
    return delta, m, v


def reference(x, meta_tokens, norm_mix, norm_ffn, s5_a_re, s5_a_im, s5_log_dt, s5_b_re, s5_b_im, s5_c_re, s5_c_im, s5_d, s5_w_glu, norm_kv, w_kv, w_q, w_o, w_ffn_in, w_ffn_out, norm_final, loss_target, m_meta_tokens, m_norm_mix, m_norm_ffn, m_s5_a_re, m_s5_a_im, m_s5_log_dt, m_s5_b_re, m_s5_b_im, m_s5_c_re, m_s5_c_im, m_s5_d, m_s5_w_glu, m_norm_kv, m_w_kv, m_w_q, m_w_o, m_w_ffn_in, m_w_ffn_out, m_norm_final, v_meta_tokens, v_norm_mix, v_norm_ffn, v_s5_a_re, v_s5_a_im, v_s5_log_dt, v_s5_b_re, v_s5_b_im, v_s5_c_re, v_s5_c_im, v_s5_d, v_s5_w_glu, v_norm_kv, v_w_kv, v_w_q, v_w_o, v_w_ffn_in, v_w_ffn_out, v_norm_final):
    given = dict(x=x, meta_tokens=meta_tokens, norm_mix=norm_mix, norm_ffn=norm_ffn, s5_a_re=s5_a_re, s5_a_im=s5_a_im, s5_log_dt=s5_log_dt, s5_b_re=s5_b_re, s5_b_im=s5_b_im, s5_c_re=s5_c_re, s5_c_im=s5_c_im, s5_d=s5_d, s5_w_glu=s5_w_glu, norm_kv=norm_kv, w_kv=w_kv, w_q=w_q, w_o=w_o, w_ffn_in=w_ffn_in, w_ffn_out=w_ffn_out, norm_final=norm_final, loss_target=loss_target, m_meta_tokens=m_meta_tokens, m_norm_mix=m_norm_mix, m_norm_ffn=m_norm_ffn, m_s5_a_re=m_s5_a_re, m_s5_a_im=m_s5_a_im, m_s5_log_dt=m_s5_log_dt, m_s5_b_re=m_s5_b_re, m_s5_b_im=m_s5_b_im, m_s5_c_re=m_s5_c_re, m_s5_c_im=m_s5_c_im, m_s5_d=m_s5_d, m_s5_w_glu=m_s5_w_glu, m_norm_kv=m_norm_kv, m_w_kv=m_w_kv, m_w_q=m_w_q, m_w_o=m_w_o, m_w_ffn_in=m_w_ffn_in, m_w_ffn_out=m_w_ffn_out, m_norm_final=m_norm_final, v_meta_tokens=v_meta_tokens, v_norm_mix=v_norm_mix, v_norm_ffn=v_norm_ffn, v_s5_a_re=v_s5_a_re, v_s5_a_im=v_s5_a_im, v_s5_log_dt=v_s5_log_dt, v_s5_b_re=v_s5_b_re, v_s5_b_im=v_s5_b_im, v_s5_c_re=v_s5_c_re, v_s5_c_im=v_s5_c_im, v_s5_d=v_s5_d, v_s5_w_glu=v_s5_w_glu, v_norm_kv=v_norm_kv, v_w_kv=v_w_kv, v_w_q=v_w_q, v_w_o=v_w_o, v_w_ffn_in=v_w_ffn_in, v_w_ffn_out=v_w_ffn_out, v_norm_final=v_norm_final)
    weights = {n: given[n] for n in TWIN_WEIGHTS}
    shared = {n: given[n] for n in SHARED_INPUTS}
    per_example = {n: given[n] for n in ['x']}
    grad_fn = _jax.value_and_grad(_loss, argnums=(0, 1))

    def one_microbatch(ex, loss_target):
        ex = dict(ex)
        diff = ex.pop(TWIN_DIFF_INPUT)
        return grad_fn(weights, diff, {**shared, **ex}, loss_target)

    if N_MICROBATCH == 1:
        loss, (grad_w, grad_x) = one_microbatch(per_example, given["loss_target"])
    else:
        def body(carry, xs):
            loss_sum, grad_sum = carry
            l_k, (gw_k, gx_k) = one_microbatch(xs[0], xs[1])
            with _jax.named_scope("update"):
                return (loss_sum + l_k, _jax.tree.map(_jnp.add, grad_sum, gw_k)), gx_k

        init = (_jnp.zeros((), _jnp.float32), _jax.tree.map(_jnp.zeros_like, weights))
        (loss, grad_w), grad_x = _jax.lax.scan(body, init, (per_example, given["loss_target"]))
    with _jax.named_scope("update"):
        delta_w, new_m, new_v = {}, {}, {}
        for n in TWIN_WEIGHTS:
            delta_w[n], new_m[n], new_v[n] = _adamw(weights[n], grad_w[n], given["m_" + n], given["v_" + n])
    return (loss, grad_x, *[grad_w[n] for n in TWIN_WEIGHTS], *[delta_w[n] for n in TWIN_WEIGHTS],
            *[new_m[n] for n in TWIN_WEIGHTS], *[new_v[n] for n in TWIN_WEIGHTS])
```

```python
import functools
import math

import jax
import jax.numpy as jnp
from jax import lax
from jax.experimental import pallas as pl
from jax.experimental.pallas import tpu as pltpu

F32 = jnp.float32
BF16 = jnp.bfloat16

N_META = 16
D_MODEL = 1024
S5_GROUPS = 64
S5_GROUP = 16
S5_STATE = 64
N_HEADS = 16
HEAD_DIM = 64
D_FF = 2816
RMS_EPS = 1e-6
ADAM_LR, ADAM_B1, ADAM_B2, ADAM_EPS, ADAM_WD, ADAM_STEP = 0.001, 0.9, 0.999, 1e-08, 0.01, 10

LANES = 128
ATTN_BLOCK = 128
GROUPS_PER_BLOCK = 8
N_DEV = 8
MESH = pl.DeviceIdType.MESH
VMEM_LIMIT = 56 * 1024 * 1024


def _pallas(body, **kw):
    return pl.pallas_call(body, **kw)


def _cparams(*sem):
    return pltpu.CompilerParams(dimension_semantics=sem, vmem_limit_bytes=VMEM_LIMIT)


def _pick(n, prefs):
    for p in prefs:
        if n % p == 0:
            return p
    return n


def _dot(a, b, ca=1, cb=0):
    return lax.dot_general(a, b, (((ca,), (cb,)), ((), ())), preferred_element_type=F32)


def _matmul(a, b, *, ta=False, tb=False, name):
    if ta:
        K, M = a.shape
    else:
        M, K = a.shape
    if tb:
        N, K2 = b.shape
    else:
        K2, N = b.shape
    assert K == K2, (a.shape, b.shape)
    tm = _pick(M, (512, 384, 256, 128))
    tn = _pick(N, (512, 256, 128))
    tk = _pick(K, (1024, 1408, 384, 512, 256, 128))
    nk = K // tk

    def body(a_ref, b_ref, o_ref, acc_ref):
        k = pl.program_id(2)
        part = _dot(a_ref[...].astype(BF16), b_ref[...].astype(BF16), 0 if ta else 1, 1 if tb else 0)

        @pl.when(k == 0)
        def _():
            acc_ref[...] = part

        @pl.when(k > 0)
        def _():
            acc_ref[...] += part

        @pl.when(k == nk - 1)
        def _():
            o_ref[...] = acc_ref[...]

    a_spec = pl.BlockSpec((tk, tm), lambda i, j, k: (k, i)) if ta else pl.BlockSpec((tm, tk), lambda i, j, k: (i, k))
    b_spec = pl.BlockSpec((tn, tk), lambda i, j, k: (j, k)) if tb else pl.BlockSpec((tk, tn), lambda i, j, k: (k, j))
    return _pallas(
        body, name=name, grid=(M // tm, N // tn, nk), in_specs=[a_spec, b_spec],
        out_specs=pl.BlockSpec((tm, tn), lambda i, j, k: (i, j)),
        out_shape=jax.ShapeDtypeStruct((M, N), F32),
        scratch_shapes=[pltpu.VMEM((tm, tn), F32)],
        compiler_params=_cparams("parallel", "parallel", "arbitrary"),
    )(a, b)


def _make_mm(name):
    @jax.custom_vjp
    def mm(x, w, proxy):
        return _matmul(x, w, name=name + "_fwd")

    def fwd(x, w, proxy):
        return _matmul(x, w, name=name + "_fwd"), (x, w)

    def bwd(res, dy):
        x, w = res
        dx = _matmul(dy, w, tb=True, name=name + "_dx")
        dw = _matmul(x, dy, ta=True, name=name + "_dw")
        return dx, jnp.zeros_like(w), dw

    mm.defvjp(fwd, bwd)
    return mm


def _rowwise(fn, row_ins, full_ins, row_outs, acc_outs, *, tm, name):
    L = row_ins[0].shape[0]
    n_row, n_full, n_ro = len(row_ins), len(full_ins), len(row_outs)

    def body(*refs):
        ins = [r[...] for r in refs[:n_row + n_full]]
        outs = refs[n_row + n_full:]
        res = fn(*ins)
        for r, val in zip(outs[:n_ro], res[:n_ro]):
            r[...] = val.astype(r.dtype)
        if acc_outs:
            first = pl.program_id(0) == 0
            for r, val in zip(outs[n_ro:], res[n_ro:]):
                @pl.when(first)
                def _(r=r, val=val):
                    r[...] = val

                @pl.when(jnp.logical_not(first))
                def _(r=r, val=val):
                    r[...] += val

    in_specs = [pl.BlockSpec((tm, a.shape[1]), lambda i: (i, 0)) for a in row_ins]
    in_specs += [pl.BlockSpec(a.shape, lambda i, nd=a.ndim: (0,) * nd) for a in full_ins]
    out_specs = [pl.BlockSpec((tm, s[1]), lambda i: (i, 0)) for s, _ in row_outs]
    out_specs += [pl.BlockSpec(s, lambda i, nd=len(s): (0,) * nd) for s, _ in acc_outs]
    out_shape = [jax.ShapeDtypeStruct(s, d) for s, d in list(row_outs) + list(acc_outs)]
    return _pallas(
        body, name=name, grid=(L // tm,), in_specs=in_specs, out_specs=out_specs, out_shape=out_shape,
        compiler_params=_cparams("arbitrary" if acc_outs else "parallel"),
    )(*row_ins, *full_ins)


def _colsum(x):
    return jnp.sum(x, axis=0, keepdims=True)


def _make_rmsnorm(name):
    @jax.custom_vjp
    def rmsnorm(x, g):
        return fwd(x, g)[0]

    def fwd(x, g):
        L, d = x.shape
        g2 = g.reshape(1, d)

        def f(xb, gb):
            r = lax.rsqrt(jnp.mean(xb * xb, axis=-1, keepdims=True) + RMS_EPS)
            return (xb * r * gb,)

        (y,) = _rowwise(f, [x], [g2], [((L, d), F32)], [], tm=_pick(L, (384, 128)), name=name + "_fwd")
        return y, (x, g)

    def bwd(res, dy):
        x, g = res
        L, d = x.shape
        g2 = g.reshape(1, d)

        def f(xb, dyb, gb):
            r = lax.rsqrt(jnp.mean(xb * xb, axis=-1, keepdims=True) + RMS_EPS)
            xh = xb * r
            dxh = dyb * gb
            dx = r * (dxh - xh * jnp.mean(dxh * xh, axis=-1, keepdims=True))
            return dx, _colsum(dyb * xh)

        dx, dg = _rowwise(f, [x, dy], [g2], [((L, d), F32)], [((1, d), F32)], tm=_pick(L, (384, 128)), name=name + "_bwd")
        return dx, dg.reshape(g.shape)

    rmsnorm.defvjp(fwd, bwd)
    return rmsnorm


def _sigmoid(x):
    return 1.0 / (1.0 + jnp.exp(-x))


def _make_swiglu(name):
    @jax.custom_vjp
    def swiglu(gu):
        return fwd(gu)[0]

    def fwd(gu):
        L, f2 = gu.shape
        half = f2 // 2

        def f(b):
            g, u = b[:, :half], b[:, half:]
            return (g * _sigmoid(g) * u,)

        (a,) = _rowwise(f, [gu], [], [((L, half), F32)], [], tm=128, name=name + "_fwd")
        return a, (gu,)

    def bwd(res, da):
        (gu,) = res
        L, f2 = gu.shape
        half = f2 // 2

        def f(b, dab):
            g, u = b[:, :half], b[:, half:]
            s = _sigmoid(g)
            dg = dab * u * (s + g * s * (1.0 - s))
            du = dab * g * s
            return (jnp.concatenate([dg, du], axis=1),)

        (dgu,) = _rowwise(f, [gu, da], [], [((L, f2), F32)], [], tm=128, name=name + "_bwd")
        return (dgu,)

    swiglu.defvjp(fwd, bwd)
    return swiglu


def _make_glu(name):
    @jax.custom_vjp
    def glu(vg):
        return fwd(vg)[0]

    def fwd(vg):
        L, f2 = vg.shape
        half = f2 // 2

        def f(b):
            return (b[:, :half] * _sigmoid(b[:, half:]),)

        (a,) = _rowwise(f, [vg], [], [((L, half), F32)], [], tm=128, name=name + "_fwd")
        return a, (vg,)

    def bwd(res, da):
        (vg,) = res
        L, f2 = vg.shape
        half = f2 // 2

        def f(b, dab):
            val, gate = b[:, :half], b[:, half:]
            s = _sigmoid(gate)
            return (jnp.concatenate([dab * s, dab * val * s * (1.0 - s)], axis=1),)

        (dvg,) = _rowwise(f, [vg, da], [], [((L, f2), F32)], [], tm=128, name=name + "_bwd")
        return (dvg,)

    glu.defvjp(fwd, bwd)
    return glu


GELU_K = math.sqrt(2.0 / math.pi)
GELU_C = 0.044715


def _make_s5_post(name):
    @jax.custom_vjp
    def post(y, u, d):
        return fwd(y, u, d)[0]

    def fwd(y, u, d):
        L, dm = y.shape

        def f(yb, ub, db):
            s = yb + db * ub
            return (0.5 * s * (1.0 + jnp.tanh(GELU_K * (s + GELU_C * s * s * s))),)

        (z,) = _rowwise(f, [y, u], [d.reshape(1, dm)], [((L, dm), F32)], [], tm=_pick(L, (384, 128)), name=name + "_fwd")
        return z, (y, u, d)

    def bwd(res, dz):
        y, u, d = res
        L, dm = y.shape

        def f(yb, ub, dzb, db):
            s = yb + db * ub
            t = jnp.tanh(GELU_K * (s + GELU_C * s * s * s))
            ds = dzb * (0.5 * (1.0 + t) + 0.5 * s * (1.0 - t * t) * GELU_K * (1.0 + 3.0 * GELU_C * s * s))
            return ds, ds * db, _colsum(ds * ub)

        dy, du, dd = _rowwise(f, [y, u, dz], [d.reshape(1, dm)], [((L, dm), F32), ((L, dm), F32)], [((1, dm), F32)],
                              tm=_pick(L, (384, 128)), name=name + "_bwd")
        return dy, du, dd.reshape(d.shape)

    post.defvjp(fwd, bwd)
    return post


def _bdmm(a_list, w_rows, *, name):
    L = a_list[0].shape[0]
    nb, ka, kb = w_rows[0][0].shape
    tm = _pick(L, (384, 128))
    n_a, n_o = len(a_list), len(w_rows)
    ws = [w for row in w_rows for w in row]

    def body(*refs):
        a_vals = [r[...].astype(BF16) for r in refs[:n_a]]
        w_refs = refs[n_a:n_a + n_a * n_o]
        outs = refs[n_a + n_a * n_o:]
        for o in range(n_o):
            acc = None
            for i in range(n_a):
                p = _dot(a_vals[i], w_refs[o * n_a + i][...].astype(BF16))
                acc = p if acc is None else acc + p
            outs[o][...] = acc

    return _pallas(
        body, name=name, grid=(L // tm, nb),
        in_specs=[pl.BlockSpec((tm, ka), lambda i, j: (i, j)) for _ in a_list]
        + [pl.BlockSpec((None, ka, kb), lambda i, j: (j, 0, 0)) for _ in ws],
        out_specs=[pl.BlockSpec((tm, kb), lambda i, j: (i, j)) for _ in range(n_o)],
        out_shape=[jax.ShapeDtypeStruct((L, nb * kb), F32) for _ in range(n_o)],
        compiler_params=_cparams("parallel", "parallel"),
    )(*a_list, *ws)


def _bdmm_tn(a, g, nb, *, name):
    L = a.shape[0]
    ka, kb = a.shape[1] // nb, g.shape[1] // nb
    tm = _pick(L, (384, 128))

    def body(a_ref, g_ref, o_ref):
        i = pl.program_id(1)
        part = _dot(a_ref[...].astype(BF16), g_ref[...].astype(BF16), 0, 0)

        @pl.when(i == 0)
        def _():
            o_ref[...] = part

        @pl.when(i > 0)
        def _():
            o_ref[...] += part

    return _pallas(
        body, name=name, grid=(nb, L // tm),
        in_specs=[pl.BlockSpec((tm, ka), lambda j, i: (i, j)), pl.BlockSpec((tm, kb), lambda j, i: (i, j))],
        out_specs=pl.BlockSpec((None, ka, kb), lambda j, i: (j, 0, 0)),
        out_shape=jax.ShapeDtypeStruct((nb, ka, kb), F32),
        compiler_params=_cparams("parallel", "arbitrary"),
    )(a, g)


SCAN_ROWS = 128


def _scan_fwd(br, bi, ar, ai, *, name):
    L, S, _ = br.shape
    tb = SCAN_ROWS

    def body(br_ref, bi_ref, ar_ref, ai_ref, xr_ref, xi_ref, carry_ref):
        @pl.when(pl.program_id(0) == 0)
        def _():
            carry_ref[...] = jnp.zeros_like(carry_ref)

        a_r, a_i = ar_ref[...], ai_ref[...]

        def step(t, c):
            xr, xi = c
            nr = a_r * xr - a_i * xi + br_ref[t]
            ni = a_r * xi + a_i * xr + bi_ref[t]
            xr_ref[t] = nr
            xi_ref[t] = ni
            return nr, ni

        xr, xi = lax.fori_loop(0, tb, step, (carry_ref[0], carry_ref[1]), unroll=8)
        carry_ref[0] = xr
        carry_ref[1] = xi

    blk = pl.BlockSpec((tb, S, LANES), lambda i: (i, 0, 0))
    par = pl.BlockSpec((S, LANES), lambda i: (0, 0))
    return _pallas(
        body, name=name, grid=(L // tb,), in_specs=[blk, blk, par, par], out_specs=[blk, blk],
        out_shape=[jax.ShapeDtypeStruct(br.shape, F32)] * 2,
        scratch_shapes=[pltpu.VMEM((2, S, LANES), F32)],
        compiler_params=_cparams("arbitrary"),
    )(br, bi, ar, ai)


def _scan_bwd(gr_in, gi_in, ar, ai, xr, xi, *, name):
    L, S, _ = gr_in.shape
    tb = SCAN_ROWS
    nblk = L // tb

    def body(gr_ref, gi_ref, ar_ref, ai_ref, xr_ref, xi_ref, or_ref, oi_ref, dar_ref, dai_ref, carry_ref):
        @pl.when(pl.program_id(0) == 0)
        def _():
            carry_ref[...] = jnp.zeros_like(carry_ref)
            dar_ref[...] = jnp.zeros_like(dar_ref)
            dai_ref[...] = jnp.zeros_like(dai_ref)

        a_r, a_i = ar_ref[...], ai_ref[...]

        def step(s, c):
            gr, gi, dr, di = c
            t = tb - 1 - s
            x_r, x_i = xr_ref[t], xi_ref[t]
            dr = dr + gr * x_r + gi * x_i
            di = di + gi * x_r - gr * x_i
            nr = a_r * gr + a_i * gi + gr_ref[t]
            ni = a_r * gi - a_i * gr + gi_ref[t]
            or_ref[t] = nr
            oi_ref[t] = ni
            return nr, ni, dr, di

        gr, gi, dr, di = lax.fori_loop(0, tb, step, (carry_ref[0], carry_ref[1], dar_ref[...], dai_ref[...]), unroll=8)
        carry_ref[0] = gr
        carry_ref[1] = gi
        dar_ref[...] = dr
        dai_ref[...] = di

    blk = pl.BlockSpec((tb, S, LANES), lambda i: (nblk - 1 - i, 0, 0))
    par = pl.BlockSpec((S, LANES), lambda i: (0, 0))
    return _pallas(
        body, name=name, grid=(nblk,), in_specs=[blk, blk, par, par, blk, blk], out_specs=[blk, blk, par, par],
        out_shape=[jax.ShapeDtypeStruct(gr_in.shape, F32)] * 2 + [jax.ShapeDtypeStruct((S, LANES), F32)] * 2,
        scratch_shapes=[pltpu.VMEM((2, S, LANES), F32)],
        compiler_params=_cparams("arbitrary"),
    )(gr_in, gi_in, ar, ai, xr, xi)


def _make_ssm(name):
    nb = GROUPS_PER_BLOCK

    @jax.custom_vjp
    def ssm(u, wr, wi, cr, cin, lam_r, lam_i):
        return fwd(u, wr, wi, cr, cin, lam_r, lam_i)[0]

    def fwd(u, wr, wi, cr, cin, lam_r, lam_i):
        L = u.shape[0]
        bur, bui = _bdmm([u], [[wr], [wi]], name=name + "_bu")
        S = bur.shape[1] // LANES
        xr, xi = _scan_fwd(bur.reshape(L, S, LANES), bui.reshape(L, S, LANES), lam_r, lam_i, name=name + "_scan")
        xr, xi = xr.reshape(L, S * LANES), xi.reshape(L, S * LANES)
        (y,) = _bdmm([xr, xi], [[cr, cin]], name=name + "_cx")
        return y, (u, wr, wi, cr, cin, lam_r, lam_i, xr, xi)

    def bwd(res, dy):
        u, wr, wi, cr, cin, lam_r, lam_i, xr, xi = res
        L = u.shape[0]
        S = xr.shape[1] // LANES
        tr = lambda w: jnp.swapaxes(w, 1, 2)
        gin_r, gin_i = _bdmm([dy], [[tr(cr)], [tr(cin)]], name=name + "_gin")
        gr, gi, dlr, dli = _scan_bwd(gin_r.reshape(L, S, LANES), gin_i.reshape(L, S, LANES), lam_r, lam_i,
                                     xr.reshape(L, S, LANES), xi.reshape(L, S, LANES), name=name + "_rscan")
        gr, gi = gr.reshape(L, S * LANES), gi.reshape(L, S * LANES)
        (du,) = _bdmm([gr, gi], [[tr(wr), tr(wi)]], name=name + "_du")
        dwr = _bdmm_tn(u, gr, nb, name=name + "_dwr")
        dwi = _bdmm_tn(u, gi, nb, name=name + "_dwi")
        dcr = _bdmm_tn(xr, dy, nb, name=name + "_dcr")
        dcin = _bdmm_tn(xi, dy, nb, name=name + "_dci")
        return du, dwr, dwi, dcr, dcin, dlr, dli

    ssm.defvjp(fwd, bwd)
    return ssm


def _s5_discretize(a_re, a_im, log_dt, b_re, b_im, c_re, c_im):
    G, P, C, nb = S5_GROUPS, S5_STATE, S5_GROUP, GROUPS_PER_BLOCK
    dt = jnp.exp(log_dt)[:, None]
    mag = jnp.exp(dt * a_re)
    ang = dt * a_im
    abar_re = mag * jnp.cos(ang)
    abar_im = mag * jnp.sin(ang)
    den = a_re * a_re + a_im * a_im
    coef_re = ((abar_re - 1.0) * a_re + abar_im * a_im) / den
    coef_im = (abar_im * a_re - (abar_re - 1.0) * a_im) / den
    bbar_re = coef_re[..., None] * b_re - coef_im[..., None] * b_im
    bbar_im = coef_re[..., None] * b_im + coef_im[..., None] * b_re
    eye = jnp.eye(nb, dtype=F32)

    def blocks_in(bb):
        return jnp.einsum("jgpc,gh->jgchp", bb.reshape(G // nb, nb, P, C), eye).reshape(G // nb, nb * C, nb * P)

    def blocks_out(cc):
        return jnp.einsum("jgcp,gh->jgphc", cc.reshape(G // nb, nb, C, P), eye).reshape(G // nb, nb * P, nb * C)

    lam_r = abar_re.reshape(G * P // LANES, LANES)
    lam_i = abar_im.reshape(G * P // LANES, LANES)
    return blocks_in(bbar_re), blocks_in(bbar_im), blocks_out(c_re), blocks_out(-c_im), lam_r, lam_i


def _split_hi_lo(x):
    hi = x.astype(BF16)
    return hi, (x - hi.astype(F32)).astype(BF16)


def _attn_scores(qh, kj, strict):
    z = _dot(qh, kj, 1, 1)
    lb = jnp.minimum(z, 0.0) - jnp.log(1.0 + jnp.exp(-jnp.abs(z)))
    return lb, jnp.where(strict, lb - z, 0.0)


def _attn_fwd(q, k, v, *, name):
    L = q.shape[0]
    tb = ATTN_BLOCK
    scale = 1.0 / math.sqrt(HEAD_DIM)

    def body(q_ref, k_ref, v_ref, o_ref):
        i = pl.program_id(1)
        row = lax.broadcasted_iota(jnp.int32, (tb, tb), 0)
        col = lax.broadcasted_iota(jnp.int32, (tb, tb), 1)
        after = (row > col).astype(BF16)
        for hh in range(2):
            sl = slice(hh * HEAD_DIM, (hh + 1) * HEAD_DIM)
            qh = (q_ref[:, sl] * scale).astype(BF16)

            def step(jj, carry, sl=sl, qh=qh):
                c, acc = carry
                j = i - jj
                r0 = pl.multiple_of(j * tb, tb)
                kj = k_ref[pl.ds(r0, tb), sl].astype(BF16)
                vj = v_ref[pl.ds(r0, tb), sl].astype(BF16)
                strict = (col + j * tb) < (row + i * tb)
                lb, l1m = _attn_scores(qh, kj, strict)
                hi, lo = _split_hi_lo(l1m)
                remain = _dot(hi, after) + _dot(lo, after) + c
                w = jnp.where(strict, jnp.exp(lb + remain), 0.0)
                acc = acc + _dot(w.astype(BF16), vj)
                c = c + jnp.sum(l1m, axis=1, keepdims=True)
                return c, acc

            _, acc = lax.fori_loop(0, i + 1, step, (jnp.zeros((tb, 1), F32), jnp.zeros((tb, HEAD_DIM), F32)))
            o_ref[:, sl] = acc

    blk = pl.BlockSpec((tb, LANES), lambda h, i: (i, h))
    whole = pl.BlockSpec((L, LANES), lambda h, i: (0, h))
    return _pallas(
        body, name=name, grid=(N_HEADS // 2, L // tb), in_specs=[blk, whole, whole], out_specs=blk,
        out_shape=jax.ShapeDtypeStruct(q.shape, F32),
        compiler_params=_cparams("parallel", "arbitrary"),
    )(q, k, v)


def _attn_bwd(q, k, v, do, *, name):
    L = q.shape[0]
    tb = ATTN_BLOCK
    scale = 1.0 / math.sqrt(HEAD_DIM)

    def body(q_ref, k_ref, v_ref, do_ref, dq_ref, dk_ref, dv_ref, e_scr, sig_scr):
        i = pl.program_id(1)

        @pl.when(i == 0)
        def _():
            dk_ref[...] = jnp.zeros_like(dk_ref)
            dv_ref[...] = jnp.zeros_like(dv_ref)

        row = lax.broadcasted_iota(jnp.int32, (tb, tb), 0)
        col = lax.broadcasted_iota(jnp.int32, (tb, tb), 1)
        after = (row > col).astype(BF16)
        before = (row < col).astype(BF16)
        for hh in range(2):
            sl = slice(hh * HEAD_DIM, (hh + 1) * HEAD_DIM)
            q32 = q_ref[:, sl]
            qh = (q32 * scale).astype(BF16)
            qb = q32.astype(BF16)
            dob = do_ref[:, sl].astype(BF16)

            def sweep_left(jj, c, sl=sl, qh=qh, dob=dob):
                j = i - jj
                r0 = pl.multiple_of(j * tb, tb)
                kj = k_ref[pl.ds(r0, tb), sl].astype(BF16)
                vj = v_ref[pl.ds(r0, tb), sl].astype(BF16)
                strict = (col + j * tb) < (row + i * tb)
                lb, l1m = _attn_scores(qh, kj, strict)
                hi, lo = _split_hi_lo(l1m)
                remain = _dot(hi, after) + _dot(lo, after) + c
                w = jnp.where(strict, jnp.exp(lb + remain), 0.0)
                e_scr[j] = _dot(dob, vj, 1, 1) * w
                sig_scr[j] = jnp.exp(lb)
                dv_ref[pl.ds(r0, tb), sl] += _dot(w.astype(BF16), dob, 0, 0)
                return c + jnp.sum(l1m, axis=1, keepdims=True)

            lax.fori_loop(0, i + 1, sweep_left, jnp.zeros((tb, 1), F32))

            def sweep_right(j, carry, sl=sl, qb=qb):
                c2, dq = carry
                r0 = pl.multiple_of(j * tb, tb)
                kj = k_ref[pl.ds(r0, tb), sl].astype(BF16)
                strict = (col + j * tb) < (row + i * tb)
                e = e_scr[j]
                sig = sig_scr[j]
                ehi, elo = _split_hi_lo(e)
                left = _dot(ehi, before) + _dot(elo, before) + c2
                dz = jnp.where(strict, e * (1.0 - sig) - left * sig, 0.0).astype(BF16)
                dq = dq + _dot(dz, kj)
                dk_ref[pl.ds(r0, tb), sl] += scale * _dot(dz, qb, 0, 0)
                return c2 + jnp.sum(e, axis=1, keepdims=True), dq

            _, dq = lax.fori_loop(0, i + 1, sweep_right, (jnp.zeros((tb, 1), F32), jnp.zeros((tb, HEAD_DIM), F32)))
            dq_ref[:, sl] = dq * scale

    blk = pl.BlockSpec((tb, LANES), lambda h, i: (i, h))
    whole = pl.BlockSpec((L, LANES), lambda h, i: (0, h))
    return _pallas(
        body, name=name, grid=(N_HEADS // 2, L // tb), in_specs=[blk, whole, whole, blk],
        out_specs=[blk, whole, whole], out_shape=[jax.ShapeDtypeStruct(q.shape, F32)] * 3,
        scratch_shapes=[pltpu.VMEM((L // tb, tb, tb), F32), pltpu.VMEM((L // tb, tb, tb), F32)],
        compiler_params=_cparams("parallel", "arbitrary"),
    )(q, k, v, do)


def _make_attn(name):
    @jax.custom_vjp
    def attn(q, k, v):
        return _attn_fwd(q, k, v, name=name + "_fwd")

    def fwd(q, k, v):
        return _attn_fwd(q, k, v, name=name + "_fwd"), (q, k, v)

    def bwd(res, do):
        q, k, v = res
        return tuple(_attn_bwd(q, k, v, do, name=name + "_bwd"))

    attn.defvjp(fwd, bwd)
    return attn


def _make_loss_head(n_valid, name):
    def run(h, g, target):
        L, d = h.shape
        tm = _pick(L, (384, 128))

        def body(h_ref, t_ref, g_ref, loss_ref, dh_ref, dg_ref):
            i = pl.program_id(0)
            x = h_ref[...]
            gb = g_ref[...]
            rows = lax.broadcasted_iota(jnp.int32, (tm, 1), 0) + i * tm
            valid = jnp.logical_and(rows >= N_META, rows < n_valid)
            r = lax.rsqrt(jnp.mean(x * x, axis=-1, keepdims=True) + RMS_EPS)
            xh = x * r
            err = jnp.where(valid, xh * gb - t_ref[...], 0.0)
            dy = err * (1.0 / d)
            dxh = dy * gb
            dh_ref[...] = r * (dxh - xh * jnp.mean(dxh * xh, axis=-1, keepdims=True))
            part = 0.5 / d * jnp.sum(jnp.sum(err * err, axis=1, keepdims=True), axis=0, keepdims=True)
            dgp = _colsum(dy * xh)

            @pl.when(i == 0)
            def _():
                loss_ref[...] = part
                dg_ref[...] = dgp

            @pl.when(i > 0)
            def _():
                loss_ref[...] += part
                dg_ref[...] += dgp

        rowspec = pl.BlockSpec((tm, d), lambda i: (i, 0))
        return _pallas(
            body, name=name, grid=(L // tm,),
            in_specs=[rowspec, rowspec, pl.BlockSpec((1, d), lambda i: (0, 0))],
            out_specs=[pl.BlockSpec((1, 1), lambda i: (0, 0)), rowspec, pl.BlockSpec((1, d), lambda i: (0, 0))],
            out_shape=[jax.ShapeDtypeStruct((1, 1), F32), jax.ShapeDtypeStruct((L, d), F32), jax.ShapeDtypeStruct((1, d), F32)],
            compiler_params=_cparams("arbitrary"),
        )(h, target, g.reshape(1, d))

    @jax.custom_vjp
    def loss_head(h, g, target):
        return run(h, g, target)[0][0, 0]

    def fwd(h, g, target):
        loss, dh, dg = run(h, g, target)
        return loss[0, 0], (dh, dg.reshape(g.shape), target)

    def bwd(res, ct):
        dh, dg, target = res
        return ct * dh, ct * dg, jnp.zeros_like(target)

    loss_head.defvjp(fwd, bwd)
    return loss_head


def _forward_loss(x_pad, target_pad, n_valid, small, big, proxies):
    rms = _make_rmsnorm
    Lp = x_pad.shape[0]
    h = x_pad.at[:N_META].set(small["meta_tokens"])

    u = rms("mix0")(h, small["norm_mix"][0])
    wr, wi, cr, cin, lam_r, lam_i = _s5_discretize(
        small["s5_a_re"][0], small["s5_a_im"][0], small["s5_log_dt"][0], small["s5_b_re"][0], small["s5_b_im"][0],
        small["s5_c_re"][0], small["s5_c_im"][0])
    y = _make_ssm("ssm")(u, wr, wi, cr, cin, lam_r, lam_i)
    z = _make_s5_post("s5post")(y, u, small["s5_d"][0])
    vg = _make_mm("glu_mm")(z, big["s5_w_glu"], proxies["s5_w_glu"])
    h = h + _make_glu("glu")(vg)

    def ffn(h, i):
        f = rms(f"ffn{i}")(h, small["norm_ffn"][i])
        gu = _make_mm(f"ffn_in{i}")(f, big[f"w_ffn_in{i}"], proxies[f"w_ffn_in{i}"])
        a = _make_swiglu(f"swiglu{i}")(gu)
        return h + _make_mm(f"ffn_out{i}")(a, big[f"w_ffn_out{i}"], proxies[f"w_ffn_out{i}"])

    h = ffn(h, 0)
    kvn = rms("kv")(h, small["norm_kv"])
    k = _make_mm("k_mm")(kvn, big["w_k"], proxies["w_k"])
    v = _make_mm("v_mm")(kvn, big["w_v"], proxies["w_v"])

    qn = rms("mix1")(h, small["norm_mix"][1])
    q = _make_mm("q_mm")(qn, big["w_q"], proxies["w_q"])
    o = _make_attn("attn")(q, k, v)
    h = h + _make_mm("o_mm")(o, big["w_o"], proxies["w_o"])
    h = ffn(h, 1)
    return _make_loss_head(n_valid, "loss_head")(h, small["norm_final"], target_pad)


def _coords():
    return lax.axis_index("x"), lax.axis_index("y"), lax.axis_index("c")


def _flip(bits):
    x, y, c = _coords()
    fx, fy, fc = bits
    return (x ^ fx if fx else x, y ^ fy if fy else y, c ^ fc if fc else c)


def _exchange(ins, out_shapes, copies, local_copies, *, aliases=None, name):
    n_in, n_out = len(ins), len(out_shapes)
    n_cp, n_loc = len(copies), len(local_copies)
    aliases = aliases or {}

    def body(*refs):
        in_refs, out_refs = refs[:n_in], refs[n_in:n_in + n_out]
        send_sems, recv_sems, loc_sems = refs[n_in + n_out:]
        me = _coords()
        locs = []
        for n, (ii, oi, dfn) in enumerate(local_copies):
            cp = pltpu.make_async_copy(in_refs[ii], out_refs[oi].at[dfn(*me)], loc_sems.at[n])
            cp.start()
            locs.append(cp)
        sends = []
        for n, (ii, sfn, bits, oi, dfn) in enumerate(copies):
            src = in_refs[ii] if sfn is None else in_refs[ii].at[sfn(*me)]
            cp = pltpu.make_async_remote_copy(
                src_ref=src, dst_ref=out_refs[oi].at[dfn(*me)], send_sem=send_sems.at[n], recv_sem=recv_sems.at[n],
                device_id=_flip(bits), device_id_type=MESH)
            cp.start()
            sends.append(cp)
        for n, (ii, sfn, bits, oi, dfn) in enumerate(copies):
            peer = _flip(bits)
            src = in_refs[ii] if sfn is None else in_refs[ii].at[sfn(*me)]
            pltpu.make_async_remote_copy(
                src_ref=src, dst_ref=out_refs[oi].at[dfn(*peer)], send_sem=send_sems.at[n], recv_sem=recv_sems.at[n],
                device_id=peer, device_id_type=MESH).wait_recv()
        for cp in sends:
            cp.wait_send()
        for cp in locs:
            cp.wait()

    any_spec = pl.BlockSpec(memory_space=pl.ANY)
    return _pallas(
        body, name=name, in_specs=[any_spec] * n_in, out_specs=[any_spec] * n_out,
        out_shape=[jax.ShapeDtypeStruct(s, d) for s, d in out_shapes],
        scratch_shapes=[pltpu.SemaphoreType.DMA((max(n_cp, 1),)), pltpu.SemaphoreType.DMA((max(n_cp, 1),)),
                        pltpu.SemaphoreType.DMA((max(n_loc, 1),))],
        input_output_aliases=aliases,
        compiler_params=pltpu.CompilerParams(has_side_effects=True),
    )(*ins)


ICI_FLIPS = ((1, 0, 0), (0, 1, 0), (1, 1, 0))
D2D_FLIP = (0, 0, 1)


def _slot_of(x, y, c):
    return 4 * x + 2 * y + c


def _all_gather(shards, *, name):
    n = len(shards)
    outs = [((N_DEV,) + s.shape, s.dtype) for s in shards]
    copies = [(t, None, bits, t, _slot_of) for t in range(n) for bits in ICI_FLIPS]
    local = [(t, t, _slot_of) for t in range(n)]
    bufs = _exchange(shards, outs, copies, local, name=name + "_ici")
    copies2 = [(t, (lambda x, y, c, q=q: 2 * q + c), D2D_FLIP, t, (lambda x, y, c, q=q: 2 * q + c))
               for t in range(n) for q in range(4)]
    return _exchange(bufs, outs, copies2, [], aliases={t: t for t in range(n)}, name=name + "_d2d")


def _reduce_scatter_stage1(grads, smalls, *, name):
    n = len(grads)
    outs = [((4,) + g.shape[1:], g.dtype) for g in grads] + [((N_DEV,) + s.shape, s.dtype) for s in smalls]
    copies = [(t, (lambda x, y, c, q=q: 2 * q + (1 - c)), D2D_FLIP, t, (lambda x, y, c, q=q: q))
              for t in range(n) for q in range(4)]
    copies += [(n + t, None, bits, n + t, _slot_of) for t in range(len(smalls)) for bits in ICI_FLIPS]
    local = [(n + t, n + t, _slot_of) for t in range(len(smalls))]
    res = _exchange(list(grads) + list(smalls), outs, copies, local, name=name)
    return res[:n], res[n:]


def _reduce_scatter_stage2(parts, small_bufs, *, name):
    n = len(parts)
    outs = [((3,) + p.shape[1:], p.dtype) for p in parts] + [(s.shape, s.dtype) for s in small_bufs]
    copies = []
    for t in range(n):
        for r, bits in enumerate(ICI_FLIPS):
            copies.append((t, (lambda x, y, c, b=bits: 2 * (x ^ b[0]) + (y ^ b[1])), bits, t, (lambda x, y, c, r=r: r)))
    for t in range(len(small_bufs)):
        for q in range(4):
            copies.append((n + t, (lambda x, y, c, q=q: 2 * q + c), D2D_FLIP, n + t, (lambda x, y, c, q=q: 2 * q + c)))
    res = _exchange(list(parts) + list(small_bufs), outs, copies, [],
                    aliases={n + t: n + t for t in range(len(small_bufs))}, name=name)
    return res[:n], res[n:]


def _adam_math(w, g, m, v):
    m = ADAM_B1 * m + (1.0 - ADAM_B1) * g
    v = ADAM_B2 * v + (1.0 - ADAM_B2) * (g * g)
    m_hat = m / (1.0 - ADAM_B1 ** ADAM_STEP)
    v_hat = v / (1.0 - ADAM_B2 ** ADAM_STEP)
    delta = -ADAM_LR * (m_hat / (jnp.sqrt(v_hat) + ADAM_EPS) + ADAM_WD * w)
    return delta, m, v


def _pair_sum(grads8, landing, *, name):
    _, r, cdim = grads8.shape
    tr = _pick(r, (256, 128, 64, 32, 16, 8))
    c = lax.axis_index("c").astype(jnp.int32).reshape(1)

    def body(c_ref, g_ref, l_ref, o_ref):
        o_ref[...] = g_ref[...] + l_ref[...]

    return _pallas(
        body, name=name,
        grid_spec=pltpu.PrefetchScalarGridSpec(
            num_scalar_prefetch=1, grid=(4, r // tr),
            in_specs=[pl.BlockSpec((None, tr, cdim), lambda q, i, c_ref: (2 * q + c_ref[0], i, 0)),
                      pl.BlockSpec((None, tr, cdim), lambda q, i, c_ref: (q, i, 0))],
            out_specs=pl.BlockSpec((None, tr, cdim), lambda q, i, c_ref: (q, i, 0))),
        out_shape=jax.ShapeDtypeStruct((4, r, cdim), F32),
        compiler_params=_cparams("parallel", "parallel"),
    )(c, grads8, landing)


def _shard_adamw(part, landing2, w, m, v, *, name):
    r, cdim = w.shape
    tr = _pick(r, (256, 128, 64, 32, 16, 8))
    q = (2 * lax.axis_index("x") + lax.axis_index("y")).astype(jnp.int32).reshape(1)

    def body(q_ref, p_ref, l_ref, w_ref, m_ref, v_ref, g_out, d_out, m_out, v_out):
        g = p_ref[...] + l_ref[0] + l_ref[1] + l_ref[2]
        d, mn, vn = _adam_math(w_ref[...], g, m_ref[...], v_ref[...])
        g_out[...] = g
        d_out[...] = d
        m_out[...] = mn
        v_out[...] = vn

    blk = pl.BlockSpec((tr, cdim), lambda i, q_ref: (i, 0))
    return _pallas(
        body, name=name,
        grid_spec=pltpu.PrefetchScalarGridSpec(
            num_scalar_prefetch=1, grid=(r // tr,),
            in_specs=[pl.BlockSpec((None, tr, cdim), lambda i, q_ref: (q_ref[0], i, 0)),
                      pl.BlockSpec((3, tr, cdim), lambda i, q_ref: (0, i, 0)), blk, blk, blk],
            out_specs=[blk] * 4),
        out_shape=[jax.ShapeDtypeStruct((r, cdim), F32)] * 4,
        compiler_params=_cparams("parallel"),
    )(q, part, landing2, w, m, v)


def _small_adamw(slots, w, m, v, *, name):
    r, cdim = w.shape

    def body(s_ref, w_ref, m_ref, v_ref, g_out, d_out, m_out, v_out):
        g = s_ref[0]
        for n in range(1, N_DEV):
            g = g + s_ref[n]
        d, mn, vn = _adam_math(w_ref[...], g, m_ref[...], v_ref[...])
        g_out[...] = g
        d_out[...] = d
        m_out[...] = mn
        v_out[...] = vn

    tr = _pick(r, (256, 128, 64, 32, 16, 8))
    blk = pl.BlockSpec((tr, cdim), lambda i: (i, 0))
    return _pallas(
        body, name=name, grid=(r // tr,),
        in_specs=[pl.BlockSpec((N_DEV, tr, cdim), lambda i: (0, i, 0)), blk, blk, blk], out_specs=[blk] * 4,
        out_shape=[jax.ShapeDtypeStruct((r, cdim), F32)] * 4,
        compiler_params=_cparams("parallel"),
    )(slots, w, m, v)


def _plain_adamw(g, w, m, v, *, name):
    def body(g_ref, w_ref, m_ref, v_ref, d_out, m_out, v_out):
        d, mn, vn = _adam_math(w_ref[...], g_ref[...], m_ref[...], v_ref[...])
        d_out[...] = d
        m_out[...] = mn
        v_out[...] = vn

    spec = pl.BlockSpec(g.shape, lambda: (0,) * g.ndim)
    return _pallas(body, name=name, in_specs=[spec] * 4, out_specs=[spec] * 3,
                   out_shape=[jax.ShapeDtypeStruct(g.shape, F32)] * 3)(g, w, m, v)


def _cast_bf16(x, *, name):
    r, cdim = x.shape
    tr = _pick(r, (256, 128, 64, 32, 16))

    def body(x_ref, o_ref):
        o_ref[...] = x_ref[...].astype(BF16)

    return _pallas(body, name=name, grid=(r // tr,), in_specs=[pl.BlockSpec((tr, cdim), lambda i: (i, 0))],
                   out_specs=pl.BlockSpec((tr, cdim), lambda i: (i, 0)), out_shape=jax.ShapeDtypeStruct(x.shape, BF16),
                   compiler_params=_cparams("parallel"))(x)


SMALL_NAMES = ("norm_mix", "norm_ffn", "s5_a_re", "s5_a_im", "s5_log_dt", "s5_b_re", "s5_b_im", "s5_c_re", "s5_c_im",
               "norm_kv", "norm_final")


def _pack_rows(arrs):
    rows = []
    for a in arrs:
        flat = a.reshape(-1)
        pad = (-flat.shape[0]) % (8 * LANES)
        rows.append(jnp.pad(flat, (0, pad)).reshape(-1, LANES))
    return jnp.concatenate(rows, axis=0)


def _unpack_rows(packed, like):
    out, r0 = [], 0
    for a in like:
        n = math.prod(a.shape)
        nr = (n + 8 * LANES - 1) // (8 * LANES) * 8
        out.append(packed[r0:r0 + nr].reshape(-1)[:n].reshape(a.shape))
        r0 += nr
    return out


def kernel(x, meta_tokens, norm_mix, norm_ffn, s5_a_re, s5_a_im, s5_log_dt, s5_b_re, s5_b_im, s5_c_re, s5_c_im, s5_d, s5_w_glu, norm_kv, w_kv, w_q, w_o, w_ffn_in, w_ffn_out, norm_final, loss_target, m_meta_tokens, m_norm_mix, m_norm_ffn, m_s5_a_re, m_s5_a_im, m_s5_log_dt, m_s5_b_re, m_s5_b_im, m_s5_c_re, m_s5_c_im, m_s5_d, m_s5_w_glu, m_norm_kv, m_w_kv, m_w_q, m_w_o, m_w_ffn_in, m_w_ffn_out, m_norm_final, v_meta_tokens, v_norm_mix, v_norm_ffn, v_s5_a_re, v_s5_a_im, v_s5_log_dt, v_s5_b_re, v_s5_b_im, v_s5_c_re, v_s5_c_im, v_s5_d, v_s5_w_glu, v_norm_kv, v_w_kv, v_w_q, v_w_o, v_w_ffn_in, v_w_ffn_out, v_norm_final):
    args = dict(locals())
    seq = x.shape[1]
    n_valid = N_META + seq
    Lp = (n_valid + ATTN_BLOCK - 1) // ATTN_BLOCK * ATTN_BLOCK
    dm = D_MODEL
    my_slot = _slot_of(*_coords())

    big_shards = {
        "s5_w_glu": s5_w_glu[0], "w_kv": w_kv, "w_q": w_q[0], "w_o": w_o[0],
        "w_ffn_in": w_ffn_in.reshape(-1, w_ffn_in.shape[-1]), "w_ffn_out": w_ffn_out.reshape(-1, w_ffn_out.shape[-1]),
    }
    big_names = list(big_shards)
    cast = [_cast_bf16(big_shards[n], name="cast_" + n) for n in big_names]
    col_shard = jnp.concatenate([meta_tokens, s5_d, jnp.zeros((7, LANES), F32)], axis=0)
    gathered = _all_gather(cast + [col_shard], name="ag")
    gw = dict(zip(big_names, gathered[:-1]))
    gcol = gathered[-1]
    meta_full = jnp.swapaxes(gcol[:, :N_META], 0, 1).reshape(N_META, dm)
    d_full = gcol[:, N_META].reshape(1, dm)

    def cols(g8):
        return jnp.swapaxes(g8, 0, 1).reshape(g8.shape[1], -1)

    w_glu_full = cols(gw["s5_w_glu"])
    w_kv_full = cols(gw["w_kv"])
    n_ffn = w_ffn_in.shape[0]
    w_in_full = cols(gw["w_ffn_in"]).reshape(n_ffn, dm, -1)
    w_out_full = jnp.swapaxes(gw["w_ffn_out"].reshape(N_DEV, n_ffn, -1, dm), 0, 1).reshape(n_ffn, -1, dm)
    big = {
        "s5_w_glu": w_glu_full, "w_k": w_kv_full[:, :dm], "w_v": w_kv_full[:, dm:],
        "w_q": gw["w_q"].reshape(dm, dm), "w_o": gw["w_o"].reshape(dm, dm),
        "w_ffn_in0": w_in_full[0], "w_ffn_in1": w_in_full[1], "w_ffn_out0": w_out_full[0], "w_ffn_out1": w_out_full[1],
    }
    proxies = {n: jnp.zeros(w.shape, F32) for n, w in big.items()}
    small = {n: args[n] for n in SMALL_NAMES}
    small["meta_tokens"] = meta_full
    small["s5_d"] = d_full

    x_pad = jnp.zeros((Lp, dm), F32).at[N_META:n_valid].set(x[0])
    t_pad = jnp.zeros((Lp, dm), F32).at[N_META:n_valid].set(loss_target[0])
    loss_local, vjp = jax.vjp(lambda s, p, xp: _forward_loss(xp, t_pad, n_valid, s, big, p), small, proxies, x_pad)
    g_small, g_big, g_xpad = vjp(jnp.ones((), F32))
    loss = lax.psum(loss_local, ("x", "y", "c"))
    grad_x = g_xpad[N_META:n_valid][None]

    def to_cols(g, nshard=N_DEV):
        return jnp.swapaxes(g.reshape(g.shape[0], nshard, -1), 0, 1)

    g_in = jnp.stack([g_big["w_ffn_in0"], g_big["w_ffn_in1"]])
    g_out = jnp.stack([g_big["w_ffn_out0"], g_big["w_ffn_out1"]])
    grads8 = {
        "s5_w_glu": to_cols(g_big["s5_w_glu"]),
        "w_kv": to_cols(jnp.concatenate([g_big["w_k"], g_big["w_v"]], axis=1)),
        "w_q": g_big["w_q"].reshape(N_DEV, -1, dm), "w_o": g_big["w_o"].reshape(N_DEV, -1, dm),
        "w_ffn_in": jnp.transpose(g_in.reshape(n_ffn, dm, N_DEV, -1), (2, 0, 1, 3)).reshape(N_DEV, n_ffn * dm, -1),
        "w_ffn_out": jnp.swapaxes(g_out.reshape(n_ffn, N_DEV, -1, dm), 0, 1).reshape(N_DEV, -1, dm),
    }
    small_list = [g_small[n] for n in SMALL_NAMES] + [g_small["meta_tokens"], g_small["s5_d"]]
    packed = _pack_rows(small_list)

    landing, small_bufs = _reduce_scatter_stage1([grads8[n] for n in big_names], [packed], name="rs1")
    parts = [_pair_sum(grads8[n], l, name="pair_sum_" + n) for n, l in zip(big_names, landing)]
    landing2, small_bufs = _reduce_scatter_stage2(parts, small_bufs, name="rs2")

    out = {}
    for n, part, l2 in zip(big_names, parts, landing2):
        w = big_shards[n]
        m = args["m_" + n].reshape(w.shape)
        v = args["v_" + n].reshape(w.shape)
        g, d, mn, vn = _shard_adamw(part, l2, w, m, v, name="adamw_" + n)
        shape = args[n].shape
        out[n] = (g.reshape(shape), d.reshape(shape), mn.reshape(shape), vn.reshape(shape))

    zeros_tail = [jnp.zeros_like(g_small["meta_tokens"]), jnp.zeros_like(g_small["s5_d"])]
    pw = _pack_rows([args[n] for n in SMALL_NAMES] + zeros_tail)
    pm = _pack_rows([args["m_" + n] for n in SMALL_NAMES] + zeros_tail)
    pv = _pack_rows([args["v_" + n] for n in SMALL_NAMES] + zeros_tail)
    sg, sd, sm, sv = _small_adamw(small_bufs[0], pw, pm, pv, name="adamw_small")
    like = small_list
    ug, ud, um, uv = (_unpack_rows(a, like) for a in (sg, sd, sm, sv))
    for i, n in enumerate(SMALL_NAMES):
        out[n] = (ug[i], ud[i], um[i], uv[i])
    g_meta = lax.dynamic_slice_in_dim(ug[-2], my_slot * LANES, LANES, axis=1)
    g_d = lax.dynamic_slice_in_dim(ug[-1].reshape(1, dm), my_slot * LANES, LANES, axis=1)
    pad7 = jnp.zeros((7, LANES), F32)
    gc = jnp.concatenate([g_meta, g_d, pad7], axis=0)
    wc = jnp.concatenate([meta_tokens, s5_d, pad7], axis=0)
    mc = jnp.concatenate([m_meta_tokens, m_s5_d, pad7], axis=0)
    vc = jnp.concatenate([v_meta_tokens, v_s5_d, pad7], axis=0)
    dc, mcn, vcn = _plain_adamw(gc, wc, mc, vc, name="adamw_cols")
    out["meta_tokens"] = (g_meta, dc[:N_META], mcn[:N_META], vcn[:N_META])
    out["s5_d"] = (g_d, dc[N_META:N_META + 1], mcn[N_META:N_META + 1], vcn[N_META:N_META + 1])

    order = ["meta_tokens", "norm_mix", "norm_ffn", "s5_a_re", "s5_a_im", "s5_log_dt", "s5_b_re", "s5_b_im", "s5_c_re",
             "s5_c_im", "s5_d", "s5_w_glu", "norm_kv", "w_kv", "w_q", "w_o", "w_ffn_in", "w_ffn_out", "norm_final"]
    res = [loss, grad_x]
    for k in range(4):
        res += [out[n][k] for n in order]
    return tuple(res)
```

```python
import functools
import math

import jax
import jax.numpy as jnp
from jax import lax
from jax.experimental import pallas as pl
from jax.experimental.pallas import tpu as pltpu

F32 = jnp.float32
BF16 = jnp.bfloat16

N_META = 16
D_MODEL = 1024
S5_GROUPS = 64
S5_GROUP = 16
S5_STATE = 64
N_HEADS = 16
HEAD_DIM = 64
D_FF = 2816
RMS_EPS = 1e-6
ADAM_LR, ADAM_B1, ADAM_B2, ADAM_EPS, ADAM_WD, ADAM_STEP = 0.001, 0.9, 0.999, 1e-08, 0.01, 10

LANES = 128
ATTN_BLOCK = 128
GROUPS_PER_BLOCK = 8
N_DEV = 8
MESH = pl.DeviceIdType.MESH
VMEM_LIMIT = 56 * 1024 * 1024


def _pallas(body, **kw):
    return pl.pallas_call(body, **kw)


def _cparams(*sem):
    return pltpu.CompilerParams(dimension_semantics=sem, vmem_limit_bytes=VMEM_LIMIT)


def _pick(n, prefs):
    for p in prefs:
        if n % p == 0:
            return p
    return n


def _dot(a, b, ca=1, cb=0):
    return lax.dot_general(a, b, (((ca,), (cb,)), ((), ())), preferred_element_type=F32)


MATMUL_VMEM_BUDGET = 36 * 1024 * 1024
MATMUL_TILES = (1408, 1024, 512, 384, 256, 128)


def _matmul_tiles(M, N, K, a_bytes, b_bytes):
    best = None
    for tm in (t for t in MATMUL_TILES if M % t == 0):
        for tn in (t for t in MATMUL_TILES if N % t == 0):
            for tk in (t for t in MATMUL_TILES if K % t == 0):
                vmem = 2 * (tm * tk * a_bytes + tk * tn * b_bytes + tm * tn * 4) + tm * tn * 4
                if vmem > MATMUL_VMEM_BUDGET:
                    continue
                traffic = M * K * a_bytes * (N // tn) + K * N * b_bytes * (M // tm) + M * N * 4
                traffic += (K // tk - 1) * M * N * 8 // 3
                key = (traffic, -tk, -tm)
                if best is None or key < best[0]:
                    best = (key, (tm, tn, tk))
    return best[1]


def _matmul(a, b, *, ta=False, tb=False, name):
    if ta:
        K, M = a.shape
    else:
        M, K = a.shape
    if tb:
        N, K2 = b.shape
    else:
        K2, N = b.shape
    assert K == K2, (a.shape, b.shape)
    tm, tn, tk = _matmul_tiles(M, N, K, a.dtype.itemsize, b.dtype.itemsize)
    nk = K // tk

    def body(a_ref, b_ref, o_ref, acc_ref):
        k = pl.program_id(2)
        part = _dot(a_ref[...].astype(BF16), b_ref[...].astype(BF16), 0 if ta else 1, 1 if tb else 0)

        @pl.when(k == 0)
        def _():
            acc_ref[...] = part

        @pl.when(k > 0)
        def _():
            acc_ref[...] += part

        @pl.when(k == nk - 1)
        def _():
            o_ref[...] = acc_ref[...]

    a_spec = pl.BlockSpec((tk, tm), lambda i, j, k: (k, i)) if ta else pl.BlockSpec((tm, tk), lambda i, j, k: (i, k))
    b_spec = pl.BlockSpec((tn, tk), lambda i, j, k: (j, k)) if tb else pl.BlockSpec((tk, tn), lambda i, j, k: (k, j))
    return _pallas(
        body, name=name, grid=(M // tm, N // tn, nk), in_specs=[a_spec, b_spec],
        out_specs=pl.BlockSpec((tm, tn), lambda i, j, k: (i, j)),
        out_shape=jax.ShapeDtypeStruct((M, N), F32),
        scratch_shapes=[pltpu.VMEM((tm, tn), F32)],
        compiler_params=_cparams("parallel", "parallel", "arbitrary"),
    )(a, b)


def _make_mm(name):
    @jax.custom_vjp
    def mm(x, w, proxy):
        return _matmul(x, w, name=name + "_fwd")

    def fwd(x, w, proxy):
        return _matmul(x, w, name=name + "_fwd"), (x, w)

    def bwd(res, dy):
        x, w = res
        dx = _matmul(dy, w, tb=True, name=name + "_dx")
        dw = _matmul(x, dy, ta=True, name=name + "_dw")
        return dx, jnp.zeros_like(w), dw

    mm.defvjp(fwd, bwd)
    return mm


def _rowwise(fn, row_ins, full_ins, row_outs, acc_outs, *, tm, name):
    L = row_ins[0].shape[0]
    n_row, n_full, n_ro = len(row_ins), len(full_ins), len(row_outs)

    def body(*refs):
        ins = [r[...] for r in refs[:n_row + n_full]]
        outs = refs[n_row + n_full:]
        res = fn(*ins)
        for r, val in zip(outs[:n_ro], res[:n_ro]):
            r[...] = val.astype(r.dtype)
        if acc_outs:
            first = pl.program_id(0) == 0
            for r, val in zip(outs[n_ro:], res[n_ro:]):
                @pl.when(first)
                def _(r=r, val=val):
                    r[...] = val

                @pl.when(jnp.logical_not(first))
                def _(r=r, val=val):
                    r[...] += val

    in_specs = [pl.BlockSpec((tm, a.shape[1]), lambda i: (i, 0)) for a in row_ins]
    in_specs += [pl.BlockSpec(a.shape, lambda i, nd=a.ndim: (0,) * nd) for a in full_ins]
    out_specs = [pl.BlockSpec((tm, s[1]), lambda i: (i, 0)) for s, _ in row_outs]
    out_specs += [pl.BlockSpec(s, lambda i, nd=len(s): (0,) * nd) for s, _ in acc_outs]
    out_shape = [jax.ShapeDtypeStruct(s, d) for s, d in list(row_outs) + list(acc_outs)]
    return _pallas(
        body, name=name, grid=(L // tm,), in_specs=in_specs, out_specs=out_specs, out_shape=out_shape,
        compiler_params=_cparams("arbitrary" if acc_outs else "parallel"),
    )(*row_ins, *full_ins)


def _colsum(x):
    return jnp.sum(x, axis=0, keepdims=True)


def _make_rmsnorm(name):
    @jax.custom_vjp
    def rmsnorm(x, g):
        return fwd(x, g)[0]

    def fwd(x, g):
        L, d = x.shape
        g2 = g.reshape(1, d)

        def f(xb, gb):
            r = lax.rsqrt(jnp.mean(xb * xb, axis=-1, keepdims=True) + RMS_EPS)
            return (xb * r * gb,)

        (y,) = _rowwise(f, [x], [g2], [((L, d), F32)], [], tm=_pick(L, (384, 128)), name=name + "_fwd")
        return y, (x, g)

    def bwd(res, dy):
        x, g = res
        L, d = x.shape
        g2 = g.reshape(1, d)

        def f(xb, dyb, gb):
            r = lax.rsqrt(jnp.mean(xb * xb, axis=-1, keepdims=True) + RMS_EPS)
            xh = xb * r
            dxh = dyb * gb
            dx = r * (dxh - xh * jnp.mean(dxh * xh, axis=-1, keepdims=True))
            return dx, _colsum(dyb * xh)

        dx, dg = _rowwise(f, [x, dy], [g2], [((L, d), F32)], [((1, d), F32)], tm=_pick(L, (384, 128)), name=name + "_bwd")
        return dx, dg.reshape(g.shape)

    rmsnorm.defvjp(fwd, bwd)
    return rmsnorm


def _sigmoid(x):
    return 1.0 / (1.0 + jnp.exp(-x))


def _make_swiglu(name):
    @jax.custom_vjp
    def swiglu(gu):
        return fwd(gu)[0]

    def fwd(gu):
        L, f2 = gu.shape
        half = f2 // 2

        def f(b):
            g, u = b[:, :half], b[:, half:]
            return (g * _sigmoid(g) * u,)

        (a,) = _rowwise(f, [gu], [], [((L, half), F32)], [], tm=128, name=name + "_fwd")
        return a, (gu,)

    def bwd(res, da):
        (gu,) = res
        L, f2 = gu.shape
        half = f2 // 2

        def f(b, dab):
            g, u = b[:, :half], b[:, half:]
            s = _sigmoid(g)
            dg = dab * u * (s + g * s * (1.0 - s))
            du = dab * g * s
            return (jnp.concatenate([dg, du], axis=1),)

        (dgu,) = _rowwise(f, [gu, da], [], [((L, f2), F32)], [], tm=128, name=name + "_bwd")
        return (dgu,)

    swiglu.defvjp(fwd, bwd)
    return swiglu


def _make_glu(name):
    @jax.custom_vjp
    def glu(vg):
        return fwd(vg)[0]

    def fwd(vg):
        L, f2 = vg.shape
        half = f2 // 2

        def f(b):
            return (b[:, :half] * _sigmoid(b[:, half:]),)

        (a,) = _rowwise(f, [vg], [], [((L, half), F32)], [], tm=128, name=name + "_fwd")
        return a, (vg,)

    def bwd(res, da):
        (vg,) = res
        L, f2 = vg.shape
        half = f2 // 2

        def f(b, dab):
            val, gate = b[:, :half], b[:, half:]
            s = _sigmoid(gate)
            return (jnp.concatenate([dab * s, dab * val * s * (1.0 - s)], axis=1),)

        (dvg,) = _rowwise(f, [vg, da], [], [((L, f2), F32)], [], tm=128, name=name + "_bwd")
        return (dvg,)

    glu.defvjp(fwd, bwd)
    return glu


GELU_K = math.sqrt(2.0 / math.pi)
GELU_C = 0.044715


def _make_s5_post(name):
    @jax.custom_vjp
    def post(y, u, d):
        return fwd(y, u, d)[0]

    def fwd(y, u, d):
        L, dm = y.shape

        def f(yb, ub, db):
            s = yb + db * ub
            return (0.5 * s * (1.0 + jnp.tanh(GELU_K * (s + GELU_C * s * s * s))),)

        (z,) = _rowwise(f, [y, u], [d.reshape(1, dm)], [((L, dm), F32)], [], tm=_pick(L, (384, 128)), name=name + "_fwd")
        return z, (y, u, d)

    def bwd(res, dz):
        y, u, d = res
        L, dm = y.shape

        def f(yb, ub, dzb, db):
            s = yb + db * ub
            t = jnp.tanh(GELU_K * (s + GELU_C * s * s * s))
            ds = dzb * (0.5 * (1.0 + t) + 0.5 * s * (1.0 - t * t) * GELU_K * (1.0 + 3.0 * GELU_C * s * s))
            return ds, ds * db, _colsum(ds * ub)

        dy, du, dd = _rowwise(f, [y, u, dz], [d.reshape(1, dm)], [((L, dm), F32), ((L, dm), F32)], [((1, dm), F32)],
                              tm=_pick(L, (384, 128)), name=name + "_bwd")
        return dy, du, dd.reshape(d.shape)

    post.defvjp(fwd, bwd)
    return post


def _bdmm(a_list, w_rows, *, out_dtype=F32, name):
    L = a_list[0].shape[0]
    nb, ka, kb = w_rows[0][0].shape
    tm = _pick(L, (384, 128))
    n_a, n_o = len(a_list), len(w_rows)
    ws = [w for row in w_rows for w in row]

    def body(*refs):
        a_vals = [r[...].astype(BF16) for r in refs[:n_a]]
        w_refs = refs[n_a:n_a + n_a * n_o]
        outs = refs[n_a + n_a * n_o:]
        for o in range(n_o):
            acc = None
            for i in range(n_a):
                p = _dot(a_vals[i], w_refs[o * n_a + i][...].astype(BF16))
                acc = p if acc is None else acc + p
            outs[o][...] = acc.astype(out_dtype)

    return _pallas(
        body, name=name, grid=(L // tm, nb),
        in_specs=[pl.BlockSpec((tm, ka), lambda i, j: (i, j)) for _ in a_list]
        + [pl.BlockSpec((None, ka, kb), lambda i, j: (j, 0, 0)) for _ in ws],
        out_specs=[pl.BlockSpec((tm, kb), lambda i, j: (i, j)) for _ in range(n_o)],
        out_shape=[jax.ShapeDtypeStruct((L, nb * kb), out_dtype) for _ in range(n_o)],
        compiler_params=_cparams("parallel", "parallel"),
    )(*a_list, *ws)


def _bdmm_tn(a, g, nb, *, name):
    L = a.shape[0]
    ka, kb = a.shape[1] // nb, g.shape[1] // nb
    tm = _pick(L, (384, 128))

    def body(a_ref, g_ref, o_ref):
        i = pl.program_id(1)
        part = _dot(a_ref[...].astype(BF16), g_ref[...].astype(BF16), 0, 0)

        @pl.when(i == 0)
        def _():
            o_ref[...] = part

        @pl.when(i > 0)
        def _():
            o_ref[...] += part

    return _pallas(
        body, name=name, grid=(nb, L // tm),
        in_specs=[pl.BlockSpec((tm, ka), lambda j, i: (i, j)), pl.BlockSpec((tm, kb), lambda j, i: (i, j))],
        out_specs=pl.BlockSpec((None, ka, kb), lambda j, i: (j, 0, 0)),
        out_shape=jax.ShapeDtypeStruct((nb, ka, kb), F32),
        compiler_params=_cparams("parallel", "arbitrary"),
    )(a, g)


SCAN_ROWS = 128
STATE_DTYPE = BF16


def _scan_fwd(br, bi, ar, ai, *, name):
    L, S, _ = br.shape
    tb = SCAN_ROWS

    def body(br_ref, bi_ref, ar_ref, ai_ref, xr_ref, xi_ref, carry_ref):
        @pl.when(pl.program_id(0) == 0)
        def _():
            carry_ref[...] = jnp.zeros_like(carry_ref)

        a_r, a_i = ar_ref[...], ai_ref[...]

        def step(t, c):
            xr, xi = c
            nr = a_r * xr - a_i * xi + br_ref[t].astype(F32)
            ni = a_r * xi + a_i * xr + bi_ref[t].astype(F32)
            xr_ref[t] = nr.astype(xr_ref.dtype)
            xi_ref[t] = ni.astype(xi_ref.dtype)
            return nr, ni

        xr, xi = lax.fori_loop(0, tb, step, (carry_ref[0], carry_ref[1]), unroll=8)
        carry_ref[0] = xr
        carry_ref[1] = xi

    blk = pl.BlockSpec((tb, S, LANES), lambda i: (i, 0, 0))
    par = pl.BlockSpec((S, LANES), lambda i: (0, 0))
    return _pallas(
        body, name=name, grid=(L // tb,), in_specs=[blk, blk, par, par], out_specs=[blk, blk],
        out_shape=[jax.ShapeDtypeStruct(br.shape, STATE_DTYPE)] * 2,
        scratch_shapes=[pltpu.VMEM((2, S, LANES), F32)],
        compiler_params=_cparams("arbitrary"),
    )(br, bi, ar, ai)


def _scan_bwd(gr_in, gi_in, ar, ai, xr, xi, *, name):
    L, S, _ = gr_in.shape
    tb = SCAN_ROWS
    nblk = L // tb

    def body(gr_ref, gi_ref, ar_ref, ai_ref, xr_ref, xi_ref, or_ref, oi_ref, dar_ref, dai_ref, carry_ref):
        @pl.when(pl.program_id(0) == 0)
        def _():
            carry_ref[...] = jnp.zeros_like(carry_ref)
            dar_ref[...] = jnp.zeros_like(dar_ref)
            dai_ref[...] = jnp.zeros_like(dai_ref)

        a_r, a_i = ar_ref[...], ai_ref[...]

        def step(s, c):
            gr, gi, dr, di = c
            t = tb - 1 - s
            x_r, x_i = xr_ref[t].astype(F32), xi_ref[t].astype(F32)
            dr = dr + gr * x_r + gi * x_i
            di = di + gi * x_r - gr * x_i
            nr = a_r * gr + a_i * gi + gr_ref[t].astype(F32)
            ni = a_r * gi - a_i * gr + gi_ref[t].astype(F32)
            or_ref[t] = nr.astype(or_ref.dtype)
            oi_ref[t] = ni.astype(oi_ref.dtype)
            return nr, ni, dr, di

        gr, gi, dr, di = lax.fori_loop(0, tb, step, (carry_ref[0], carry_ref[1], dar_ref[...], dai_ref[...]), unroll=8)
        carry_ref[0] = gr
        carry_ref[1] = gi
        dar_ref[...] = dr
        dai_ref[...] = di

    blk = pl.BlockSpec((tb, S, LANES), lambda i: (nblk - 1 - i, 0, 0))
    par = pl.BlockSpec((S, LANES), lambda i: (0, 0))
    return _pallas(
        body, name=name, grid=(nblk,), in_specs=[blk, blk, par, par, blk, blk], out_specs=[blk, blk, par, par],
        out_shape=[jax.ShapeDtypeStruct(gr_in.shape, STATE_DTYPE)] * 2 + [jax.ShapeDtypeStruct((S, LANES), F32)] * 2,
        scratch_shapes=[pltpu.VMEM((2, S, LANES), F32)],
        compiler_params=_cparams("arbitrary"),
    )(gr_in, gi_in, ar, ai, xr, xi)


def _make_ssm(name):
    nb = GROUPS_PER_BLOCK

    @jax.custom_vjp
    def ssm(u, wr, wi, cr, cin, lam_r, lam_i):
        return fwd(u, wr, wi, cr, cin, lam_r, lam_i)[0]

    def fwd(u, wr, wi, cr, cin, lam_r, lam_i):
        L = u.shape[0]
        bur, bui = _bdmm([u], [[wr], [wi]], out_dtype=STATE_DTYPE, name=name + "_bu")
        S = bur.shape[1] // LANES
        xr, xi = _scan_fwd(bur.reshape(L, S, LANES), bui.reshape(L, S, LANES), lam_r, lam_i, name=name + "_scan")
        xr, xi = xr.reshape(L, S * LANES), xi.reshape(L, S * LANES)
        (y,) = _bdmm([xr, xi], [[cr, cin]], name=name + "_cx")
        return y, (u, wr, wi, cr, cin, lam_r, lam_i, xr, xi)

    def bwd(res, dy):
        u, wr, wi, cr, cin, lam_r, lam_i, xr, xi = res
        L = u.shape[0]
        S = xr.shape[1] // LANES
        tr = lambda w: jnp.swapaxes(w, 1, 2)
        gin_r, gin_i = _bdmm([dy], [[tr(cr)], [tr(cin)]], out_dtype=STATE_DTYPE, name=name + "_gin")
        gr, gi, dlr, dli = _scan_bwd(gin_r.reshape(L, S, LANES), gin_i.reshape(L, S, LANES), lam_r, lam_i,
                                     xr.reshape(L, S, LANES), xi.reshape(L, S, LANES), name=name + "_rscan")
        gr, gi = gr.reshape(L, S * LANES), gi.reshape(L, S * LANES)
        (du,) = _bdmm([gr, gi], [[tr(wr), tr(wi)]], name=name + "_du")
        dwr = _bdmm_tn(u, gr, nb, name=name + "_dwr")
        dwi = _bdmm_tn(u, gi, nb, name=name + "_dwi")
        dcr = _bdmm_tn(xr, dy, nb, name=name + "_dcr")
        dcin = _bdmm_tn(xi, dy, nb, name=name + "_dci")
        return du, dwr, dwi, dcr, dcin, dlr, dli

    ssm.defvjp(fwd, bwd)
    return ssm


def _s5_discretize(a_re, a_im, log_dt, b_re, b_im, c_re, c_im):
    G, P, C, nb = S5_GROUPS, S5_STATE, S5_GROUP, GROUPS_PER_BLOCK
    dt = jnp.exp(log_dt)[:, None]
    mag = jnp.exp(dt * a_re)
    ang = dt * a_im
    abar_re = mag * jnp.cos(ang)
    abar_im = mag * jnp.sin(ang)
    den = a_re * a_re + a_im * a_im
    coef_re = ((abar_re - 1.0) * a_re + abar_im * a_im) / den
    coef_im = (abar_im * a_re - (abar_re - 1.0) * a_im) / den
    bbar_re = coef_re[..., None] * b_re - coef_im[..., None] * b_im
    bbar_im = coef_re[..., None] * b_im + coef_im[..., None] * b_re
    eye = jnp.eye(nb, dtype=F32)

    def blocks_in(bb):
        return jnp.einsum("jgpc,gh->jgchp", bb.reshape(G // nb, nb, P, C), eye).reshape(G // nb, nb * C, nb * P)

    def blocks_out(cc):
        return jnp.einsum("jgcp,gh->jgphc", cc.reshape(G // nb, nb, C, P), eye).reshape(G // nb, nb * P, nb * C)

    lam_r = abar_re.reshape(G * P // LANES, LANES)
    lam_i = abar_im.reshape(G * P // LANES, LANES)
    return blocks_in(bbar_re), blocks_in(bbar_im), blocks_out(c_re), blocks_out(-c_im), lam_r, lam_i


def _split_hi_lo(x):
    hi = x.astype(BF16)
    return hi, (x - hi.astype(F32)).astype(BF16)


ATTN_Q_ROWS = 384


HEADS_PER_BLOCK = LANES // HEAD_DIM


def _head_masks(shape, axis):
    idx = lax.broadcasted_iota(jnp.int32, shape, axis) // HEAD_DIM
    return [idx == hh for hh in range(HEADS_PER_BLOCK)]


def _masked_bf16(x, masks):
    return [jnp.where(m, x, 0.0).astype(BF16) for m in masks]


def _hi_lo_sum(x, tri2):
    hi, lo = _split_hi_lo(x)
    return _dot(jnp.concatenate([hi, lo], axis=1), tri2)


def _attn_weights(qh, kt, strict, after2, c):
    z = _dot(qh, kt)
    lb = jnp.minimum(z, 0.0) - jnp.log(1.0 + jnp.exp(-jnp.abs(z)))
    l1m = lb - z
    if strict is not None:
        l1m = jnp.where(strict, l1m, 0.0)
    rem = _hi_lo_sum(l1m, after2)
    w = jnp.exp(lb + rem + c)
    if strict is not None:
        w = jnp.where(strict, w, 0.0)
    return lb, w, rem[:, 0:1] + l1m[:, 0:1]


def _tri2(cmp):
    tk = ATTN_BLOCK
    tri = cmp(lax.broadcasted_iota(jnp.int32, (tk, tk), 0), lax.broadcasted_iota(jnp.int32, (tk, tk), 1)).astype(BF16)
    return jnp.concatenate([tri, tri], axis=0)


def _attn_fwd(q, kt, v, *, name):
    L = q.shape[0]
    tk = ATTN_BLOCK
    tq = _pick(L, (ATTN_Q_ROWS, tk))
    nsub = tq // tk
    scale = 1.0 / math.sqrt(HEAD_DIM)

    def body(q_ref, kt_ref, v_ref, o_ref):
        i = pl.program_id(1)
        after2 = _tri2(lambda r, c: r > c)
        rowq = lax.broadcasted_iota(jnp.int32, (tq, tk), 0)
        colq = lax.broadcasted_iota(jnp.int32, (tq, tk), 1)
        qs = _masked_bf16(q_ref[...] * scale, _head_masks((tq, LANES), 1))
        vmasks = _head_masks((tk, LANES), 1)
        o_ref[...] = jnp.zeros_like(o_ref)

        def tile(r0, strict, cs):
            kt_j = kt_ref[:, pl.ds(r0, tk)].astype(BF16)
            vs = _masked_bf16(v_ref[pl.ds(r0, tk), :], vmasks)
            out, acc = [], None
            for qh, vh, c in zip(qs, vs, cs):
                _, w, tot = _attn_weights(qh, kt_j, strict, after2, c)
                part = _dot(w.astype(BF16), vh)
                acc = part if acc is None else acc + part
                out.append(c + tot)
            o_ref[...] += acc
            return tuple(out)

        cs = tuple(jnp.zeros((tq, 1), F32) for _ in qs)
        for jd in reversed(range(nsub)):
            cs = tile(pl.multiple_of(i * tq + jd * tk, tk), (colq + jd * tk) < rowq, cs)

        def step(jj, cs):
            return tile(pl.multiple_of((i * nsub - 1 - jj) * tk, tk), None, cs)

        lax.fori_loop(0, i * nsub, step, cs)

    blk = pl.BlockSpec((tq, LANES), lambda h, i: (i, h))
    whole = pl.BlockSpec((L, LANES), lambda h, i: (0, h))
    whole_t = pl.BlockSpec((LANES, L), lambda h, i: (h, 0))
    return _pallas(
        body, name=name, grid=(N_HEADS // HEADS_PER_BLOCK, L // tq), in_specs=[blk, whole_t, whole], out_specs=blk,
        out_shape=jax.ShapeDtypeStruct(q.shape, F32),
        compiler_params=_cparams("parallel", "arbitrary"),
    )(q, kt, v)


def _attn_bwd(q, k, kt, vt, do, *, name):
    L = q.shape[0]
    tk = ATTN_BLOCK
    tq = _pick(L, (ATTN_Q_ROWS, tk))
    nsub = tq // tk
    scale = 1.0 / math.sqrt(HEAD_DIM)

    def body(q_ref, k_ref, kt_ref, vt_ref, do_ref, dq_ref, dkt_ref, dvt_ref, e_scr, sig_scr):
        i = pl.program_id(1)

        @pl.when(i == 0)
        def _():
            dkt_ref[...] = jnp.zeros_like(dkt_ref)
            dvt_ref[...] = jnp.zeros_like(dvt_ref)

        after2 = _tri2(lambda r, c: r > c)
        before2 = _tri2(lambda r, c: r < c)
        rowq = lax.broadcasted_iota(jnp.int32, (tq, tk), 0)
        colq = lax.broadcasted_iota(jnp.int32, (tq, tk), 1)
        qmasks = _head_masks((tq, LANES), 1)
        tmasks = _head_masks((LANES, tq), 0)
        kmasks = _head_masks((tk, LANES), 1)
        q_scaled = q_ref[...] * scale
        do32 = do_ref[...]
        qs = _masked_bf16(q_scaled, qmasks)
        dos = _masked_bf16(do32, qmasks)
        qts = _masked_bf16(q_scaled.T, tmasks)
        dots = _masked_bf16(do32.T, tmasks)
        dq_ref[...] = jnp.zeros_like(dq_ref)

        def tile_left(jb, strict, cs):
            r0 = pl.multiple_of(jb * tk, tk)
            kt_j = kt_ref[:, pl.ds(r0, tk)].astype(BF16)
            vt_j = vt_ref[:, pl.ds(r0, tk)].astype(BF16)
            out, acc = [], None
            for hh, (qh, dob, dot_h, c) in enumerate(zip(qs, dos, dots, cs)):
                lb, w, tot = _attn_weights(qh, kt_j, strict, after2, c)
                e_scr[hh, jb] = _dot(dob, vt_j) * w
                sig_scr[hh, jb] = jnp.exp(lb).astype(BF16)
                part = _dot(dot_h, w.astype(BF16))
                acc = part if acc is None else acc + part
                out.append(c + tot)
            dvt_ref[:, pl.ds(r0, tk)] += acc
            return tuple(out)

        def tile_right(jb, strict, cs):
            r0 = pl.multiple_of(jb * tk, tk)
            ks = _masked_bf16(k_ref[pl.ds(r0, tk), :], kmasks)
            out, dq_acc, dk_acc = [], None, None
            for hh, (kh, qt_h, c2) in enumerate(zip(ks, qts, cs)):
                e = e_scr[hh, jb]
                sig = sig_scr[hh, jb].astype(F32)
                left = _hi_lo_sum(e, before2)
                dz = e * (1.0 - sig) - (left + c2) * sig
                if strict is not None:
                    dz = jnp.where(strict, dz, 0.0)
                dz = dz.astype(BF16)
                p1, p2 = _dot(dz, kh), _dot(qt_h, dz)
                dq_acc = p1 if dq_acc is None else dq_acc + p1
                dk_acc = p2 if dk_acc is None else dk_acc + p2
                out.append(c2 + left[:, tk - 1:tk] + e[:, tk - 1:tk])
            dq_ref[...] += dq_acc
            dkt_ref[:, pl.ds(r0, tk)] += dk_acc
            return tuple(out)

        zeros = tuple(jnp.zeros((tq, 1), F32) for _ in qs)
        cs = zeros
        for jd in reversed(range(nsub)):
            cs = tile_left(i * nsub + jd, (colq + jd * tk) < rowq, cs)
        lax.fori_loop(0, i * nsub, lambda jj, cs: tile_left(i * nsub - 1 - jj, None, cs), cs)

        cs = lax.fori_loop(0, i * nsub, lambda j, cs: tile_right(j, None, cs), zeros)
        for jd in range(nsub):
            cs = tile_right(i * nsub + jd, (colq + jd * tk) < rowq, cs)
        dq_ref[...] = dq_ref[...] * scale

    nhb = N_HEADS // HEADS_PER_BLOCK
    blk = pl.BlockSpec((tq, LANES), lambda h, i: (i, h))
    whole = pl.BlockSpec((L, LANES), lambda h, i: (0, h))
    whole_t = pl.BlockSpec((LANES, L), lambda h, i: (h, 0))
    t_shape = jax.ShapeDtypeStruct((q.shape[1], L), F32)
    return _pallas(
        body, name=name, grid=(nhb, L // tq), in_specs=[blk, whole, whole_t, whole_t, blk],
        out_specs=[blk, whole_t, whole_t], out_shape=[jax.ShapeDtypeStruct(q.shape, F32), t_shape, t_shape],
        scratch_shapes=[pltpu.VMEM((HEADS_PER_BLOCK, L // tk, tq, tk), F32), pltpu.VMEM((HEADS_PER_BLOCK, L // tk, tq, tk), BF16)],
        compiler_params=_cparams("parallel", "arbitrary"),
    )(q, k, kt, vt, do)


def _make_attn(name):
    @jax.custom_vjp
    def attn(q, k, v):
        return _attn_fwd(q, k.T, v, name=name + "_fwd")

    def fwd(q, k, v):
        kt = k.T
        return _attn_fwd(q, kt, v, name=name + "_fwd"), (q, k, kt, v)

    def bwd(res, do):
        q, k, kt, v = res
        dq, dkt, dvt = _attn_bwd(q, k, kt, v.T, do, name=name + "_bwd")
        return dq, dkt.T, dvt.T

    attn.defvjp(fwd, bwd)
    return attn


def _make_loss_head(n_valid, name):
    def run(h, g, target):
        L, d = h.shape
        tm = _pick(L, (384, 128))

        def body(h_ref, t_ref, g_ref, loss_ref, dh_ref, dg_ref):
            i = pl.program_id(0)
            x = h_ref[...]
            gb = g_ref[...]
            rows = lax.broadcasted_iota(jnp.int32, (tm, 1), 0) + i * tm
            valid = jnp.logical_and(rows >= N_META, rows < n_valid)
            r = lax.rsqrt(jnp.mean(x * x, axis=-1, keepdims=True) + RMS_EPS)
            xh = x * r
            err = jnp.where(valid, xh * gb - t_ref[...], 0.0)
            dy = err * (1.0 / d)
            dxh = dy * gb
            dh_ref[...] = r * (dxh - xh * jnp.mean(dxh * xh, axis=-1, keepdims=True))
            part = 0.5 / d * jnp.sum(jnp.sum(err * err, axis=1, keepdims=True), axis=0, keepdims=True)
            dgp = _colsum(dy * xh)

            @pl.when(i == 0)
            def _():
                loss_ref[...] = part
                dg_ref[...] = dgp

            @pl.when(i > 0)
            def _():
                loss_ref[...] += part
                dg_ref[...] += dgp

        rowspec = pl.BlockSpec((tm, d), lambda i: (i, 0))
        return _pallas(
            body, name=name, grid=(L // tm,),
            in_specs=[rowspec, rowspec, pl.BlockSpec((1, d), lambda i: (0, 0))],
            out_specs=[pl.BlockSpec((1, 1), lambda i: (0, 0)), rowspec, pl.BlockSpec((1, d), lambda i: (0, 0))],
            out_shape=[jax.ShapeDtypeStruct((1, 1), F32), jax.ShapeDtypeStruct((L, d), F32), jax.ShapeDtypeStruct((1, d), F32)],
            compiler_params=_cparams("arbitrary"),
        )(h, target, g.reshape(1, d))

    @jax.custom_vjp
    def loss_head(h, g, target):
        return run(h, g, target)[0][0, 0]

    def fwd(h, g, target):
        loss, dh, dg = run(h, g, target)
        return loss[0, 0], (dh, dg.reshape(g.shape), target)

    def bwd(res, ct):
        dh, dg, target = res
        return ct * dh, ct * dg, jnp.zeros_like(target)

    loss_head.defvjp(fwd, bwd)
    return loss_head


def _forward_loss(x_pad, target_pad, n_valid, small, big, proxies):
    rms = _make_rmsnorm
    Lp = x_pad.shape[0]
    h = x_pad.at[:N_META].set(small["meta_tokens"])

    u = rms("mix0")(h, small["norm_mix"][0])
    wr, wi, cr, cin, lam_r, lam_i = _s5_discretize(
        small["s5_a_re"][0], small["s5_a_im"][0], small["s5_log_dt"][0], small["s5_b_re"][0], small["s5_b_im"][0],
        small["s5_c_re"][0], small["s5_c_im"][0])
    y = _make_ssm("ssm")(u, wr, wi, cr, cin, lam_r, lam_i)
    z = _make_s5_post("s5post")(y, u, small["s5_d"][0])
    vg = _make_mm("glu_mm")(z, big["s5_w_glu"], proxies["s5_w_glu"])
    h = h + _make_glu("glu")(vg)

    def ffn(h, i):
        f = rms(f"ffn{i}")(h, small["norm_ffn"][i])
        gu = _make_mm(f"ffn_in{i}")(f, big[f"w_ffn_in{i}"], proxies[f"w_ffn_in{i}"])
        a = _make_swiglu(f"swiglu{i}")(gu)
        return h + _make_mm(f"ffn_out{i}")(a, big[f"w_ffn_out{i}"], proxies[f"w_ffn_out{i}"])

    h = ffn(h, 0)
    kvn = rms("kv")(h, small["norm_kv"])
    k = _make_mm("k_mm")(kvn, big["w_k"], proxies["w_k"])
    v = _make_mm("v_mm")(kvn, big["w_v"], proxies["w_v"])

    qn = rms("mix1")(h, small["norm_mix"][1])
    q = _make_mm("q_mm")(qn, big["w_q"], proxies["w_q"])
    o = _make_attn("attn")(q, k, v)
    h = h + _make_mm("o_mm")(o, big["w_o"], proxies["w_o"])
    h = ffn(h, 1)
    return _make_loss_head(n_valid, "loss_head")(h, small["norm_final"], target_pad)


def _coords():
    return lax.axis_index("x"), lax.axis_index("y"), lax.axis_index("c")


def _flip(bits):
    x, y, c = _coords()
    fx, fy, fc = bits
    return (x ^ fx if fx else x, y ^ fy if fy else y, c ^ fc if fc else c)


def _exchange(ins, out_shapes, copies, local_copies, *, aliases=None, name):
    n_in, n_out = len(ins), len(out_shapes)
    n_cp, n_loc = len(copies), len(local_copies)
    aliases = aliases or {}

    def body(*refs):
        in_refs, out_refs = refs[:n_in], refs[n_in:n_in + n_out]
        send_sems, recv_sems, loc_sems = refs[n_in + n_out:]
        me = _coords()
        locs = []
        for n, (ii, oi, dfn) in enumerate(local_copies):
            cp = pltpu.make_async_copy(in_refs[ii], out_refs[oi].at[dfn(*me)], loc_sems.at[n])
            cp.start()
            locs.append(cp)
        sends = []
        for n, (ii, sfn, bits, oi, dfn) in enumerate(copies):
            src = in_refs[ii] if sfn is None else in_refs[ii].at[sfn(*me)]
            cp = pltpu.make_async_remote_copy(
                src_ref=src, dst_ref=out_refs[oi].at[dfn(*me)], send_sem=send_sems.at[n], recv_sem=recv_sems.at[n],
                device_id=_flip(bits), device_id_type=MESH)
            cp.start()
            sends.append(cp)
        for n, (ii, sfn, bits, oi, dfn) in enumerate(copies):
            peer = _flip(bits)
            src = in_refs[ii] if sfn is None else in_refs[ii].at[sfn(*me)]
            pltpu.make_async_remote_copy(
                src_ref=src, dst_ref=out_refs[oi].at[dfn(*peer)], send_sem=send_sems.at[n], recv_sem=recv_sems.at[n],
                device_id=peer, device_id_type=MESH).wait_recv()
        for cp in sends:
            cp.wait_send()
        for cp in locs:
            cp.wait()

    any_spec = pl.BlockSpec(memory_space=pl.ANY)
    return _pallas(
        body, name=name, in_specs=[any_spec] * n_in, out_specs=[any_spec] * n_out,
        out_shape=[jax.ShapeDtypeStruct(s, d) for s, d in out_shapes],
        scratch_shapes=[pltpu.SemaphoreType.DMA((max(n_cp, 1),)), pltpu.SemaphoreType.DMA((max(n_cp, 1),)),
                        pltpu.SemaphoreType.DMA((max(n_loc, 1),))],
        input_output_aliases=aliases,
        compiler_params=pltpu.CompilerParams(has_side_effects=True),
    )(*ins)


ICI_FLIPS = ((1, 0, 0), (0, 1, 0), (1, 1, 0))
D2D_FLIP = (0, 0, 1)


def _slot_of(x, y, c):
    return 4 * x + 2 * y + c


def _all_gather(shards, *, name):
    n = len(shards)
    outs = [((N_DEV,) + s.shape, s.dtype) for s in shards]
    copies = [(t, None, bits, t, _slot_of) for t in range(n) for bits in ICI_FLIPS]
    local = [(t, t, _slot_of) for t in range(n)]
    bufs = _exchange(shards, outs, copies, local, name=name + "_ici")
    copies2 = [(t, (lambda x, y, c, q=q: 2 * q + c), D2D_FLIP, t, (lambda x, y, c, q=q: 2 * q + c))
               for t in range(n) for q in range(4)]
    return _exchange(bufs, outs, copies2, [], aliases={t: t for t in range(n)}, name=name + "_d2d")


def _reduce_scatter_stage1(grads, smalls, *, name):
    n = len(grads)
    outs = [((4,) + g.shape[1:], g.dtype) for g in grads] + [((N_DEV,) + s.shape, s.dtype) for s in smalls]
    copies = [(t, (lambda x, y, c, q=q: 2 * q + (1 - c)), D2D_FLIP, t, (lambda x, y, c, q=q: q))
              for t in range(n) for q in range(4)]
    copies += [(n + t, None, bits, n + t, _slot_of) for t in range(len(smalls)) for bits in ICI_FLIPS]
    local = [(n + t, n + t, _slot_of) for t in range(len(smalls))]
    res = _exchange(list(grads) + list(smalls), outs, copies, local, name=name)
    return res[:n], res[n:]


def _reduce_scatter_stage2(parts, small_bufs, *, name):
    n = len(parts)
    outs = [((3,) + p.shape[1:], p.dtype) for p in parts] + [(s.shape, s.dtype) for s in small_bufs]
    copies = []
    for t in range(n):
        for r, bits in enumerate(ICI_FLIPS):
            copies.append((t, (lambda x, y, c, b=bits: 2 * (x ^ b[0]) + (y ^ b[1])), bits, t, (lambda x, y, c, r=r: r)))
    for t in range(len(small_bufs)):
        for q in range(4):
            copies.append((n + t, (lambda x, y, c, q=q: 2 * q + c), D2D_FLIP, n + t, (lambda x, y, c, q=q: 2 * q + c)))
    res = _exchange(list(parts) + list(small_bufs), outs, copies, [],
                    aliases={n + t: n + t for t in range(len(small_bufs))}, name=name)
    return res[:n], res[n:]


def _adam_math(w, g, m, v):
    m = ADAM_B1 * m + (1.0 - ADAM_B1) * g
    v = ADAM_B2 * v + (1.0 - ADAM_B2) * (g * g)
    m_hat = m / (1.0 - ADAM_B1 ** ADAM_STEP)
    v_hat = v / (1.0 - ADAM_B2 ** ADAM_STEP)
    delta = -ADAM_LR * (m_hat / (jnp.sqrt(v_hat) + ADAM_EPS) + ADAM_WD * w)
    return delta, m, v


def _pair_sum(grads8, landing, *, name):
    _, r, cdim = grads8.shape
    tr = _pick(r, (256, 128, 64, 32, 16, 8))
    c = lax.axis_index("c").astype(jnp.int32).reshape(1)

    def body(c_ref, g_ref, l_ref, o_ref):
        o_ref[...] = (g_ref[...] + l_ref[...]).astype(o_ref.dtype)

    return _pallas(
        body, name=name,
        grid_spec=pltpu.PrefetchScalarGridSpec(
            num_scalar_prefetch=1, grid=(4, r // tr),
            in_specs=[pl.BlockSpec((None, tr, cdim), lambda q, i, c_ref: (2 * q + c_ref[0], i, 0)),
                      pl.BlockSpec((None, tr, cdim), lambda q, i, c_ref: (q, i, 0))],
            out_specs=pl.BlockSpec((None, tr, cdim), lambda q, i, c_ref: (q, i, 0))),
        out_shape=jax.ShapeDtypeStruct((4, r, cdim), BF16),
        compiler_params=_cparams("parallel", "parallel"),
    )(c, grads8, landing)


def _shard_adamw(grads8, landing, landing2, w, m, v, *, name):
    r, cdim = w.shape
    tr = _pick(r, (256, 128, 64, 32, 16))
    x, y, c = _coords()
    qc = jnp.stack([2 * x + y, c]).astype(jnp.int32)

    def body(qc_ref, g_ref, p_ref, l_ref, w_ref, m_ref, v_ref, g_out, d_out, m_out, v_out):
        g = g_ref[...] + p_ref[...]
        for n in range(3):
            g = g + l_ref[n].astype(F32)
        d, mn, vn = _adam_math(w_ref[...], g, m_ref[...], v_ref[...])
        g_out[...] = g
        d_out[...] = d
        m_out[...] = mn
        v_out[...] = vn

    blk = pl.BlockSpec((tr, cdim), lambda i, qc_ref: (i, 0))
    return _pallas(
        body, name=name,
        grid_spec=pltpu.PrefetchScalarGridSpec(
            num_scalar_prefetch=1, grid=(r // tr,),
            in_specs=[pl.BlockSpec((None, tr, cdim), lambda i, qc_ref: (2 * qc_ref[0] + qc_ref[1], i, 0)),
                      pl.BlockSpec((None, tr, cdim), lambda i, qc_ref: (qc_ref[0], i, 0)),
                      pl.BlockSpec((3, tr, cdim), lambda i, qc_ref: (0, i, 0)), blk, blk, blk],
            out_specs=[blk] * 4),
        out_shape=[jax.ShapeDtypeStruct((r, cdim), F32)] * 4,
        compiler_params=_cparams("parallel"),
    )(qc, grads8, landing, landing2, w, m, v)


def _small_adamw(slots, w, m, v, *, name):
    r, cdim = w.shape

    def body(s_ref, w_ref, m_ref, v_ref, g_out, d_out, m_out, v_out):
        g = s_ref[0]
        for n in range(1, N_DEV):
            g = g + s_ref[n]
        d, mn, vn = _adam_math(w_ref[...], g, m_ref[...], v_ref[...])
        g_out[...] = g
        d_out[...] = d
        m_out[...] = mn
        v_out[...] = vn

    tr = _pick(r, (256, 128, 64, 32, 16, 8))
    blk = pl.BlockSpec((tr, cdim), lambda i: (i, 0))
    return _pallas(
        body, name=name, grid=(r // tr,),
        in_specs=[pl.BlockSpec((N_DEV, tr, cdim), lambda i: (0, i, 0)), blk, blk, blk], out_specs=[blk] * 4,
        out_shape=[jax.ShapeDtypeStruct((r, cdim), F32)] * 4,
        compiler_params=_cparams("parallel"),
    )(slots, w, m, v)


def _plain_adamw(g, w, m, v, *, name):
    def body(g_ref, w_ref, m_ref, v_ref, d_out, m_out, v_out):
        d, mn, vn = _adam_math(w_ref[...], g_ref[...], m_ref[...], v_ref[...])
        d_out[...] = d
        m_out[...] = mn
        v_out[...] = vn

    spec = pl.BlockSpec(g.shape, lambda: (0,) * g.ndim)
    return _pallas(body, name=name, in_specs=[spec] * 4, out_specs=[spec] * 3,
                   out_shape=[jax.ShapeDtypeStruct(g.shape, F32)] * 3)(g, w, m, v)


def _cast_bf16(x, *, name):
    r, cdim = x.shape
    tr = _pick(r, (256, 128, 64, 32, 16))

    def body(x_ref, o_ref):
        o_ref[...] = x_ref[...].astype(BF16)

    return _pallas(body, name=name, grid=(r // tr,), in_specs=[pl.BlockSpec((tr, cdim), lambda i: (i, 0))],
                   out_specs=pl.BlockSpec((tr, cdim), lambda i: (i, 0)), out_shape=jax.ShapeDtypeStruct(x.shape, BF16),
                   compiler_params=_cparams("parallel"))(x)


SMALL_NAMES = ("norm_mix", "norm_ffn", "s5_a_re", "s5_a_im", "s5_log_dt", "s5_b_re", "s5_b_im", "s5_c_re", "s5_c_im",
               "norm_kv", "norm_final")


def _pack_rows(arrs):
    rows = []
    for a in arrs:
        flat = a.reshape(-1)
        pad = (-flat.shape[0]) % (8 * LANES)
        rows.append(jnp.pad(flat, (0, pad)).reshape(-1, LANES))
    return jnp.concatenate(rows, axis=0)


def _unpack_rows(packed, like):
    out, r0 = [], 0
    for a in like:
        n = math.prod(a.shape)
        nr = (n + 8 * LANES - 1) // (8 * LANES) * 8
        out.append(packed[r0:r0 + nr].reshape(-1)[:n].reshape(a.shape))
        r0 += nr
    return out


def kernel(x, meta_tokens, norm_mix, norm_ffn, s5_a_re, s5_a_im, s5_log_dt, s5_b_re, s5_b_im, s5_c_re, s5_c_im, s5_d, s5_w_glu, norm_kv, w_kv, w_q, w_o, w_ffn_in, w_ffn_out, norm_final, loss_target, m_meta_tokens, m_norm_mix, m_norm_ffn, m_s5_a_re, m_s5_a_im, m_s5_log_dt, m_s5_b_re, m_s5_b_im, m_s5_c_re, m_s5_c_im, m_s5_d, m_s5_w_glu, m_norm_kv, m_w_kv, m_w_q, m_w_o, m_w_ffn_in, m_w_ffn_out, m_norm_final, v_meta_tokens, v_norm_mix, v_norm_ffn, v_s5_a_re, v_s5_a_im, v_s5_log_dt, v_s5_b_re, v_s5_b_im, v_s5_c_re, v_s5_c_im, v_s5_d, v_s5_w_glu, v_norm_kv, v_w_kv, v_w_q, v_w_o, v_w_ffn_in, v_w_ffn_out, v_norm_final):
    args = dict(locals())
    seq = x.shape[1]
    n_valid = N_META + seq
    Lp = (n_valid + ATTN_BLOCK - 1) // ATTN_BLOCK * ATTN_BLOCK
    dm = D_MODEL
    my_slot = _slot_of(*_coords())

    big_shards = {
        "s5_w_glu": s5_w_glu[0], "w_kv": w_kv, "w_q": w_q[0], "w_o": w_o[0],
        "w_ffn_in": w_ffn_in.reshape(-1, w_ffn_in.shape[-1]), "w_ffn_out": w_ffn_out.reshape(-1, w_ffn_out.shape[-1]),
    }
    big_names = list(big_shards)
    cast = [_cast_bf16(big_shards[n], name="cast_" + n) for n in big_names]
    col_shard = jnp.concatenate([meta_tokens, s5_d, jnp.zeros((7, LANES), F32)], axis=0)
    gathered = _all_gather(cast + [col_shard], name="ag")
    gw = dict(zip(big_names, gathered[:-1]))
    gcol = gathered[-1]
    meta_full = jnp.swapaxes(gcol[:, :N_META], 0, 1).reshape(N_META, dm)
    d_full = gcol[:, N_META].reshape(1, dm)

    def cols(g8):
        return jnp.swapaxes(g8, 0, 1).reshape(g8.shape[1], -1)

    w_glu_full = cols(gw["s5_w_glu"])
    w_kv_full = cols(gw["w_kv"])
    n_ffn = w_ffn_in.shape[0]
    w_in_full = cols(gw["w_ffn_in"]).reshape(n_ffn, dm, -1)
    w_out_full = jnp.swapaxes(gw["w_ffn_out"].reshape(N_DEV, n_ffn, -1, dm), 0, 1).reshape(n_ffn, -1, dm)
    big = {
        "s5_w_glu": w_glu_full, "w_k": w_kv_full[:, :dm], "w_v": w_kv_full[:, dm:],
        "w_q": gw["w_q"].reshape(dm, dm), "w_o": gw["w_o"].reshape(dm, dm),
        "w_ffn_in0": w_in_full[0], "w_ffn_in1": w_in_full[1], "w_ffn_out0": w_out_full[0], "w_ffn_out1": w_out_full[1],
    }
    proxies = {n: jnp.zeros(w.shape, F32) for n, w in big.items()}
    small = {n: args[n] for n in SMALL_NAMES}
    small["meta_tokens"] = meta_full
    small["s5_d"] = d_full

    x_pad = jnp.zeros((Lp, dm), F32).at[N_META:n_valid].set(x[0])
    t_pad = jnp.zeros((Lp, dm), F32).at[N_META:n_valid].set(loss_target[0])
    loss_local, vjp = jax.vjp(lambda s, p, xp: _forward_loss(xp, t_pad, n_valid, s, big, p), small, proxies, x_pad)
    g_small, g_big, g_xpad = vjp(jnp.ones((), F32))
    loss = lax.psum(loss_local, ("x", "y", "c"))
    grad_x = g_xpad[N_META:n_valid][None]

    def to_cols(g, nshard=N_DEV):
        return jnp.swapaxes(g.reshape(g.shape[0], nshard, -1), 0, 1)

    g_in = jnp.stack([g_big["w_ffn_in0"], g_big["w_ffn_in1"]])
    g_out = jnp.stack([g_big["w_ffn_out0"], g_big["w_ffn_out1"]])
    grads8 = {
        "s5_w_glu": to_cols(g_big["s5_w_glu"]),
        "w_kv": to_cols(jnp.concatenate([g_big["w_k"], g_big["w_v"]], axis=1)),
        "w_q": g_big["w_q"].reshape(N_DEV, -1, dm), "w_o": g_big["w_o"].reshape(N_DEV, -1, dm),
        "w_ffn_in": jnp.transpose(g_in.reshape(n_ffn, dm, N_DEV, -1), (2, 0, 1, 3)).reshape(N_DEV, n_ffn * dm, -1),
        "w_ffn_out": jnp.swapaxes(g_out.reshape(n_ffn, N_DEV, -1, dm), 0, 1).reshape(N_DEV, -1, dm),
    }
    small_list = [g_small[n] for n in SMALL_NAMES] + [g_small["meta_tokens"], g_small["s5_d"]]
    packed = _pack_rows(small_list)

    landing, small_bufs = _reduce_scatter_stage1([grads8[n] for n in big_names], [packed], name="rs1")
    parts = [_pair_sum(grads8[n], l, name="pair_sum_" + n) for n, l in zip(big_names, landing)]
    landing2, small_bufs = _reduce_scatter_stage2(parts, small_bufs, name="rs2")

    out = {}
    for n, l1, l2 in zip(big_names, landing, landing2):
        w = big_shards[n]
        m = args["m_" + n].reshape(w.shape)
        v = args["v_" + n].reshape(w.shape)
        g, d, mn, vn = _shard_adamw(grads8[n], l1, l2, w, m, v, name="adamw_" + n)
        shape = args[n].shape
        out[n] = (g.reshape(shape), d.reshape(shape), mn.reshape(shape), vn.reshape(shape))

    zeros_tail = [jnp.zeros_like(g_small["meta_tokens"]), jnp.zeros_like(g_small["s5_d"])]
    pw = _pack_rows([args[n] for n in SMALL_NAMES] + zeros_tail)
    pm = _pack_rows([args["m_" + n] for n in SMALL_NAMES] + zeros_tail)
    pv = _pack_rows([args["v_" + n] for n in SMALL_NAMES] + zeros_tail)
    sg, sd, sm, sv = _small_adamw(small_bufs[0], pw, pm, pv, name="adamw_small")
    like = small_list
    ug, ud, um, uv = (_unpack_rows(a, like) for a in (sg, sd, sm, sv))
    for i, n in enumerate(SMALL_NAMES):
        out[n] = (ug[i], ud[i], um[i], uv[i])
    g_meta = lax.dynamic_slice_in_dim(ug[-2], my_slot * LANES, LANES, axis=1)
    g_d = lax.dynamic_slice_in_dim(ug[-1].reshape(1, dm), my_slot * LANES, LANES, axis=1)
    pad7 = jnp.zeros((7, LANES), F32)
    gc = jnp.concatenate([g_meta, g_d, pad7], axis=0)
    wc = jnp.concatenate([meta_tokens, s5_d, pad7], axis=0)
    mc = jnp.concatenate([m_meta_tokens, m_s5_d, pad7], axis=0)
    vc = jnp.concatenate([v_meta_tokens, v_s5_d, pad7], axis=0)
    dc, mcn, vcn = _plain_adamw(gc, wc, mc, vc, name="adamw_cols")
    out["meta_tokens"] = (g_meta, dc[:N_META], mcn[:N_META], vcn[:N_META])
    out["s5_d"] = (g_d, dc[N_META:N_META + 1], mcn[N_META:N_META + 1], vcn[N_META:N_META + 1])

    order = ["meta_tokens", "norm_mix", "norm_ffn", "s5_a_re", "s5_a_im", "s5_log_dt", "s5_b_re", "s5_b_im", "s5_c_re",
             "s5_c_im", "s5_d", "s5_w_glu", "norm_kv", "w_kv", "w_q", "w_o", "w_ffn_in", "w_ffn_out", "norm_final"]
    res = [loss, grad_x]
    for k in range(4):
        res += [out[n][k] for n in order]
    return tuple(res)
```

```python
import functools
import math

import jax
import jax.numpy as jnp
from jax import lax
from jax.experimental import pallas as pl
from jax.experimental.pallas import tpu as pltpu

F32 = jnp.float32
BF16 = jnp.bfloat16

N_META = 16
D_MODEL = 1024
S5_GROUPS = 64
S5_GROUP = 16
S5_STATE = 64
N_HEADS = 16
HEAD_DIM = 64
D_FF = 2816
RMS_EPS = 1e-6
ADAM_LR, ADAM_B1, ADAM_B2, ADAM_EPS, ADAM_WD, ADAM_STEP = 0.001, 0.9, 0.999, 1e-08, 0.01, 10

LANES = 128
ATTN_BLOCK = 128
GROUPS_PER_BLOCK = 8
N_DEV = 8
MESH = pl.DeviceIdType.MESH
VMEM_LIMIT = 56 * 1024 * 1024


def _pallas(body, **kw):
    return pl.pallas_call(body, **kw)


def _cparams(*sem):
    return pltpu.CompilerParams(dimension_semantics=sem, vmem_limit_bytes=VMEM_LIMIT)


def _pick(n, prefs):
    for p in prefs:
        if n % p == 0:
            return p
    return n


def _dot(a, b, ca=1, cb=0):
    return lax.dot_general(a, b, (((ca,), (cb,)), ((), ())), preferred_element_type=F32)


MATMUL_VMEM_BUDGET = 36 * 1024 * 1024
MATMUL_TILES = (1408, 1024, 512, 384, 256, 128)


def _matmul_tiles(M, N, K, a_bytes, b_bytes, tile_bytes, n_acc, offsets):
    best = None
    for tm in (t for t in MATMUL_TILES if M % t == 0):
        for tn in (t for t in MATMUL_TILES if N % t == 0):
            for tk in (t for t in MATMUL_TILES if K % t == 0):
                if any(off % (tn if along_n else tk) for off, along_n in offsets):
                    continue
                vmem = 2 * (tm * tk * a_bytes + tk * tn * b_bytes + tm * tn * tile_bytes) + n_acc * tm * tn * 4
                if vmem > MATMUL_VMEM_BUDGET:
                    continue
                traffic = M * K * a_bytes * (N // tn) + K * N * b_bytes * (M // tm) + M * N * tile_bytes
                traffic += (K // tk - 1) * n_acc * M * N * 8 // 3
                key = (traffic, -tk, -tm)
                if best is None or key < best[0]:
                    best = (key, (tm, tn, tk))
    return best[1]


def _gmm(a_list, b_list, accs, *, n, extras=(), epilogue=None, out_dtypes=(F32,), name):
    a0, ta0 = a_list[0]
    K, M = a0.shape if ta0 else a0.shape[::-1]
    N = n
    b_keys = sorted({(bi, off) for terms in accs for _, bi, off in terms})
    a_bytes = sum(a.dtype.itemsize for a, _ in a_list)
    b_bytes = sum(b_list[bi][0].dtype.itemsize for bi, _ in b_keys)
    tile_bytes = sum(e.dtype.itemsize for e in extras) + sum(jnp.dtype(d).itemsize for d in out_dtypes)
    offsets = [(off, not b_list[bi][1]) for bi, off in b_keys]
    tm, tn, tk = _matmul_tiles(M, N, K, a_bytes, b_bytes, tile_bytes, len(accs), offsets)
    nk = K // tk
    n_a, n_b, n_e, n_o, n_acc = len(a_list), len(b_keys), len(extras), len(out_dtypes), len(accs)
    if epilogue is None:
        epilogue = lambda *vals: vals

    def body(*refs):
        a_refs, b_refs = refs[:n_a], refs[n_a:n_a + n_b]
        e_refs = refs[n_a + n_b:n_a + n_b + n_e]
        o_refs = refs[n_a + n_b + n_e:n_a + n_b + n_e + n_o]
        acc_refs = refs[n_a + n_b + n_e + n_o:]
        a_vals = [r[...].astype(BF16) for r in a_refs]
        b_vals = {key: r[...].astype(BF16) for key, r in zip(b_keys, b_refs)}
        parts = []
        for terms in accs:
            p = None
            for ai, bi, off in terms:
                d = _dot(a_vals[ai], b_vals[(bi, off)], 0 if a_list[ai][1] else 1, 1 if b_list[bi][1] else 0)
                p = d if p is None else p + d
            parts.append(p)

        def finish(vals):
            outs = epilogue(*vals, *[r[...] for r in e_refs])
            for r, val in zip(o_refs, outs):
                r[...] = val.astype(r.dtype)

        if nk == 1:
            finish(parts)
        else:
            k = pl.program_id(2)

            @pl.when(k == 0)
            def _():
                for r, p in zip(acc_refs, parts):
                    r[...] = p

            @pl.when(k > 0)
            def _():
                for r, p in zip(acc_refs, parts):
                    r[...] += p

            @pl.when(k == nk - 1)
            def _():
                finish([r[...] for r in acc_refs])

    in_specs = [pl.BlockSpec((tk, tm), lambda i, j, k: (k, i)) if ta else pl.BlockSpec((tm, tk), lambda i, j, k: (i, k))
                for _, ta in a_list]
    for bi, off in b_keys:
        if b_list[bi][1]:
            in_specs.append(pl.BlockSpec((tn, tk), lambda i, j, k, o=off // tk: (j, k + o)))
        else:
            in_specs.append(pl.BlockSpec((tk, tn), lambda i, j, k, o=off // tn: (k, j + o)))
    tile = pl.BlockSpec((tm, tn), lambda i, j, k: (i, j))
    in_specs += [tile] * n_e
    return _pallas(
        body, name=name, grid=(M // tm, N // tn, nk), in_specs=in_specs, out_specs=[tile] * n_o,
        out_shape=[jax.ShapeDtypeStruct((M, N), d) for d in out_dtypes],
        scratch_shapes=[pltpu.VMEM((tm, tn), F32)] * (n_acc if nk > 1 else 0),
        compiler_params=_cparams("parallel", "parallel", "arbitrary"),
    )(*[a for a, _ in a_list], *[b_list[bi][0] for bi, _ in b_keys], *extras)


def _rowwise(fn, row_ins, full_ins, row_outs, acc_outs, *, tm, name):
    L = row_ins[0].shape[0]
    n_row, n_full, n_ro = len(row_ins), len(full_ins), len(row_outs)

    def body(*refs):
        ins = [r[...] for r in refs[:n_row + n_full]]
        outs = refs[n_row + n_full:]
        res = fn(*ins)
        for r, val in zip(outs[:n_ro], res[:n_ro]):
            r[...] = val.astype(r.dtype)
        if acc_outs:
            first = pl.program_id(0) == 0
            for r, val in zip(outs[n_ro:], res[n_ro:]):
                @pl.when(first)
                def _(r=r, val=val):
                    r[...] = val

                @pl.when(jnp.logical_not(first))
                def _(r=r, val=val):
                    r[...] += val

    in_specs = [pl.BlockSpec((tm, a.shape[1]), lambda i: (i, 0)) for a in row_ins]
    in_specs += [pl.BlockSpec(a.shape, lambda i, nd=a.ndim: (0,) * nd) for a in full_ins]
    out_specs = [pl.BlockSpec((tm, s[1]), lambda i: (i, 0)) for s, _ in row_outs]
    out_specs += [pl.BlockSpec(s, lambda i, nd=len(s): (0,) * nd) for s, _ in acc_outs]
    out_shape = [jax.ShapeDtypeStruct(s, d) for s, d in list(row_outs) + list(acc_outs)]
    return _pallas(
        body, name=name, grid=(L // tm,), in_specs=in_specs, out_specs=out_specs, out_shape=out_shape,
        compiler_params=_cparams("arbitrary" if acc_outs else "parallel"),
    )(*row_ins, *full_ins)


def _colsum(x):
    return jnp.sum(x, axis=0, keepdims=True)


def _rmsnorm_fwd(x, g, *, out_dtype, name):
    L, d = x.shape

    def f(xb, gb):
        r = lax.rsqrt(jnp.mean(xb * xb, axis=-1, keepdims=True) + RMS_EPS)
        return (xb * r * gb,)

    return _rowwise(f, [x], [g.reshape(1, d)], [((L, d), out_dtype)], [], tm=_pick(L, (384, 128)), name=name)[0]


def _rmsnorm_bwd(x, g, dy, resid, *, name):
    L, d = x.shape

    def f(xb, dyb, *rest):
        gb = rest[-1]
        r = lax.rsqrt(jnp.mean(xb * xb, axis=-1, keepdims=True) + RMS_EPS)
        xh = xb * r
        dxh = dyb * gb
        dx = r * (dxh - xh * jnp.mean(dxh * xh, axis=-1, keepdims=True))
        if resid is not None:
            dx = dx + rest[0]
        return dx, _colsum(dyb * xh)

    rows = [x, dy] + ([resid] if resid is not None else [])
    dx, dg = _rowwise(f, rows, [g.reshape(1, d)], [((L, d), F32)], [((1, d), F32)], tm=_pick(L, (384, 128)), name=name)
    return dx, dg.reshape(g.shape)


def _make_rmsnorm(name):
    @jax.custom_vjp
    def rmsnorm(x, g):
        return _rmsnorm_fwd(x, g, out_dtype=F32, name=name + "_fwd")

    def fwd(x, g):
        return _rmsnorm_fwd(x, g, out_dtype=F32, name=name + "_fwd"), (x, g)

    def bwd(res, dy):
        x, g = res
        return _rmsnorm_bwd(x, g, dy, None, name=name + "_bwd")

    rmsnorm.defvjp(fwd, bwd)
    return rmsnorm


def _sigmoid(x):
    return 1.0 / (1.0 + jnp.exp(-x))


GELU_K = math.sqrt(2.0 / math.pi)
GELU_C = 0.044715


def _bdmm(a_list, w_rows, *, out_dtype=F32, name):
    L = a_list[0].shape[0]
    nb, ka, kb = w_rows[0][0].shape
    tm = _pick(L, (1408, 384, 128))
    n_a, n_o = len(a_list), len(w_rows)
    ws = [w for row in w_rows for w in row]

    def body(*refs):
        a_vals = [r[...].astype(BF16) for r in refs[:n_a]]
        w_refs = refs[n_a:n_a + n_a * n_o]
        outs = refs[n_a + n_a * n_o:]
        for o in range(n_o):
            acc = None
            for i in range(n_a):
                p = _dot(a_vals[i], w_refs[o * n_a + i][...].astype(BF16))
                acc = p if acc is None else acc + p
            outs[o][...] = acc.astype(out_dtype)

    return _pallas(
        body, name=name, grid=(L // tm, nb),
        in_specs=[pl.BlockSpec((tm, ka), lambda i, j: (i, j)) for _ in a_list]
        + [pl.BlockSpec((None, ka, kb), lambda i, j: (j, 0, 0)) for _ in ws],
        out_specs=[pl.BlockSpec((tm, kb), lambda i, j: (i, j)) for _ in range(n_o)],
        out_shape=[jax.ShapeDtypeStruct((L, nb * kb), out_dtype) for _ in range(n_o)],
        compiler_params=_cparams("parallel", "parallel"),
    )(*a_list, *ws)


def _bdmm_tn(a, g, nb, *, name):
    L = a.shape[0]
    ka, kb = a.shape[1] // nb, g.shape[1] // nb
    tm = _pick(L, (1408, 384, 128))

    def body(a_ref, g_ref, o_ref):
        i = pl.program_id(1)
        part = _dot(a_ref[...].astype(BF16), g_ref[...].astype(BF16), 0, 0)

        @pl.when(i == 0)
        def _():
            o_ref[...] = part

        @pl.when(i > 0)
        def _():
            o_ref[...] += part

    return _pallas(
        body, name=name, grid=(nb, L // tm),
        in_specs=[pl.BlockSpec((tm, ka), lambda j, i: (i, j)), pl.BlockSpec((tm, kb), lambda j, i: (i, j))],
        out_specs=pl.BlockSpec((None, ka, kb), lambda j, i: (j, 0, 0)),
        out_shape=jax.ShapeDtypeStruct((nb, ka, kb), F32),
        compiler_params=_cparams("parallel", "arbitrary"),
    )(a, g)


SCAN_ROWS = 128
STATE_DTYPE = BF16


def _scan_fwd(br, bi, ar, ai, *, name):
    L, S, _ = br.shape
    tb = SCAN_ROWS

    def body(br_ref, bi_ref, ar_ref, ai_ref, xr_ref, xi_ref, carry_ref):
        @pl.when(pl.program_id(0) == 0)
        def _():
            carry_ref[...] = jnp.zeros_like(carry_ref)

        a_r, a_i = ar_ref[...], ai_ref[...]

        def step(t, c):
            xr, xi = c
            nr = a_r * xr - a_i * xi + br_ref[t].astype(F32)
            ni = a_r * xi + a_i * xr + bi_ref[t].astype(F32)
            xr_ref[t] = nr.astype(xr_ref.dtype)
            xi_ref[t] = ni.astype(xi_ref.dtype)
            return nr, ni

        xr, xi = lax.fori_loop(0, tb, step, (carry_ref[0], carry_ref[1]), unroll=8)
        carry_ref[0] = xr
        carry_ref[1] = xi

    blk = pl.BlockSpec((tb, S, LANES), lambda i: (i, 0, 0))
    par = pl.BlockSpec((S, LANES), lambda i: (0, 0))
    return _pallas(
        body, name=name, grid=(L // tb,), in_specs=[blk, blk, par, par], out_specs=[blk, blk],
        out_shape=[jax.ShapeDtypeStruct(br.shape, STATE_DTYPE)] * 2,
        scratch_shapes=[pltpu.VMEM((2, S, LANES), F32)],
        compiler_params=_cparams("arbitrary"),
    )(br, bi, ar, ai)


def _scan_bwd(gr_in, gi_in, ar, ai, xr, xi, *, name):
    L, S, _ = gr_in.shape
    tb = SCAN_ROWS
    nblk = L // tb

    def body(gr_ref, gi_ref, ar_ref, ai_ref, xr_ref, xi_ref, or_ref, oi_ref, dar_ref, dai_ref, carry_ref):
        @pl.when(pl.program_id(0) == 0)
        def _():
            carry_ref[...] = jnp.zeros_like(carry_ref)
            dar_ref[...] = jnp.zeros_like(dar_ref)
            dai_ref[...] = jnp.zeros_like(dai_ref)

        a_r, a_i = ar_ref[...], ai_ref[...]

        def step(s, c):
            gr, gi, dr, di = c
            t = tb - 1 - s
            x_r, x_i = xr_ref[t].astype(F32), xi_ref[t].astype(F32)
            dr = dr + gr * x_r + gi * x_i
            di = di + gi * x_r - gr * x_i
            nr = a_r * gr + a_i * gi + gr_ref[t].astype(F32)
            ni = a_r * gi - a_i * gr + gi_ref[t].astype(F32)
            or_ref[t] = nr.astype(or_ref.dtype)
            oi_ref[t] = ni.astype(oi_ref.dtype)
            return nr, ni, dr, di

        gr, gi, dr, di = lax.fori_loop(0, tb, step, (carry_ref[0], carry_ref[1], dar_ref[...], dai_ref[...]), unroll=8)
        carry_ref[0] = gr
        carry_ref[1] = gi
        dar_ref[...] = dr
        dai_ref[...] = di

    blk = pl.BlockSpec((tb, S, LANES), lambda i: (nblk - 1 - i, 0, 0))
    par = pl.BlockSpec((S, LANES), lambda i: (0, 0))
    return _pallas(
        body, name=name, grid=(nblk,), in_specs=[blk, blk, par, par, blk, blk], out_specs=[blk, blk, par, par],
        out_shape=[jax.ShapeDtypeStruct(gr_in.shape, STATE_DTYPE)] * 2 + [jax.ShapeDtypeStruct((S, LANES), F32)] * 2,
        scratch_shapes=[pltpu.VMEM((2, S, LANES), F32)],
        compiler_params=_cparams("arbitrary"),
    )(gr_in, gi_in, ar, ai, xr, xi)


def _make_ssm(name):
    nb = GROUPS_PER_BLOCK

    @jax.custom_vjp
    def ssm(u, wr, wi, cr, cin, lam_r, lam_i):
        return fwd(u, wr, wi, cr, cin, lam_r, lam_i)[0]

    def fwd(u, wr, wi, cr, cin, lam_r, lam_i):
        L = u.shape[0]
        bur, bui = _bdmm([u], [[wr], [wi]], out_dtype=STATE_DTYPE, name=name + "_bu")
        S = bur.shape[1] // LANES
        xr, xi = _scan_fwd(bur.reshape(L, S, LANES), bui.reshape(L, S, LANES), lam_r, lam_i, name=name + "_scan")
        xr, xi = xr.reshape(L, S * LANES), xi.reshape(L, S * LANES)
        (y,) = _bdmm([xr, xi], [[cr, cin]], name=name + "_cx")
        return y, (u, wr, wi, cr, cin, lam_r, lam_i, xr, xi)

    def bwd(res, dy):
        u, wr, wi, cr, cin, lam_r, lam_i, xr, xi = res
        L = u.shape[0]
        S = xr.shape[1] // LANES
        tr = lambda w: jnp.swapaxes(w, 1, 2)
        gin_r, gin_i = _bdmm([dy], [[tr(cr)], [tr(cin)]], out_dtype=STATE_DTYPE, name=name + "_gin")
        gr, gi, dlr, dli = _scan_bwd(gin_r.reshape(L, S, LANES), gin_i.reshape(L, S, LANES), lam_r, lam_i,
                                     xr.reshape(L, S, LANES), xi.reshape(L, S, LANES), name=name + "_rscan")
        gr, gi = gr.reshape(L, S * LANES), gi.reshape(L, S * LANES)
        (du,) = _bdmm([gr, gi], [[tr(wr), tr(wi)]], name=name + "_du")
        dwr = _bdmm_tn(u, gr, nb, name=name + "_dwr")
        dwi = _bdmm_tn(u, gi, nb, name=name + "_dwi")
        dcr = _bdmm_tn(xr, dy, nb, name=name + "_dcr")
        dcin = _bdmm_tn(xi, dy, nb, name=name + "_dci")
        return du, dwr, dwi, dcr, dcin, dlr, dli

    ssm.defvjp(fwd, bwd)
    return ssm


def _s5_discretize(a_re, a_im, log_dt, b_re, b_im, c_re, c_im):
    G, P, C, nb = S5_GROUPS, S5_STATE, S5_GROUP, GROUPS_PER_BLOCK
    dt = jnp.exp(log_dt)[:, None]
    mag = jnp.exp(dt * a_re)
    ang = dt * a_im
    abar_re = mag * jnp.cos(ang)
    abar_im = mag * jnp.sin(ang)
    den = a_re * a_re + a_im * a_im
    coef_re = ((abar_re - 1.0) * a_re + abar_im * a_im) / den
    coef_im = (abar_im * a_re - (abar_re - 1.0) * a_im) / den
    bbar_re = coef_re[..., None] * b_re - coef_im[..., None] * b_im
    bbar_im = coef_re[..., None] * b_im + coef_im[..., None] * b_re
    eye = jnp.eye(nb, dtype=F32)

    def blocks_in(bb):
        return jnp.einsum("jgpc,gh->jgchp", bb.reshape(G // nb, nb, P, C), eye).reshape(G // nb, nb * C, nb * P)

    def blocks_out(cc):
        return jnp.einsum("jgcp,gh->jgphc", cc.reshape(G // nb, nb, C, P), eye).reshape(G // nb, nb * P, nb * C)

    lam_r = abar_re.reshape(G * P // LANES, LANES)
    lam_i = abar_im.reshape(G * P // LANES, LANES)
    return blocks_in(bbar_re), blocks_in(bbar_im), blocks_out(c_re), blocks_out(-c_im), lam_r, lam_i


ATTN_Q_ROWS = 384


HEADS_PER_BLOCK = LANES // HEAD_DIM


def _head_masks(shape, axis):
    idx = lax.broadcasted_iota(jnp.int32, shape, axis) // HEAD_DIM
    return [idx == hh for hh in range(HEADS_PER_BLOCK)]


def _masked_bf16(x, masks):
    return [jnp.where(m, x, 0.0).astype(BF16) for m in masks]


def _tri_sum(x, tri2):
    hi = x.astype(BF16)
    lo = (x - hi.astype(F32)).astype(BF16)
    return _dot(jnp.concatenate([hi, lo], axis=1), tri2)


def _attn_weights(qh, kt, strict, after, c):
    z = _dot(qh, kt)
    lb = jnp.minimum(z, 0.0) - jnp.log(1.0 + jnp.exp(-jnp.abs(z)))
    l1m = lb - z
    if strict is not None:
        l1m = jnp.where(strict, l1m, 0.0)
    rem = _tri_sum(l1m, after)
    w = jnp.exp(lb + rem + c)
    if strict is not None:
        w = jnp.where(strict, w, 0.0)
    return lb, w, rem[:, 0:1] + l1m[:, 0:1]


def _tri(cmp):
    tk = ATTN_BLOCK
    tri = cmp(lax.broadcasted_iota(jnp.int32, (tk, tk), 0), lax.broadcasted_iota(jnp.int32, (tk, tk), 1)).astype(BF16)
    return jnp.concatenate([tri, tri], axis=0)


def _attn_fwd(q, kt, v, *, name):
    L = q.shape[0]
    tk = ATTN_BLOCK
    tq = _pick(L, (ATTN_Q_ROWS, tk))
    nsub = tq // tk
    scale = 1.0 / math.sqrt(HEAD_DIM)

    def body(q_ref, kt_ref, v_ref, o_ref, acc_ref):
        i = pl.program_id(1)
        after = _tri(lambda r, c: r > c)
        rowq = lax.broadcasted_iota(jnp.int32, (tq, tk), 0)
        colq = lax.broadcasted_iota(jnp.int32, (tq, tk), 1)
        qs = _masked_bf16(q_ref[...].astype(F32) * scale, _head_masks((tq, LANES), 1))
        vmasks = _head_masks((tk, LANES), 1)
        acc_ref[...] = jnp.zeros_like(acc_ref)

        def tile(r0, strict, cs):
            kt_j = kt_ref[:, pl.ds(r0, tk)].astype(BF16)
            vs = _masked_bf16(v_ref[pl.ds(r0, tk), :], vmasks)
            out, acc = [], None
            for qh, vh, c in zip(qs, vs, cs):
                _, w, tot = _attn_weights(qh, kt_j, strict, after, c)
                part = _dot(w.astype(BF16), vh)
                acc = part if acc is None else acc + part
                out.append(c + tot)
            acc_ref[...] += acc
            return tuple(out)

        cs = tuple(jnp.zeros((tq, 1), F32) for _ in qs)
        for jd in reversed(range(nsub)):
            cs = tile(pl.multiple_of(i * tq + jd * tk, tk), (colq + jd * tk) < rowq, cs)

        def step(ii, cs):
            for jd in reversed(range(nsub)):
                cs = tile(pl.multiple_of((i - 1 - ii) * tq + jd * tk, tk), None, cs)
            return cs

        lax.fori_loop(0, i, step, cs)
        o_ref[...] = acc_ref[...].astype(o_ref.dtype)

    blk = pl.BlockSpec((tq, LANES), lambda h, i: (i, h))
    whole = pl.BlockSpec((L, LANES), lambda h, i: (0, h))
    whole_t = pl.BlockSpec((LANES, L), lambda h, i: (h, 0))
    return _pallas(
        body, name=name, grid=(N_HEADS // HEADS_PER_BLOCK, L // tq), in_specs=[blk, whole_t, whole], out_specs=blk,
        out_shape=jax.ShapeDtypeStruct(q.shape, BF16),
        scratch_shapes=[pltpu.VMEM((tq, LANES), F32)],
        compiler_params=_cparams("parallel", "arbitrary"),
    )(q, kt, v)


def _attn_bwd(q, k, kt, vt, do, *, name):
    L = q.shape[0]
    tk = ATTN_BLOCK
    tq = _pick(L, (ATTN_Q_ROWS, tk))
    nsub = tq // tk
    scale = 1.0 / math.sqrt(HEAD_DIM)

    def body(q_ref, k_ref, kt_ref, vt_ref, do_ref, dq_ref, dkt_ref, dvt_ref, e_scr, sig_scr, dq_acc_ref):
        i = pl.program_id(1)

        @pl.when(i == 0)
        def _():
            dkt_ref[...] = jnp.zeros_like(dkt_ref)
            dvt_ref[...] = jnp.zeros_like(dvt_ref)

        after = _tri(lambda r, c: r > c)
        before = _tri(lambda r, c: r < c)
        rowq = lax.broadcasted_iota(jnp.int32, (tq, tk), 0)
        colq = lax.broadcasted_iota(jnp.int32, (tq, tk), 1)
        qmasks = _head_masks((tq, LANES), 1)
        tmasks = _head_masks((LANES, tq), 0)
        kmasks = _head_masks((tk, LANES), 1)
        q_scaled = q_ref[...].astype(F32) * scale
        do32 = do_ref[...].astype(F32)
        qs = _masked_bf16(q_scaled, qmasks)
        dos = _masked_bf16(do32, qmasks)
        qts = _masked_bf16(q_scaled.T, tmasks)
        dots = _masked_bf16(do32.T, tmasks)
        dq_acc_ref[...] = jnp.zeros_like(dq_acc_ref)

        def tile_left(jb, strict, cs):
            r0 = pl.multiple_of(jb * tk, tk)
            kt_j = kt_ref[:, pl.ds(r0, tk)].astype(BF16)
            vt_j = vt_ref[:, pl.ds(r0, tk)].astype(BF16)
            out, acc = [], None
            for hh, (qh, dob, dot_h, c) in enumerate(zip(qs, dos, dots, cs)):
                lb, w, tot = _attn_weights(qh, kt_j, strict, after, c)
                e_scr[hh, jb] = _dot(dob, vt_j) * w
                sig_scr[hh, jb] = jnp.exp(lb).astype(BF16)
                part = _dot(dot_h, w.astype(BF16))
                acc = part if acc is None else acc + part
                out.append(c + tot)
            dvt_ref[:, pl.ds(r0, tk)] += acc
            return tuple(out)

        def tile_right(jb, strict, cs):
            r0 = pl.multiple_of(jb * tk, tk)
            ks = _masked_bf16(k_ref[pl.ds(r0, tk), :], kmasks)
            out, dq_acc, dk_acc = [], None, None
            for hh, (kh, qt_h, c2) in enumerate(zip(ks, qts, cs)):
                e = e_scr[hh, jb]
                sig = sig_scr[hh, jb].astype(F32)
                left = _tri_sum(e, before)
                dz = e - (e + left + c2) * sig
                if strict is not None:
                    dz = jnp.where(strict, dz, 0.0)
                dz = dz.astype(BF16)
                p1, p2 = _dot(dz, kh), _dot(qt_h, dz)
                dq_acc = p1 if dq_acc is None else dq_acc + p1
                dk_acc = p2 if dk_acc is None else dk_acc + p2
                out.append(c2 + left[:, tk - 1:tk] + e[:, tk - 1:tk])
            dq_acc_ref[...] += dq_acc
            dkt_ref[:, pl.ds(r0, tk)] += dk_acc
            return tuple(out)

        zeros = tuple(jnp.zeros((tq, 1), F32) for _ in qs)
        cs = zeros
        for jd in reversed(range(nsub)):
            cs = tile_left(i * nsub + jd, (colq + jd * tk) < rowq, cs)

        def left_step(ii, cs):
            for jd in reversed(range(nsub)):
                cs = tile_left((i - 1 - ii) * nsub + jd, None, cs)
            return cs

        def right_step(ii, cs):
            for jd in range(nsub):
                cs = tile_right(ii * nsub + jd, None, cs)
            return cs

        lax.fori_loop(0, i, left_step, cs)
        cs = lax.fori_loop(0, i, right_step, zeros)
        for jd in range(nsub):
            cs = tile_right(i * nsub + jd, (colq + jd * tk) < rowq, cs)
        dq_ref[...] = (dq_acc_ref[...] * scale).astype(dq_ref.dtype)

    nhb = N_HEADS // HEADS_PER_BLOCK
    blk = pl.BlockSpec((tq, LANES), lambda h, i: (i, h))
    whole = pl.BlockSpec((L, LANES), lambda h, i: (0, h))
    whole_t = pl.BlockSpec((LANES, L), lambda h, i: (h, 0))
    t_shape = jax.ShapeDtypeStruct((q.shape[1], L), F32)
    return _pallas(
        body, name=name, grid=(nhb, L // tq), in_specs=[blk, whole, whole_t, whole_t, blk],
        out_specs=[blk, whole_t, whole_t], out_shape=[jax.ShapeDtypeStruct(q.shape, BF16), t_shape, t_shape],
        scratch_shapes=[pltpu.VMEM((HEADS_PER_BLOCK, L // tk, tq, tk), F32), pltpu.VMEM((HEADS_PER_BLOCK, L // tk, tq, tk), BF16),
                        pltpu.VMEM((tq, LANES), F32)],
        compiler_params=_cparams("parallel", "arbitrary"),
    )(q, k, kt, vt, do)


def _make_loss_head(n_valid, name):
    def run(h, g, target):
        L, d = h.shape
        tm = _pick(L, (384, 128))

        def body(h_ref, t_ref, g_ref, loss_ref, dh_ref, dg_ref):
            i = pl.program_id(0)
            x = h_ref[...]
            gb = g_ref[...]
            rows = lax.broadcasted_iota(jnp.int32, (tm, 1), 0) + i * tm
            valid = jnp.logical_and(rows >= N_META, rows < n_valid)
            r = lax.rsqrt(jnp.mean(x * x, axis=-1, keepdims=True) + RMS_EPS)
            xh = x * r
            err = jnp.where(valid, xh * gb - t_ref[...], 0.0)
            dy = err * (1.0 / d)
            dxh = dy * gb
            dh_ref[...] = r * (dxh - xh * jnp.mean(dxh * xh, axis=-1, keepdims=True))
            part = 0.5 / d * jnp.sum(jnp.sum(err * err, axis=1, keepdims=True), axis=0, keepdims=True)
            dgp = _colsum(dy * xh)

            @pl.when(i == 0)
            def _():
                loss_ref[...] = part
                dg_ref[...] = dgp

            @pl.when(i > 0)
            def _():
                loss_ref[...] += part
                dg_ref[...] += dgp

        rowspec = pl.BlockSpec((tm, d), lambda i: (i, 0))
        return _pallas(
            body, name=name, grid=(L // tm,),
            in_specs=[rowspec, rowspec, pl.BlockSpec((1, d), lambda i: (0, 0))],
            out_specs=[pl.BlockSpec((1, 1), lambda i: (0, 0)), rowspec, pl.BlockSpec((1, d), lambda i: (0, 0))],
            out_shape=[jax.ShapeDtypeStruct((1, 1), F32), jax.ShapeDtypeStruct((L, d), F32), jax.ShapeDtypeStruct((1, d), F32)],
            compiler_params=_cparams("arbitrary"),
        )(h, target, g.reshape(1, d))

    @jax.custom_vjp
    def loss_head(h, g, target):
        return run(h, g, target)[0][0, 0]

    def fwd(h, g, target):
        loss, dh, dg = run(h, g, target)
        return loss[0, 0], (dh, dg.reshape(g.shape), target)

    def bwd(res, ct):
        dh, dg, target = res
        return ct * dh, ct * dg, jnp.zeros_like(target)

    loss_head.defvjp(fwd, bwd)
    return loss_head


def _mm(a, b, *, ta=False, tb=False, out_dtype=F32, extras=(), epilogue=None, name):
    n = b.shape[0] if tb else b.shape[1]
    return _gmm([(a, ta)], [(b, tb)], [[(0, 0, 0)]], n=n, extras=extras, epilogue=epilogue, out_dtypes=(out_dtype,), name=name)[0]


def _add_epilogue(acc, resid):
    return (acc + resid,)


def _make_ffn_block(name):
    @jax.custom_vjp
    def ffn(h, gain, w_in, w_out, p_in, p_out):
        return fwd(h, gain, w_in, w_out, p_in, p_out)[0]

    def fwd(h, gain, w_in, w_out, p_in, p_out):
        dff = w_out.shape[0]
        f = _rmsnorm_fwd(h, gain, out_dtype=BF16, name=name + "_norm")

        def swiglu(g, u):
            return g, u, g * _sigmoid(g) * u

        g, u, a = _gmm([(f, False)], [(w_in, False)], [[(0, 0, 0)], [(0, 0, dff)]], n=dff, epilogue=swiglu,
                       out_dtypes=(BF16, BF16, BF16), name=name + "_in")
        h2 = _mm(a, w_out, extras=[h], epilogue=_add_epilogue, name=name + "_out")
        return h2, (h, gain, f, g, u, a, w_in, w_out)

    def bwd(res, dh2):
        h, gain, f, g, u, a, w_in, w_out = res
        dff, dm = w_out.shape

        def dswiglu(da, gb, ub):
            gb, ub = gb.astype(F32), ub.astype(F32)
            s = _sigmoid(gb)
            return da * ub * (s + gb * s * (1.0 - s)), da * gb * s

        dg, du = _gmm([(dh2, False)], [(w_out, True)], [[(0, 0, 0)]], n=dff, extras=[g, u], epilogue=dswiglu,
                      out_dtypes=(BF16, BF16), name=name + "_da")
        dw_out = _mm(a, dh2, ta=True, name=name + "_dwout")
        (df,) = _gmm([(dg, False), (du, False)], [(w_in, True)], [[(0, 0, 0), (1, 0, dff)]], n=dm, name=name + "_df")
        dh, dgain = _rmsnorm_bwd(h, gain, df, dh2, name=name + "_dnorm")
        dw_in = jnp.concatenate([_mm(f, dg, ta=True, name=name + "_dwg"), _mm(f, du, ta=True, name=name + "_dwu")], axis=1)
        return dh, dgain, jnp.zeros_like(w_in), jnp.zeros_like(w_out), dw_in, dw_out

    ffn.defvjp(fwd, bwd)
    return ffn


def _make_s5_out_block(name):
    def post(yb, ub, db):
        s = yb + db * ub
        return (0.5 * s * (1.0 + jnp.tanh(GELU_K * (s + GELU_C * s * s * s))),)

    @jax.custom_vjp
    def block(y, u, d, h, w_glu, p_glu):
        return fwd(y, u, d, h, w_glu, p_glu)[0]

    def fwd(y, u, d, h, w_glu, p_glu):
        L, dm = y.shape
        (z,) = _rowwise(post, [y, u], [d.reshape(1, dm)], [((L, dm), BF16)], [], tm=_pick(L, (384, 128)), name=name + "_gelu")

        def glu(val, gate, resid):
            return resid + val * _sigmoid(gate), val, gate

        h2, val, gate = _gmm([(z, False)], [(w_glu, False)], [[(0, 0, 0)], [(0, 0, dm)]], n=dm, extras=[h], epilogue=glu,
                             out_dtypes=(F32, BF16, BF16), name=name + "_glu")
        return h2, (y, u, d, z, val, gate, w_glu)

    def bwd(res, dh2):
        y, u, d, z, val, gate, w_glu = res
        L, dm = y.shape

        def dglu(dh2b, valb, gateb):
            s = _sigmoid(gateb.astype(F32))
            return dh2b * s, dh2b * valb.astype(F32) * s * (1.0 - s)

        dval, dgate = _rowwise(dglu, [dh2, val, gate], [], [((L, dm), BF16)] * 2, [], tm=_pick(L, (384, 128)), name=name + "_dglu")
        (dz,) = _gmm([(dval, False), (dgate, False)], [(w_glu, True)], [[(0, 0, 0), (1, 0, dm)]], n=dm, name=name + "_dz")

        def dpost(yb, ub, dzb, db):
            s = yb + db * ub
            t = jnp.tanh(GELU_K * (s + GELU_C * s * s * s))
            ds = dzb * (0.5 * (1.0 + t) + 0.5 * s * (1.0 - t * t) * GELU_K * (1.0 + 3.0 * GELU_C * s * s))
            return ds, ds * db, _colsum(ds * ub)

        dy, du, dd = _rowwise(dpost, [y, u, dz], [d.reshape(1, dm)], [((L, dm), F32)] * 2, [((1, dm), F32)],
                              tm=_pick(L, (384, 128)), name=name + "_dgelu")
        dw = jnp.concatenate([_mm(z, dval, ta=True, name=name + "_dwv"), _mm(z, dgate, ta=True, name=name + "_dwg")], axis=1)
        return dy, du, dd.reshape(d.shape), dh2, jnp.zeros_like(w_glu), dw

    block.defvjp(fwd, bwd)
    return block


def _make_attn_layer(name):
    @jax.custom_vjp
    def layer(h, g_kv, g_q, w_k, w_v, w_q, w_o, p_k, p_v, p_q, p_o):
        return fwd(h, g_kv, g_q, w_k, w_v, w_q, w_o, p_k, p_v, p_q, p_o)[0]

    def fwd(h, g_kv, g_q, w_k, w_v, w_q, w_o, p_k, p_v, p_q, p_o):
        dm = h.shape[1]
        kvn = _rmsnorm_fwd(h, g_kv, out_dtype=BF16, name=name + "_kvnorm")
        qn = _rmsnorm_fwd(h, g_q, out_dtype=BF16, name=name + "_qnorm")
        k, v = _gmm([(kvn, False)], [(w_k, False), (w_v, False)], [[(0, 0, 0)], [(0, 1, 0)]], n=dm,
                    out_dtypes=(BF16, BF16), name=name + "_kv")
        q = _mm(qn, w_q, out_dtype=BF16, name=name + "_q")
        kt, vt = k.T, v.T
        o = _attn_fwd(q, kt, v, name=name + "_fwd")
        h2 = _mm(o, w_o, extras=[h], epilogue=_add_epilogue, name=name + "_o")
        return h2, (h, g_kv, g_q, kvn, qn, q, k, kt, vt, o, w_k, w_v, w_q, w_o)

    def bwd(res, dh2):
        h, g_kv, g_q, kvn, qn, q, k, kt, vt, o, w_k, w_v, w_q, w_o = res
        dm = h.shape[1]
        do = _mm(dh2, w_o, tb=True, out_dtype=BF16, name=name + "_do")
        dw_o = _mm(o, dh2, ta=True, name=name + "_dwo")
        dq, dkt, dvt = _attn_bwd(q, k, kt, vt, do, name=name + "_bwd")
        dqn = _mm(dq, w_q, tb=True, name=name + "_dqn")
        dw_q = _mm(qn, dq, ta=True, name=name + "_dwq")
        (dkvn,) = _gmm([(dkt, True), (dvt, True)], [(w_k, True), (w_v, True)], [[(0, 0, 0), (1, 1, 0)]], n=dm, name=name + "_dkvn")
        dw_k = _mm(kvn, dkt, ta=True, tb=True, name=name + "_dwk")
        dw_v = _mm(kvn, dvt, ta=True, tb=True, name=name + "_dwv")
        dh, dg_q = _rmsnorm_bwd(h, g_q, dqn, dh2, name=name + "_dqnorm")
        dh, dg_kv = _rmsnorm_bwd(h, g_kv, dkvn, dh, name=name + "_dkvnorm")
        zeros = [jnp.zeros_like(w) for w in (w_k, w_v, w_q, w_o)]
        return (dh, dg_kv, dg_q, *zeros, dw_k, dw_v, dw_q, dw_o)

    layer.defvjp(fwd, bwd)
    return layer


def _forward_loss(x_pad, target_pad, n_valid, small, big, proxies):
    h = x_pad.at[:N_META].set(small["meta_tokens"])

    u = _make_rmsnorm("mix0")(h, small["norm_mix"][0])
    wr, wi, cr, cin, lam_r, lam_i = _s5_discretize(
        small["s5_a_re"][0], small["s5_a_im"][0], small["s5_log_dt"][0], small["s5_b_re"][0], small["s5_b_im"][0],
        small["s5_c_re"][0], small["s5_c_im"][0])
    y = _make_ssm("ssm")(u, wr, wi, cr, cin, lam_r, lam_i)
    h = _make_s5_out_block("s5out")(y, u, small["s5_d"][0], h, big["s5_w_glu"], proxies["s5_w_glu"])
    h = _make_ffn_block("ffn0")(h, small["norm_ffn"][0], big["w_ffn_in0"], big["w_ffn_out0"],
                                proxies["w_ffn_in0"], proxies["w_ffn_out0"])

    h = _make_attn_layer("attn")(h, small["norm_kv"], small["norm_mix"][1], big["w_k"], big["w_v"], big["w_q"], big["w_o"],
                                 proxies["w_k"], proxies["w_v"], proxies["w_q"], proxies["w_o"])
    h = _make_ffn_block("ffn1")(h, small["norm_ffn"][1], big["w_ffn_in1"], big["w_ffn_out1"],
                                proxies["w_ffn_in1"], proxies["w_ffn_out1"])
    return _make_loss_head(n_valid, "loss_head")(h, small["norm_final"], target_pad)


def _coords():
    return lax.axis_index("x"), lax.axis_index("y"), lax.axis_index("c")


def _flip(bits):
    x, y, c = _coords()
    fx, fy, fc = bits
    return (x ^ fx if fx else x, y ^ fy if fy else y, c ^ fc if fc else c)


def _exchange(ins, out_shapes, copies, local_copies, *, aliases=None, name):
    n_in, n_out = len(ins), len(out_shapes)
    n_cp, n_loc = len(copies), len(local_copies)
    aliases = aliases or {}

    def body(*refs):
        in_refs, out_refs = refs[:n_in], refs[n_in:n_in + n_out]
        send_sems, recv_sems, loc_sems = refs[n_in + n_out:]
        me = _coords()
        locs = []
        for n, (ii, oi, dfn) in enumerate(local_copies):
            cp = pltpu.make_async_copy(in_refs[ii], out_refs[oi].at[dfn(*me)], loc_sems.at[n])
            cp.start()
            locs.append(cp)
        sends = []
        for n, (ii, sfn, bits, oi, dfn) in enumerate(copies):
            src = in_refs[ii] if sfn is None else in_refs[ii].at[sfn(*me)]
            cp = pltpu.make_async_remote_copy(
                src_ref=src, dst_ref=out_refs[oi].at[dfn(*me)], send_sem=send_sems.at[n], recv_sem=recv_sems.at[n],
                device_id=_flip(bits), device_id_type=MESH)
            cp.start()
            sends.append(cp)
        for n, (ii, sfn, bits, oi, dfn) in enumerate(copies):
            peer = _flip(bits)
            src = in_refs[ii] if sfn is None else in_refs[ii].at[sfn(*me)]
            pltpu.make_async_remote_copy(
                src_ref=src, dst_ref=out_refs[oi].at[dfn(*peer)], send_sem=send_sems.at[n], recv_sem=recv_sems.at[n],
                device_id=peer, device_id_type=MESH).wait_recv()
        for cp in sends:
            cp.wait_send()
        for cp in locs:
            cp.wait()

    any_spec = pl.BlockSpec(memory_space=pl.ANY)
    return _pallas(
        body, name=name, in_specs=[any_spec] * n_in, out_specs=[any_spec] * n_out,
        out_shape=[jax.ShapeDtypeStruct(s, d) for s, d in out_shapes],
        scratch_shapes=[pltpu.SemaphoreType.DMA((max(n_cp, 1),)), pltpu.SemaphoreType.DMA((max(n_cp, 1),)),
                        pltpu.SemaphoreType.DMA((max(n_loc, 1),))],
        input_output_aliases=aliases,
        compiler_params=pltpu.CompilerParams(has_side_effects=True),
    )(*ins)


ICI_FLIPS = ((1, 0, 0), (0, 1, 0), (1, 1, 0))
D2D_FLIP = (0, 0, 1)


def _slot_of(x, y, c):
    return 4 * x + 2 * y + c


def _all_gather(shards, *, name):
    n = len(shards)
    outs = [((N_DEV,) + s.shape, s.dtype) for s in shards]
    copies = [(t, None, bits, t, _slot_of) for t in range(n) for bits in ICI_FLIPS]
    local = [(t, t, _slot_of) for t in range(n)]
    bufs = _exchange(shards, outs, copies, local, name=name + "_ici")
    copies2 = [(t, (lambda x, y, c, q=q: 2 * q + c), D2D_FLIP, t, (lambda x, y, c, q=q: 2 * q + c))
               for t in range(n) for q in range(4)]
    return _exchange(bufs, outs, copies2, [], aliases={t: t for t in range(n)}, name=name + "_d2d")


def _reduce_scatter_stage1(grads, smalls, *, name):
    n = len(grads)
    outs = [((4,) + g.shape[1:], g.dtype) for g in grads] + [((N_DEV,) + s.shape, s.dtype) for s in smalls]
    copies = [(t, (lambda x, y, c, q=q: 2 * q + (1 - c)), D2D_FLIP, t, (lambda x, y, c, q=q: q))
              for t in range(n) for q in range(4)]
    copies += [(n + t, None, bits, n + t, _slot_of) for t in range(len(smalls)) for bits in ICI_FLIPS]
    local = [(n + t, n + t, _slot_of) for t in range(len(smalls))]
    res = _exchange(list(grads) + list(smalls), outs, copies, local, name=name)
    return res[:n], res[n:]


def _reduce_scatter_stage2(parts, small_bufs, *, name):
    n = len(parts)
    outs = [((3,) + p.shape[1:], p.dtype) for p in parts] + [(s.shape, s.dtype) for s in small_bufs]
    copies = []
    for t in range(n):
        for r, bits in enumerate(ICI_FLIPS):
            copies.append((t, (lambda x, y, c, b=bits: 2 * (x ^ b[0]) + (y ^ b[1])), bits, t, (lambda x, y, c, r=r: r)))
    for t in range(len(small_bufs)):
        for q in range(4):
            copies.append((n + t, (lambda x, y, c, q=q: 2 * q + c), D2D_FLIP, n + t, (lambda x, y, c, q=q: 2 * q + c)))
    res = _exchange(list(parts) + list(small_bufs), outs, copies, [],
                    aliases={n + t: n + t for t in range(len(small_bufs))}, name=name)
    return res[:n], res[n:]


def _adam_math(w, g, m, v):
    m = ADAM_B1 * m + (1.0 - ADAM_B1) * g
    v = ADAM_B2 * v + (1.0 - ADAM_B2) * (g * g)
    m_hat = m / (1.0 - ADAM_B1 ** ADAM_STEP)
    v_hat = v / (1.0 - ADAM_B2 ** ADAM_STEP)
    delta = -ADAM_LR * (m_hat / (jnp.sqrt(v_hat) + ADAM_EPS) + ADAM_WD * w)
    return delta, m, v


def _pair_sum(grads8, landing, *, name):
    _, r, cdim = grads8.shape
    tr = _pick(r, (256, 128, 64, 32, 16, 8))
    c = lax.axis_index("c").astype(jnp.int32).reshape(1)

    def body(c_ref, g_ref, l_ref, o_ref):
        o_ref[...] = (g_ref[...] + l_ref[...]).astype(o_ref.dtype)

    return _pallas(
        body, name=name,
        grid_spec=pltpu.PrefetchScalarGridSpec(
            num_scalar_prefetch=1, grid=(4, r // tr),
            in_specs=[pl.BlockSpec((None, tr, cdim), lambda q, i, c_ref: (2 * q + c_ref[0], i, 0)),
                      pl.BlockSpec((None, tr, cdim), lambda q, i, c_ref: (q, i, 0))],
            out_specs=pl.BlockSpec((None, tr, cdim), lambda q, i, c_ref: (q, i, 0))),
        out_shape=jax.ShapeDtypeStruct((4, r, cdim), BF16),
        compiler_params=_cparams("parallel", "parallel"),
    )(c, grads8, landing)


def _shard_adamw(grads8, landing, landing2, w, m, v, *, name):
    r, cdim = w.shape
    tr = _pick(r, (256, 128, 64, 32, 16))
    x, y, c = _coords()
    qc = jnp.stack([2 * x + y, c]).astype(jnp.int32)

    def body(qc_ref, g_ref, p_ref, l_ref, w_ref, m_ref, v_ref, g_out, d_out, m_out, v_out):
        g = g_ref[...] + p_ref[...]
        for n in range(3):
            g = g + l_ref[n].astype(F32)
        d, mn, vn = _adam_math(w_ref[...], g, m_ref[...], v_ref[...])
        g_out[...] = g
        d_out[...] = d
        m_out[...] = mn
        v_out[...] = vn

    blk = pl.BlockSpec((tr, cdim), lambda i, qc_ref: (i, 0))
    return _pallas(
        body, name=name,
        grid_spec=pltpu.PrefetchScalarGridSpec(
            num_scalar_prefetch=1, grid=(r // tr,),
            in_specs=[pl.BlockSpec((None, tr, cdim), lambda i, qc_ref: (2 * qc_ref[0] + qc_ref[1], i, 0)),
                      pl.BlockSpec((None, tr, cdim), lambda i, qc_ref: (qc_ref[0], i, 0)),
                      pl.BlockSpec((3, tr, cdim), lambda i, qc_ref: (0, i, 0)), blk, blk, blk],
            out_specs=[blk] * 4),
        out_shape=[jax.ShapeDtypeStruct((r, cdim), F32)] * 4,
        compiler_params=_cparams("parallel"),
    )(qc, grads8, landing, landing2, w, m, v)


def _small_adamw(slots, w, m, v, *, name):
    r, cdim = w.shape

    def body(s_ref, w_ref, m_ref, v_ref, g_out, d_out, m_out, v_out):
        g = s_ref[0]
        for n in range(1, N_DEV):
            g = g + s_ref[n]
        d, mn, vn = _adam_math(w_ref[...], g, m_ref[...], v_ref[...])
        g_out[...] = g
        d_out[...] = d
        m_out[...] = mn
        v_out[...] = vn

    tr = _pick(r, (256, 128, 64, 32, 16, 8))
    blk = pl.BlockSpec((tr, cdim), lambda i: (i, 0))
    return _pallas(
        body, name=name, grid=(r // tr,),
        in_specs=[pl.BlockSpec((N_DEV, tr, cdim), lambda i: (0, i, 0)), blk, blk, blk], out_specs=[blk] * 4,
        out_shape=[jax.ShapeDtypeStruct((r, cdim), F32)] * 4,
        compiler_params=_cparams("parallel"),
    )(slots, w, m, v)


def _plain_adamw(g, w, m, v, *, name):
    def body(g_ref, w_ref, m_ref, v_ref, d_out, m_out, v_out):
        d, mn, vn = _adam_math(w_ref[...], g_ref[...], m_ref[...], v_ref[...])
        d_out[...] = d
        m_out[...] = mn
        v_out[...] = vn

    spec = pl.BlockSpec(g.shape, lambda: (0,) * g.ndim)
    return _pallas(body, name=name, in_specs=[spec] * 4, out_specs=[spec] * 3,
                   out_shape=[jax.ShapeDtypeStruct(g.shape, F32)] * 3)(g, w, m, v)


def _cast_bf16(x, *, name):
    r, cdim = x.shape
    tr = _pick(r, (256, 128, 64, 32, 16))

    def body(x_ref, o_ref):
        o_ref[...] = x_ref[...].astype(BF16)

    return _pallas(body, name=name, grid=(r // tr,), in_specs=[pl.BlockSpec((tr, cdim), lambda i: (i, 0))],
                   out_specs=pl.BlockSpec((tr, cdim), lambda i: (i, 0)), out_shape=jax.ShapeDtypeStruct(x.shape, BF16),
                   compiler_params=_cparams("parallel"))(x)


SMALL_NAMES = ("norm_mix", "norm_ffn", "s5_a_re", "s5_a_im", "s5_log_dt", "s5_b_re", "s5_b_im", "s5_c_re", "s5_c_im",
               "norm_kv", "norm_final")


def _pack_rows(arrs):
    rows = []
    for a in arrs:
        flat = a.reshape(-1)
        pad = (-flat.shape[0]) % (8 * LANES)
        rows.append(jnp.pad(flat, (0, pad)).reshape(-1, LANES))
    return jnp.concatenate(rows, axis=0)


def _unpack_rows(packed, like):
    out, r0 = [], 0
    for a in like:
        n = math.prod(a.shape)
        nr = (n + 8 * LANES - 1) // (8 * LANES) * 8
        out.append(packed[r0:r0 + nr].reshape(-1)[:n].reshape(a.shape))
        r0 += nr
    return out


def kernel(x, meta_tokens, norm_mix, norm_ffn, s5_a_re, s5_a_im, s5_log_dt, s5_b_re, s5_b_im, s5_c_re, s5_c_im, s5_d, s5_w_glu, norm_kv, w_kv, w_q, w_o, w_ffn_in, w_ffn_out, norm_final, loss_target, m_meta_tokens, m_norm_mix, m_norm_ffn, m_s5_a_re, m_s5_a_im, m_s5_log_dt, m_s5_b_re, m_s5_b_im, m_s5_c_re, m_s5_c_im, m_s5_d, m_s5_w_glu, m_norm_kv, m_w_kv, m_w_q, m_w_o, m_w_ffn_in, m_w_ffn_out, m_norm_final, v_meta_tokens, v_norm_mix, v_norm_ffn, v_s5_a_re, v_s5_a_im, v_s5_log_dt, v_s5_b_re, v_s5_b_im, v_s5_c_re, v_s5_c_im, v_s5_d, v_s5_w_glu, v_norm_kv, v_w_kv, v_w_q, v_w_o, v_w_ffn_in, v_w_ffn_out, v_norm_final):
    args = dict(locals())
    seq = x.shape[1]
    n_valid = N_META + seq
    Lp = (n_valid + ATTN_BLOCK - 1) // ATTN_BLOCK * ATTN_BLOCK
    dm = D_MODEL
    my_slot = _slot_of(*_coords())

    big_shards = {
        "s5_w_glu": s5_w_glu[0], "w_kv": w_kv, "w_q": w_q[0], "w_o": w_o[0],
        "w_ffn_in": w_ffn_in.reshape(-1, w_ffn_in.shape[-1]), "w_ffn_out": w_ffn_out.reshape(-1, w_ffn_out.shape[-1]),
    }
    big_names = list(big_shards)
    cast = [_cast_bf16(big_shards[n], name="cast_" + n) for n in big_names]
    col_shard = jnp.concatenate([meta_tokens, s5_d, jnp.zeros((7, LANES), F32)], axis=0)
    gathered = _all_gather(cast + [col_shard], name="ag")
    gw = dict(zip(big_names, gathered[:-1]))
    gcol = gathered[-1]
    meta_full = jnp.swapaxes(gcol[:, :N_META], 0, 1).reshape(N_META, dm)
    d_full = gcol[:, N_META].reshape(1, dm)

    def cols(g8):
        return jnp.swapaxes(g8, 0, 1).reshape(g8.shape[1], -1)

    w_glu_full = cols(gw["s5_w_glu"])
    w_kv_full = cols(gw["w_kv"])
    n_ffn = w_ffn_in.shape[0]
    w_in_full = cols(gw["w_ffn_in"]).reshape(n_ffn, dm, -1)
    w_out_full = jnp.swapaxes(gw["w_ffn_out"].reshape(N_DEV, n_ffn, -1, dm), 0, 1).reshape(n_ffn, -1, dm)
    big = {
        "s5_w_glu": w_glu_full, "w_k": w_kv_full[:, :dm], "w_v": w_kv_full[:, dm:],
        "w_q": gw["w_q"].reshape(dm, dm), "w_o": gw["w_o"].reshape(dm, dm),
        "w_ffn_in0": w_in_full[0], "w_ffn_in1": w_in_full[1], "w_ffn_out0": w_out_full[0], "w_ffn_out1": w_out_full[1],
    }
    proxies = {n: jnp.zeros(w.shape, F32) for n, w in big.items()}
    small = {n: args[n] for n in SMALL_NAMES}
    small["meta_tokens"] = meta_full
    small["s5_d"] = d_full

    x_pad = jnp.zeros((Lp, dm), F32).at[N_META:n_valid].set(x[0])
    t_pad = jnp.zeros((Lp, dm), F32).at[N_META:n_valid].set(loss_target[0])
    loss_local, vjp = jax.vjp(lambda s, p, xp: _forward_loss(xp, t_pad, n_valid, s, big, p), small, proxies, x_pad)
    g_small, g_big, g_xpad = vjp(jnp.ones((), F32))
    loss = lax.psum(loss_local, ("x", "y", "c"))
    grad_x = g_xpad[N_META:n_valid][None]

    def to_cols(g, nshard=N_DEV):
        return jnp.swapaxes(g.reshape(g.shape[0], nshard, -1), 0, 1)

    g_in = jnp.stack([g_big["w_ffn_in0"], g_big["w_ffn_in1"]])
    g_out = jnp.stack([g_big["w_ffn_out0"], g_big["w_ffn_out1"]])
    grads8 = {
        "s5_w_glu": to_cols(g_big["s5_w_glu"]),
        "w_kv": to_cols(jnp.concatenate([g_big["w_k"], g_big["w_v"]], axis=1)),
        "w_q": g_big["w_q"].reshape(N_DEV, -1, dm), "w_o": g_big["w_o"].reshape(N_DEV, -1, dm),
        "w_ffn_in": jnp.transpose(g_in.reshape(n_ffn, dm, N_DEV, -1), (2, 0, 1, 3)).reshape(N_DEV, n_ffn * dm, -1),
        "w_ffn_out": jnp.swapaxes(g_out.reshape(n_ffn, N_DEV, -1, dm), 0, 1).reshape(N_DEV, -1, dm),
    }
    small_list = [g_small[n] for n in SMALL_NAMES] + [g_small["meta_tokens"], g_small["s5_d"]]
    packed = _pack_rows(small_list)

    landing, small_bufs = _reduce_scatter_stage1([grads8[n] for n in big_names], [packed], name="rs1")
    parts = [_pair_sum(grads8[n], l, name="pair_sum_" + n) for n, l in zip(big_names, landing)]
    landing2, small_bufs = _reduce_scatter_stage2(parts, small_bufs, name="rs2")

    out = {}
    for n, l1, l2 in zip(big_names, landing, landing2):
        w = big_shards[n]
        m = args["m_" + n].reshape(w.shape)
        v = args["v_" + n].reshape(w.shape)
        g, d, mn, vn = _shard_adamw(grads8[n], l1, l2, w, m, v, name="adamw_" + n)
        shape = args[n].shape
        out[n] = (g.reshape(shape), d.reshape(shape), mn.reshape(shape), vn.reshape(shape))

    zeros_tail = [jnp.zeros_like(g_small["meta_tokens"]), jnp.zeros_like(g_small["s5_d"])]
    pw = _pack_rows([args[n] for n in SMALL_NAMES] + zeros_tail)
    pm = _pack_rows([args["m_" + n] for n in SMALL_NAMES] + zeros_tail)
    pv = _pack_rows([args["v_" + n] for n in SMALL_NAMES] + zeros_tail)
    sg, sd, sm, sv = _small_adamw(small_bufs[0], pw, pm, pv, name="adamw_small")
    like = small_list
    ug, ud, um, uv = (_unpack_rows(a, like) for a in (sg, sd, sm, sv))
    for i, n in enumerate(SMALL_NAMES):
        out[n] = (ug[i], ud[i], um[i], uv[i])
    g_meta = lax.dynamic_slice_in_dim(ug[-2], my_slot * LANES, LANES, axis=1)
    g_d = lax.dynamic_slice_in_dim(ug[-1].reshape(1, dm), my_slot * LANES, LANES, axis=1)
    pad7 = jnp.zeros((7, LANES), F32)
    gc = jnp.concatenate([g_meta, g_d, pad7], axis=0)
    wc = jnp.concatenate([meta_tokens, s5_d, pad7], axis=0)
    mc = jnp.concatenate([m_meta_tokens, m_s5_d, pad7], axis=0)
    vc = jnp.concatenate([v_meta_tokens, v_s5_d, pad7], axis=0)
    dc, mcn, vcn = _plain_adamw(gc, wc, mc, vc, name="adamw_cols")
    out["meta_tokens"] = (g_meta, dc[:N_META], mcn[:N_META], vcn[:N_META])
    out["s5_d"] = (g_d, dc[N_META:N_META + 1], mcn[N_META:N_META + 1], vcn[N_META:N_META + 1])

    order = ["meta_tokens", "norm_mix", "norm_ffn", "s5_a_re", "s5_a_im", "s5_log_dt", "s5_b_re", "s5_b_im", "s5_c_re",
             "s5_c_im", "s5_d", "s5_w_glu", "norm_kv", "w_kv", "w_q", "w_o", "w_ffn_in", "w_ffn_out", "norm_final"]
    res = [loss, grad_x]
    for k in range(4):
        res += [out[n][k] for n in order]
    return tuple(res)
```

```python
import functools
import math

import jax
import jax.numpy as jnp
from jax import lax
from jax.experimental import pallas as pl
from jax.experimental.pallas import tpu as pltpu

F32 = jnp.float32
BF16 = jnp.bfloat16

N_META = 16
D_MODEL = 1024
S5_GROUPS = 64
S5_GROUP = 16
S5_STATE = 64
N_HEADS = 16
HEAD_DIM = 64
D_FF = 2816
RMS_EPS = 1e-6
ADAM_LR, ADAM_B1, ADAM_B2, ADAM_EPS, ADAM_WD, ADAM_STEP = 0.001, 0.9, 0.999, 1e-08, 0.01, 10

LANES = 128
ATTN_BLOCK = 128
GROUPS_PER_BLOCK = 8
N_DEV = 8
MESH = pl.DeviceIdType.MESH
VMEM_LIMIT = 56 * 1024 * 1024


def _pallas(body, **kw):
    return pl.pallas_call(body, **kw)


def _cparams(*sem):
    return pltpu.CompilerParams(dimension_semantics=sem, vmem_limit_bytes=VMEM_LIMIT)


def _pick(n, prefs):
    for p in prefs:
        if n % p == 0:
            return p
    return n


def _dot(a, b, ca=1, cb=0):
    return lax.dot_general(a, b, (((ca,), (cb,)), ((), ())), preferred_element_type=F32)


MATMUL_VMEM_BUDGET = 36 * 1024 * 1024
MATMUL_TILES = (1408, 1024, 512, 384, 256, 128)


def _matmul_tiles(M, N, K, a_bytes, b_bytes, tile_bytes, n_acc, offsets):
    best = None
    for tm in (t for t in MATMUL_TILES if M % t == 0):
        for tn in (t for t in MATMUL_TILES if N % t == 0):
            for tk in (t for t in MATMUL_TILES if K % t == 0):
                if any(off % (tn if along_n else tk) for off, along_n in offsets):
                    continue
                vmem = 2 * (tm * tk * a_bytes + tk * tn * b_bytes + tm * tn * tile_bytes) + n_acc * tm * tn * 4
                if vmem > MATMUL_VMEM_BUDGET:
                    continue
                traffic = M * K * a_bytes * (N // tn) + K * N * b_bytes * (M // tm) + M * N * tile_bytes
                traffic += (K // tk - 1) * n_acc * M * N * 8 // 3
                key = (traffic, -tk, -tm)
                if best is None or key < best[0]:
                    best = (key, (tm, tn, tk))
    return best[1]


def _gmm(a_list, b_list, accs, *, n, extras=(), epilogue=None, out_dtypes=(F32,), name):
    a0, ta0 = a_list[0]
    K, M = a0.shape if ta0 else a0.shape[::-1]
    N = n
    b_keys = sorted({(bi, off) for terms in accs for _, bi, off in terms})
    a_bytes = sum(a.dtype.itemsize for a, _ in a_list)
    b_bytes = sum(b_list[bi][0].dtype.itemsize for bi, _ in b_keys)
    tile_bytes = sum(e.dtype.itemsize for e in extras) + sum(jnp.dtype(d).itemsize for d in out_dtypes)
    offsets = [(off, not b_list[bi][1]) for bi, off in b_keys]
    tm, tn, tk = _matmul_tiles(M, N, K, a_bytes, b_bytes, tile_bytes, len(accs), offsets)
    nk = K // tk
    n_a, n_b, n_e, n_o, n_acc = len(a_list), len(b_keys), len(extras), len(out_dtypes), len(accs)
    if epilogue is None:
        epilogue = lambda *vals: vals

    def body(*refs):
        a_refs, b_refs = refs[:n_a], refs[n_a:n_a + n_b]
        e_refs = refs[n_a + n_b:n_a + n_b + n_e]
        o_refs = refs[n_a + n_b + n_e:n_a + n_b + n_e + n_o]
        acc_refs = refs[n_a + n_b + n_e + n_o:]
        a_vals = [r[...].astype(BF16) for r in a_refs]
        b_vals = {key: r[...].astype(BF16) for key, r in zip(b_keys, b_refs)}
        parts = []
        for terms in accs:
            p = None
            for ai, bi, off in terms:
                d = _dot(a_vals[ai], b_vals[(bi, off)], 0 if a_list[ai][1] else 1, 1 if b_list[bi][1] else 0)
                p = d if p is None else p + d
            parts.append(p)

        def finish(vals):
            outs = epilogue(*vals, *[r[...] for r in e_refs])
            for r, val in zip(o_refs, outs):
                r[...] = val.astype(r.dtype)

        if nk == 1:
            finish(parts)
        else:
            k = pl.program_id(2)

            @pl.when(k == 0)
            def _():
                for r, p in zip(acc_refs, parts):
                    r[...] = p

            @pl.when(k > 0)
            def _():
                for r, p in zip(acc_refs, parts):
                    r[...] += p

            @pl.when(k == nk - 1)
            def _():
                finish([r[...] for r in acc_refs])

    in_specs = [pl.BlockSpec((tk, tm), lambda i, j, k: (k, i)) if ta else pl.BlockSpec((tm, tk), lambda i, j, k: (i, k))
                for _, ta in a_list]
    for bi, off in b_keys:
        if b_list[bi][1]:
            in_specs.append(pl.BlockSpec((tn, tk), lambda i, j, k, o=off // tk: (j, k + o)))
        else:
            in_specs.append(pl.BlockSpec((tk, tn), lambda i, j, k, o=off // tn: (k, j + o)))
    tile = pl.BlockSpec((tm, tn), lambda i, j, k: (i, j))
    in_specs += [tile] * n_e
    return _pallas(
        body, name=name, grid=(M // tm, N // tn, nk), in_specs=in_specs, out_specs=[tile] * n_o,
        out_shape=[jax.ShapeDtypeStruct((M, N), d) for d in out_dtypes],
        scratch_shapes=[pltpu.VMEM((tm, tn), F32)] * (n_acc if nk > 1 else 0),
        compiler_params=_cparams("parallel", "parallel", "arbitrary"),
    )(*[a for a, _ in a_list], *[b_list[bi][0] for bi, _ in b_keys], *extras)


def _rowwise(fn, row_ins, full_ins, row_outs, acc_outs, *, tm, name):
    L = row_ins[0].shape[0]
    n_row, n_full, n_ro = len(row_ins), len(full_ins), len(row_outs)

    def body(*refs):
        ins = [r[...] for r in refs[:n_row + n_full]]
        outs = refs[n_row + n_full:]
        res = fn(*ins)
        for r, val in zip(outs[:n_ro], res[:n_ro]):
            r[...] = val.astype(r.dtype)
        if acc_outs:
            first = pl.program_id(0) == 0
            for r, val in zip(outs[n_ro:], res[n_ro:]):
                @pl.when(first)
                def _(r=r, val=val):
                    r[...] = val

                @pl.when(jnp.logical_not(first))
                def _(r=r, val=val):
                    r[...] += val

    in_specs = [pl.BlockSpec((tm, a.shape[1]), lambda i: (i, 0)) for a in row_ins]
    in_specs += [pl.BlockSpec(a.shape, lambda i, nd=a.ndim: (0,) * nd) for a in full_ins]
    out_specs = [pl.BlockSpec((tm, s[1]), lambda i: (i, 0)) for s, _ in row_outs]
    out_specs += [pl.BlockSpec(s, lambda i, nd=len(s): (0,) * nd) for s, _ in acc_outs]
    out_shape = [jax.ShapeDtypeStruct(s, d) for s, d in list(row_outs) + list(acc_outs)]
    return _pallas(
        body, name=name, grid=(L // tm,), in_specs=in_specs, out_specs=out_specs, out_shape=out_shape,
        compiler_params=_cparams("arbitrary" if acc_outs else "parallel"),
    )(*row_ins, *full_ins)


def _colsum(x):
    return jnp.sum(x, axis=0, keepdims=True)


def _rmsnorm_fwd(x, g, *, out_dtype, name):
    L, d = x.shape

    def f(xb, gb):
        r = lax.rsqrt(jnp.mean(xb * xb, axis=-1, keepdims=True) + RMS_EPS)
        return (xb * r * gb,)

    return _rowwise(f, [x], [g.reshape(1, d)], [((L, d), out_dtype)], [], tm=_pick(L, (384, 128)), name=name)[0]


def _rmsnorm_bwd(x, g, dy, resid, *, name):
    L, d = x.shape

    def f(xb, dyb, *rest):
        gb = rest[-1]
        r = lax.rsqrt(jnp.mean(xb * xb, axis=-1, keepdims=True) + RMS_EPS)
        xh = xb * r
        dxh = dyb * gb
        dx = r * (dxh - xh * jnp.mean(dxh * xh, axis=-1, keepdims=True))
        if resid is not None:
            dx = dx + rest[0]
        return dx, _colsum(dyb * xh)

    rows = [x, dy] + ([resid] if resid is not None else [])
    dx, dg = _rowwise(f, rows, [g.reshape(1, d)], [((L, d), F32)], [((1, d), F32)], tm=_pick(L, (384, 128)), name=name)
    return dx, dg.reshape(g.shape)


def _sigmoid(x):
    return 1.0 / (1.0 + jnp.exp(-x))


GELU_K = math.sqrt(2.0 / math.pi)
GELU_C = 0.044715


def _bdmm(a_list, w_rows, *, out_dtype=F32, name):
    L = a_list[0].shape[0]
    nb, ka, kb = w_rows[0][0].shape
    tm = _pick(L, (1408, 384, 128))
    n_a, n_o = len(a_list), len(w_rows)
    ws = [w for row in w_rows for w in row]

    def body(*refs):
        a_vals = [r[...].astype(BF16) for r in refs[:n_a]]
        w_refs = refs[n_a:n_a + n_a * n_o]
        outs = refs[n_a + n_a * n_o:]
        for o in range(n_o):
            acc = None
            for i in range(n_a):
                p = _dot(a_vals[i], w_refs[o * n_a + i][...].astype(BF16))
                acc = p if acc is None else acc + p
            outs[o][...] = acc.astype(out_dtype)

    return _pallas(
        body, name=name, grid=(L // tm, nb),
        in_specs=[pl.BlockSpec((tm, ka), lambda i, j: (i, j)) for _ in a_list]
        + [pl.BlockSpec((None, ka, kb), lambda i, j: (j, 0, 0)) for _ in ws],
        out_specs=[pl.BlockSpec((tm, kb), lambda i, j: (i, j)) for _ in range(n_o)],
        out_shape=[jax.ShapeDtypeStruct((L, nb * kb), out_dtype) for _ in range(n_o)],
        compiler_params=_cparams("parallel", "parallel"),
    )(*a_list, *ws)


def _bdmm_tn(a, g, nb, *, name):
    L = a.shape[0]
    ka, kb = a.shape[1] // nb, g.shape[1] // nb
    tm = _pick(L, (1408, 384, 128))

    def body(a_ref, g_ref, o_ref):
        i = pl.program_id(1)
        part = _dot(a_ref[...].astype(BF16), g_ref[...].astype(BF16), 0, 0)

        @pl.when(i == 0)
        def _():
            o_ref[...] = part

        @pl.when(i > 0)
        def _():
            o_ref[...] += part

    return _pallas(
        body, name=name, grid=(nb, L // tm),
        in_specs=[pl.BlockSpec((tm, ka), lambda j, i: (i, j)), pl.BlockSpec((tm, kb), lambda j, i: (i, j))],
        out_specs=pl.BlockSpec((None, ka, kb), lambda j, i: (j, 0, 0)),
        out_shape=jax.ShapeDtypeStruct((nb, ka, kb), F32),
        compiler_params=_cparams("parallel", "arbitrary"),
    )(a, g)


SCAN_ROWS = 128
STATE_DTYPE = BF16


def _scan_fwd(br, bi, ar, ai, *, name):
    L, S, _ = br.shape
    tb = SCAN_ROWS

    def body(br_ref, bi_ref, ar_ref, ai_ref, xr_ref, xi_ref, carry_ref):
        @pl.when(pl.program_id(0) == 0)
        def _():
            carry_ref[...] = jnp.zeros_like(carry_ref)

        a_r, a_i = ar_ref[...], ai_ref[...]

        def step(t, c):
            xr, xi = c
            nr = a_r * xr - a_i * xi + br_ref[t].astype(F32)
            ni = a_r * xi + a_i * xr + bi_ref[t].astype(F32)
            xr_ref[t] = nr.astype(xr_ref.dtype)
            xi_ref[t] = ni.astype(xi_ref.dtype)
            return nr, ni

        xr, xi = lax.fori_loop(0, tb, step, (carry_ref[0], carry_ref[1]), unroll=8)
        carry_ref[0] = xr
        carry_ref[1] = xi

    blk = pl.BlockSpec((tb, S, LANES), lambda i: (i, 0, 0))
    par = pl.BlockSpec((S, LANES), lambda i: (0, 0))
    return _pallas(
        body, name=name, grid=(L // tb,), in_specs=[blk, blk, par, par], out_specs=[blk, blk],
        out_shape=[jax.ShapeDtypeStruct(br.shape, STATE_DTYPE)] * 2,
        scratch_shapes=[pltpu.VMEM((2, S, LANES), F32)],
        compiler_params=_cparams("arbitrary"),
    )(br, bi, ar, ai)


def _scan_bwd(gr_in, gi_in, ar, ai, xr, xi, *, name):
    L, S, _ = gr_in.shape
    tb = SCAN_ROWS
    nblk = L // tb

    def body(gr_ref, gi_ref, ar_ref, ai_ref, xr_ref, xi_ref, or_ref, oi_ref, dar_ref, dai_ref, carry_ref):
        @pl.when(pl.program_id(0) == 0)
        def _():
            carry_ref[...] = jnp.zeros_like(carry_ref)
            dar_ref[...] = jnp.zeros_like(dar_ref)
            dai_ref[...] = jnp.zeros_like(dai_ref)

        a_r, a_i = ar_ref[...], ai_ref[...]

        def step(s, c):
            gr, gi, dr, di = c
            t = tb - 1 - s
            x_r, x_i = xr_ref[t].astype(F32), xi_ref[t].astype(F32)
            dr = dr + gr * x_r + gi * x_i
            di = di + gi * x_r - gr * x_i
            nr = a_r * gr + a_i * gi + gr_ref[t].astype(F32)
            ni = a_r * gi - a_i * gr + gi_ref[t].astype(F32)
            or_ref[t] = nr.astype(or_ref.dtype)
            oi_ref[t] = ni.astype(oi_ref.dtype)
            return nr, ni, dr, di

        gr, gi, dr, di = lax.fori_loop(0, tb, step, (carry_ref[0], carry_ref[1], dar_ref[...], dai_ref[...]), unroll=8)
        carry_ref[0] = gr
        carry_ref[1] = gi
        dar_ref[...] = dr
        dai_ref[...] = di

    blk = pl.BlockSpec((tb, S, LANES), lambda i: (nblk - 1 - i, 0, 0))
    par = pl.BlockSpec((S, LANES), lambda i: (0, 0))
    return _pallas(
        body, name=name, grid=(nblk,), in_specs=[blk, blk, par, par, blk, blk], out_specs=[blk, blk, par, par],
        out_shape=[jax.ShapeDtypeStruct(gr_in.shape, STATE_DTYPE)] * 2 + [jax.ShapeDtypeStruct((S, LANES), F32)] * 2,
        scratch_shapes=[pltpu.VMEM((2, S, LANES), F32)],
        compiler_params=_cparams("arbitrary"),
    )(gr_in, gi_in, ar, ai, xr, xi)


def _make_ssm(name):
    nb = GROUPS_PER_BLOCK

    def fwd(u, wr, wi, cr, cin, lam_r, lam_i):
        L = u.shape[0]
        bur, bui = _bdmm([u], [[wr], [wi]], out_dtype=STATE_DTYPE, name=name + "_bu")
        S = bur.shape[1] // LANES
        xr, xi = _scan_fwd(bur.reshape(L, S, LANES), bui.reshape(L, S, LANES), lam_r, lam_i, name=name + "_scan")
        xr, xi = xr.reshape(L, S * LANES), xi.reshape(L, S * LANES)
        (y,) = _bdmm([xr, xi], [[cr, cin]], name=name + "_cx")
        return y, (u, wr, wi, cr, cin, lam_r, lam_i, xr, xi)

    def bwd(res, dy):
        u, wr, wi, cr, cin, lam_r, lam_i, xr, xi = res
        L = u.shape[0]
        S = xr.shape[1] // LANES
        tr = lambda w: jnp.swapaxes(w, 1, 2)
        gin_r, gin_i = _bdmm([dy], [[tr(cr)], [tr(cin)]], out_dtype=STATE_DTYPE, name=name + "_gin")
        gr, gi, dlr, dli = _scan_bwd(gin_r.reshape(L, S, LANES), gin_i.reshape(L, S, LANES), lam_r, lam_i,
                                     xr.reshape(L, S, LANES), xi.reshape(L, S, LANES), name=name + "_rscan")
        gr, gi = gr.reshape(L, S * LANES), gi.reshape(L, S * LANES)
        (du,) = _bdmm([gr, gi], [[tr(wr), tr(wi)]], name=name + "_du")
        dwr = _bdmm_tn(u, gr, nb, name=name + "_dwr")
        dwi = _bdmm_tn(u, gi, nb, name=name + "_dwi")
        dcr = _bdmm_tn(xr, dy, nb, name=name + "_dcr")
        dcin = _bdmm_tn(xi, dy, nb, name=name + "_dci")
        return du, dwr, dwi, dcr, dcin, dlr, dli

    return fwd, bwd


def _s5_discretize(a_re, a_im, log_dt, b_re, b_im, c_re, c_im):
    G, P, C, nb = S5_GROUPS, S5_STATE, S5_GROUP, GROUPS_PER_BLOCK
    dt = jnp.exp(log_dt)[:, None]
    mag = jnp.exp(dt * a_re)
    ang = dt * a_im
    abar_re = mag * jnp.cos(ang)
    abar_im = mag * jnp.sin(ang)
    den = a_re * a_re + a_im * a_im
    coef_re = ((abar_re - 1.0) * a_re + abar_im * a_im) / den
    coef_im = (abar_im * a_re - (abar_re - 1.0) * a_im) / den
    bbar_re = coef_re[..., None] * b_re - coef_im[..., None] * b_im
    bbar_im = coef_re[..., None] * b_im + coef_im[..., None] * b_re
    eye = jnp.eye(nb, dtype=F32)

    def blocks_in(bb):
        return jnp.einsum("jgpc,gh->jgchp", bb.reshape(G // nb, nb, P, C), eye).reshape(G // nb, nb * C, nb * P)

    def blocks_out(cc):
        return jnp.einsum("jgcp,gh->jgphc", cc.reshape(G // nb, nb, C, P), eye).reshape(G // nb, nb * P, nb * C)

    lam_r = abar_re.reshape(G * P // LANES, LANES)
    lam_i = abar_im.reshape(G * P // LANES, LANES)
    return blocks_in(bbar_re), blocks_in(bbar_im), blocks_out(c_re), blocks_out(-c_im), lam_r, lam_i


ATTN_Q_ROWS = 384


HEADS_PER_BLOCK = LANES // HEAD_DIM


def _head_masks(shape, axis):
    idx = lax.broadcasted_iota(jnp.int32, shape, axis) // HEAD_DIM
    return [idx == hh for hh in range(HEADS_PER_BLOCK)]


def _masked_bf16(x, masks):
    return [jnp.where(m, x, 0.0).astype(BF16) for m in masks]


def _tri_sum(x, tri2):
    hi = x.astype(BF16)
    lo = (x - hi.astype(F32)).astype(BF16)
    return _dot(jnp.concatenate([hi, lo], axis=1), tri2)


def _attn_weights(qh, kt, strict, after, c):
    z = _dot(qh, kt)
    lb = jnp.minimum(z, 0.0) - jnp.log(1.0 + jnp.exp(-jnp.abs(z)))
    l1m = lb - z
    if strict is not None:
        l1m = jnp.where(strict, l1m, 0.0)
    rem = _tri_sum(l1m, after)
    w = jnp.exp(lb + rem + c)
    if strict is not None:
        w = jnp.where(strict, w, 0.0)
    return lb, w, rem[:, 0:1] + l1m[:, 0:1]


def _tri(cmp):
    tk = ATTN_BLOCK
    tri = cmp(lax.broadcasted_iota(jnp.int32, (tk, tk), 0), lax.broadcasted_iota(jnp.int32, (tk, tk), 1)).astype(BF16)
    return jnp.concatenate([tri, tri], axis=0)


def _attn_fwd(q, kt, v, *, name):
    L = q.shape[0]
    tk = ATTN_BLOCK
    tq = _pick(L, (ATTN_Q_ROWS, tk))
    nsub = tq // tk
    scale = 1.0 / math.sqrt(HEAD_DIM)

    def body(q_ref, kt_ref, v_ref, o_ref, acc_ref):
        i = pl.program_id(1)
        after = _tri(lambda r, c: r > c)
        rowq = lax.broadcasted_iota(jnp.int32, (tq, tk), 0)
        colq = lax.broadcasted_iota(jnp.int32, (tq, tk), 1)
        qs = _masked_bf16(q_ref[...].astype(F32) * scale, _head_masks((tq, LANES), 1))
        vmasks = _head_masks((tk, LANES), 1)
        acc_ref[...] = jnp.zeros_like(acc_ref)

        def tile(r0, strict, cs):
            kt_j = kt_ref[:, pl.ds(r0, tk)].astype(BF16)
            vs = _masked_bf16(v_ref[pl.ds(r0, tk), :], vmasks)
            out, acc = [], None
            for qh, vh, c in zip(qs, vs, cs):
                _, w, tot = _attn_weights(qh, kt_j, strict, after, c)
                part = _dot(w.astype(BF16), vh)
                acc = part if acc is None else acc + part
                out.append(c + tot)
            acc_ref[...] += acc
            return tuple(out)

        cs = tuple(jnp.zeros((tq, 1), F32) for _ in qs)
        for jd in reversed(range(nsub)):
            cs = tile(pl.multiple_of(i * tq + jd * tk, tk), (colq + jd * tk) < rowq, cs)

        def step(ii, cs):
            for jd in reversed(range(nsub)):
                cs = tile(pl.multiple_of((i - 1 - ii) * tq + jd * tk, tk), None, cs)
            return cs

        lax.fori_loop(0, i, step, cs)
        o_ref[...] = acc_ref[...].astype(o_ref.dtype)

    blk = pl.BlockSpec((tq, LANES), lambda h, i: (i, h))
    whole = pl.BlockSpec((L, LANES), lambda h, i: (0, h))
    whole_t = pl.BlockSpec((LANES, L), lambda h, i: (h, 0))
    return _pallas(
        body, name=name, grid=(N_HEADS // HEADS_PER_BLOCK, L // tq), in_specs=[blk, whole_t, whole], out_specs=blk,
        out_shape=jax.ShapeDtypeStruct(q.shape, BF16),
        scratch_shapes=[pltpu.VMEM((tq, LANES), F32)],
        compiler_params=_cparams("parallel", "arbitrary"),
    )(q, kt, v)


def _attn_bwd(q, k, kt, vt, do, *, name):
    L = q.shape[0]
    tk = ATTN_BLOCK
    tq = _pick(L, (ATTN_Q_ROWS, tk))
    nsub = tq // tk
    scale = 1.0 / math.sqrt(HEAD_DIM)

    def body(q_ref, k_ref, kt_ref, vt_ref, do_ref, dq_ref, dkt_ref, dvt_ref, e_scr, sig_scr, dq_acc_ref):
        i = pl.program_id(1)

        @pl.when(i == 0)
        def _():
            dkt_ref[...] = jnp.zeros_like(dkt_ref)
            dvt_ref[...] = jnp.zeros_like(dvt_ref)

        after = _tri(lambda r, c: r > c)
        before = _tri(lambda r, c: r < c)
        rowq = lax.broadcasted_iota(jnp.int32, (tq, tk), 0)
        colq = lax.broadcasted_iota(jnp.int32, (tq, tk), 1)
        qmasks = _head_masks((tq, LANES), 1)
        tmasks = _head_masks((LANES, tq), 0)
        kmasks = _head_masks((tk, LANES), 1)
        q_scaled = q_ref[...].astype(F32) * scale
        do32 = do_ref[...].astype(F32)
        qs = _masked_bf16(q_scaled, qmasks)
        dos = _masked_bf16(do32, qmasks)
        qts = _masked_bf16(q_scaled.T, tmasks)
        dots = _masked_bf16(do32.T, tmasks)
        dq_acc_ref[...] = jnp.zeros_like(dq_acc_ref)

        def tile_left(jb, strict, cs):
            r0 = pl.multiple_of(jb * tk, tk)
            kt_j = kt_ref[:, pl.ds(r0, tk)].astype(BF16)
            vt_j = vt_ref[:, pl.ds(r0, tk)].astype(BF16)
            out, acc = [], None
            for hh, (qh, dob, dot_h, c) in enumerate(zip(qs, dos, dots, cs)):
                lb, w, tot = _attn_weights(qh, kt_j, strict, after, c)
                e_scr[hh, jb] = _dot(dob, vt_j) * w
                sig_scr[hh, jb] = jnp.exp(lb).astype(BF16)
                part = _dot(dot_h, w.astype(BF16))
                acc = part if acc is None else acc + part
                out.append(c + tot)
            dvt_ref[:, pl.ds(r0, tk)] += acc
            return tuple(out)

        def tile_right(jb, strict, cs):
            r0 = pl.multiple_of(jb * tk, tk)
            ks = _masked_bf16(k_ref[pl.ds(r0, tk), :], kmasks)
            out, dq_acc, dk_acc = [], None, None
            for hh, (kh, qt_h, c2) in enumerate(zip(ks, qts, cs)):
                e = e_scr[hh, jb]
                sig = sig_scr[hh, jb].astype(F32)
                left = _tri_sum(e, before)
                dz = e - (e + left + c2) * sig
                if strict is not None:
                    dz = jnp.where(strict, dz, 0.0)
                dz = dz.astype(BF16)
                p1, p2 = _dot(dz, kh), _dot(qt_h, dz)
                dq_acc = p1 if dq_acc is None else dq_acc + p1
                dk_acc = p2 if dk_acc is None else dk_acc + p2
                out.append(c2 + left[:, tk - 1:tk] + e[:, tk - 1:tk])
            dq_acc_ref[...] += dq_acc
            dkt_ref[:, pl.ds(r0, tk)] += dk_acc
            return tuple(out)

        zeros = tuple(jnp.zeros((tq, 1), F32) for _ in qs)
        cs = zeros
        for jd in reversed(range(nsub)):
            cs = tile_left(i * nsub + jd, (colq + jd * tk) < rowq, cs)

        def left_step(ii, cs):
            for jd in reversed(range(nsub)):
                cs = tile_left((i - 1 - ii) * nsub + jd, None, cs)
            return cs

        def right_step(ii, cs):
            for jd in range(nsub):
                cs = tile_right(ii * nsub + jd, None, cs)
            return cs

        lax.fori_loop(0, i, left_step, cs)
        cs = lax.fori_loop(0, i, right_step, zeros)
        for jd in range(nsub):
            cs = tile_right(i * nsub + jd, (colq + jd * tk) < rowq, cs)
        dq_ref[...] = (dq_acc_ref[...] * scale).astype(dq_ref.dtype)

    nhb = N_HEADS // HEADS_PER_BLOCK
    blk = pl.BlockSpec((tq, LANES), lambda h, i: (i, h))
    whole = pl.BlockSpec((L, LANES), lambda h, i: (0, h))
    whole_t = pl.BlockSpec((LANES, L), lambda h, i: (h, 0))
    t_shape = jax.ShapeDtypeStruct((q.shape[1], L), F32)
    return _pallas(
        body, name=name, grid=(nhb, L // tq), in_specs=[blk, whole, whole_t, whole_t, blk],
        out_specs=[blk, whole_t, whole_t], out_shape=[jax.ShapeDtypeStruct(q.shape, BF16), t_shape, t_shape],
        scratch_shapes=[pltpu.VMEM((HEADS_PER_BLOCK, L // tk, tq, tk), F32), pltpu.VMEM((HEADS_PER_BLOCK, L // tk, tq, tk), BF16),
                        pltpu.VMEM((tq, LANES), F32)],
        compiler_params=_cparams("parallel", "arbitrary"),
    )(q, k, kt, vt, do)


def _make_loss_head(n_valid, name):
    def run(h, g, target):
        L, d = h.shape
        tm = _pick(L, (384, 128))

        def body(h_ref, t_ref, g_ref, loss_ref, dh_ref, dg_ref):
            i = pl.program_id(0)
            x = h_ref[...]
            gb = g_ref[...]
            rows = lax.broadcasted_iota(jnp.int32, (tm, 1), 0) + i * tm
            valid = jnp.logical_and(rows >= N_META, rows < n_valid)
            r = lax.rsqrt(jnp.mean(x * x, axis=-1, keepdims=True) + RMS_EPS)
            xh = x * r
            err = jnp.where(valid, xh * gb - t_ref[...], 0.0)
            dy = err * (1.0 / d)
            dxh = dy * gb
            dh_ref[...] = r * (dxh - xh * jnp.mean(dxh * xh, axis=-1, keepdims=True))
            part = 0.5 / d * jnp.sum(jnp.sum(err * err, axis=1, keepdims=True), axis=0, keepdims=True)
            dgp = _colsum(dy * xh)

            @pl.when(i == 0)
            def _():
                loss_ref[...] = part
                dg_ref[...] = dgp

            @pl.when(i > 0)
            def _():
                loss_ref[...] += part
                dg_ref[...] += dgp

        rowspec = pl.BlockSpec((tm, d), lambda i: (i, 0))
        return _pallas(
            body, name=name, grid=(L // tm,),
            in_specs=[rowspec, rowspec, pl.BlockSpec((1, d), lambda i: (0, 0))],
            out_specs=[pl.BlockSpec((1, 1), lambda i: (0, 0)), rowspec, pl.BlockSpec((1, d), lambda i: (0, 0))],
            out_shape=[jax.ShapeDtypeStruct((1, 1), F32), jax.ShapeDtypeStruct((L, d), F32), jax.ShapeDtypeStruct((1, d), F32)],
            compiler_params=_cparams("arbitrary"),
        )(h, target, g.reshape(1, d))

    return run


def _mm(a, b, *, ta=False, tb=False, out_dtype=F32, extras=(), epilogue=None, name):
    n = b.shape[0] if tb else b.shape[1]
    return _gmm([(a, ta)], [(b, tb)], [[(0, 0, 0)]], n=n, extras=extras, epilogue=epilogue, out_dtypes=(out_dtype,), name=name)[0]


def _add_epilogue(acc, resid):
    return (acc + resid,)


def _make_ffn_block(name):
    def fwd(h, gain, w_in, w_out):
        dff = w_out.shape[0]
        f = _rmsnorm_fwd(h, gain, out_dtype=BF16, name=name + "_norm")

        def swiglu(g, u):
            return g, u, g * _sigmoid(g) * u

        g, u, a = _gmm([(f, False)], [(w_in, False)], [[(0, 0, 0)], [(0, 0, dff)]], n=dff, epilogue=swiglu,
                       out_dtypes=(BF16, BF16, BF16), name=name + "_in")
        h2 = _mm(a, w_out, extras=[h], epilogue=_add_epilogue, name=name + "_out")
        return h2, (h, gain, f, g, u, a, w_in, w_out)

    def bwd(res, dh2):
        h, gain, f, g, u, a, w_in, w_out = res
        dff, dm = w_out.shape

        def dswiglu(da, gb, ub):
            gb, ub = gb.astype(F32), ub.astype(F32)
            s = _sigmoid(gb)
            return da * ub * (s + gb * s * (1.0 - s)), da * gb * s

        dg, du = _gmm([(dh2, False)], [(w_out, True)], [[(0, 0, 0)]], n=dff, extras=[g, u], epilogue=dswiglu,
                      out_dtypes=(BF16, BF16), name=name + "_da")
        dw_out = _mm(a, dh2, ta=True, name=name + "_dwout")
        (df,) = _gmm([(dg, False), (du, False)], [(w_in, True)], [[(0, 0, 0), (1, 0, dff)]], n=dm, name=name + "_df")
        dh, dgain = _rmsnorm_bwd(h, gain, df, dh2, name=name + "_dnorm")
        dw_in = jnp.concatenate([_mm(f, dg, ta=True, name=name + "_dwg"), _mm(f, du, ta=True, name=name + "_dwu")], axis=1)
        return dh, dgain, dw_in, dw_out

    return fwd, bwd


def _make_s5_out_block(name):
    def post(yb, ub, db):
        s = yb + db * ub
        return (0.5 * s * (1.0 + jnp.tanh(GELU_K * (s + GELU_C * s * s * s))),)

    def fwd(y, u, d, h, w_glu):
        L, dm = y.shape
        (z,) = _rowwise(post, [y, u], [d.reshape(1, dm)], [((L, dm), BF16)], [], tm=_pick(L, (384, 128)), name=name + "_gelu")

        def glu(val, gate, resid):
            return resid + val * _sigmoid(gate), val, gate

        h2, val, gate = _gmm([(z, False)], [(w_glu, False)], [[(0, 0, 0)], [(0, 0, dm)]], n=dm, extras=[h], epilogue=glu,
                             out_dtypes=(F32, BF16, BF16), name=name + "_glu")
        return h2, (y, u, d, z, val, gate, w_glu)

    def bwd(res, dh2):
        y, u, d, z, val, gate, w_glu = res
        L, dm = y.shape

        def dglu(dh2b, valb, gateb):
            s = _sigmoid(gateb.astype(F32))
            return dh2b * s, dh2b * valb.astype(F32) * s * (1.0 - s)

        dval, dgate = _rowwise(dglu, [dh2, val, gate], [], [((L, dm), BF16)] * 2, [], tm=_pick(L, (384, 128)), name=name + "_dglu")
        (dz,) = _gmm([(dval, False), (dgate, False)], [(w_glu, True)], [[(0, 0, 0), (1, 0, dm)]], n=dm, name=name + "_dz")

        def dpost(yb, ub, dzb, db):
            s = yb + db * ub
            t = jnp.tanh(GELU_K * (s + GELU_C * s * s * s))
            ds = dzb * (0.5 * (1.0 + t) + 0.5 * s * (1.0 - t * t) * GELU_K * (1.0 + 3.0 * GELU_C * s * s))
            return ds, ds * db, _colsum(ds * ub)

        dy, du, dd = _rowwise(dpost, [y, u, dz], [d.reshape(1, dm)], [((L, dm), F32)] * 2, [((1, dm), F32)],
                              tm=_pick(L, (384, 128)), name=name + "_dgelu")
        dw = jnp.concatenate([_mm(z, dval, ta=True, name=name + "_dwv"), _mm(z, dgate, ta=True, name=name + "_dwg")], axis=1)
        return dy, du, dd.reshape(d.shape), dw

    return fwd, bwd


def _make_attn_layer(name):
    def fwd(h, g_kv, g_q, w_k, w_v, w_q, w_o):
        dm = h.shape[1]
        kvn = _rmsnorm_fwd(h, g_kv, out_dtype=BF16, name=name + "_kvnorm")
        qn = _rmsnorm_fwd(h, g_q, out_dtype=BF16, name=name + "_qnorm")
        k, v = _gmm([(kvn, False)], [(w_k, False), (w_v, False)], [[(0, 0, 0)], [(0, 1, 0)]], n=dm,
                    out_dtypes=(BF16, BF16), name=name + "_kv")
        q = _mm(qn, w_q, out_dtype=BF16, name=name + "_q")
        kt, vt = k.T, v.T
        o = _attn_fwd(q, kt, v, name=name + "_fwd")
        h2 = _mm(o, w_o, extras=[h], epilogue=_add_epilogue, name=name + "_o")
        return h2, (h, g_kv, g_q, kvn, qn, q, k, kt, vt, o, w_k, w_v, w_q, w_o)

    def bwd(res, dh2):
        h, g_kv, g_q, kvn, qn, q, k, kt, vt, o, w_k, w_v, w_q, w_o = res
        dm = h.shape[1]
        do = _mm(dh2, w_o, tb=True, out_dtype=BF16, name=name + "_do")
        dw_o = _mm(o, dh2, ta=True, name=name + "_dwo")
        dq, dkt, dvt = _attn_bwd(q, k, kt, vt, do, name=name + "_bwd")
        dqn = _mm(dq, w_q, tb=True, name=name + "_dqn")
        dw_q = _mm(qn, dq, ta=True, name=name + "_dwq")
        (dkvn,) = _gmm([(dkt, True), (dvt, True)], [(w_k, True), (w_v, True)], [[(0, 0, 0), (1, 1, 0)]], n=dm, name=name + "_dkvn")
        dw_k = _mm(kvn, dkt, ta=True, tb=True, name=name + "_dwk")
        dw_v = _mm(kvn, dvt, ta=True, tb=True, name=name + "_dwv")
        dh, dg_q = _rmsnorm_bwd(h, g_q, dqn, dh2, name=name + "_dqnorm")
        dh, dg_kv = _rmsnorm_bwd(h, g_kv, dkvn, dh, name=name + "_dkvnorm")
        return dh, dg_kv, dg_q, dw_k, dw_v, dw_q, dw_o

    return fwd, bwd


S5_PARAMS = ("s5_a_re", "s5_a_im", "s5_log_dt", "s5_b_re", "s5_b_im", "s5_c_re", "s5_c_im")


def _forward_backward(x_pad, target_pad, n_valid, small, get_weights, put_grads):
    gs = {}
    h0 = x_pad.at[:N_META].set(small["meta_tokens"])

    ssm_fwd, ssm_bwd = _make_ssm("ssm")
    s5out_fwd, s5out_bwd = _make_s5_out_block("s5out")
    ffn_fwd, ffn_bwd = zip(_make_ffn_block("ffn0"), _make_ffn_block("ffn1"))
    attn_fwd, attn_bwd = _make_attn_layer("attn")
    u = _rmsnorm_fwd(h0, small["norm_mix"][0], out_dtype=F32, name="mix0_norm")
    disc, disc_vjp = jax.vjp(_s5_discretize, *[small[n][0] for n in S5_PARAMS])
    y, ssm_res = ssm_fwd(u, *disc)
    w = get_weights("s5", y)
    h1, s5out_res = s5out_fwd(y, u, small["s5_d"][0], h0, w["s5_w_glu"])
    w = get_weights("ffn0", h1)
    h2, ffn0_res = ffn_fwd[0](h1, small["norm_ffn"][0], w["w_ffn_in"], w["w_ffn_out"])
    w = get_weights("attn", h2)
    h3, attn_res = attn_fwd(h2, small["norm_kv"], small["norm_mix"][1], w["w_k"], w["w_v"], w["w_q"], w["w_o"])
    w = get_weights("ffn1", h3)
    h4, ffn1_res = ffn_fwd[1](h3, small["norm_ffn"][1], w["w_ffn_in"], w["w_ffn_out"])
    loss, dh, dg_final = _make_loss_head(n_valid, "loss_head")(h4, small["norm_final"], target_pad)
    gs["norm_final"] = dg_final.reshape(small["norm_final"].shape)

    dh, dg_ffn1, dw_in, dw_out = ffn_bwd[1](ffn1_res, dh)
    dh = put_grads("ffn1", {"w_ffn_in": dw_in, "w_ffn_out": dw_out}, dh)
    dh, gs["norm_kv"], dg_mix1, dw_k, dw_v, dw_q, dw_o = attn_bwd(attn_res, dh)
    dh = put_grads("attn", {"w_k": dw_k, "w_v": dw_v, "w_q": dw_q, "w_o": dw_o}, dh)
    dh, dg_ffn0, dw_in, dw_out = ffn_bwd[0](ffn0_res, dh)
    dh = put_grads("ffn0", {"w_ffn_in": dw_in, "w_ffn_out": dw_out}, dh)
    dy, du, dd, dw_glu = s5out_bwd(s5out_res, dh)
    dh = put_grads("s5", {"s5_w_glu": dw_glu}, dh)
    du_ssm, *ddisc = ssm_bwd(ssm_res, dy)
    dh, dg_mix0 = _rmsnorm_bwd(h0, small["norm_mix"][0], du + du_ssm, dh, name="mix0_dnorm")
    for n, g in zip(S5_PARAMS, disc_vjp(tuple(ddisc))):
        gs[n] = g[None]
    gs["s5_d"] = dd.reshape(small["s5_d"].shape)
    gs["norm_mix"] = jnp.stack([dg_mix0, dg_mix1])
    gs["norm_ffn"] = jnp.stack([dg_ffn0, dg_ffn1])
    gs["meta_tokens"] = dh[:N_META]
    return loss, dh, gs


def _coords():
    return lax.axis_index("x"), lax.axis_index("y"), lax.axis_index("c")


def _flip(bits):
    x, y, c = _coords()
    fx, fy, fc = bits
    return (x ^ fx if fx else x, y ^ fy if fy else y, c ^ fc if fc else c)


def _exchange(ins, out_shapes, copies, local_copies, *, aliases=None, name):
    n_in, n_out = len(ins), len(out_shapes)
    n_cp, n_loc = len(copies), len(local_copies)
    aliases = aliases or {}

    def body(*refs):
        in_refs, out_refs = refs[:n_in], refs[n_in:n_in + n_out]
        send_sems, recv_sems, loc_sems = refs[n_in + n_out:]
        me = _coords()
        locs = []
        for n, (ii, oi, dfn) in enumerate(local_copies):
            cp = pltpu.make_async_copy(in_refs[ii], out_refs[oi].at[dfn(*me)], loc_sems.at[n])
            cp.start()
            locs.append(cp)
        sends = []
        for n, (ii, sfn, bits, oi, dfn) in enumerate(copies):
            src = in_refs[ii] if sfn is None else in_refs[ii].at[sfn(*me)]
            cp = pltpu.make_async_remote_copy(
                src_ref=src, dst_ref=out_refs[oi].at[dfn(*me)], send_sem=send_sems.at[n], recv_sem=recv_sems.at[n],
                device_id=_flip(bits), device_id_type=MESH)
            cp.start()
            sends.append(cp)
        for n, (ii, sfn, bits, oi, dfn) in enumerate(copies):
            peer = _flip(bits)
            src = in_refs[ii] if sfn is None else in_refs[ii].at[sfn(*me)]
            pltpu.make_async_remote_copy(
                src_ref=src, dst_ref=out_refs[oi].at[dfn(*peer)], send_sem=send_sems.at[n], recv_sem=recv_sems.at[n],
                device_id=peer, device_id_type=MESH).wait_recv()
        for cp in sends:
            cp.wait_send()
        for cp in locs:
            cp.wait()

    any_spec = pl.BlockSpec(memory_space=pl.ANY)
    return _pallas(
        body, name=name, in_specs=[any_spec] * n_in, out_specs=[any_spec] * n_out,
        out_shape=[jax.ShapeDtypeStruct(s, d) for s, d in out_shapes],
        scratch_shapes=[pltpu.SemaphoreType.DMA((max(n_cp, 1),)), pltpu.SemaphoreType.DMA((max(n_cp, 1),)),
                        pltpu.SemaphoreType.DMA((max(n_loc, 1),))],
        input_output_aliases=aliases,
        compiler_params=pltpu.CompilerParams(has_side_effects=True),
    )(*ins)


ICI_FLIPS = ((1, 0, 0), (0, 1, 0), (1, 1, 0))
D2D_FLIP = (0, 0, 1)


def _slot_of(x, y, c):
    return 4 * x + 2 * y + c


def _all_gather(shards, *, name):
    n = len(shards)
    outs = [((N_DEV,) + s.shape, s.dtype) for s in shards]
    copies = [(t, None, bits, t, _slot_of) for t in range(n) for bits in ICI_FLIPS]
    local = [(t, t, _slot_of) for t in range(n)]
    bufs = _exchange(shards, outs, copies, local, name=name + "_ici")
    copies2 = [(t, (lambda x, y, c, q=q: 2 * q + c), D2D_FLIP, t, (lambda x, y, c, q=q: 2 * q + c))
               for t in range(n) for q in range(4)]
    return _exchange(bufs, outs, copies2, [], aliases={t: t for t in range(n)}, name=name + "_d2d")


ALL_FLIPS = tuple((m >> 2 & 1, m >> 1 & 1, m & 1) for m in range(1, N_DEV))
HBM_SPEC = pl.BlockSpec(memory_space=pltpu.HBM)
SEM_SPEC = pl.BlockSpec(memory_space=pltpu.SEMAPHORE)
SIDE_EFFECT = pltpu.SideEffectType.DATAFLOW_SIDE_EFFECTING


def _copy_desc(copies, n, src_refs, land_refs, send_sems, recv_sems, sender):
    si, sfn, bits, li, dfn = copies[n]
    src = src_refs[si] if sfn is None else src_refs[si].at[sfn(*sender)]
    return pltpu.make_async_remote_copy(
        src_ref=src, dst_ref=land_refs[li].at[dfn(*sender)], send_sem=send_sems.at[n], recv_sem=recv_sems.at[n],
        device_id=_flip(bits), device_id_type=MESH)


def _start_copies(srcs, lands, copies, *, name):
    n_s, n_l, n_c = len(srcs), len(lands), len(copies)

    def body(*refs):
        src_refs = refs[:n_s]
        send_sems, recv_sems = refs[n_s], refs[n_s + 1]
        land_refs = refs[2 * n_s + 2:2 * n_s + 2 + n_l]
        token = refs[-1]
        me = _coords()
        for n in range(n_c):
            _copy_desc(copies, n, src_refs, land_refs, send_sems, recv_sems, me).start()
        token[...] = jnp.zeros_like(token)

    res = _pallas(
        body, name=name,
        out_shape=(pltpu.SemaphoreType.DMA((n_c,)), pltpu.SemaphoreType.DMA((n_c,)),
                   *[pltpu.HBM(s.shape, s.dtype) for s in srcs], *[pltpu.HBM(shape, dtype) for shape, dtype in lands],
                   jax.ShapeDtypeStruct((8, LANES), F32)),
        in_specs=[HBM_SPEC] * n_s,
        out_specs=(SEM_SPEC, SEM_SPEC, *[HBM_SPEC] * (n_s + n_l), pl.BlockSpec(memory_space=pltpu.VMEM)),
        input_output_aliases={i: 2 + i for i in range(n_s)},
        compiler_params=pltpu.CompilerParams(has_side_effects=SIDE_EFFECT),
    )(*[pltpu.with_memory_space_constraint(s, pltpu.HBM) for s in srcs])
    return res[0], res[1], list(res[2:2 + n_s]), list(res[2 + n_s:2 + n_s + n_l]), res[-1]


def _wait_copies(send_sems, recv_sems, srcs, lands, copies, which, after, *, name):
    src_ids, land_ids = sorted(srcs), sorted(lands)
    n_s, n_l = len(src_ids), len(land_ids)
    srcs, lands = [srcs[i] for i in src_ids], [lands[i] for i in land_ids]

    def body(*refs):
        src_refs, land_refs = dict(zip(src_ids, refs[:n_s])), dict(zip(land_ids, refs[n_s:n_s + n_l]))
        s_sems, r_sems = refs[n_s + n_l], refs[n_s + n_l + 1]
        me = _coords()
        for n in which:
            cp = _copy_desc(copies, n, src_refs, land_refs, s_sems, r_sems, me)
            cp.wait_send()
            _copy_desc(copies, n, src_refs, land_refs, s_sems, r_sems, _flip(copies[n][2])).wait_recv()

    res = _pallas(
        body, name=name,
        out_shape=tuple(pltpu.HBM(b.shape, b.dtype) for b in list(srcs) + list(lands)),
        in_specs=[HBM_SPEC] * (n_s + n_l) + [SEM_SPEC, SEM_SPEC, pl.BlockSpec(memory_space=pl.ANY)],
        out_specs=tuple([HBM_SPEC] * (n_s + n_l)),
        input_output_aliases={i: i for i in range(n_s + n_l)},
        compiler_params=pltpu.CompilerParams(has_side_effects=SIDE_EFFECT),
    )(*srcs, *lands, send_sems, recv_sems, after)
    return dict(zip(src_ids, res[:n_s])), dict(zip(land_ids, res[n_s:]))


def _adam_math(w, g, m, v):
    m = ADAM_B1 * m + (1.0 - ADAM_B1) * g
    v = ADAM_B2 * v + (1.0 - ADAM_B2) * (g * g)
    m_hat = m / (1.0 - ADAM_B1 ** ADAM_STEP)
    v_hat = v / (1.0 - ADAM_B2 ** ADAM_STEP)
    delta = -ADAM_LR * (m_hat / (jnp.sqrt(v_hat) + ADAM_EPS) + ADAM_WD * w)
    return delta, m, v


def _shard_adamw(own, landing, w, m, v, *, name):
    r, cdim = w.shape
    tr = _pick(r, (256, 128, 64, 32, 16))

    def body(g_ref, l_ref, w_ref, m_ref, v_ref, g_out, d_out, m_out, v_out):
        g = g_ref[...]
        for n in range(N_DEV - 1):
            g = g + l_ref[n].astype(F32)
        d, mn, vn = _adam_math(w_ref[...], g, m_ref[...], v_ref[...])
        g_out[...] = g
        d_out[...] = d
        m_out[...] = mn
        v_out[...] = vn

    blk = pl.BlockSpec((tr, cdim), lambda i: (i, 0))
    return _pallas(
        body, name=name, grid=(r // tr,),
        in_specs=[blk, pl.BlockSpec((N_DEV - 1, tr, cdim), lambda i: (0, i, 0)), blk, blk, blk], out_specs=[blk] * 4,
        out_shape=[jax.ShapeDtypeStruct((r, cdim), F32)] * 4,
        compiler_params=_cparams("parallel"),
    )(own, landing, w, m, v)


def _small_adamw(slots, w, m, v, *, name):
    r, cdim = w.shape

    def body(s_ref, w_ref, m_ref, v_ref, g_out, d_out, m_out, v_out):
        g = s_ref[0]
        for n in range(1, N_DEV):
            g = g + s_ref[n]
        d, mn, vn = _adam_math(w_ref[...], g, m_ref[...], v_ref[...])
        g_out[...] = g
        d_out[...] = d
        m_out[...] = mn
        v_out[...] = vn

    tr = _pick(r, (256, 128, 64, 32, 16, 8))
    blk = pl.BlockSpec((tr, cdim), lambda i: (i, 0))
    return _pallas(
        body, name=name, grid=(r // tr,),
        in_specs=[pl.BlockSpec((N_DEV, tr, cdim), lambda i: (0, i, 0)), blk, blk, blk], out_specs=[blk] * 4,
        out_shape=[jax.ShapeDtypeStruct((r, cdim), F32)] * 4,
        compiler_params=_cparams("parallel"),
    )(slots, w, m, v)


def _plain_adamw(g, w, m, v, *, name):
    def body(g_ref, w_ref, m_ref, v_ref, d_out, m_out, v_out):
        d, mn, vn = _adam_math(w_ref[...], g_ref[...], m_ref[...], v_ref[...])
        d_out[...] = d
        m_out[...] = mn
        v_out[...] = vn

    spec = pl.BlockSpec(g.shape, lambda: (0,) * g.ndim)
    return _pallas(body, name=name, in_specs=[spec] * 4, out_specs=[spec] * 3,
                   out_shape=[jax.ShapeDtypeStruct(g.shape, F32)] * 3)(g, w, m, v)


def _cast_bf16(x, *, name):
    r, cdim = x.shape
    tr = _pick(r, (256, 128, 64, 32, 16))

    def body(x_ref, o_ref):
        o_ref[...] = x_ref[...].astype(BF16)

    return _pallas(body, name=name, grid=(r // tr,), in_specs=[pl.BlockSpec((tr, cdim), lambda i: (i, 0))],
                   out_specs=pl.BlockSpec((tr, cdim), lambda i: (i, 0)), out_shape=jax.ShapeDtypeStruct(x.shape, BF16),
                   compiler_params=_cparams("parallel"))(x)


SMALL_NAMES = ("norm_mix", "norm_ffn", "s5_a_re", "s5_a_im", "s5_log_dt", "s5_b_re", "s5_b_im", "s5_c_re", "s5_c_im",
               "norm_kv", "norm_final")


def _pack_rows(arrs):
    rows = []
    for a in arrs:
        flat = a.reshape(-1)
        pad = (-flat.shape[0]) % (8 * LANES)
        rows.append(jnp.pad(flat, (0, pad)).reshape(-1, LANES))
    return jnp.concatenate(rows, axis=0)


def _unpack_rows(packed, like):
    out, r0 = [], 0
    for a in like:
        n = math.prod(a.shape)
        nr = (n + 8 * LANES - 1) // (8 * LANES) * 8
        out.append(packed[r0:r0 + nr].reshape(-1)[:n].reshape(a.shape))
        r0 += nr
    return out


def kernel(x, meta_tokens, norm_mix, norm_ffn, s5_a_re, s5_a_im, s5_log_dt, s5_b_re, s5_b_im, s5_c_re, s5_c_im, s5_d, s5_w_glu, norm_kv, w_kv, w_q, w_o, w_ffn_in, w_ffn_out, norm_final, loss_target, m_meta_tokens, m_norm_mix, m_norm_ffn, m_s5_a_re, m_s5_a_im, m_s5_log_dt, m_s5_b_re, m_s5_b_im, m_s5_c_re, m_s5_c_im, m_s5_d, m_s5_w_glu, m_norm_kv, m_w_kv, m_w_q, m_w_o, m_w_ffn_in, m_w_ffn_out, m_norm_final, v_meta_tokens, v_norm_mix, v_norm_ffn, v_s5_a_re, v_s5_a_im, v_s5_log_dt, v_s5_b_re, v_s5_b_im, v_s5_c_re, v_s5_c_im, v_s5_d, v_s5_w_glu, v_norm_kv, v_w_kv, v_w_q, v_w_o, v_w_ffn_in, v_w_ffn_out, v_norm_final):
    args = dict(locals())
    seq = x.shape[1]
    n_valid = N_META + seq
    Lp = (n_valid + ATTN_BLOCK - 1) // ATTN_BLOCK * ATTN_BLOCK
    dm = D_MODEL
    my_slot = _slot_of(*_coords())

    n_ffn = w_ffn_in.shape[0]
    groups = {"s5": ["s5_w_glu"], "ffn0": ["w_ffn_in0", "w_ffn_out0"], "attn": ["w_kv", "w_q", "w_o"],
              "ffn1": ["w_ffn_in1", "w_ffn_out1"]}
    shards = {"s5_w_glu": s5_w_glu[0], "w_kv": w_kv, "w_q": w_q[0], "w_o": w_o[0]}
    for l in range(n_ffn):
        shards[f"w_ffn_in{l}"], shards[f"w_ffn_out{l}"] = w_ffn_in[l], w_ffn_out[l]
    by_columns = {"s5_w_glu", "w_kv", "w_ffn_in0", "w_ffn_in1"}
    names = [n for g in groups.values() for n in g]

    def gathered_full(n, g8):
        if n in by_columns:
            return jnp.swapaxes(g8, 0, 1).reshape(g8.shape[1], -1)
        return g8.reshape(-1, g8.shape[2])

    col_shard = jnp.concatenate([meta_tokens, s5_d, jnp.zeros((7, LANES), F32)], axis=0)
    ag_srcs = [col_shard] + [_cast_bf16(shards[n], name="cast_" + n) for n in names]
    ag_copies = [(t, None, bits, t, _slot_of) for t in range(len(ag_srcs)) for bits in ALL_FLIPS]
    ag_send, ag_recv, ag_srcs, ag_lands, ag_token = _start_copies(
        ag_srcs, [((N_DEV,) + s.shape, s.dtype) for s in ag_srcs], ag_copies, name="ag_start")

    def gather_wait(tag, ids, after):
        which = [t * len(ALL_FLIPS) + r for t in ids for r in range(len(ALL_FLIPS))]
        srcs, lands = _wait_copies(ag_send, ag_recv, {t: ag_srcs[t] for t in ids}, {t: ag_lands[t] for t in ids},
                                   ag_copies, which, after, name="ag_wait_" + tag)
        return {t: lax.dynamic_update_index_in_dim(lands[t], srcs[t], my_slot, 0) for t in ids}

    def get_weights(group, after):
        ids = [1 + names.index(n) for n in groups[group]]
        full = {names[t - 1]: gathered_full(names[t - 1], g8) for t, g8 in gather_wait(group, ids, after).items()}
        if group == "attn":
            w_kv_full = full.pop("w_kv")
            full["w_k"], full["w_v"] = w_kv_full[:, :dm], w_kv_full[:, dm:]
        return {n.rstrip("01") if n.startswith("w_ffn") else n: w for n, w in full.items()}

    gcol = gather_wait("cols", [0], ag_token)[0]
    small = {n: args[n] for n in SMALL_NAMES}
    small["meta_tokens"] = jnp.swapaxes(gcol[:, :N_META], 0, 1).reshape(N_META, dm)
    small["s5_d"] = gcol[:, N_META].reshape(1, dm)

    scatters = {}

    def put_grads(group, grads, dh):
        if group == "attn":
            grads = {"w_kv": jnp.concatenate([grads.pop("w_k"), grads.pop("w_v")], axis=1), **grads}
        own, send = {}, []
        for n in groups[group]:
            g = grads[n.rstrip("01") if n.startswith("w_ffn") else n]
            r, c = shards[n].shape
            if n in by_columns:
                own[n] = lax.dynamic_slice_in_dim(g, my_slot * c, c, axis=1)
                send.append(jnp.swapaxes(g.reshape(r, N_DEV, c), 0, 1).astype(BF16))
            else:
                own[n] = lax.dynamic_slice_in_dim(g, my_slot * r, r, axis=0)
                send.append(g.reshape(N_DEV, r, c).astype(BF16))
        copies = [(t, (lambda x, y, c, b=bits: _slot_of(x ^ b[0], y ^ b[1], c ^ b[2])), bits, t, (lambda x, y, c, k=k: k))
                  for t in range(len(send)) for k, bits in enumerate(ALL_FLIPS)]
        lands = [((len(ALL_FLIPS),) + s.shape[1:], BF16) for s in send]
        s_sem, r_sem, srcs, lands, token = _start_copies(send, lands, copies, name="rs_start_" + group)
        scatters[group] = (s_sem, r_sem, srcs, lands, copies, own)
        return lax.optimization_barrier((dh, token))[0]

    x_pad = jnp.zeros((Lp, dm), F32).at[N_META:n_valid].set(x[0])
    t_pad = jnp.zeros((Lp, dm), F32).at[N_META:n_valid].set(loss_target[0])
    loss_local, g_xpad, g_small = _forward_backward(x_pad, t_pad, n_valid, small, get_weights, put_grads)
    loss = lax.psum(loss_local[0, 0], ("x", "y", "c"))
    grad_x = g_xpad[N_META:n_valid][None]

    small_list = [g_small[n] for n in SMALL_NAMES] + [g_small["meta_tokens"], g_small["s5_d"]]
    small_bufs = _all_gather([_pack_rows(small_list)], name="small_ag")

    out, per_tensor = {}, {}
    for group, (s_sem, r_sem, srcs, lands, copies, own) in scatters.items():
        ids = list(range(len(srcs)))
        _, landed = _wait_copies(s_sem, r_sem, dict(zip(ids, srcs)), dict(zip(ids, lands)), copies, list(range(len(copies))),
                                 small_bufs[0], name="rs_wait_" + group)
        for t, n in enumerate(groups[group]):
            base, layer = (n[:-1], int(n[-1])) if n.startswith("w_ffn") else (n, None)
            pick = (lambda a: a[layer]) if layer is not None else (lambda a: a.reshape(shards[n].shape))
            per_tensor[n] = _shard_adamw(own[n], landed[t], shards[n], pick(args["m_" + base]), pick(args["v_" + base]),
                                         name="adamw_" + n)
    for base in ("s5_w_glu", "w_kv", "w_q", "w_o"):
        out[base] = tuple(a.reshape(args[base].shape) for a in per_tensor[base])
    for base in ("w_ffn_in", "w_ffn_out"):
        out[base] = tuple(jnp.stack([per_tensor[f"{base}{l}"][k] for l in range(n_ffn)]) for k in range(4))

    zeros_tail = [jnp.zeros_like(g_small["meta_tokens"]), jnp.zeros_like(g_small["s5_d"])]
    pw = _pack_rows([args[n] for n in SMALL_NAMES] + zeros_tail)
    pm = _pack_rows([args["m_" + n] for n in SMALL_NAMES] + zeros_tail)
    pv = _pack_rows([args["v_" + n] for n in SMALL_NAMES] + zeros_tail)
    sg, sd, sm, sv = _small_adamw(small_bufs[0], pw, pm, pv, name="adamw_small")
    like = small_list
    ug, ud, um, uv = (_unpack_rows(a, like) for a in (sg, sd, sm, sv))
    for i, n in enumerate(SMALL_NAMES):
        out[n] = (ug[i], ud[i], um[i], uv[i])
    g_meta = lax.dynamic_slice_in_dim(ug[-2], my_slot * LANES, LANES, axis=1)
    g_d = lax.dynamic_slice_in_dim(ug[-1].reshape(1, dm), my_slot * LANES, LANES, axis=1)
    pad7 = jnp.zeros((7, LANES), F32)
    gc = jnp.concatenate([g_meta, g_d, pad7], axis=0)
    wc = jnp.concatenate([meta_tokens, s5_d, pad7], axis=0)
    mc = jnp.concatenate([m_meta_tokens, m_s5_d, pad7], axis=0)
    vc = jnp.concatenate([v_meta_tokens, v_s5_d, pad7], axis=0)
    dc, mcn, vcn = _plain_adamw(gc, wc, mc, vc, name="adamw_cols")
    out["meta_tokens"] = (g_meta, dc[:N_META], mcn[:N_META], vcn[:N_META])
    out["s5_d"] = (g_d, dc[N_META:N_META + 1], mcn[N_META:N_META + 1], vcn[N_META:N_META + 1])

    order = ["meta_tokens", "norm_mix", "norm_ffn", "s5_a_re", "s5_a_im", "s5_log_dt", "s5_b_re", "s5_b_im", "s5_c_re",
             "s5_c_im", "s5_d", "s5_w_glu", "norm_kv", "w_kv", "w_q", "w_o", "w_ffn_in", "w_ffn_out", "norm_final"]
    res = [loss, grad_x]
    for k in range(4):
        res += [out[n][k] for n in order]
    return tuple(res)
```

```python
import functools
import math

import jax
import jax.numpy as jnp
from jax import lax
from jax.experimental import pallas as pl
from jax.experimental.pallas import tpu as pltpu

F32 = jnp.float32
BF16 = jnp.bfloat16

N_META = 16
D_MODEL = 1024
S5_GROUPS = 64
S5_GROUP = 16
S5_STATE = 64
N_HEADS = 16
HEAD_DIM = 64
D_FF = 2816
RMS_EPS = 1e-6
ADAM_LR, ADAM_B1, ADAM_B2, ADAM_EPS, ADAM_WD, ADAM_STEP = 0.001, 0.9, 0.999, 1e-08, 0.01, 10

LANES = 128
ATTN_BLOCK = 128
GROUPS_PER_BLOCK = 8
N_DEV = 8
MESH = pl.DeviceIdType.MESH
VMEM_LIMIT = 56 * 1024 * 1024


def _pallas(body, **kw):
    return pl.pallas_call(body, **kw)


def _cparams(*sem):
    return pltpu.CompilerParams(dimension_semantics=sem, vmem_limit_bytes=VMEM_LIMIT)


def _pick(n, prefs):
    for p in prefs:
        if n % p == 0:
            return p
    return n


def _dot(a, b, ca=1, cb=0):
    return lax.dot_general(a, b, (((ca,), (cb,)), ((), ())), preferred_element_type=F32)


MATMUL_VMEM_BUDGET = 36 * 1024 * 1024
MATMUL_TILES = (1408, 1024, 512, 384, 256, 128)


def _matmul_tiles(M, N, K, a_bytes, b_bytes, tile_bytes, n_acc, offsets):
    best = None
    for tm in (t for t in MATMUL_TILES if M % t == 0):
        for tn in (t for t in MATMUL_TILES if N % t == 0):
            for tk in (t for t in MATMUL_TILES if K % t == 0):
                if any(off % (tn if along_n else tk) for off, along_n in offsets):
                    continue
                vmem = 2 * (tm * tk * a_bytes + tk * tn * b_bytes + tm * tn * tile_bytes) + n_acc * tm * tn * 4
                if vmem > MATMUL_VMEM_BUDGET:
                    continue
                traffic = M * K * a_bytes * (N // tn) + K * N * b_bytes * (M // tm) + M * N * tile_bytes
                traffic += (K // tk - 1) * n_acc * M * N * 8 // 3
                key = (traffic, -tk, -tm)
                if best is None or key < best[0]:
                    best = (key, (tm, tn, tk))
    return best[1]


def _gmm(a_list, b_list, accs, *, n, extras=(), epilogue=None, out_dtypes=(F32,), name):
    a0, ta0 = a_list[0]
    K, M = a0.shape if ta0 else a0.shape[::-1]
    N = n
    b_keys = sorted({(bi, off) for terms in accs for _, bi, off in terms})
    a_bytes = sum(a.dtype.itemsize for a, _ in a_list)
    b_bytes = sum(b_list[bi][0].dtype.itemsize for bi, _ in b_keys)
    tile_bytes = sum(e.dtype.itemsize for e in extras) + sum(jnp.dtype(d).itemsize for d in out_dtypes)
    offsets = [(off, not b_list[bi][1]) for bi, off in b_keys]
    tm, tn, tk = _matmul_tiles(M, N, K, a_bytes, b_bytes, tile_bytes, len(accs), offsets)
    nk = K // tk
    n_a, n_b, n_e, n_o, n_acc = len(a_list), len(b_keys), len(extras), len(out_dtypes), len(accs)
    if epilogue is None:
        epilogue = lambda *vals: vals

    def body(*refs):
        a_refs, b_refs = refs[:n_a], refs[n_a:n_a + n_b]
        e_refs = refs[n_a + n_b:n_a + n_b + n_e]
        o_refs = refs[n_a + n_b + n_e:n_a + n_b + n_e + n_o]
        acc_refs = refs[n_a + n_b + n_e + n_o:]
        a_vals = [r[...].astype(BF16) for r in a_refs]
        b_vals = {key: r[...].astype(BF16) for key, r in zip(b_keys, b_refs)}
        parts = []
        for terms in accs:
            p = None
            for ai, bi, off in terms:
                d = _dot(a_vals[ai], b_vals[(bi, off)], 0 if a_list[ai][1] else 1, 1 if b_list[bi][1] else 0)
                p = d if p is None else p + d
            parts.append(p)

        def finish(vals):
            outs = epilogue(*vals, *[r[...] for r in e_refs])
            for r, val in zip(o_refs, outs):
                r[...] = val.astype(r.dtype)

        if nk == 1:
            finish(parts)
        else:
            k = pl.program_id(2)

            @pl.when(k == 0)
            def _():
                for r, p in zip(acc_refs, parts):
                    r[...] = p

            @pl.when(k > 0)
            def _():
                for r, p in zip(acc_refs, parts):
                    r[...] += p

            @pl.when(k == nk - 1)
            def _():
                finish([r[...] for r in acc_refs])

    in_specs = [pl.BlockSpec((tk, tm), lambda i, j, k: (k, i)) if ta else pl.BlockSpec((tm, tk), lambda i, j, k: (i, k))
                for _, ta in a_list]
    for bi, off in b_keys:
        if b_list[bi][1]:
            in_specs.append(pl.BlockSpec((tn, tk), lambda i, j, k, o=off // tk: (j, k + o)))
        else:
            in_specs.append(pl.BlockSpec((tk, tn), lambda i, j, k, o=off // tn: (k, j + o)))
    tile = pl.BlockSpec((tm, tn), lambda i, j, k: (i, j))
    in_specs += [tile] * n_e
    return _pallas(
        body, name=name, grid=(M // tm, N // tn, nk), in_specs=in_specs, out_specs=[tile] * n_o,
        out_shape=[jax.ShapeDtypeStruct((M, N), d) for d in out_dtypes],
        scratch_shapes=[pltpu.VMEM((tm, tn), F32)] * (n_acc if nk > 1 else 0),
        compiler_params=_cparams("parallel", "parallel", "arbitrary"),
    )(*[a for a, _ in a_list], *[b_list[bi][0] for bi, _ in b_keys], *extras)


def _rowwise(fn, row_ins, full_ins, row_outs, acc_outs, *, tm, name):
    L = row_ins[0].shape[0]
    n_row, n_full, n_ro = len(row_ins), len(full_ins), len(row_outs)

    def body(*refs):
        ins = [r[...] for r in refs[:n_row + n_full]]
        outs = refs[n_row + n_full:]
        res = fn(*ins)
        for r, val in zip(outs[:n_ro], res[:n_ro]):
            r[...] = val.astype(r.dtype)
        if acc_outs:
            first = pl.program_id(0) == 0
            for r, val in zip(outs[n_ro:], res[n_ro:]):
                @pl.when(first)
                def _(r=r, val=val):
                    r[...] = val

                @pl.when(jnp.logical_not(first))
                def _(r=r, val=val):
                    r[...] += val

    in_specs = [pl.BlockSpec((tm, a.shape[1]), lambda i: (i, 0)) for a in row_ins]
    in_specs += [pl.BlockSpec(a.shape, lambda i, nd=a.ndim: (0,) * nd) for a in full_ins]
    out_specs = [pl.BlockSpec((tm, s[1]), lambda i: (i, 0)) for s, _ in row_outs]
    out_specs += [pl.BlockSpec(s, lambda i, nd=len(s): (0,) * nd) for s, _ in acc_outs]
    out_shape = [jax.ShapeDtypeStruct(s, d) for s, d in list(row_outs) + list(acc_outs)]
    return _pallas(
        body, name=name, grid=(L // tm,), in_specs=in_specs, out_specs=out_specs, out_shape=out_shape,
        compiler_params=_cparams("arbitrary" if acc_outs else "parallel"),
    )(*row_ins, *full_ins)


def _colsum(x):
    return jnp.sum(x, axis=0, keepdims=True)


def _rmsnorm_fwd(x, g, *, out_dtype, name):
    L, d = x.shape

    def f(xb, gb):
        r = lax.rsqrt(jnp.mean(xb * xb, axis=-1, keepdims=True) + RMS_EPS)
        return (xb * r * gb,)

    return _rowwise(f, [x], [g.reshape(1, d)], [((L, d), out_dtype)], [], tm=_pick(L, (384, 128)), name=name)[0]


def _rmsnorm_bwd(x, g, dy, resid, *, name):
    L, d = x.shape

    def f(xb, dyb, *rest):
        gb = rest[-1]
        r = lax.rsqrt(jnp.mean(xb * xb, axis=-1, keepdims=True) + RMS_EPS)
        xh = xb * r
        dxh = dyb * gb
        dx = r * (dxh - xh * jnp.mean(dxh * xh, axis=-1, keepdims=True))
        if resid is not None:
            dx = dx + rest[0]
        return dx, _colsum(dyb * xh)

    rows = [x, dy] + ([resid] if resid is not None else [])
    dx, dg = _rowwise(f, rows, [g.reshape(1, d)], [((L, d), F32)], [((1, d), F32)], tm=_pick(L, (384, 128)), name=name)
    return dx, dg.reshape(g.shape)


def _sigmoid(x):
    return 1.0 / (1.0 + jnp.exp(-x))


GELU_K = math.sqrt(2.0 / math.pi)
GELU_C = 0.044715


def _bdmm(a_list, w_rows, *, out_dtype=F32, name):
    L = a_list[0].shape[0]
    nb, ka, kb = w_rows[0][0].shape
    tm = _pick(L, (1408, 384, 128))
    n_a, n_o = len(a_list), len(w_rows)
    ws = [w for row in w_rows for w in row]

    def body(*refs):
        a_vals = [r[...].astype(BF16) for r in refs[:n_a]]
        w_refs = refs[n_a:n_a + n_a * n_o]
        outs = refs[n_a + n_a * n_o:]
        for o in range(n_o):
            acc = None
            for i in range(n_a):
                p = _dot(a_vals[i], w_refs[o * n_a + i][...].astype(BF16))
                acc = p if acc is None else acc + p
            outs[o][...] = acc.astype(out_dtype)

    return _pallas(
        body, name=name, grid=(L // tm, nb),
        in_specs=[pl.BlockSpec((tm, ka), lambda i, j: (i, j)) for _ in a_list]
        + [pl.BlockSpec((None, ka, kb), lambda i, j: (j, 0, 0)) for _ in ws],
        out_specs=[pl.BlockSpec((tm, kb), lambda i, j: (i, j)) for _ in range(n_o)],
        out_shape=[jax.ShapeDtypeStruct((L, nb * kb), out_dtype) for _ in range(n_o)],
        compiler_params=_cparams("parallel", "parallel"),
    )(*a_list, *ws)


def _bdmm_tn(a, g, nb, *, name):
    L = a.shape[0]
    ka, kb = a.shape[1] // nb, g.shape[1] // nb
    tm = _pick(L, (1408, 384, 128))

    def body(a_ref, g_ref, o_ref):
        i = pl.program_id(1)
        part = _dot(a_ref[...].astype(BF16), g_ref[...].astype(BF16), 0, 0)

        @pl.when(i == 0)
        def _():
            o_ref[...] = part

        @pl.when(i > 0)
        def _():
            o_ref[...] += part

    return _pallas(
        body, name=name, grid=(nb, L // tm),
        in_specs=[pl.BlockSpec((tm, ka), lambda j, i: (i, j)), pl.BlockSpec((tm, kb), lambda j, i: (i, j))],
        out_specs=pl.BlockSpec((None, ka, kb), lambda j, i: (j, 0, 0)),
        out_shape=jax.ShapeDtypeStruct((nb, ka, kb), F32),
        compiler_params=_cparams("parallel", "arbitrary"),
    )(a, g)


SCAN_ROWS = 128
STATE_DTYPE = BF16


def _scan_fwd(br, bi, ar, ai, *, name):
    L, S, _ = br.shape
    tb = SCAN_ROWS

    def body(br_ref, bi_ref, ar_ref, ai_ref, xr_ref, xi_ref, carry_ref):
        @pl.when(pl.program_id(0) == 0)
        def _():
            carry_ref[...] = jnp.zeros_like(carry_ref)

        a_r, a_i = ar_ref[...], ai_ref[...]

        def step(t, c):
            xr, xi = c
            nr = a_r * xr - a_i * xi + br_ref[t].astype(F32)
            ni = a_r * xi + a_i * xr + bi_ref[t].astype(F32)
            xr_ref[t] = nr.astype(xr_ref.dtype)
            xi_ref[t] = ni.astype(xi_ref.dtype)
            return nr, ni

        xr, xi = lax.fori_loop(0, tb, step, (carry_ref[0], carry_ref[1]), unroll=8)
        carry_ref[0] = xr
        carry_ref[1] = xi

    blk = pl.BlockSpec((tb, S, LANES), lambda i: (i, 0, 0))
    par = pl.BlockSpec((S, LANES), lambda i: (0, 0))
    return _pallas(
        body, name=name, grid=(L // tb,), in_specs=[blk, blk, par, par], out_specs=[blk, blk],
        out_shape=[jax.ShapeDtypeStruct(br.shape, STATE_DTYPE)] * 2,
        scratch_shapes=[pltpu.VMEM((2, S, LANES), F32)],
        compiler_params=_cparams("arbitrary"),
    )(br, bi, ar, ai)


def _scan_bwd(gr_in, gi_in, ar, ai, xr, xi, *, name):
    L, S, _ = gr_in.shape
    tb = SCAN_ROWS
    nblk = L // tb

    def body(gr_ref, gi_ref, ar_ref, ai_ref, xr_ref, xi_ref, or_ref, oi_ref, dar_ref, dai_ref, carry_ref):
        @pl.when(pl.program_id(0) == 0)
        def _():
            carry_ref[...] = jnp.zeros_like(carry_ref)
            dar_ref[...] = jnp.zeros_like(dar_ref)
            dai_ref[...] = jnp.zeros_like(dai_ref)

        a_r, a_i = ar_ref[...], ai_ref[...]

        def step(s, c):
            gr, gi, dr, di = c
            t = tb - 1 - s
            x_r, x_i = xr_ref[t].astype(F32), xi_ref[t].astype(F32)
            dr = dr + gr * x_r + gi * x_i
            di = di + gi * x_r - gr * x_i
            nr = a_r * gr + a_i * gi + gr_ref[t].astype(F32)
            ni = a_r * gi - a_i * gr + gi_ref[t].astype(F32)
            or_ref[t] = nr.astype(or_ref.dtype)
            oi_ref[t] = ni.astype(oi_ref.dtype)
            return nr, ni, dr, di

        gr, gi, dr, di = lax.fori_loop(0, tb, step, (carry_ref[0], carry_ref[1], dar_ref[...], dai_ref[...]), unroll=8)
        carry_ref[0] = gr
        carry_ref[1] = gi
        dar_ref[...] = dr
        dai_ref[...] = di

    blk = pl.BlockSpec((tb, S, LANES), lambda i: (nblk - 1 - i, 0, 0))
    par = pl.BlockSpec((S, LANES), lambda i: (0, 0))
    return _pallas(
        body, name=name, grid=(nblk,), in_specs=[blk, blk, par, par, blk, blk], out_specs=[blk, blk, par, par],
        out_shape=[jax.ShapeDtypeStruct(gr_in.shape, STATE_DTYPE)] * 2 + [jax.ShapeDtypeStruct((S, LANES), F32)] * 2,
        scratch_shapes=[pltpu.VMEM((2, S, LANES), F32)],
        compiler_params=_cparams("arbitrary"),
    )(gr_in, gi_in, ar, ai, xr, xi)


def _make_ssm(name):
    nb = GROUPS_PER_BLOCK

    def fwd(u, wr, wi, cr, cin, lam_r, lam_i):
        L = u.shape[0]
        bur, bui = _bdmm([u], [[wr], [wi]], out_dtype=STATE_DTYPE, name=name + "_bu")
        S = bur.shape[1] // LANES
        xr, xi = _scan_fwd(bur.reshape(L, S, LANES), bui.reshape(L, S, LANES), lam_r, lam_i, name=name + "_scan")
        xr, xi = xr.reshape(L, S * LANES), xi.reshape(L, S * LANES)
        (y,) = _bdmm([xr, xi], [[cr, cin]], name=name + "_cx")
        return y, (u, wr, wi, cr, cin, lam_r, lam_i, xr, xi)

    def bwd(res, dy):
        u, wr, wi, cr, cin, lam_r, lam_i, xr, xi = res
        L = u.shape[0]
        S = xr.shape[1] // LANES
        tr = lambda w: jnp.swapaxes(w, 1, 2)
        gin_r, gin_i = _bdmm([dy], [[tr(cr)], [tr(cin)]], out_dtype=STATE_DTYPE, name=name + "_gin")
        gr, gi, dlr, dli = _scan_bwd(gin_r.reshape(L, S, LANES), gin_i.reshape(L, S, LANES), lam_r, lam_i,
                                     xr.reshape(L, S, LANES), xi.reshape(L, S, LANES), name=name + "_rscan")
        gr, gi = gr.reshape(L, S * LANES), gi.reshape(L, S * LANES)
        (du,) = _bdmm([gr, gi], [[tr(wr), tr(wi)]], name=name + "_du")
        dwr = _bdmm_tn(u, gr, nb, name=name + "_dwr")
        dwi = _bdmm_tn(u, gi, nb, name=name + "_dwi")
        dcr = _bdmm_tn(xr, dy, nb, name=name + "_dcr")
        dcin = _bdmm_tn(xi, dy, nb, name=name + "_dci")
        return du, dwr, dwi, dcr, dcin, dlr, dli

    return fwd, bwd


def _s5_discretize(a_re, a_im, log_dt, b_re, b_im, c_re, c_im):
    G, P, C, nb = S5_GROUPS, S5_STATE, S5_GROUP, GROUPS_PER_BLOCK
    dt = jnp.exp(log_dt)[:, None]
    mag = jnp.exp(dt * a_re)
    ang = dt * a_im
    abar_re = mag * jnp.cos(ang)
    abar_im = mag * jnp.sin(ang)
    den = a_re * a_re + a_im * a_im
    coef_re = ((abar_re - 1.0) * a_re + abar_im * a_im) / den
    coef_im = (abar_im * a_re - (abar_re - 1.0) * a_im) / den
    bbar_re = coef_re[..., None] * b_re - coef_im[..., None] * b_im
    bbar_im = coef_re[..., None] * b_im + coef_im[..., None] * b_re
    eye = jnp.eye(nb, dtype=F32)

    def blocks_in(bb):
        return jnp.einsum("jgpc,gh->jgchp", bb.reshape(G // nb, nb, P, C), eye).reshape(G // nb, nb * C, nb * P)

    def blocks_out(cc):
        return jnp.einsum("jgcp,gh->jgphc", cc.reshape(G // nb, nb, C, P), eye).reshape(G // nb, nb * P, nb * C)

    lam_r = abar_re.reshape(G * P // LANES, LANES)
    lam_i = abar_im.reshape(G * P // LANES, LANES)
    return blocks_in(bbar_re), blocks_in(bbar_im), blocks_out(c_re), blocks_out(-c_im), lam_r, lam_i


ATTN_Q_ROWS = 384


HEADS_PER_BLOCK = LANES // HEAD_DIM


def _head_masks(shape, axis):
    idx = lax.broadcasted_iota(jnp.int32, shape, axis) // HEAD_DIM
    return [idx == hh for hh in range(HEADS_PER_BLOCK)]


def _masked_bf16(x, masks):
    return [jnp.where(m, x, 0.0).astype(BF16) for m in masks]


def _tri_sum(x, tri2):
    hi = x.astype(BF16)
    lo = (x - hi.astype(F32)).astype(BF16)
    return _dot(jnp.concatenate([hi, lo], axis=1), tri2)


def _attn_weights(qh, kt, strict, after, c):
    z = _dot(qh, kt)
    lb = jnp.minimum(z, 0.0) - jnp.log(1.0 + jnp.exp(-jnp.abs(z)))
    l1m = lb - z
    if strict is not None:
        l1m = jnp.where(strict, l1m, 0.0)
    rem = _tri_sum(l1m, after)
    w = jnp.exp(lb + rem + c)
    if strict is not None:
        w = jnp.where(strict, w, 0.0)
    return lb, w, rem[:, 0:1] + l1m[:, 0:1]


def _tri(cmp):
    tk = ATTN_BLOCK
    tri = cmp(lax.broadcasted_iota(jnp.int32, (tk, tk), 0), lax.broadcasted_iota(jnp.int32, (tk, tk), 1)).astype(BF16)
    return jnp.concatenate([tri, tri], axis=0)


def _attn_fwd(q, kt, v, *, name):
    L = q.shape[0]
    tk = ATTN_BLOCK
    tq = _pick(L, (ATTN_Q_ROWS, tk))
    nsub = tq // tk
    scale = 1.0 / math.sqrt(HEAD_DIM)

    def body(q_ref, kt_ref, v_ref, o_ref, acc_ref):
        i = pl.program_id(1)
        after = _tri(lambda r, c: r > c)
        rowq = lax.broadcasted_iota(jnp.int32, (tq, tk), 0)
        colq = lax.broadcasted_iota(jnp.int32, (tq, tk), 1)
        qs = _masked_bf16(q_ref[...].astype(F32) * scale, _head_masks((tq, LANES), 1))
        vmasks = _head_masks((tk, LANES), 1)
        acc_ref[...] = jnp.zeros_like(acc_ref)

        def tile(r0, strict, cs):
            kt_j = kt_ref[:, pl.ds(r0, tk)].astype(BF16)
            vs = _masked_bf16(v_ref[pl.ds(r0, tk), :], vmasks)
            out, acc = [], None
            for qh, vh, c in zip(qs, vs, cs):
                _, w, tot = _attn_weights(qh, kt_j, strict, after, c)
                part = _dot(w.astype(BF16), vh)
                acc = part if acc is None else acc + part
                out.append(c + tot)
            acc_ref[...] += acc
            return tuple(out)

        cs = tuple(jnp.zeros((tq, 1), F32) for _ in qs)
        for jd in reversed(range(nsub)):
            cs = tile(pl.multiple_of(i * tq + jd * tk, tk), (colq + jd * tk) < rowq, cs)

        def step(ii, cs):
            for jd in reversed(range(nsub)):
                cs = tile(pl.multiple_of((i - 1 - ii) * tq + jd * tk, tk), None, cs)
            return cs

        lax.fori_loop(0, i, step, cs)
        o_ref[...] = acc_ref[...].astype(o_ref.dtype)

    blk = pl.BlockSpec((tq, LANES), lambda h, i: (i, h))
    whole = pl.BlockSpec((L, LANES), lambda h, i: (0, h))
    whole_t = pl.BlockSpec((LANES, L), lambda h, i: (h, 0))
    return _pallas(
        body, name=name, grid=(N_HEADS // HEADS_PER_BLOCK, L // tq), in_specs=[blk, whole_t, whole], out_specs=blk,
        out_shape=jax.ShapeDtypeStruct(q.shape, BF16),
        scratch_shapes=[pltpu.VMEM((tq, LANES), F32)],
        compiler_params=_cparams("parallel", "arbitrary"),
    )(q, kt, v)


def _attn_bwd(q, k, kt, vt, do, *, name):
    L = q.shape[0]
    tk = ATTN_BLOCK
    tq = _pick(L, (ATTN_Q_ROWS, tk))
    nsub = tq // tk
    scale = 1.0 / math.sqrt(HEAD_DIM)

    def body(q_ref, k_ref, kt_ref, vt_ref, do_ref, dq_ref, dkt_ref, dvt_ref, e_scr, sig_scr, dq_acc_ref):
        i = pl.program_id(1)

        @pl.when(i == 0)
        def _():
            dkt_ref[...] = jnp.zeros_like(dkt_ref)
            dvt_ref[...] = jnp.zeros_like(dvt_ref)

        after = _tri(lambda r, c: r > c)
        before = _tri(lambda r, c: r < c)
        rowq = lax.broadcasted_iota(jnp.int32, (tq, tk), 0)
        colq = lax.broadcasted_iota(jnp.int32, (tq, tk), 1)
        qmasks = _head_masks((tq, LANES), 1)
        tmasks = _head_masks((LANES, tq), 0)
        kmasks = _head_masks((tk, LANES), 1)
        q_scaled = q_ref[...].astype(F32) * scale
        do32 = do_ref[...].astype(F32)
        qs = _masked_bf16(q_scaled, qmasks)
        dos = _masked_bf16(do32, qmasks)
        qts = _masked_bf16(q_scaled.T, tmasks)
        dots = _masked_bf16(do32.T, tmasks)
        dq_acc_ref[...] = jnp.zeros_like(dq_acc_ref)

        def tile_left(jb, strict, cs):
            r0 = pl.multiple_of(jb * tk, tk)
            kt_j = kt_ref[:, pl.ds(r0, tk)].astype(BF16)
            vt_j = vt_ref[:, pl.ds(r0, tk)].astype(BF16)
            out, acc = [], None
            for hh, (qh, dob, dot_h, c) in enumerate(zip(qs, dos, dots, cs)):
                lb, w, tot = _attn_weights(qh, kt_j, strict, after, c)
                e_scr[hh, jb] = _dot(dob, vt_j) * w
                sig_scr[hh, jb] = jnp.exp(lb).astype(BF16)
                part = _dot(dot_h, w.astype(BF16))
                acc = part if acc is None else acc + part
                out.append(c + tot)
            dvt_ref[:, pl.ds(r0, tk)] += acc
            return tuple(out)

        def tile_right(jb, strict, cs):
            r0 = pl.multiple_of(jb * tk, tk)
            ks = _masked_bf16(k_ref[pl.ds(r0, tk), :], kmasks)
            out, dq_acc, dk_acc = [], None, None
            for hh, (kh, qt_h, c2) in enumerate(zip(ks, qts, cs)):
                e = e_scr[hh, jb]
                sig = sig_scr[hh, jb].astype(F32)
                left = _tri_sum(e, before)
                dz = e - (e + left + c2) * sig
                if strict is not None:
                    dz = jnp.where(strict, dz, 0.0)
                dz = dz.astype(BF16)
                p1, p2 = _dot(dz, kh), _dot(qt_h, dz)
                dq_acc = p1 if dq_acc is None else dq_acc + p1
                dk_acc = p2 if dk_acc is None else dk_acc + p2
                out.append(c2 + left[:, tk - 1:tk] + e[:, tk - 1:tk])
            dq_acc_ref[...] += dq_acc
            dkt_ref[:, pl.ds(r0, tk)] += dk_acc
            return tuple(out)

        zeros = tuple(jnp.zeros((tq, 1), F32) for _ in qs)
        cs = zeros
        for jd in reversed(range(nsub)):
            cs = tile_left(i * nsub + jd, (colq + jd * tk) < rowq, cs)

        def left_step(ii, cs):
            for jd in reversed(range(nsub)):
                cs = tile_left((i - 1 - ii) * nsub + jd, None, cs)
            return cs

        def right_step(ii, cs):
            for jd in range(nsub):
                cs = tile_right(ii * nsub + jd, None, cs)
            return cs

        lax.fori_loop(0, i, left_step, cs)
        cs = lax.fori_loop(0, i, right_step, zeros)
        for jd in range(nsub):
            cs = tile_right(i * nsub + jd, (colq + jd * tk) < rowq, cs)
        dq_ref[...] = (dq_acc_ref[...] * scale).astype(dq_ref.dtype)

    nhb = N_HEADS // HEADS_PER_BLOCK
    blk = pl.BlockSpec((tq, LANES), lambda h, i: (i, h))
    whole = pl.BlockSpec((L, LANES), lambda h, i: (0, h))
    whole_t = pl.BlockSpec((LANES, L), lambda h, i: (h, 0))
    t_shape = jax.ShapeDtypeStruct((q.shape[1], L), F32)
    return _pallas(
        body, name=name, grid=(nhb, L // tq), in_specs=[blk, whole, whole_t, whole_t, blk],
        out_specs=[blk, whole_t, whole_t], out_shape=[jax.ShapeDtypeStruct(q.shape, BF16), t_shape, t_shape],
        scratch_shapes=[pltpu.VMEM((HEADS_PER_BLOCK, L // tk, tq, tk), F32), pltpu.VMEM((HEADS_PER_BLOCK, L // tk, tq, tk), BF16),
                        pltpu.VMEM((tq, LANES), F32)],
        compiler_params=_cparams("parallel", "arbitrary"),
    )(q, k, kt, vt, do)


def _make_loss_head(n_valid, name):
    def run(h, g, target):
        L, d = h.shape
        tm = _pick(L, (384, 128))

        def body(h_ref, t_ref, g_ref, loss_ref, dh_ref, dg_ref):
            i = pl.program_id(0)
            x = h_ref[...]
            gb = g_ref[...]
            rows = lax.broadcasted_iota(jnp.int32, (tm, 1), 0) + i * tm
            valid = jnp.logical_and(rows >= N_META, rows < n_valid)
            r = lax.rsqrt(jnp.mean(x * x, axis=-1, keepdims=True) + RMS_EPS)
            xh = x * r
            err = jnp.where(valid, xh * gb - t_ref[...], 0.0)
            dy = err * (1.0 / d)
            dxh = dy * gb
            dh_ref[...] = r * (dxh - xh * jnp.mean(dxh * xh, axis=-1, keepdims=True))
            part = 0.5 / d * jnp.sum(jnp.sum(err * err, axis=1, keepdims=True), axis=0, keepdims=True)
            dgp = _colsum(dy * xh)

            @pl.when(i == 0)
            def _():
                loss_ref[...] = part
                dg_ref[...] = dgp

            @pl.when(i > 0)
            def _():
                loss_ref[...] += part
                dg_ref[...] += dgp

        rowspec = pl.BlockSpec((tm, d), lambda i: (i, 0))
        return _pallas(
            body, name=name, grid=(L // tm,),
            in_specs=[rowspec, rowspec, pl.BlockSpec((1, d), lambda i: (0, 0))],
            out_specs=[pl.BlockSpec((1, 1), lambda i: (0, 0)), rowspec, pl.BlockSpec((1, d), lambda i: (0, 0))],
            out_shape=[jax.ShapeDtypeStruct((1, 1), F32), jax.ShapeDtypeStruct((L, d), F32), jax.ShapeDtypeStruct((1, d), F32)],
            compiler_params=_cparams("arbitrary"),
        )(h, target, g.reshape(1, d))

    return run


def _mm(a, b, *, ta=False, tb=False, out_dtype=F32, extras=(), epilogue=None, name):
    n = b.shape[0] if tb else b.shape[1]
    return _gmm([(a, ta)], [(b, tb)], [[(0, 0, 0)]], n=n, extras=extras, epilogue=epilogue, out_dtypes=(out_dtype,), name=name)[0]


def _add_epilogue(acc, resid):
    return (acc + resid,)


def _make_ffn_block(name):
    def fwd(h, gain, w_in, w_out):
        dff = w_out.shape[0]
        f = _rmsnorm_fwd(h, gain, out_dtype=BF16, name=name + "_norm")

        def swiglu(g, u):
            return g, u, g * _sigmoid(g) * u

        g, u, a = _gmm([(f, False)], [(w_in, False)], [[(0, 0, 0)], [(0, 0, dff)]], n=dff, epilogue=swiglu,
                       out_dtypes=(BF16, BF16, BF16), name=name + "_in")
        h2 = _mm(a, w_out, extras=[h], epilogue=_add_epilogue, name=name + "_out")
        return h2, (h, gain, f, g, u, a, w_in, w_out)

    def bwd(res, dh2, emit):
        h, gain, f, g, u, a, w_in, w_out = res
        dff, dm = w_out.shape

        def dswiglu(da, gb, ub):
            gb, ub = gb.astype(F32), ub.astype(F32)
            s = _sigmoid(gb)
            return da * ub * (s + gb * s * (1.0 - s)), da * gb * s

        dg, du = _gmm([(dh2, False)], [(w_out, True)], [[(0, 0, 0)]], n=dff, extras=[g, u], epilogue=dswiglu,
                      out_dtypes=(BF16, BF16), name=name + "_da")
        dw_out = _mm(a, dh2, ta=True, name=name + "_dwout")
        (df,) = _gmm([(dg, False), (du, False)], [(w_in, True)], [[(0, 0, 0), (1, 0, dff)]], n=dm, name=name + "_df")
        dw_in = jnp.concatenate([_mm(f, dg, ta=True, name=name + "_dwg"), _mm(f, du, ta=True, name=name + "_dwu")], axis=1)
        tok = emit({"w_ffn_in": dw_in, "w_ffn_out": dw_out})
        dh, dgain = _rmsnorm_bwd(h, gain + tok, df, dh2, name=name + "_dnorm")
        return dh, dgain

    return fwd, bwd


def _make_s5_out_block(name):
    def post(yb, ub, db):
        s = yb + db * ub
        return (0.5 * s * (1.0 + jnp.tanh(GELU_K * (s + GELU_C * s * s * s))),)

    def fwd(y, u, d, h, w_glu):
        L, dm = y.shape
        (z,) = _rowwise(post, [y, u], [d.reshape(1, dm)], [((L, dm), BF16)], [], tm=_pick(L, (384, 128)), name=name + "_gelu")

        def glu(val, gate, resid):
            return resid + val * _sigmoid(gate), val, gate

        h2, val, gate = _gmm([(z, False)], [(w_glu, False)], [[(0, 0, 0)], [(0, 0, dm)]], n=dm, extras=[h], epilogue=glu,
                             out_dtypes=(F32, BF16, BF16), name=name + "_glu")
        return h2, (y, u, d, z, val, gate, w_glu)

    def bwd(res, dh2, emit):
        y, u, d, z, val, gate, w_glu = res
        L, dm = y.shape

        def dglu(dh2b, valb, gateb):
            s = _sigmoid(gateb.astype(F32))
            return dh2b * s, dh2b * valb.astype(F32) * s * (1.0 - s)

        dval, dgate = _rowwise(dglu, [dh2, val, gate], [], [((L, dm), BF16)] * 2, [], tm=_pick(L, (384, 128)), name=name + "_dglu")
        (dz,) = _gmm([(dval, False), (dgate, False)], [(w_glu, True)], [[(0, 0, 0), (1, 0, dm)]], n=dm, name=name + "_dz")

        def dpost(yb, ub, dzb, db):
            s = yb + db * ub
            t = jnp.tanh(GELU_K * (s + GELU_C * s * s * s))
            ds = dzb * (0.5 * (1.0 + t) + 0.5 * s * (1.0 - t * t) * GELU_K * (1.0 + 3.0 * GELU_C * s * s))
            return ds, ds * db, _colsum(ds * ub)

        dw = jnp.concatenate([_mm(z, dval, ta=True, name=name + "_dwv"), _mm(z, dgate, ta=True, name=name + "_dwg")], axis=1)
        tok = emit({"s5_w_glu": dw})
        dy, du, dd = _rowwise(dpost, [y, u, dz], [(d + tok).reshape(1, dm)], [((L, dm), F32)] * 2, [((1, dm), F32)],
                              tm=_pick(L, (384, 128)), name=name + "_dgelu")
        return dy, du, dd.reshape(d.shape)

    return fwd, bwd


def _make_attn_layer(name):
    def fwd(h, g_kv, g_q, w_k, w_v, w_q, w_o):
        dm = h.shape[1]
        kvn = _rmsnorm_fwd(h, g_kv, out_dtype=BF16, name=name + "_kvnorm")
        qn = _rmsnorm_fwd(h, g_q, out_dtype=BF16, name=name + "_qnorm")
        k, v = _gmm([(kvn, False)], [(w_k, False), (w_v, False)], [[(0, 0, 0)], [(0, 1, 0)]], n=dm,
                    out_dtypes=(BF16, BF16), name=name + "_kv")
        q = _mm(qn, w_q, out_dtype=BF16, name=name + "_q")
        kt, vt = k.T, v.T
        o = _attn_fwd(q, kt, v, name=name + "_fwd")
        h2 = _mm(o, w_o, extras=[h], epilogue=_add_epilogue, name=name + "_o")
        return h2, (h, g_kv, g_q, kvn, qn, q, k, kt, vt, o, w_k, w_v, w_q, w_o)

    def bwd(res, dh2, emit):
        h, g_kv, g_q, kvn, qn, q, k, kt, vt, o, w_k, w_v, w_q, w_o = res
        dm = h.shape[1]
        do = _mm(dh2, w_o, tb=True, out_dtype=BF16, name=name + "_do")
        dw_o = _mm(o, dh2, ta=True, name=name + "_dwo")
        dq, dkt, dvt = _attn_bwd(q, k, kt, vt, do, name=name + "_bwd")
        dqn = _mm(dq, w_q, tb=True, name=name + "_dqn")
        dw_q = _mm(qn, dq, ta=True, name=name + "_dwq")
        (dkvn,) = _gmm([(dkt, True), (dvt, True)], [(w_k, True), (w_v, True)], [[(0, 0, 0), (1, 1, 0)]], n=dm, name=name + "_dkvn")
        dw_k = _mm(kvn, dkt, ta=True, tb=True, name=name + "_dwk")
        dw_v = _mm(kvn, dvt, ta=True, tb=True, name=name + "_dwv")
        tok = emit({"w_k": dw_k, "w_v": dw_v, "w_q": dw_q, "w_o": dw_o})
        dh, dg_q = _rmsnorm_bwd(h, g_q + tok, dqn, dh2, name=name + "_dqnorm")
        dh, dg_kv = _rmsnorm_bwd(h, g_kv, dkvn, dh, name=name + "_dkvnorm")
        return dh, dg_kv, dg_q

    return fwd, bwd


S5_PARAMS = ("s5_a_re", "s5_a_im", "s5_log_dt", "s5_b_re", "s5_b_im", "s5_c_re", "s5_c_im")


def _forward_backward(x_pad, target_pad, n_valid, small, get_weights, put_grads):
    gs = {}
    h0 = x_pad.at[:N_META].set(small["meta_tokens"])

    ssm_fwd, ssm_bwd = _make_ssm("ssm")
    s5out_fwd, s5out_bwd = _make_s5_out_block("s5out")
    ffn_fwd, ffn_bwd = zip(_make_ffn_block("ffn0"), _make_ffn_block("ffn1"))
    attn_fwd, attn_bwd = _make_attn_layer("attn")
    u = _rmsnorm_fwd(h0, small["norm_mix"][0], out_dtype=F32, name="mix0_norm")
    disc, disc_vjp = jax.vjp(_s5_discretize, *[small[n][0] for n in S5_PARAMS])
    y, ssm_res = ssm_fwd(u, *disc)
    w = get_weights("s5", y)
    h1, s5out_res = s5out_fwd(y, u, small["s5_d"][0], h0, w["s5_w_glu"])
    w = get_weights("ffn0", h1)
    h2, ffn0_res = ffn_fwd[0](h1, small["norm_ffn"][0], w["w_ffn_in"], w["w_ffn_out"])
    w = get_weights("attn", h2)
    h3, attn_res = attn_fwd(h2, small["norm_kv"], small["norm_mix"][1], w["w_k"], w["w_v"], w["w_q"], w["w_o"])
    w = get_weights("ffn1", h3)
    h4, ffn1_res = ffn_fwd[1](h3, small["norm_ffn"][1], w["w_ffn_in"], w["w_ffn_out"])
    loss, dh, dg_final = _make_loss_head(n_valid, "loss_head")(h4, small["norm_final"], target_pad)
    gs["norm_final"] = dg_final.reshape(small["norm_final"].shape)

    dh, dg_ffn1 = ffn_bwd[1](ffn1_res, dh, functools.partial(put_grads, "ffn1"))
    dh, gs["norm_kv"], dg_mix1 = attn_bwd(attn_res, dh, functools.partial(put_grads, "attn"))
    dh, dg_ffn0 = ffn_bwd[0](ffn0_res, dh, functools.partial(put_grads, "ffn0"))
    dy, du, dd = s5out_bwd(s5out_res, dh, functools.partial(put_grads, "s5"))
    du_ssm, *ddisc = ssm_bwd(ssm_res, dy)
    dh, dg_mix0 = _rmsnorm_bwd(h0, small["norm_mix"][0], du + du_ssm, dh, name="mix0_dnorm")
    for n, g in zip(S5_PARAMS, disc_vjp(tuple(ddisc))):
        gs[n] = g[None]
    gs["s5_d"] = dd.reshape(small["s5_d"].shape)
    gs["norm_mix"] = jnp.stack([dg_mix0, dg_mix1])
    gs["norm_ffn"] = jnp.stack([dg_ffn0, dg_ffn1])
    gs["meta_tokens"] = dh[:N_META]
    return loss, dh, gs


def _coords():
    return lax.axis_index("x"), lax.axis_index("y"), lax.axis_index("c")


def _flip(bits):
    x, y, c = _coords()
    fx, fy, fc = bits
    return (x ^ fx if fx else x, y ^ fy if fy else y, c ^ fc if fc else c)


def _exchange(ins, out_shapes, copies, local_copies, *, aliases=None, name):
    n_in, n_out = len(ins), len(out_shapes)
    n_cp, n_loc = len(copies), len(local_copies)
    aliases = aliases or {}

    def body(*refs):
        in_refs, out_refs = refs[:n_in], refs[n_in:n_in + n_out]
        send_sems, recv_sems, loc_sems = refs[n_in + n_out:]
        me = _coords()
        locs = []
        for n, (ii, oi, dfn) in enumerate(local_copies):
            cp = pltpu.make_async_copy(in_refs[ii], out_refs[oi].at[dfn(*me)], loc_sems.at[n])
            cp.start()
            locs.append(cp)
        sends = []
        for n, (ii, sfn, bits, oi, dfn) in enumerate(copies):
            src = in_refs[ii] if sfn is None else in_refs[ii].at[sfn(*me)]
            cp = pltpu.make_async_remote_copy(
                src_ref=src, dst_ref=out_refs[oi].at[dfn(*me)], send_sem=send_sems.at[n], recv_sem=recv_sems.at[n],
                device_id=_flip(bits), device_id_type=MESH)
            cp.start()
            sends.append(cp)
        for n, (ii, sfn, bits, oi, dfn) in enumerate(copies):
            peer = _flip(bits)
            src = in_refs[ii] if sfn is None else in_refs[ii].at[sfn(*me)]
            pltpu.make_async_remote_copy(
                src_ref=src, dst_ref=out_refs[oi].at[dfn(*peer)], send_sem=send_sems.at[n], recv_sem=recv_sems.at[n],
                device_id=peer, device_id_type=MESH).wait_recv()
        for cp in sends:
            cp.wait_send()
        for cp in locs:
            cp.wait()

    any_spec = pl.BlockSpec(memory_space=pl.ANY)
    return _pallas(
        body, name=name, in_specs=[any_spec] * n_in, out_specs=[any_spec] * n_out,
        out_shape=[jax.ShapeDtypeStruct(s, d) for s, d in out_shapes],
        scratch_shapes=[pltpu.SemaphoreType.DMA((max(n_cp, 1),)), pltpu.SemaphoreType.DMA((max(n_cp, 1),)),
                        pltpu.SemaphoreType.DMA((max(n_loc, 1),))],
        input_output_aliases=aliases,
        compiler_params=pltpu.CompilerParams(has_side_effects=True),
    )(*ins)


ICI_FLIPS = ((1, 0, 0), (0, 1, 0), (1, 1, 0))
D2D_FLIP = (0, 0, 1)


def _slot_of(x, y, c):
    return 4 * x + 2 * y + c


def _all_gather(shards, *, name):
    n = len(shards)
    outs = [((N_DEV,) + s.shape, s.dtype) for s in shards]
    copies = [(t, None, bits, t, _slot_of) for t in range(n) for bits in ICI_FLIPS]
    local = [(t, t, _slot_of) for t in range(n)]
    bufs = _exchange(shards, outs, copies, local, name=name + "_ici")
    copies2 = [(t, (lambda x, y, c, q=q: 2 * q + c), D2D_FLIP, t, (lambda x, y, c, q=q: 2 * q + c))
               for t in range(n) for q in range(4)]
    return _exchange(bufs, outs, copies2, [], aliases={t: t for t in range(n)}, name=name + "_d2d")


ALL_FLIPS = tuple((m >> 2 & 1, m >> 1 & 1, m & 1) for m in range(1, N_DEV))
HBM_SPEC = pl.BlockSpec(memory_space=pltpu.HBM)
SEM_SPEC = pl.BlockSpec(memory_space=pltpu.SEMAPHORE)
SIDE_EFFECT = pltpu.SideEffectType.DATAFLOW_SIDE_EFFECTING


def _copy_desc(copies, n, src_refs, land_refs, send_sems, recv_sems, sender):
    si, sfn, bits, li, dfn = copies[n]
    src = src_refs[si] if sfn is None else src_refs[si].at[sfn(*sender)]
    return pltpu.make_async_remote_copy(
        src_ref=src, dst_ref=land_refs[li].at[dfn(*sender)], send_sem=send_sems.at[n], recv_sem=recv_sems.at[n],
        device_id=_flip(bits), device_id_type=MESH)


def _start_copies(srcs, lands, copies, *, name):
    n_s, n_l, n_c = len(srcs), len(lands), len(copies)

    def body(*refs):
        src_refs = refs[:n_s]
        send_sems, recv_sems = refs[n_s], refs[n_s + 1]
        land_refs = refs[2 * n_s + 2:2 * n_s + 2 + n_l]
        token = refs[-1]
        me = _coords()
        for n in range(n_c):
            _copy_desc(copies, n, src_refs, land_refs, send_sems, recv_sems, me).start()
        token[...] = jnp.zeros_like(token)

    res = _pallas(
        body, name=name,
        out_shape=(pltpu.SemaphoreType.DMA((n_c,)), pltpu.SemaphoreType.DMA((n_c,)),
                   *[pltpu.HBM(s.shape, s.dtype) for s in srcs], *[pltpu.HBM(shape, dtype) for shape, dtype in lands],
                   jax.ShapeDtypeStruct((8, LANES), F32)),
        in_specs=[HBM_SPEC] * n_s,
        out_specs=(SEM_SPEC, SEM_SPEC, *[HBM_SPEC] * (n_s + n_l), pl.BlockSpec(memory_space=pltpu.VMEM)),
        input_output_aliases={i: 2 + i for i in range(n_s)},
        compiler_params=pltpu.CompilerParams(has_side_effects=SIDE_EFFECT),
    )(*[pltpu.with_memory_space_constraint(s, pltpu.HBM) for s in srcs])
    return res[0], res[1], list(res[2:2 + n_s]), list(res[2 + n_s:2 + n_s + n_l]), res[-1]


def _wait_copies(send_sems, recv_sems, srcs, lands, copies, which, after, *, name):
    src_ids, land_ids = sorted(srcs), sorted(lands)
    n_s, n_l = len(src_ids), len(land_ids)
    srcs, lands = [srcs[i] for i in src_ids], [lands[i] for i in land_ids]

    def body(*refs):
        src_refs, land_refs = dict(zip(src_ids, refs[:n_s])), dict(zip(land_ids, refs[n_s:n_s + n_l]))
        s_sems, r_sems = refs[n_s + n_l], refs[n_s + n_l + 1]
        me = _coords()
        for n in which:
            cp = _copy_desc(copies, n, src_refs, land_refs, s_sems, r_sems, me)
            cp.wait_send()
            _copy_desc(copies, n, src_refs, land_refs, s_sems, r_sems, _flip(copies[n][2])).wait_recv()

    res = _pallas(
        body, name=name,
        out_shape=tuple(pltpu.HBM(b.shape, b.dtype) for b in list(srcs) + list(lands)),
        in_specs=[HBM_SPEC] * (n_s + n_l) + [SEM_SPEC, SEM_SPEC, pl.BlockSpec(memory_space=pl.ANY)],
        out_specs=tuple([HBM_SPEC] * (n_s + n_l)),
        input_output_aliases={i: i for i in range(n_s + n_l)},
        compiler_params=pltpu.CompilerParams(has_side_effects=SIDE_EFFECT),
    )(*srcs, *lands, send_sems, recv_sems, after)
    return dict(zip(src_ids, res[:n_s])), dict(zip(land_ids, res[n_s:]))


def _adam_math(w, g, m, v):
    m = ADAM_B1 * m + (1.0 - ADAM_B1) * g
    v = ADAM_B2 * v + (1.0 - ADAM_B2) * (g * g)
    m_hat = m / (1.0 - ADAM_B1 ** ADAM_STEP)
    v_hat = v / (1.0 - ADAM_B2 ** ADAM_STEP)
    delta = -ADAM_LR * (m_hat / (jnp.sqrt(v_hat) + ADAM_EPS) + ADAM_WD * w)
    return delta, m, v


def _shard_adamw(own, landing, w, m, v, *, name):
    r, cdim = w.shape
    tr = _pick(r, (256, 128, 64, 32, 16))

    def body(g_ref, l_ref, w_ref, m_ref, v_ref, g_out, d_out, m_out, v_out):
        g = g_ref[...]
        for n in range(N_DEV - 1):
            g = g + l_ref[n].astype(F32)
        d, mn, vn = _adam_math(w_ref[...], g, m_ref[...], v_ref[...])
        g_out[...] = g
        d_out[...] = d
        m_out[...] = mn
        v_out[...] = vn

    blk = pl.BlockSpec((tr, cdim), lambda i: (i, 0))
    return _pallas(
        body, name=name, grid=(r // tr,),
        in_specs=[blk, pl.BlockSpec((N_DEV - 1, tr, cdim), lambda i: (0, i, 0)), blk, blk, blk], out_specs=[blk] * 4,
        out_shape=[jax.ShapeDtypeStruct((r, cdim), F32)] * 4,
        compiler_params=_cparams("parallel"),
    )(own, landing, w, m, v)


def _small_adamw(slots, w, m, v, *, name):
    r, cdim = w.shape

    def body(s_ref, w_ref, m_ref, v_ref, g_out, d_out, m_out, v_out):
        g = s_ref[0]
        for n in range(1, N_DEV):
            g = g + s_ref[n]
        d, mn, vn = _adam_math(w_ref[...], g, m_ref[...], v_ref[...])
        g_out[...] = g
        d_out[...] = d
        m_out[...] = mn
        v_out[...] = vn

    tr = _pick(r, (256, 128, 64, 32, 16, 8))
    blk = pl.BlockSpec((tr, cdim), lambda i: (i, 0))
    return _pallas(
        body, name=name, grid=(r // tr,),
        in_specs=[pl.BlockSpec((N_DEV, tr, cdim), lambda i: (0, i, 0)), blk, blk, blk], out_specs=[blk] * 4,
        out_shape=[jax.ShapeDtypeStruct((r, cdim), F32)] * 4,
        compiler_params=_cparams("parallel"),
    )(slots, w, m, v)


def _plain_adamw(g, w, m, v, *, name):
    def body(g_ref, w_ref, m_ref, v_ref, d_out, m_out, v_out):
        d, mn, vn = _adam_math(w_ref[...], g_ref[...], m_ref[...], v_ref[...])
        d_out[...] = d
        m_out[...] = mn
        v_out[...] = vn

    spec = pl.BlockSpec(g.shape, lambda: (0,) * g.ndim)
    return _pallas(body, name=name, in_specs=[spec] * 4, out_specs=[spec] * 3,
                   out_shape=[jax.ShapeDtypeStruct(g.shape, F32)] * 3)(g, w, m, v)


def _cast_bf16(x, *, name):
    r, cdim = x.shape
    tr = _pick(r, (256, 128, 64, 32, 16))

    def body(x_ref, o_ref):
        o_ref[...] = x_ref[...].astype(BF16)

    return _pallas(body, name=name, grid=(r // tr,), in_specs=[pl.BlockSpec((tr, cdim), lambda i: (i, 0))],
                   out_specs=pl.BlockSpec((tr, cdim), lambda i: (i, 0)), out_shape=jax.ShapeDtypeStruct(x.shape, BF16),
                   compiler_params=_cparams("parallel"))(x)


SMALL_NAMES = ("norm_mix", "norm_ffn", "s5_a_re", "s5_a_im", "s5_log_dt", "s5_b_re", "s5_b_im", "s5_c_re", "s5_c_im",
               "norm_kv", "norm_final")


def _pack_rows(arrs):
    rows = []
    for a in arrs:
        flat = a.reshape(-1)
        pad = (-flat.shape[0]) % (8 * LANES)
        rows.append(jnp.pad(flat, (0, pad)).reshape(-1, LANES))
    return jnp.concatenate(rows, axis=0)


def _unpack_rows(packed, like):
    out, r0 = [], 0
    for a in like:
        n = math.prod(a.shape)
        nr = (n + 8 * LANES - 1) // (8 * LANES) * 8
        out.append(packed[r0:r0 + nr].reshape(-1)[:n].reshape(a.shape))
        r0 += nr
    return out


def kernel(x, meta_tokens, norm_mix, norm_ffn, s5_a_re, s5_a_im, s5_log_dt, s5_b_re, s5_b_im, s5_c_re, s5_c_im, s5_d, s5_w_glu, norm_kv, w_kv, w_q, w_o, w_ffn_in, w_ffn_out, norm_final, loss_target, m_meta_tokens, m_norm_mix, m_norm_ffn, m_s5_a_re, m_s5_a_im, m_s5_log_dt, m_s5_b_re, m_s5_b_im, m_s5_c_re, m_s5_c_im, m_s5_d, m_s5_w_glu, m_norm_kv, m_w_kv, m_w_q, m_w_o, m_w_ffn_in, m_w_ffn_out, m_norm_final, v_meta_tokens, v_norm_mix, v_norm_ffn, v_s5_a_re, v_s5_a_im, v_s5_log_dt, v_s5_b_re, v_s5_b_im, v_s5_c_re, v_s5_c_im, v_s5_d, v_s5_w_glu, v_norm_kv, v_w_kv, v_w_q, v_w_o, v_w_ffn_in, v_w_ffn_out, v_norm_final):
    args = dict(locals())
    seq = x.shape[1]
    n_valid = N_META + seq
    Lp = (n_valid + ATTN_BLOCK - 1) // ATTN_BLOCK * ATTN_BLOCK
    dm = D_MODEL
    my_slot = _slot_of(*_coords())

    n_ffn = w_ffn_in.shape[0]
    groups = {"s5": ["s5_w_glu"], "ffn0": ["w_ffn_in0", "w_ffn_out0"], "attn": ["w_kv", "w_q", "w_o"],
              "ffn1": ["w_ffn_in1", "w_ffn_out1"]}
    shards = {"s5_w_glu": s5_w_glu[0], "w_kv": w_kv, "w_q": w_q[0], "w_o": w_o[0]}
    for l in range(n_ffn):
        shards[f"w_ffn_in{l}"], shards[f"w_ffn_out{l}"] = w_ffn_in[l], w_ffn_out[l]
    by_columns = {"s5_w_glu", "w_kv", "w_ffn_in0", "w_ffn_in1"}
    names = [n for g in groups.values() for n in g]

    def gathered_full(n, g8):
        if n in by_columns:
            return jnp.swapaxes(g8, 0, 1).reshape(g8.shape[1], -1)
        return g8.reshape(-1, g8.shape[2])

    col_shard = jnp.concatenate([meta_tokens, s5_d, jnp.zeros((7, LANES), F32)], axis=0)
    ag_srcs = [col_shard] + [_cast_bf16(shards[n], name="cast_" + n) for n in names]
    ag_copies = [(t, None, bits, t, _slot_of) for t in range(len(ag_srcs)) for bits in ALL_FLIPS]
    ag_send, ag_recv, ag_srcs, ag_lands, ag_token = _start_copies(
        ag_srcs, [((N_DEV,) + s.shape, s.dtype) for s in ag_srcs], ag_copies, name="ag_start")

    def gather_wait(tag, ids, after):
        which = [t * len(ALL_FLIPS) + r for t in ids for r in range(len(ALL_FLIPS))]
        srcs, lands = _wait_copies(ag_send, ag_recv, {t: ag_srcs[t] for t in ids}, {t: ag_lands[t] for t in ids},
                                   ag_copies, which, after, name="ag_wait_" + tag)
        return {t: lax.dynamic_update_index_in_dim(lands[t], srcs[t], my_slot, 0) for t in ids}

    def get_weights(group, after):
        ids = [1 + names.index(n) for n in groups[group]]
        full = {names[t - 1]: gathered_full(names[t - 1], g8) for t, g8 in gather_wait(group, ids, after).items()}
        if group == "attn":
            w_kv_full = full.pop("w_kv")
            full["w_k"], full["w_v"] = w_kv_full[:, :dm], w_kv_full[:, dm:]
        return {n.rstrip("01") if n.startswith("w_ffn") else n: w for n, w in full.items()}

    gcol = gather_wait("cols", [0], ag_token)[0]
    small = {n: args[n] for n in SMALL_NAMES}
    small["meta_tokens"] = jnp.swapaxes(gcol[:, :N_META], 0, 1).reshape(N_META, dm)
    small["s5_d"] = gcol[:, N_META].reshape(1, dm)

    scatters = {}

    def put_grads(group, grads):
        if group == "attn":
            grads = {"w_kv": jnp.concatenate([grads.pop("w_k"), grads.pop("w_v")], axis=1), **grads}
        own, send = {}, []
        for n in groups[group]:
            g = grads[n.rstrip("01") if n.startswith("w_ffn") else n]
            r, c = shards[n].shape
            if n in by_columns:
                own[n] = lax.dynamic_slice_in_dim(g, my_slot * c, c, axis=1)
                send.append(jnp.swapaxes(g.reshape(r, N_DEV, c), 0, 1).astype(BF16))
            else:
                own[n] = lax.dynamic_slice_in_dim(g, my_slot * r, r, axis=0)
                send.append(g.reshape(N_DEV, r, c).astype(BF16))
        copies = [(t, (lambda x, y, c, b=bits: _slot_of(x ^ b[0], y ^ b[1], c ^ b[2])), bits, t, (lambda x, y, c, k=k: k))
                  for t in range(len(send)) for k, bits in enumerate(ALL_FLIPS)]
        lands = [((len(ALL_FLIPS),) + s.shape[1:], BF16) for s in send]
        s_sem, r_sem, srcs, lands, token = _start_copies(send, lands, copies, name="rs_start_" + group)
        scatters[group] = (s_sem, r_sem, srcs, lands, copies, own)
        return token[0, 0]

    x_pad = jnp.zeros((Lp, dm), F32).at[N_META:n_valid].set(x[0])
    t_pad = jnp.zeros((Lp, dm), F32).at[N_META:n_valid].set(loss_target[0])
    loss_local, g_xpad, g_small = _forward_backward(x_pad, t_pad, n_valid, small, get_weights, put_grads)
    loss = lax.psum(loss_local[0, 0], ("x", "y", "c"))
    grad_x = g_xpad[N_META:n_valid][None]

    small_list = [g_small[n] for n in SMALL_NAMES] + [g_small["meta_tokens"], g_small["s5_d"]]
    small_bufs = _all_gather([_pack_rows(small_list)], name="small_ag")

    out, per_tensor = {}, {}
    for group, (s_sem, r_sem, srcs, lands, copies, own) in scatters.items():
        ids = list(range(len(srcs)))
        _, landed = _wait_copies(s_sem, r_sem, dict(zip(ids, srcs)), dict(zip(ids, lands)), copies, list(range(len(copies))),
                                 small_bufs[0], name="rs_wait_" + group)
        for t, n in enumerate(groups[group]):
            base, layer = (n[:-1], int(n[-1])) if n.startswith("w_ffn") else (n, None)
            pick = (lambda a: a[layer]) if layer is not None else (lambda a: a.reshape(shards[n].shape))
            per_tensor[n] = _shard_adamw(own[n], landed[t], shards[n], pick(args["m_" + base]), pick(args["v_" + base]),
                                         name="adamw_" + n)
    for base in ("s5_w_glu", "w_kv", "w_q", "w_o"):
        out[base] = tuple(a.reshape(args[base].shape) for a in per_tensor[base])
    for base in ("w_ffn_in", "w_ffn_out"):
        out[base] = tuple(jnp.stack([per_tensor[f"{base}{l}"][k] for l in range(n_ffn)]) for k in range(4))

    zeros_tail = [jnp.zeros_like(g_small["meta_tokens"]), jnp.zeros_like(g_small["s5_d"])]
    pw = _pack_rows([args[n] for n in SMALL_NAMES] + zeros_tail)
    pm = _pack_rows([args["m_" + n] for n in SMALL_NAMES] + zeros_tail)
    pv = _pack_rows([args["v_" + n] for n in SMALL_NAMES] + zeros_tail)
    sg, sd, sm, sv = _small_adamw(small_bufs[0], pw, pm, pv, name="adamw_small")
    like = small_list
    ug, ud, um, uv = (_unpack_rows(a, like) for a in (sg, sd, sm, sv))
    for i, n in enumerate(SMALL_NAMES):
        out[n] = (ug[i], ud[i], um[i], uv[i])
    g_meta = lax.dynamic_slice_in_dim(ug[-2], my_slot * LANES, LANES, axis=1)
    g_d = lax.dynamic_slice_in_dim(ug[-1].reshape(1, dm), my_slot * LANES, LANES, axis=1)
    pad7 = jnp.zeros((7, LANES), F32)
    gc = jnp.concatenate([g_meta, g_d, pad7], axis=0)
    wc = jnp.concatenate([meta_tokens, s5_d, pad7], axis=0)
    mc = jnp.concatenate([m_meta_tokens, m_s5_d, pad7], axis=0)
    vc = jnp.concatenate([v_meta_tokens, v_s5_d, pad7], axis=0)
    dc, mcn, vcn = _plain_adamw(gc, wc, mc, vc, name="adamw_cols")
    out["meta_tokens"] = (g_meta, dc[:N_META], mcn[:N_META], vcn[:N_META])
    out["s5_d"] = (g_d, dc[N_META:N_META + 1], mcn[N_META:N_META + 1], vcn[N_META:N_META + 1])

    order = ["meta_tokens", "norm_mix", "norm_ffn", "s5_a_re", "s5_a_im", "s5_log_dt", "s5_b_re", "s5_b_im", "s5_c_re",
             "s5_c_im", "s5_d", "s5_w_glu", "norm_kv", "w_kv", "w_q", "w_o", "w_ffn_in", "w_ffn_out", "norm_final"]
    res = [loss, grad_x]
    for k in range(4):
        res += [out[n][k] for n in order]
    return tuple(res)
```

```python
import functools
import math

import jax
import jax.numpy as jnp
from jax import lax
from jax.experimental import pallas as pl
from jax.experimental.pallas import tpu as pltpu

F32 = jnp.float32
BF16 = jnp.bfloat16

N_META = 16
D_MODEL = 1024
S5_GROUPS = 64
S5_GROUP = 16
S5_STATE = 64
N_HEADS = 16
HEAD_DIM = 64
D_FF = 2816
RMS_EPS = 1e-6
ADAM_LR, ADAM_B1, ADAM_B2, ADAM_EPS, ADAM_WD, ADAM_STEP = 0.001, 0.9, 0.999, 1e-08, 0.01, 10

LANES = 128
ATTN_BLOCK = 128
GROUPS_PER_BLOCK = 8
N_DEV = 8
MESH = pl.DeviceIdType.MESH
VMEM_LIMIT = 56 * 1024 * 1024


def _pallas(body, **kw):
    return pl.pallas_call(body, **kw)


def _cparams(*sem):
    return pltpu.CompilerParams(dimension_semantics=sem, vmem_limit_bytes=VMEM_LIMIT)


def _pick(n, prefs):
    for p in prefs:
        if n % p == 0:
            return p
    return n


def _dot(a, b, ca=1, cb=0):
    return lax.dot_general(a, b, (((ca,), (cb,)), ((), ())), preferred_element_type=F32)


MATMUL_VMEM_BUDGET = 36 * 1024 * 1024
MATMUL_TILES = (1408, 1024, 512, 384, 256, 128)


def _matmul_tiles(M, N, K, a_bytes, b_bytes, tile_bytes, n_acc, offsets):
    best = None
    for tm in (t for t in MATMUL_TILES if M % t == 0):
        for tn in (t for t in MATMUL_TILES if N % t == 0):
            for tk in (t for t in MATMUL_TILES if K % t == 0):
                if any(off % (tn if along_n else tk) for off, along_n in offsets):
                    continue
                vmem = 2 * (tm * tk * a_bytes + tk * tn * b_bytes + tm * tn * tile_bytes) + n_acc * tm * tn * 4
                if vmem > MATMUL_VMEM_BUDGET:
                    continue
                traffic = M * K * a_bytes * (N // tn) + K * N * b_bytes * (M // tm) + M * N * tile_bytes
                traffic += (K // tk - 1) * n_acc * M * N * 8 // 3
                key = (traffic, -tk, -tm)
                if best is None or key < best[0]:
                    best = (key, (tm, tn, tk))
    return best[1]


def _gmm(a_list, b_list, accs, *, n, extras=(), epilogue=None, out_dtypes=(F32,), name):
    a0, ta0 = a_list[0]
    K, M = a0.shape if ta0 else a0.shape[::-1]
    N = n
    b_keys = sorted({(bi, off) for terms in accs for _, bi, off in terms})
    a_bytes = sum(a.dtype.itemsize for a, _ in a_list)
    b_bytes = sum(b_list[bi][0].dtype.itemsize for bi, _ in b_keys)
    tile_bytes = sum(e.dtype.itemsize for e in extras) + sum(jnp.dtype(d).itemsize for d in out_dtypes)
    offsets = [(off, not b_list[bi][1]) for bi, off in b_keys]
    tm, tn, tk = _matmul_tiles(M, N, K, a_bytes, b_bytes, tile_bytes, len(accs), offsets)
    nk = K // tk
    n_a, n_b, n_e, n_o, n_acc = len(a_list), len(b_keys), len(extras), len(out_dtypes), len(accs)
    if epilogue is None:
        epilogue = lambda *vals: vals

    def body(*refs):
        a_refs, b_refs = refs[:n_a], refs[n_a:n_a + n_b]
        e_refs = refs[n_a + n_b:n_a + n_b + n_e]
        o_refs = refs[n_a + n_b + n_e:n_a + n_b + n_e + n_o]
        acc_refs = refs[n_a + n_b + n_e + n_o:]
        a_vals = [r[...].astype(BF16) for r in a_refs]
        b_vals = {key: r[...].astype(BF16) for key, r in zip(b_keys, b_refs)}
        parts = []
        for terms in accs:
            p = None
            for ai, bi, off in terms:
                d = _dot(a_vals[ai], b_vals[(bi, off)], 0 if a_list[ai][1] else 1, 1 if b_list[bi][1] else 0)
                p = d if p is None else p + d
            parts.append(p)

        def finish(vals):
            outs = epilogue(*vals, *[r[...] for r in e_refs])
            for r, val in zip(o_refs, outs):
                r[...] = val.astype(r.dtype)

        if nk == 1:
            finish(parts)
        else:
            k = pl.program_id(2)

            @pl.when(k == 0)
            def _():
                for r, p in zip(acc_refs, parts):
                    r[...] = p

            @pl.when(k > 0)
            def _():
                for r, p in zip(acc_refs, parts):
                    r[...] += p

            @pl.when(k == nk - 1)
            def _():
                finish([r[...] for r in acc_refs])

    in_specs = [pl.BlockSpec((tk, tm), lambda i, j, k: (k, i)) if ta else pl.BlockSpec((tm, tk), lambda i, j, k: (i, k))
                for _, ta in a_list]
    for bi, off in b_keys:
        if b_list[bi][1]:
            in_specs.append(pl.BlockSpec((tn, tk), lambda i, j, k, o=off // tk: (j, k + o)))
        else:
            in_specs.append(pl.BlockSpec((tk, tn), lambda i, j, k, o=off // tn: (k, j + o)))
    tile = pl.BlockSpec((tm, tn), lambda i, j, k: (i, j))
    in_specs += [tile] * n_e
    return _pallas(
        body, name=name, grid=(M // tm, N // tn, nk), in_specs=in_specs, out_specs=[tile] * n_o,
        out_shape=[jax.ShapeDtypeStruct((M, N), d) for d in out_dtypes],
        scratch_shapes=[pltpu.VMEM((tm, tn), F32)] * (n_acc if nk > 1 else 0),
        compiler_params=_cparams("parallel", "parallel", "arbitrary"),
    )(*[a for a, _ in a_list], *[b_list[bi][0] for bi, _ in b_keys], *extras)


def _rowwise(fn, row_ins, full_ins, row_outs, acc_outs, *, tm, name):
    L = row_ins[0].shape[0]
    n_row, n_full, n_ro = len(row_ins), len(full_ins), len(row_outs)

    def body(*refs):
        ins = [r[...] for r in refs[:n_row + n_full]]
        outs = refs[n_row + n_full:]
        res = fn(*ins)
        for r, val in zip(outs[:n_ro], res[:n_ro]):
            r[...] = val.astype(r.dtype)
        if acc_outs:
            first = pl.program_id(0) == 0
            for r, val in zip(outs[n_ro:], res[n_ro:]):
                @pl.when(first)
                def _(r=r, val=val):
                    r[...] = val

                @pl.when(jnp.logical_not(first))
                def _(r=r, val=val):
                    r[...] += val

    in_specs = [pl.BlockSpec((tm, a.shape[1]), lambda i: (i, 0)) for a in row_ins]
    in_specs += [pl.BlockSpec(a.shape, lambda i, nd=a.ndim: (0,) * nd) for a in full_ins]
    out_specs = [pl.BlockSpec((tm, s[1]), lambda i: (i, 0)) for s, _ in row_outs]
    out_specs += [pl.BlockSpec(s, lambda i, nd=len(s): (0,) * nd) for s, _ in acc_outs]
    out_shape = [jax.ShapeDtypeStruct(s, d) for s, d in list(row_outs) + list(acc_outs)]
    return _pallas(
        body, name=name, grid=(L // tm,), in_specs=in_specs, out_specs=out_specs, out_shape=out_shape,
        compiler_params=_cparams("arbitrary" if acc_outs else "parallel"),
    )(*row_ins, *full_ins)


def _colsum(x):
    return jnp.sum(x, axis=0, keepdims=True)


def _rmsnorm_fwd(x, g, *, out_dtype, name):
    L, d = x.shape

    def f(xb, gb):
        r = lax.rsqrt(jnp.mean(xb * xb, axis=-1, keepdims=True) + RMS_EPS)
        return (xb * r * gb,)

    return _rowwise(f, [x], [g.reshape(1, d)], [((L, d), out_dtype)], [], tm=_pick(L, (384, 128)), name=name)[0]


def _rmsnorm_bwd(x, g, dy, resid, *, name):
    L, d = x.shape

    def f(xb, dyb, *rest):
        gb = rest[-1]
        r = lax.rsqrt(jnp.mean(xb * xb, axis=-1, keepdims=True) + RMS_EPS)
        xh = xb * r
        dxh = dyb * gb
        dx = r * (dxh - xh * jnp.mean(dxh * xh, axis=-1, keepdims=True))
        if resid is not None:
            dx = dx + rest[0]
        return dx, _colsum(dyb * xh)

    rows = [x, dy] + ([resid] if resid is not None else [])
    dx, dg = _rowwise(f, rows, [g.reshape(1, d)], [((L, d), F32)], [((1, d), F32)], tm=_pick(L, (384, 128)), name=name)
    return dx, dg.reshape(g.shape)


def _sigmoid(x):
    return 1.0 / (1.0 + jnp.exp(-x))


GELU_K = math.sqrt(2.0 / math.pi)
GELU_C = 0.044715


def _bdmm(a_list, w_rows, *, out_dtype=F32, name):
    L = a_list[0].shape[0]
    nb, ka, kb = w_rows[0][0].shape
    tm = _pick(L, (1408, 384, 128))
    n_a, n_o = len(a_list), len(w_rows)
    ws = [w for row in w_rows for w in row]

    def body(*refs):
        a_vals = [r[...].astype(BF16) for r in refs[:n_a]]
        w_refs = refs[n_a:n_a + n_a * n_o]
        outs = refs[n_a + n_a * n_o:]
        for o in range(n_o):
            acc = None
            for i in range(n_a):
                p = _dot(a_vals[i], w_refs[o * n_a + i][...].astype(BF16))
                acc = p if acc is None else acc + p
            outs[o][...] = acc.astype(out_dtype)

    return _pallas(
        body, name=name, grid=(L // tm, nb),
        in_specs=[pl.BlockSpec((tm, ka), lambda i, j: (i, j)) for _ in a_list]
        + [pl.BlockSpec((None, ka, kb), lambda i, j: (j, 0, 0)) for _ in ws],
        out_specs=[pl.BlockSpec((tm, kb), lambda i, j: (i, j)) for _ in range(n_o)],
        out_shape=[jax.ShapeDtypeStruct((L, nb * kb), out_dtype) for _ in range(n_o)],
        compiler_params=_cparams("parallel", "parallel"),
    )(*a_list, *ws)


def _bdmm_tn(a, g, nb, *, name):
    L = a.shape[0]
    ka, kb = a.shape[1] // nb, g.shape[1] // nb
    tm = _pick(L, (1408, 384, 128))

    def body(a_ref, g_ref, o_ref):
        i = pl.program_id(1)
        part = _dot(a_ref[...].astype(BF16), g_ref[...].astype(BF16), 0, 0)

        @pl.when(i == 0)
        def _():
            o_ref[...] = part

        @pl.when(i > 0)
        def _():
            o_ref[...] += part

    return _pallas(
        body, name=name, grid=(nb, L // tm),
        in_specs=[pl.BlockSpec((tm, ka), lambda j, i: (i, j)), pl.BlockSpec((tm, kb), lambda j, i: (i, j))],
        out_specs=pl.BlockSpec((None, ka, kb), lambda j, i: (j, 0, 0)),
        out_shape=jax.ShapeDtypeStruct((nb, ka, kb), F32),
        compiler_params=_cparams("parallel", "arbitrary"),
    )(a, g)


SCAN_ROWS = 128
STATE_DTYPE = BF16


def _scan_fwd(br, bi, ar, ai, *, name):
    L, S, _ = br.shape
    tb = SCAN_ROWS

    def body(br_ref, bi_ref, ar_ref, ai_ref, xr_ref, xi_ref, carry_ref):
        @pl.when(pl.program_id(0) == 0)
        def _():
            carry_ref[...] = jnp.zeros_like(carry_ref)

        a_r, a_i = ar_ref[...], ai_ref[...]

        def step(t, c):
            xr, xi = c
            nr = a_r * xr - a_i * xi + br_ref[t].astype(F32)
            ni = a_r * xi + a_i * xr + bi_ref[t].astype(F32)
            xr_ref[t] = nr.astype(xr_ref.dtype)
            xi_ref[t] = ni.astype(xi_ref.dtype)
            return nr, ni

        xr, xi = lax.fori_loop(0, tb, step, (carry_ref[0], carry_ref[1]), unroll=8)
        carry_ref[0] = xr
        carry_ref[1] = xi

    blk = pl.BlockSpec((tb, S, LANES), lambda i: (i, 0, 0))
    par = pl.BlockSpec((S, LANES), lambda i: (0, 0))
    return _pallas(
        body, name=name, grid=(L // tb,), in_specs=[blk, blk, par, par], out_specs=[blk, blk],
        out_shape=[jax.ShapeDtypeStruct(br.shape, STATE_DTYPE)] * 2,
        scratch_shapes=[pltpu.VMEM((2, S, LANES), F32)],
        compiler_params=_cparams("arbitrary"),
    )(br, bi, ar, ai)


def _scan_bwd(gr_in, gi_in, ar, ai, xr, xi, *, name):
    L, S, _ = gr_in.shape
    tb = SCAN_ROWS
    nblk = L // tb

    def body(gr_ref, gi_ref, ar_ref, ai_ref, xr_ref, xi_ref, or_ref, oi_ref, dar_ref, dai_ref, carry_ref):
        @pl.when(pl.program_id(0) == 0)
        def _():
            carry_ref[...] = jnp.zeros_like(carry_ref)
            dar_ref[...] = jnp.zeros_like(dar_ref)
            dai_ref[...] = jnp.zeros_like(dai_ref)

        a_r, a_i = ar_ref[...], ai_ref[...]

        def step(s, c):
            gr, gi, dr, di = c
            t = tb - 1 - s
            x_r, x_i = xr_ref[t].astype(F32), xi_ref[t].astype(F32)
            dr = dr + gr * x_r + gi * x_i
            di = di + gi * x_r - gr * x_i
            nr = a_r * gr + a_i * gi + gr_ref[t].astype(F32)
            ni = a_r * gi - a_i * gr + gi_ref[t].astype(F32)
            or_ref[t] = nr.astype(or_ref.dtype)
            oi_ref[t] = ni.astype(oi_ref.dtype)
            return nr, ni, dr, di

        gr, gi, dr, di = lax.fori_loop(0, tb, step, (carry_ref[0], carry_ref[1], dar_ref[...], dai_ref[...]), unroll=8)
        carry_ref[0] = gr
        carry_ref[1] = gi
        dar_ref[...] = dr
        dai_ref[...] = di

    blk = pl.BlockSpec((tb, S, LANES), lambda i: (nblk - 1 - i, 0, 0))
    par = pl.BlockSpec((S, LANES), lambda i: (0, 0))
    return _pallas(
        body, name=name, grid=(nblk,), in_specs=[blk, blk, par, par, blk, blk], out_specs=[blk, blk, par, par],
        out_shape=[jax.ShapeDtypeStruct(gr_in.shape, STATE_DTYPE)] * 2 + [jax.ShapeDtypeStruct((S, LANES), F32)] * 2,
        scratch_shapes=[pltpu.VMEM((2, S, LANES), F32)],
        compiler_params=_cparams("arbitrary"),
    )(gr_in, gi_in, ar, ai, xr, xi)


def _make_ssm(name):
    nb = GROUPS_PER_BLOCK

    def fwd(u, wr, wi, cr, cin, lam_r, lam_i):
        L = u.shape[0]
        bur, bui = _bdmm([u], [[wr], [wi]], out_dtype=STATE_DTYPE, name=name + "_bu")
        S = bur.shape[1] // LANES
        xr, xi = _scan_fwd(bur.reshape(L, S, LANES), bui.reshape(L, S, LANES), lam_r, lam_i, name=name + "_scan")
        xr, xi = xr.reshape(L, S * LANES), xi.reshape(L, S * LANES)
        (y,) = _bdmm([xr, xi], [[cr, cin]], name=name + "_cx")
        return y, (u, wr, wi, cr, cin, lam_r, lam_i, xr, xi)

    def bwd(res, dy):
        u, wr, wi, cr, cin, lam_r, lam_i, xr, xi = res
        L = u.shape[0]
        S = xr.shape[1] // LANES
        tr = lambda w: jnp.swapaxes(w, 1, 2)
        gin_r, gin_i = _bdmm([dy], [[tr(cr)], [tr(cin)]], out_dtype=STATE_DTYPE, name=name + "_gin")
        gr, gi, dlr, dli = _scan_bwd(gin_r.reshape(L, S, LANES), gin_i.reshape(L, S, LANES), lam_r, lam_i,
                                     xr.reshape(L, S, LANES), xi.reshape(L, S, LANES), name=name + "_rscan")
        gr, gi = gr.reshape(L, S * LANES), gi.reshape(L, S * LANES)
        (du,) = _bdmm([gr, gi], [[tr(wr), tr(wi)]], name=name + "_du")
        dwr = _bdmm_tn(u, gr, nb, name=name + "_dwr")
        dwi = _bdmm_tn(u, gi, nb, name=name + "_dwi")
        dcr = _bdmm_tn(xr, dy, nb, name=name + "_dcr")
        dcin = _bdmm_tn(xi, dy, nb, name=name + "_dci")
        return du, dwr, dwi, dcr, dcin, dlr, dli

    return fwd, bwd


def _s5_discretize(a_re, a_im, log_dt, b_re, b_im, c_re, c_im):
    G, P, C, nb = S5_GROUPS, S5_STATE, S5_GROUP, GROUPS_PER_BLOCK
    dt = jnp.exp(log_dt)[:, None]
    mag = jnp.exp(dt * a_re)
    ang = dt * a_im
    abar_re = mag * jnp.cos(ang)
    abar_im = mag * jnp.sin(ang)
    den = a_re * a_re + a_im * a_im
    coef_re = ((abar_re - 1.0) * a_re + abar_im * a_im) / den
    coef_im = (abar_im * a_re - (abar_re - 1.0) * a_im) / den
    bbar_re = coef_re[..., None] * b_re - coef_im[..., None] * b_im
    bbar_im = coef_re[..., None] * b_im + coef_im[..., None] * b_re
    eye = jnp.eye(nb, dtype=F32)

    def blocks_in(bb):
        return jnp.einsum("jgpc,gh->jgchp", bb.reshape(G // nb, nb, P, C), eye).reshape(G // nb, nb * C, nb * P)

    def blocks_out(cc):
        return jnp.einsum("jgcp,gh->jgphc", cc.reshape(G // nb, nb, C, P), eye).reshape(G // nb, nb * P, nb * C)

    lam_r = abar_re.reshape(G * P // LANES, LANES)
    lam_i = abar_im.reshape(G * P // LANES, LANES)
    return blocks_in(bbar_re), blocks_in(bbar_im), blocks_out(c_re), blocks_out(-c_im), lam_r, lam_i


ATTN_Q_ROWS = 384


HEADS_PER_BLOCK = LANES // HEAD_DIM


def _head_masks(shape, axis):
    idx = lax.broadcasted_iota(jnp.int32, shape, axis) // HEAD_DIM
    return [idx == hh for hh in range(HEADS_PER_BLOCK)]


def _masked_bf16(x, masks):
    return [jnp.where(m, x, 0.0).astype(BF16) for m in masks]


def _tri_sum(x, tri2):
    hi = x.astype(BF16)
    lo = (x - hi.astype(F32)).astype(BF16)
    return _dot(jnp.concatenate([hi, lo], axis=1), tri2)


def _attn_weights(qh, kt, strict, after, c):
    z = _dot(qh, kt)
    lb = jnp.minimum(z, 0.0) - jnp.log(1.0 + jnp.exp(-jnp.abs(z)))
    l1m = lb - z
    if strict is not None:
        l1m = jnp.where(strict, l1m, 0.0)
    rem = _tri_sum(l1m, after)
    w = jnp.exp(lb + rem + c)
    if strict is not None:
        w = jnp.where(strict, w, 0.0)
    return lb, w, rem[:, 0:1] + l1m[:, 0:1]


def _tri(cmp):
    tk = ATTN_BLOCK
    tri = cmp(lax.broadcasted_iota(jnp.int32, (tk, tk), 0), lax.broadcasted_iota(jnp.int32, (tk, tk), 1)).astype(BF16)
    return jnp.concatenate([tri, tri], axis=0)


def _attn_tile_index(i, jb, nsub):
    return nsub * (i * (i + 1) // 2) + jb


def _attn_fwd(q, kt, v, *, name):
    L = q.shape[0]
    tk = ATTN_BLOCK
    tq = _pick(L, (ATTN_Q_ROWS, tk))
    nsub = tq // tk
    n_qt = L // tq
    n_tiles = _attn_tile_index(n_qt, 0, nsub)
    scale = 1.0 / math.sqrt(HEAD_DIM)

    def body(q_ref, kt_ref, v_ref, o_ref, w_hbm, s_hbm, acc_ref, w_stage, s_stage, sems):
        hp, i = pl.program_id(0), pl.program_id(1)
        after = _tri(lambda r, c: r > c)
        rowq = lax.broadcasted_iota(jnp.int32, (tq, tk), 0)
        colq = lax.broadcasted_iota(jnp.int32, (tq, tk), 1)
        qs = _masked_bf16(q_ref[...].astype(F32) * scale, _head_masks((tq, LANES), 1))
        vmasks = _head_masks((tk, LANES), 1)
        acc_ref[...] = jnp.zeros_like(acc_ref)

        def saves(buf, ii):
            cps = []
            for jd in range(nsub):
                idx = _attn_tile_index(i, ii * nsub + jd, nsub)
                cps.append(pltpu.make_async_copy(w_stage.at[buf, jd], w_hbm.at[hp, idx], sems.at[0, buf, jd]))
                cps.append(pltpu.make_async_copy(s_stage.at[buf, jd], s_hbm.at[hp, idx], sems.at[1, buf, jd]))
            return cps

        def tile(jb, buf, jd, strict, cs):
            r0 = pl.multiple_of(jb * tk, tk)
            kt_j = kt_ref[:, pl.ds(r0, tk)].astype(BF16)
            vs = _masked_bf16(v_ref[pl.ds(r0, tk), :], vmasks)
            out, acc = [], None
            for hh, (qh, vh, c) in enumerate(zip(qs, vs, cs)):
                lb, w, tot = _attn_weights(qh, kt_j, strict, after, c)
                w = w.astype(BF16)
                w_stage[buf, jd, hh] = w
                s_stage[buf, jd, hh] = lb.astype(BF16)
                part = _dot(w, vh)
                acc = part if acc is None else acc + part
                out.append(c + tot)
            acc_ref[...] += acc
            return tuple(out)

        cs = tuple(jnp.zeros((tq, 1), F32) for _ in qs)
        for jd in reversed(range(nsub)):
            cs = tile(i * nsub + jd, 0, jd, (colq + jd * tk) < rowq, cs)
        for cp in saves(0, i):
            cp.start()

        def step(ii, cs):
            buf = (ii + 1) % 2

            @pl.when(ii >= 1)
            def _():
                for cp in saves(buf, 0):
                    cp.wait()

            for jd in reversed(range(nsub)):
                cs = tile((i - 1 - ii) * nsub + jd, buf, jd, None, cs)
            for cp in saves(buf, i - 1 - ii):
                cp.start()
            return cs

        lax.fori_loop(0, i, step, cs)
        o_ref[...] = acc_ref[...].astype(o_ref.dtype)
        for cp in saves(0, 0):
            cp.wait()

        @pl.when(i >= 1)
        def _():
            for cp in saves(1, 0):
                cp.wait()

    blk = pl.BlockSpec((tq, LANES), lambda h, i: (i, h))
    whole = pl.BlockSpec((L, LANES), lambda h, i: (0, h))
    whole_t = pl.BlockSpec((LANES, L), lambda h, i: (h, 0))
    any_spec = pl.BlockSpec(memory_space=pl.ANY)
    nhb = N_HEADS // HEADS_PER_BLOCK
    saved = jax.ShapeDtypeStruct((nhb, n_tiles, HEADS_PER_BLOCK, tq, tk), BF16)
    return _pallas(
        body, name=name, grid=(nhb, n_qt), in_specs=[blk, whole_t, whole], out_specs=[blk, any_spec, any_spec],
        out_shape=[jax.ShapeDtypeStruct(q.shape, BF16), saved, saved],
        scratch_shapes=[pltpu.VMEM((tq, LANES), F32), pltpu.VMEM((2, nsub, HEADS_PER_BLOCK, tq, tk), BF16),
                        pltpu.VMEM((2, nsub, HEADS_PER_BLOCK, tq, tk), BF16), pltpu.SemaphoreType.DMA((2, 2, nsub))],
        compiler_params=_cparams("parallel", "arbitrary"),
    )(q, kt, v)


def _attn_bwd(q, k, vt, do, w_saved, s_saved, *, name):
    L = q.shape[0]
    tk = ATTN_BLOCK
    tq = _pick(L, (ATTN_Q_ROWS, tk))
    nsub = tq // tk
    scale = 1.0 / math.sqrt(HEAD_DIM)

    def body(q_ref, k_ref, vt_ref, do_ref, w_hbm, s_hbm, dq_ref, dkt_ref, dvt_ref, e_scr, sig_scr, dq_acc_ref, w_stage, sems):
        hp, i = pl.program_id(0), pl.program_id(1)

        @pl.when(i == 0)
        def _():
            dkt_ref[...] = jnp.zeros_like(dkt_ref)
            dvt_ref[...] = jnp.zeros_like(dvt_ref)

        before = _tri(lambda r, c: r < c)
        rowq = lax.broadcasted_iota(jnp.int32, (tq, tk), 0)
        colq = lax.broadcasted_iota(jnp.int32, (tq, tk), 1)
        qmasks = _head_masks((tq, LANES), 1)
        tmasks = _head_masks((LANES, tq), 0)
        kmasks = _head_masks((tk, LANES), 1)
        q_scaled = q_ref[...].astype(F32) * scale
        do32 = do_ref[...].astype(F32)
        dos = _masked_bf16(do32, qmasks)
        qts = _masked_bf16(q_scaled.T, tmasks)
        dots = _masked_bf16(do32.T, tmasks)
        dq_acc_ref[...] = jnp.zeros_like(dq_acc_ref)

        def loads(buf, ii):
            cps = []
            for jd in range(nsub):
                jb = ii * nsub + jd
                idx = _attn_tile_index(i, jb, nsub)
                cps.append(pltpu.make_async_copy(w_hbm.at[hp, idx], w_stage.at[buf, jd], sems.at[0, buf, jd]))
                cps.append(pltpu.make_async_copy(s_hbm.at[hp, idx], sig_scr.at[jb], sems.at[1, buf, jd]))
            return cps

        def weigh(buf, ii):
            for jd in range(nsub):
                jb = ii * nsub + jd
                r0 = pl.multiple_of(jb * tk, tk)
                vt_j = vt_ref[:, pl.ds(r0, tk)].astype(BF16)
                acc = None
                for hh, (dob, dot_h) in enumerate(zip(dos, dots)):
                    w = w_stage[buf, jd, hh]
                    e_scr[hh, jb] = _dot(dob, vt_j) * w.astype(F32)
                    part = _dot(dot_h, w)
                    acc = part if acc is None else acc + part
                dvt_ref[:, pl.ds(r0, tk)] += acc

        def tile_right(jb, strict, cs):
            r0 = pl.multiple_of(jb * tk, tk)
            ks = _masked_bf16(k_ref[pl.ds(r0, tk), :], kmasks)
            out, dq_acc, dk_acc = [], None, None
            for hh, (kh, qt_h, c2) in enumerate(zip(ks, qts, cs)):
                e = e_scr[hh, jb]
                sig = jnp.exp(sig_scr[jb, hh].astype(F32))
                left = _tri_sum(e, before)
                dz = e - (e + left + c2) * sig
                if strict is not None:
                    dz = jnp.where(strict, dz, 0.0)
                dz = dz.astype(BF16)
                p1, p2 = _dot(dz, kh), _dot(qt_h, dz)
                dq_acc = p1 if dq_acc is None else dq_acc + p1
                dk_acc = p2 if dk_acc is None else dk_acc + p2
                out.append(c2 + left[:, tk - 1:tk] + e[:, tk - 1:tk])
            dq_acc_ref[...] += dq_acc
            dkt_ref[:, pl.ds(r0, tk)] += dk_acc
            return tuple(out)

        for cp in loads(0, 0):
            cp.start()

        def weigh_step(ii, carry):
            buf = ii % 2
            for cp in loads(1 - buf, ii + 1):
                cp.start()
            for cp in loads(buf, ii):
                cp.wait()
            weigh(buf, ii)
            return carry

        lax.fori_loop(0, i, weigh_step, 0)
        for cp in loads(i % 2, i):
            cp.wait()
        weigh(i % 2, i)

        def right_step(ii, cs):
            for jd in range(nsub):
                cs = tile_right(ii * nsub + jd, None, cs)
            return cs

        cs = lax.fori_loop(0, i, right_step, tuple(jnp.zeros((tq, 1), F32) for _ in dos))
        for jd in range(nsub):
            cs = tile_right(i * nsub + jd, (colq + jd * tk) < rowq, cs)
        dq_ref[...] = (dq_acc_ref[...] * scale).astype(dq_ref.dtype)

    nhb = N_HEADS // HEADS_PER_BLOCK
    blk = pl.BlockSpec((tq, LANES), lambda h, i: (i, h))
    whole = pl.BlockSpec((L, LANES), lambda h, i: (0, h))
    whole_t = pl.BlockSpec((LANES, L), lambda h, i: (h, 0))
    t_shape = jax.ShapeDtypeStruct((q.shape[1], L), F32)
    any_spec = pl.BlockSpec(memory_space=pl.ANY)
    return _pallas(
        body, name=name, grid=(nhb, L // tq), in_specs=[blk, whole, whole_t, blk, any_spec, any_spec],
        out_specs=[blk, whole_t, whole_t], out_shape=[jax.ShapeDtypeStruct(q.shape, BF16), t_shape, t_shape],
        scratch_shapes=[pltpu.VMEM((HEADS_PER_BLOCK, L // tk, tq, tk), F32), pltpu.VMEM((L // tk, HEADS_PER_BLOCK, tq, tk), BF16),
                        pltpu.VMEM((tq, LANES), F32), pltpu.VMEM((2, nsub, HEADS_PER_BLOCK, tq, tk), BF16),
                        pltpu.SemaphoreType.DMA((2, 2, nsub))],
        compiler_params=_cparams("parallel", "arbitrary"),
    )(q, k, vt, do, w_saved, s_saved)


def _make_loss_head(n_valid, name):
    def run(h, g, target):
        L, d = h.shape
        tm = _pick(L, (384, 128))

        def body(h_ref, t_ref, g_ref, loss_ref, dh_ref, dg_ref):
            i = pl.program_id(0)
            x = h_ref[...]
            gb = g_ref[...]
            rows = lax.broadcasted_iota(jnp.int32, (tm, 1), 0) + i * tm
            valid = jnp.logical_and(rows >= N_META, rows < n_valid)
            r = lax.rsqrt(jnp.mean(x * x, axis=-1, keepdims=True) + RMS_EPS)
            xh = x * r
            err = jnp.where(valid, xh * gb - t_ref[...], 0.0)
            dy = err * (1.0 / d)
            dxh = dy * gb
            dh_ref[...] = r * (dxh - xh * jnp.mean(dxh * xh, axis=-1, keepdims=True))
            part = 0.5 / d * jnp.sum(jnp.sum(err * err, axis=1, keepdims=True), axis=0, keepdims=True)
            dgp = _colsum(dy * xh)

            @pl.when(i == 0)
            def _():
                loss_ref[...] = part
                dg_ref[...] = dgp

            @pl.when(i > 0)
            def _():
                loss_ref[...] += part
                dg_ref[...] += dgp

        rowspec = pl.BlockSpec((tm, d), lambda i: (i, 0))
        return _pallas(
            body, name=name, grid=(L // tm,),
            in_specs=[rowspec, rowspec, pl.BlockSpec((1, d), lambda i: (0, 0))],
            out_specs=[pl.BlockSpec((1, 1), lambda i: (0, 0)), rowspec, pl.BlockSpec((1, d), lambda i: (0, 0))],
            out_shape=[jax.ShapeDtypeStruct((1, 1), F32), jax.ShapeDtypeStruct((L, d), F32), jax.ShapeDtypeStruct((1, d), F32)],
            compiler_params=_cparams("arbitrary"),
        )(h, target, g.reshape(1, d))

    return run


def _mm(a, b, *, ta=False, tb=False, out_dtype=F32, extras=(), epilogue=None, name):
    n = b.shape[0] if tb else b.shape[1]
    return _gmm([(a, ta)], [(b, tb)], [[(0, 0, 0)]], n=n, extras=extras, epilogue=epilogue, out_dtypes=(out_dtype,), name=name)[0]


def _add_epilogue(acc, resid):
    return (acc + resid,)


def _make_ffn_block(name):
    def fwd(h, gain, w_in, w_out):
        dff = w_out.shape[0]
        f = _rmsnorm_fwd(h, gain, out_dtype=BF16, name=name + "_norm")

        def swiglu(g, u):
            return g, u, g * _sigmoid(g) * u

        g, u, a = _gmm([(f, False)], [(w_in, False)], [[(0, 0, 0)], [(0, 0, dff)]], n=dff, epilogue=swiglu,
                       out_dtypes=(BF16, BF16, BF16), name=name + "_in")
        h2 = _mm(a, w_out, extras=[h], epilogue=_add_epilogue, name=name + "_out")
        return h2, (h, gain, f, g, u, a, w_in, w_out)

    def bwd(res, dh2, emit):
        h, gain, f, g, u, a, w_in, w_out = res
        dff, dm = w_out.shape

        def dswiglu(da, gb, ub):
            gb, ub = gb.astype(F32), ub.astype(F32)
            s = _sigmoid(gb)
            return da * ub * (s + gb * s * (1.0 - s)), da * gb * s

        dg, du = _gmm([(dh2, False)], [(w_out, True)], [[(0, 0, 0)]], n=dff, extras=[g, u], epilogue=dswiglu,
                      out_dtypes=(BF16, BF16), name=name + "_da")
        dw_out = _mm(a, dh2, ta=True, name=name + "_dwout")
        (df,) = _gmm([(dg, False), (du, False)], [(w_in, True)], [[(0, 0, 0), (1, 0, dff)]], n=dm, name=name + "_df")
        dw_in = jnp.concatenate([_mm(f, dg, ta=True, name=name + "_dwg"), _mm(f, du, ta=True, name=name + "_dwu")], axis=1)
        tok = emit({"w_ffn_in": dw_in, "w_ffn_out": dw_out})
        dh, dgain = _rmsnorm_bwd(h, gain + tok, df, dh2, name=name + "_dnorm")
        return dh, dgain

    return fwd, bwd


def _make_s5_out_block(name):
    def post(yb, ub, db):
        s = yb + db * ub
        return (0.5 * s * (1.0 + jnp.tanh(GELU_K * (s + GELU_C * s * s * s))),)

    def fwd(y, u, d, h, w_glu):
        L, dm = y.shape
        (z,) = _rowwise(post, [y, u], [d.reshape(1, dm)], [((L, dm), BF16)], [], tm=_pick(L, (384, 128)), name=name + "_gelu")

        def glu(val, gate, resid):
            return resid + val * _sigmoid(gate), val, gate

        h2, val, gate = _gmm([(z, False)], [(w_glu, False)], [[(0, 0, 0)], [(0, 0, dm)]], n=dm, extras=[h], epilogue=glu,
                             out_dtypes=(F32, BF16, BF16), name=name + "_glu")
        return h2, (y, u, d, z, val, gate, w_glu)

    def bwd(res, dh2, emit):
        y, u, d, z, val, gate, w_glu = res
        L, dm = y.shape

        def dglu(dh2b, valb, gateb):
            s = _sigmoid(gateb.astype(F32))
            return dh2b * s, dh2b * valb.astype(F32) * s * (1.0 - s)

        dval, dgate = _rowwise(dglu, [dh2, val, gate], [], [((L, dm), BF16)] * 2, [], tm=_pick(L, (384, 128)), name=name + "_dglu")
        (dz,) = _gmm([(dval, False), (dgate, False)], [(w_glu, True)], [[(0, 0, 0), (1, 0, dm)]], n=dm, name=name + "_dz")

        def dpost(yb, ub, dzb, db):
            s = yb + db * ub
            t = jnp.tanh(GELU_K * (s + GELU_C * s * s * s))
            ds = dzb * (0.5 * (1.0 + t) + 0.5 * s * (1.0 - t * t) * GELU_K * (1.0 + 3.0 * GELU_C * s * s))
            return ds, ds * db, _colsum(ds * ub)

        dw = jnp.concatenate([_mm(z, dval, ta=True, name=name + "_dwv"), _mm(z, dgate, ta=True, name=name + "_dwg")], axis=1)
        tok = emit({"s5_w_glu": dw})
        dy, du, dd = _rowwise(dpost, [y, u, dz], [(d + tok).reshape(1, dm)], [((L, dm), F32)] * 2, [((1, dm), F32)],
                              tm=_pick(L, (384, 128)), name=name + "_dgelu")
        return dy, du, dd.reshape(d.shape)

    return fwd, bwd


def _make_attn_layer(name):
    def fwd(h, g_kv, g_q, w_k, w_v, w_q, w_o):
        dm = h.shape[1]
        kvn = _rmsnorm_fwd(h, g_kv, out_dtype=BF16, name=name + "_kvnorm")
        qn = _rmsnorm_fwd(h, g_q, out_dtype=BF16, name=name + "_qnorm")
        k, v = _gmm([(kvn, False)], [(w_k, False), (w_v, False)], [[(0, 0, 0)], [(0, 1, 0)]], n=dm,
                    out_dtypes=(BF16, BF16), name=name + "_kv")
        q = _mm(qn, w_q, out_dtype=BF16, name=name + "_q")
        o, w_saved, s_saved = _attn_fwd(q, k.T, v, name=name + "_fwd")
        h2 = _mm(o, w_o, extras=[h], epilogue=_add_epilogue, name=name + "_o")
        return h2, (h, g_kv, g_q, kvn, qn, q, k, v.T, o, w_saved, s_saved, w_k, w_v, w_q, w_o)

    def bwd(res, dh2, emit):
        h, g_kv, g_q, kvn, qn, q, k, vt, o, w_saved, s_saved, w_k, w_v, w_q, w_o = res
        dm = h.shape[1]
        do = _mm(dh2, w_o, tb=True, out_dtype=BF16, name=name + "_do")
        dw_o = _mm(o, dh2, ta=True, name=name + "_dwo")
        dq, dkt, dvt = _attn_bwd(q, k, vt, do, w_saved, s_saved, name=name + "_bwd")
        dqn = _mm(dq, w_q, tb=True, name=name + "_dqn")
        dw_q = _mm(qn, dq, ta=True, name=name + "_dwq")
        (dkvn,) = _gmm([(dkt, True), (dvt, True)], [(w_k, True), (w_v, True)], [[(0, 0, 0), (1, 1, 0)]], n=dm, name=name + "_dkvn")
        dw_k = _mm(kvn, dkt, ta=True, tb=True, name=name + "_dwk")
        dw_v = _mm(kvn, dvt, ta=True, tb=True, name=name + "_dwv")
        tok = emit({"w_k": dw_k, "w_v": dw_v, "w_q": dw_q, "w_o": dw_o})
        dh, dg_q = _rmsnorm_bwd(h, g_q + tok, dqn, dh2, name=name + "_dqnorm")
        dh, dg_kv = _rmsnorm_bwd(h, g_kv, dkvn, dh, name=name + "_dkvnorm")
        return dh, dg_kv, dg_q

    return fwd, bwd


S5_PARAMS = ("s5_a_re", "s5_a_im", "s5_log_dt", "s5_b_re", "s5_b_im", "s5_c_re", "s5_c_im")


def _forward_backward(x_pad, target_pad, n_valid, small, get_weights, put_grads):
    gs = {}
    h0 = x_pad.at[:N_META].set(small["meta_tokens"])

    ssm_fwd, ssm_bwd = _make_ssm("ssm")
    s5out_fwd, s5out_bwd = _make_s5_out_block("s5out")
    ffn_fwd, ffn_bwd = zip(_make_ffn_block("ffn0"), _make_ffn_block("ffn1"))
    attn_fwd, attn_bwd = _make_attn_layer("attn")
    u = _rmsnorm_fwd(h0, small["norm_mix"][0], out_dtype=F32, name="mix0_norm")
    disc, disc_vjp = jax.vjp(_s5_discretize, *[small[n][0] for n in S5_PARAMS])
    y, ssm_res = ssm_fwd(u, *disc)
    w = get_weights("s5", y)
    h1, s5out_res = s5out_fwd(y, u, small["s5_d"][0], h0, w["s5_w_glu"])
    w = get_weights("ffn0", h1)
    h2, ffn0_res = ffn_fwd[0](h1, small["norm_ffn"][0], w["w_ffn_in"], w["w_ffn_out"])
    w = get_weights("attn", h2)
    h3, attn_res = attn_fwd(h2, small["norm_kv"], small["norm_mix"][1], w["w_k"], w["w_v"], w["w_q"], w["w_o"])
    w = get_weights("ffn1", h3)
    h4, ffn1_res = ffn_fwd[1](h3, small["norm_ffn"][1], w["w_ffn_in"], w["w_ffn_out"])
    loss, dh, dg_final = _make_loss_head(n_valid, "loss_head")(h4, small["norm_final"], target_pad)
    gs["norm_final"] = dg_final.reshape(small["norm_final"].shape)

    dh, dg_ffn1 = ffn_bwd[1](ffn1_res, dh, functools.partial(put_grads, "ffn1"))
    dh, gs["norm_kv"], dg_mix1 = attn_bwd(attn_res, dh, functools.partial(put_grads, "attn"))
    dh, dg_ffn0 = ffn_bwd[0](ffn0_res, dh, functools.partial(put_grads, "ffn0"))
    dy, du, dd = s5out_bwd(s5out_res, dh, functools.partial(put_grads, "s5"))
    du_ssm, *ddisc = ssm_bwd(ssm_res, dy)
    dh, dg_mix0 = _rmsnorm_bwd(h0, small["norm_mix"][0], du + du_ssm, dh, name="mix0_dnorm")
    for n, g in zip(S5_PARAMS, disc_vjp(tuple(ddisc))):
        gs[n] = g[None]
    gs["s5_d"] = dd.reshape(small["s5_d"].shape)
    gs["norm_mix"] = jnp.stack([dg_mix0, dg_mix1])
    gs["norm_ffn"] = jnp.stack([dg_ffn0, dg_ffn1])
    gs["meta_tokens"] = dh[:N_META]
    return loss, dh, gs


def _coords():
    return lax.axis_index("x"), lax.axis_index("y"), lax.axis_index("c")


def _flip(bits):
    x, y, c = _coords()
    fx, fy, fc = bits
    return (x ^ fx if fx else x, y ^ fy if fy else y, c ^ fc if fc else c)


def _exchange(ins, out_shapes, copies, local_copies, *, aliases=None, name):
    n_in, n_out = len(ins), len(out_shapes)
    n_cp, n_loc = len(copies), len(local_copies)
    aliases = aliases or {}

    def body(*refs):
        in_refs, out_refs = refs[:n_in], refs[n_in:n_in + n_out]
        send_sems, recv_sems, loc_sems = refs[n_in + n_out:]
        me = _coords()
        locs = []
        for n, (ii, oi, dfn) in enumerate(local_copies):
            cp = pltpu.make_async_copy(in_refs[ii], out_refs[oi].at[dfn(*me)], loc_sems.at[n])
            cp.start()
            locs.append(cp)
        sends = []
        for n, (ii, sfn, bits, oi, dfn) in enumerate(copies):
            src = in_refs[ii] if sfn is None else in_refs[ii].at[sfn(*me)]
            cp = pltpu.make_async_remote_copy(
                src_ref=src, dst_ref=out_refs[oi].at[dfn(*me)], send_sem=send_sems.at[n], recv_sem=recv_sems.at[n],
                device_id=_flip(bits), device_id_type=MESH)
            cp.start()
            sends.append(cp)
        for n, (ii, sfn, bits, oi, dfn) in enumerate(copies):
            peer = _flip(bits)
            src = in_refs[ii] if sfn is None else in_refs[ii].at[sfn(*me)]
            pltpu.make_async_remote_copy(
                src_ref=src, dst_ref=out_refs[oi].at[dfn(*peer)], send_sem=send_sems.at[n], recv_sem=recv_sems.at[n],
                device_id=peer, device_id_type=MESH).wait_recv()
        for cp in sends:
            cp.wait_send()
        for cp in locs:
            cp.wait()

    any_spec = pl.BlockSpec(memory_space=pl.ANY)
    return _pallas(
        body, name=name, in_specs=[any_spec] * n_in, out_specs=[any_spec] * n_out,
        out_shape=[jax.ShapeDtypeStruct(s, d) for s, d in out_shapes],
        scratch_shapes=[pltpu.SemaphoreType.DMA((max(n_cp, 1),)), pltpu.SemaphoreType.DMA((max(n_cp, 1),)),
                        pltpu.SemaphoreType.DMA((max(n_loc, 1),))],
        input_output_aliases=aliases,
        compiler_params=pltpu.CompilerParams(has_side_effects=True),
    )(*ins)


ICI_FLIPS = ((1, 0, 0), (0, 1, 0), (1, 1, 0))
D2D_FLIP = (0, 0, 1)


def _slot_of(x, y, c):
    return 4 * x + 2 * y + c


def _all_gather(shards, *, name):
    n = len(shards)
    outs = [((N_DEV,) + s.shape, s.dtype) for s in shards]
    copies = [(t, None, bits, t, _slot_of) for t in range(n) for bits in ICI_FLIPS]
    local = [(t, t, _slot_of) for t in range(n)]
    bufs = _exchange(shards, outs, copies, local, name=name + "_ici")
    copies2 = [(t, (lambda x, y, c, q=q: 2 * q + c), D2D_FLIP, t, (lambda x, y, c, q=q: 2 * q + c))
               for t in range(n) for q in range(4)]
    return _exchange(bufs, outs, copies2, [], aliases={t: t for t in range(n)}, name=name + "_d2d")


ALL_FLIPS = tuple((m >> 2 & 1, m >> 1 & 1, m & 1) for m in range(1, N_DEV))
HBM_SPEC = pl.BlockSpec(memory_space=pltpu.HBM)
SEM_SPEC = pl.BlockSpec(memory_space=pltpu.SEMAPHORE)
SIDE_EFFECT = pltpu.SideEffectType.DATAFLOW_SIDE_EFFECTING


def _copy_desc(copies, n, src_refs, land_refs, send_sems, recv_sems, sender):
    si, sfn, bits, li, dfn = copies[n]
    src = src_refs[si] if sfn is None else src_refs[si].at[sfn(*sender)]
    return pltpu.make_async_remote_copy(
        src_ref=src, dst_ref=land_refs[li].at[dfn(*sender)], send_sem=send_sems.at[n], recv_sem=recv_sems.at[n],
        device_id=_flip(bits), device_id_type=MESH)


def _start_copies(srcs, lands, copies, *, name):
    n_s, n_l, n_c = len(srcs), len(lands), len(copies)

    def body(*refs):
        src_refs = refs[:n_s]
        send_sems, recv_sems = refs[n_s], refs[n_s + 1]
        land_refs = refs[2 * n_s + 2:2 * n_s + 2 + n_l]
        token = refs[-1]
        me = _coords()
        for n in range(n_c):
            _copy_desc(copies, n, src_refs, land_refs, send_sems, recv_sems, me).start()
        token[...] = jnp.zeros_like(token)

    res = _pallas(
        body, name=name,
        out_shape=(pltpu.SemaphoreType.DMA((n_c,)), pltpu.SemaphoreType.DMA((n_c,)),
                   *[pltpu.HBM(s.shape, s.dtype) for s in srcs], *[pltpu.HBM(shape, dtype) for shape, dtype in lands],
                   jax.ShapeDtypeStruct((8, LANES), F32)),
        in_specs=[HBM_SPEC] * n_s,
        out_specs=(SEM_SPEC, SEM_SPEC, *[HBM_SPEC] * (n_s + n_l), pl.BlockSpec(memory_space=pltpu.VMEM)),
        input_output_aliases={i: 2 + i for i in range(n_s)},
        compiler_params=pltpu.CompilerParams(has_side_effects=SIDE_EFFECT),
    )(*[pltpu.with_memory_space_constraint(s, pltpu.HBM) for s in srcs])
    return res[0], res[1], list(res[2:2 + n_s]), list(res[2 + n_s:2 + n_s + n_l]), res[-1]


def _wait_copies(send_sems, recv_sems, srcs, lands, copies, which, after, *, name):
    src_ids, land_ids = sorted(srcs), sorted(lands)
    n_s, n_l = len(src_ids), len(land_ids)
    srcs, lands = [srcs[i] for i in src_ids], [lands[i] for i in land_ids]

    def body(*refs):
        src_refs, land_refs = dict(zip(src_ids, refs[:n_s])), dict(zip(land_ids, refs[n_s:n_s + n_l]))
        s_sems, r_sems = refs[n_s + n_l], refs[n_s + n_l + 1]
        me = _coords()
        for n in which:
            cp = _copy_desc(copies, n, src_refs, land_refs, s_sems, r_sems, me)
            cp.wait_send()
            _copy_desc(copies, n, src_refs, land_refs, s_sems, r_sems, _flip(copies[n][2])).wait_recv()

    res = _pallas(
        body, name=name,
        out_shape=tuple(pltpu.HBM(b.shape, b.dtype) for b in list(srcs) + list(lands)),
        in_specs=[HBM_SPEC] * (n_s + n_l) + [SEM_SPEC, SEM_SPEC, pl.BlockSpec(memory_space=pl.ANY)],
        out_specs=tuple([HBM_SPEC] * (n_s + n_l)),
        input_output_aliases={i: i for i in range(n_s + n_l)},
        compiler_params=pltpu.CompilerParams(has_side_effects=SIDE_EFFECT),
    )(*srcs, *lands, send_sems, recv_sems, after)
    return dict(zip(src_ids, res[:n_s])), dict(zip(land_ids, res[n_s:]))


def _adam_math(w, g, m, v):
    m = ADAM_B1 * m + (1.0 - ADAM_B1) * g
    v = ADAM_B2 * v + (1.0 - ADAM_B2) * (g * g)
    m_hat = m / (1.0 - ADAM_B1 ** ADAM_STEP)
    v_hat = v / (1.0 - ADAM_B2 ** ADAM_STEP)
    delta = -ADAM_LR * (m_hat / (jnp.sqrt(v_hat) + ADAM_EPS) + ADAM_WD * w)
    return delta, m, v


def _shard_adamw(own, landing, w, m, v, *, name):
    r, cdim = w.shape
    tr = _pick(r, (256, 128, 64, 32, 16))

    def body(g_ref, l_ref, w_ref, m_ref, v_ref, g_out, d_out, m_out, v_out):
        g = g_ref[...]
        for n in range(N_DEV - 1):
            g = g + l_ref[n].astype(F32)
        d, mn, vn = _adam_math(w_ref[...], g, m_ref[...], v_ref[...])
        g_out[...] = g
        d_out[...] = d
        m_out[...] = mn
        v_out[...] = vn

    blk = pl.BlockSpec((tr, cdim), lambda i: (i, 0))
    return _pallas(
        body, name=name, grid=(r // tr,),
        in_specs=[blk, pl.BlockSpec((N_DEV - 1, tr, cdim), lambda i: (0, i, 0)), blk, blk, blk], out_specs=[blk] * 4,
        out_shape=[jax.ShapeDtypeStruct((r, cdim), F32)] * 4,
        compiler_params=_cparams("parallel"),
    )(own, landing, w, m, v)


def _small_adamw(slots, w, m, v, *, name):
    r, cdim = w.shape

    def body(s_ref, w_ref, m_ref, v_ref, g_out, d_out, m_out, v_out):
        g = s_ref[0]
        for n in range(1, N_DEV):
            g = g + s_ref[n]
        d, mn, vn = _adam_math(w_ref[...], g, m_ref[...], v_ref[...])
        g_out[...] = g
        d_out[...] = d
        m_out[...] = mn
        v_out[...] = vn

    tr = _pick(r, (256, 128, 64, 32, 16, 8))
    blk = pl.BlockSpec((tr, cdim), lambda i: (i, 0))
    return _pallas(
        body, name=name, grid=(r // tr,),
        in_specs=[pl.BlockSpec((N_DEV, tr, cdim), lambda i: (0, i, 0)), blk, blk, blk], out_specs=[blk] * 4,
        out_shape=[jax.ShapeDtypeStruct((r, cdim), F32)] * 4,
        compiler_params=_cparams("parallel"),
    )(slots, w, m, v)


def _plain_adamw(g, w, m, v, *, name):
    def body(g_ref, w_ref, m_ref, v_ref, d_out, m_out, v_out):
        d, mn, vn = _adam_math(w_ref[...], g_ref[...], m_ref[...], v_ref[...])
        d_out[...] = d
        m_out[...] = mn
        v_out[...] = vn

    spec = pl.BlockSpec(g.shape, lambda: (0,) * g.ndim)
    return _pallas(body, name=name, in_specs=[spec] * 4, out_specs=[spec] * 3,
                   out_shape=[jax.ShapeDtypeStruct(g.shape, F32)] * 3)(g, w, m, v)


def _cast_bf16(x, *, name):
    r, cdim = x.shape
    tr = _pick(r, (256, 128, 64, 32, 16))

    def body(x_ref, o_ref):
        o_ref[...] = x_ref[...].astype(BF16)

    return _pallas(body, name=name, grid=(r // tr,), in_specs=[pl.BlockSpec((tr, cdim), lambda i: (i, 0))],
                   out_specs=pl.BlockSpec((tr, cdim), lambda i: (i, 0)), out_shape=jax.ShapeDtypeStruct(x.shape, BF16),
                   compiler_params=_cparams("parallel"))(x)


SMALL_NAMES = ("norm_mix", "norm_ffn", "s5_a_re", "s5_a_im", "s5_log_dt", "s5_b_re", "s5_b_im", "s5_c_re", "s5_c_im",
               "norm_kv", "norm_final")


def _pack_rows(arrs):
    rows = []
    for a in arrs:
        flat = a.reshape(-1)
        pad = (-flat.shape[0]) % (8 * LANES)
        rows.append(jnp.pad(flat, (0, pad)).reshape(-1, LANES))
    return jnp.concatenate(rows, axis=0)


def _unpack_rows(packed, like):
    out, r0 = [], 0
    for a in like:
        n = math.prod(a.shape)
        nr = (n + 8 * LANES - 1) // (8 * LANES) * 8
        out.append(packed[r0:r0 + nr].reshape(-1)[:n].reshape(a.shape))
        r0 += nr
    return out


def kernel(x, meta_tokens, norm_mix, norm_ffn, s5_a_re, s5_a_im, s5_log_dt, s5_b_re, s5_b_im, s5_c_re, s5_c_im, s5_d, s5_w_glu, norm_kv, w_kv, w_q, w_o, w_ffn_in, w_ffn_out, norm_final, loss_target, m_meta_tokens, m_norm_mix, m_norm_ffn, m_s5_a_re, m_s5_a_im, m_s5_log_dt, m_s5_b_re, m_s5_b_im, m_s5_c_re, m_s5_c_im, m_s5_d, m_s5_w_glu, m_norm_kv, m_w_kv, m_w_q, m_w_o, m_w_ffn_in, m_w_ffn_out, m_norm_final, v_meta_tokens, v_norm_mix, v_norm_ffn, v_s5_a_re, v_s5_a_im, v_s5_log_dt, v_s5_b_re, v_s5_b_im, v_s5_c_re, v_s5_c_im, v_s5_d, v_s5_w_glu, v_norm_kv, v_w_kv, v_w_q, v_w_o, v_w_ffn_in, v_w_ffn_out, v_norm_final):
    args = dict(locals())
    seq = x.shape[1]
    n_valid = N_META + seq
    Lp = (n_valid + ATTN_BLOCK - 1) // ATTN_BLOCK * ATTN_BLOCK
    dm = D_MODEL
    my_slot = _slot_of(*_coords())

    n_ffn = w_ffn_in.shape[0]
    groups = {"s5": ["s5_w_glu"], "ffn0": ["w_ffn_in0", "w_ffn_out0"], "attn": ["w_kv", "w_q", "w_o"],
              "ffn1": ["w_ffn_in1", "w_ffn_out1"]}
    shards = {"s5_w_glu": s5_w_glu[0], "w_kv": w_kv, "w_q": w_q[0], "w_o": w_o[0]}
    for l in range(n_ffn):
        shards[f"w_ffn_in{l}"], shards[f"w_ffn_out{l}"] = w_ffn_in[l], w_ffn_out[l]
    by_columns = {"s5_w_glu", "w_kv", "w_ffn_in0", "w_ffn_in1"}
    names = [n for g in groups.values() for n in g]

    def gathered_full(n, g8):
        if n in by_columns:
            return jnp.swapaxes(g8, 0, 1).reshape(g8.shape[1], -1)
        return g8.reshape(-1, g8.shape[2])

    col_shard = jnp.concatenate([meta_tokens, s5_d, jnp.zeros((7, LANES), F32)], axis=0)
    ag_srcs = [col_shard] + [_cast_bf16(shards[n], name="cast_" + n) for n in names]
    ag_copies = [(t, None, bits, t, _slot_of) for t in range(len(ag_srcs)) for bits in ALL_FLIPS]
    ag_send, ag_recv, ag_srcs, ag_lands, ag_token = _start_copies(
        ag_srcs, [((N_DEV,) + s.shape, s.dtype) for s in ag_srcs], ag_copies, name="ag_start")

    def gather_wait(tag, ids, after):
        which = [t * len(ALL_FLIPS) + r for t in ids for r in range(len(ALL_FLIPS))]
        srcs, lands = _wait_copies(ag_send, ag_recv, {t: ag_srcs[t] for t in ids}, {t: ag_lands[t] for t in ids},
                                   ag_copies, which, after, name="ag_wait_" + tag)
        return {t: lax.dynamic_update_index_in_dim(lands[t], srcs[t], my_slot, 0) for t in ids}

    def get_weights(group, after):
        ids = [1 + names.index(n) for n in groups[group]]
        full = {names[t - 1]: gathered_full(names[t - 1], g8) for t, g8 in gather_wait(group, ids, after).items()}
        if group == "attn":
            w_kv_full = full.pop("w_kv")
            full["w_k"], full["w_v"] = w_kv_full[:, :dm], w_kv_full[:, dm:]
        return {n.rstrip("01") if n.startswith("w_ffn") else n: w for n, w in full.items()}

    gcol = gather_wait("cols", [0], ag_token)[0]
    small = {n: args[n] for n in SMALL_NAMES}
    small["meta_tokens"] = jnp.swapaxes(gcol[:, :N_META], 0, 1).reshape(N_META, dm)
    small["s5_d"] = gcol[:, N_META].reshape(1, dm)

    scatters = {}

    def put_grads(group, grads):
        if group == "attn":
            grads = {"w_kv": jnp.concatenate([grads.pop("w_k"), grads.pop("w_v")], axis=1), **grads}
        own, send = {}, []
        for n in groups[group]:
            g = grads[n.rstrip("01") if n.startswith("w_ffn") else n]
            r, c = shards[n].shape
            if n in by_columns:
                own[n] = lax.dynamic_slice_in_dim(g, my_slot * c, c, axis=1)
                send.append(jnp.swapaxes(g.reshape(r, N_DEV, c), 0, 1).astype(BF16))
            else:
                own[n] = lax.dynamic_slice_in_dim(g, my_slot * r, r, axis=0)
                send.append(g.reshape(N_DEV, r, c).astype(BF16))
        copies = [(t, (lambda x, y, c, b=bits: _slot_of(x ^ b[0], y ^ b[1], c ^ b[2])), bits, t, (lambda x, y, c, k=k: k))
                  for t in range(len(send)) for k, bits in enumerate(ALL_FLIPS)]
        lands = [((len(ALL_FLIPS),) + s.shape[1:], BF16) for s in send]
        s_sem, r_sem, srcs, lands, token = _start_copies(send, lands, copies, name="rs_start_" + group)
        scatters[group] = (s_sem, r_sem, srcs, lands, copies, own)
        return token[0, 0]

    x_pad = jnp.zeros((Lp, dm), F32).at[N_META:n_valid].set(x[0])
    t_pad = jnp.zeros((Lp, dm), F32).at[N_META:n_valid].set(loss_target[0])
    loss_local, g_xpad, g_small = _forward_backward(x_pad, t_pad, n_valid, small, get_weights, put_grads)
    loss = lax.psum(loss_local[0, 0], ("x", "y", "c"))
    grad_x = g_xpad[N_META:n_valid][None]

    small_list = [g_small[n] for n in SMALL_NAMES] + [g_small["meta_tokens"], g_small["s5_d"]]
    small_bufs = _all_gather([_pack_rows(small_list)], name="small_ag")

    out, per_tensor = {}, {}
    for group, (s_sem, r_sem, srcs, lands, copies, own) in scatters.items():
        ids = list(range(len(srcs)))
        _, landed = _wait_copies(s_sem, r_sem, dict(zip(ids, srcs)), dict(zip(ids, lands)), copies, list(range(len(copies))),
                                 small_bufs[0], name="rs_wait_" + group)
        for t, n in enumerate(groups[group]):
            base, layer = (n[:-1], int(n[-1])) if n.startswith("w_ffn") else (n, None)
            pick = (lambda a: a[layer]) if layer is not None else (lambda a: a.reshape(shards[n].shape))
            per_tensor[n] = _shard_adamw(own[n], landed[t], shards[n], pick(args["m_" + base]), pick(args["v_" + base]),
                                         name="adamw_" + n)
    for base in ("s5_w_glu", "w_kv", "w_q", "w_o"):
        out[base] = tuple(a.reshape(args[base].shape) for a in per_tensor[base])
    for base in ("w_ffn_in", "w_ffn_out"):
        out[base] = tuple(jnp.stack([per_tensor[f"{base}{l}"][k] for l in range(n_ffn)]) for k in range(4))

    zeros_tail = [jnp.zeros_like(g_small["meta_tokens"]), jnp.zeros_like(g_small["s5_d"])]
    pw = _pack_rows([args[n] for n in SMALL_NAMES] + zeros_tail)
    pm = _pack_rows([args["m_" + n] for n in SMALL_NAMES] + zeros_tail)
    pv = _pack_rows([args["v_" + n] for n in SMALL_NAMES] + zeros_tail)
    sg, sd, sm, sv = _small_adamw(small_bufs[0], pw, pm, pv, name="adamw_small")
    like = small_list
    ug, ud, um, uv = (_unpack_rows(a, like) for a in (sg, sd, sm, sv))
    for i, n in enumerate(SMALL_NAMES):
        out[n] = (ug[i], ud[i], um[i], uv[i])
    g_meta = lax.dynamic_slice_in_dim(ug[-2], my_slot * LANES, LANES, axis=1)
    g_d = lax.dynamic_slice_in_dim(ug[-1].reshape(1, dm), my_slot * LANES, LANES, axis=1)
    pad7 = jnp.zeros((7, LANES), F32)
    gc = jnp.concatenate([g_meta, g_d, pad7], axis=0)
    wc = jnp.concatenate([meta_tokens, s5_d, pad7], axis=0)
    mc = jnp.concatenate([m_meta_tokens, m_s5_d, pad7], axis=0)
    vc = jnp.concatenate([v_meta_tokens, v_s5_d, pad7], axis=0)
    dc, mcn, vcn = _plain_adamw(gc, wc, mc, vc, name="adamw_cols")
    out["meta_tokens"] = (g_meta, dc[:N_META], mcn[:N_META], vcn[:N_META])
    out["s5_d"] = (g_d, dc[N_META:N_META + 1], mcn[N_META:N_META + 1], vcn[N_META:N_META + 1])

    order = ["meta_tokens", "norm_mix", "norm_ffn", "s5_a_re", "s5_a_im", "s5_log_dt", "s5_b_re", "s5_b_im", "s5_c_re",
             "s5_c_im", "s5_d", "s5_w_glu", "norm_kv", "w_kv", "w_q", "w_o", "w_ffn_in", "w_ffn_out", "norm_final"]
    res = [loss, grad_x]
    for k in range(4):
        res += [out[n][k] for n in order]
    return tuple(res)
```

```python
import functools
import math

import jax
import jax.numpy as jnp
from jax import lax
from jax.experimental import pallas as pl
from jax.experimental.pallas import tpu as pltpu

F32 = jnp.float32
BF16 = jnp.bfloat16

N_META = 16
D_MODEL = 1024
S5_GROUPS = 64
S5_GROUP = 16
S5_STATE = 64
N_HEADS = 16
HEAD_DIM = 64
D_FF = 2816
RMS_EPS = 1e-6
ADAM_LR, ADAM_B1, ADAM_B2, ADAM_EPS, ADAM_WD, ADAM_STEP = 0.001, 0.9, 0.999, 1e-08, 0.01, 10

LANES = 128
ATTN_BLOCK = 128
GROUPS_PER_BLOCK = 8
N_DEV = 8
MESH = pl.DeviceIdType.MESH
VMEM_LIMIT = 56 * 1024 * 1024


def _pallas(body, **kw):
    return pl.pallas_call(body, **kw)


def _cparams(*sem):
    return pltpu.CompilerParams(dimension_semantics=sem, vmem_limit_bytes=VMEM_LIMIT)


def _pick(n, prefs):
    for p in prefs:
        if n % p == 0:
            return p
    return n


def _dot(a, b, ca=1, cb=0):
    return lax.dot_general(a, b, (((ca,), (cb,)), ((), ())), preferred_element_type=F32)


MATMUL_VMEM_BUDGET = 36 * 1024 * 1024
MATMUL_TILES = (1408, 1024, 512, 384, 256, 128)


def _matmul_tiles(M, N, K, a_bytes, b_bytes, tile_bytes, n_acc, offsets):
    best = None
    for tm in (t for t in MATMUL_TILES if M % t == 0):
        for tn in (t for t in MATMUL_TILES if N % t == 0):
            for tk in (t for t in MATMUL_TILES if K % t == 0):
                if any(off % (tn if along_n else tk) for off, along_n in offsets):
                    continue
                vmem = 2 * (tm * tk * a_bytes + tk * tn * b_bytes + tm * tn * tile_bytes) + n_acc * tm * tn * 4
                if vmem > MATMUL_VMEM_BUDGET:
                    continue
                traffic = M * K * a_bytes * (N // tn) + K * N * b_bytes * (M // tm) + M * N * tile_bytes
                traffic += (K // tk - 1) * n_acc * M * N * 8 // 3
                key = (traffic, -tk, -tm)
                if best is None or key < best[0]:
                    best = (key, (tm, tn, tk))
    return best[1]


def _gmm(a_list, b_list, accs, *, n, extras=(), epilogue=None, out_dtypes=(F32,), name):
    a0, ta0 = a_list[0]
    K, M = a0.shape if ta0 else a0.shape[::-1]
    N = n
    b_keys = sorted({(bi, off) for terms in accs for _, bi, off in terms})
    a_bytes = sum(a.dtype.itemsize for a, _ in a_list)
    b_bytes = sum(b_list[bi][0].dtype.itemsize for bi, _ in b_keys)
    tile_bytes = sum(e.dtype.itemsize for e in extras) + sum(jnp.dtype(d).itemsize for d in out_dtypes)
    offsets = [(off, not b_list[bi][1]) for bi, off in b_keys]
    tm, tn, tk = _matmul_tiles(M, N, K, a_bytes, b_bytes, tile_bytes, len(accs), offsets)
    nk = K // tk
    n_a, n_b, n_e, n_o, n_acc = len(a_list), len(b_keys), len(extras), len(out_dtypes), len(accs)
    if epilogue is None:
        epilogue = lambda *vals: vals

    def body(*refs):
        a_refs, b_refs = refs[:n_a], refs[n_a:n_a + n_b]
        e_refs = refs[n_a + n_b:n_a + n_b + n_e]
        o_refs = refs[n_a + n_b + n_e:n_a + n_b + n_e + n_o]
        acc_refs = refs[n_a + n_b + n_e + n_o:]
        a_vals = [r[...].astype(BF16) for r in a_refs]
        b_vals = {key: r[...].astype(BF16) for key, r in zip(b_keys, b_refs)}
        parts = []
        for terms in accs:
            p = None
            for ai, bi, off in terms:
                d = _dot(a_vals[ai], b_vals[(bi, off)], 0 if a_list[ai][1] else 1, 1 if b_list[bi][1] else 0)
                p = d if p is None else p + d
            parts.append(p)

        def finish(vals):
            outs = epilogue(*vals, *[r[...] for r in e_refs])
            for r, val in zip(o_refs, outs):
                r[...] = val.astype(r.dtype)

        if nk == 1:
            finish(parts)
        else:
            k = pl.program_id(2)

            @pl.when(k == 0)
            def _():
                for r, p in zip(acc_refs, parts):
                    r[...] = p

            @pl.when(k > 0)
            def _():
                for r, p in zip(acc_refs, parts):
                    r[...] += p

            @pl.when(k == nk - 1)
            def _():
                finish([r[...] for r in acc_refs])

    in_specs = [pl.BlockSpec((tk, tm), lambda i, j, k: (k, i)) if ta else pl.BlockSpec((tm, tk), lambda i, j, k: (i, k))
                for _, ta in a_list]
    for bi, off in b_keys:
        if b_list[bi][1]:
            in_specs.append(pl.BlockSpec((tn, tk), lambda i, j, k, o=off // tk: (j, k + o)))
        else:
            in_specs.append(pl.BlockSpec((tk, tn), lambda i, j, k, o=off // tn: (k, j + o)))
    tile = pl.BlockSpec((tm, tn), lambda i, j, k: (i, j))
    in_specs += [tile] * n_e
    return _pallas(
        body, name=name, grid=(M // tm, N // tn, nk), in_specs=in_specs, out_specs=[tile] * n_o,
        out_shape=[jax.ShapeDtypeStruct((M, N), d) for d in out_dtypes],
        scratch_shapes=[pltpu.VMEM((tm, tn), F32)] * (n_acc if nk > 1 else 0),
        compiler_params=_cparams("parallel", "parallel", "arbitrary"),
    )(*[a for a, _ in a_list], *[b_list[bi][0] for bi, _ in b_keys], *extras)


def _rowwise(fn, row_ins, full_ins, row_outs, acc_outs, *, tm, name):
    L = row_ins[0].shape[0]
    n_row, n_full, n_ro = len(row_ins), len(full_ins), len(row_outs)

    def body(*refs):
        ins = [r[...] for r in refs[:n_row + n_full]]
        outs = refs[n_row + n_full:]
        res = fn(*ins)
        for r, val in zip(outs[:n_ro], res[:n_ro]):
            r[...] = val.astype(r.dtype)
        if acc_outs:
            first = pl.program_id(0) == 0
            for r, val in zip(outs[n_ro:], res[n_ro:]):
                @pl.when(first)
                def _(r=r, val=val):
                    r[...] = val

                @pl.when(jnp.logical_not(first))
                def _(r=r, val=val):
                    r[...] += val

    in_specs = [pl.BlockSpec((tm, a.shape[1]), lambda i: (i, 0)) for a in row_ins]
    in_specs += [pl.BlockSpec(a.shape, lambda i, nd=a.ndim: (0,) * nd) for a in full_ins]
    out_specs = [pl.BlockSpec((tm, s[1]), lambda i: (i, 0)) for s, _ in row_outs]
    out_specs += [pl.BlockSpec(s, lambda i, nd=len(s): (0,) * nd) for s, _ in acc_outs]
    out_shape = [jax.ShapeDtypeStruct(s, d) for s, d in list(row_outs) + list(acc_outs)]
    return _pallas(
        body, name=name, grid=(L // tm,), in_specs=in_specs, out_specs=out_specs, out_shape=out_shape,
        compiler_params=_cparams("arbitrary" if acc_outs else "parallel"),
    )(*row_ins, *full_ins)


def _colsum(x):
    return jnp.sum(x, axis=0, keepdims=True)


def _rmsnorm_fwd(x, g, *, out_dtype, name):
    L, d = x.shape

    def f(xb, gb):
        r = lax.rsqrt(jnp.mean(xb * xb, axis=-1, keepdims=True) + RMS_EPS)
        return (xb * r * gb,)

    return _rowwise(f, [x], [g.reshape(1, d)], [((L, d), out_dtype)], [], tm=_pick(L, (384, 128)), name=name)[0]


def _rmsnorm_bwd(x, g, dy, resid, *, name):
    L, d = x.shape

    def f(xb, dyb, *rest):
        gb = rest[-1]
        r = lax.rsqrt(jnp.mean(xb * xb, axis=-1, keepdims=True) + RMS_EPS)
        xh = xb * r
        dxh = dyb * gb
        dx = r * (dxh - xh * jnp.mean(dxh * xh, axis=-1, keepdims=True))
        if resid is not None:
            dx = dx + rest[0]
        return dx, _colsum(dyb * xh)

    rows = [x, dy] + ([resid] if resid is not None else [])
    dx, dg = _rowwise(f, rows, [g.reshape(1, d)], [((L, d), F32)], [((1, d), F32)], tm=_pick(L, (384, 128)), name=name)
    return dx, dg.reshape(g.shape)


def _sigmoid(x):
    return 1.0 / (1.0 + jnp.exp(-x))


GELU_K = math.sqrt(2.0 / math.pi)
GELU_C = 0.044715


def _bdmm(a_list, w_rows, *, out_dtype=F32, name):
    L = a_list[0].shape[0]
    nb, ka, kb = w_rows[0][0].shape
    tm = _pick(L, (1408, 384, 128))
    n_a, n_o = len(a_list), len(w_rows)
    ws = [w for row in w_rows for w in row]

    def body(*refs):
        a_vals = [r[...].astype(BF16) for r in refs[:n_a]]
        w_refs = refs[n_a:n_a + n_a * n_o]
        outs = refs[n_a + n_a * n_o:]
        for o in range(n_o):
            acc = None
            for i in range(n_a):
                p = _dot(a_vals[i], w_refs[o * n_a + i][...].astype(BF16))
                acc = p if acc is None else acc + p
            outs[o][...] = acc.astype(out_dtype)

    return _pallas(
        body, name=name, grid=(L // tm, nb),
        in_specs=[pl.BlockSpec((tm, ka), lambda i, j: (i, j)) for _ in a_list]
        + [pl.BlockSpec((None, ka, kb), lambda i, j: (j, 0, 0)) for _ in ws],
        out_specs=[pl.BlockSpec((tm, kb), lambda i, j: (i, j)) for _ in range(n_o)],
        out_shape=[jax.ShapeDtypeStruct((L, nb * kb), out_dtype) for _ in range(n_o)],
        compiler_params=_cparams("parallel", "parallel"),
    )(*a_list, *ws)


def _bdmm_tn(a, g, nb, *, name):
    L = a.shape[0]
    ka, kb = a.shape[1] // nb, g.shape[1] // nb
    tm = _pick(L, (1408, 384, 128))

    def body(a_ref, g_ref, o_ref):
        i = pl.program_id(1)
        part = _dot(a_ref[...].astype(BF16), g_ref[...].astype(BF16), 0, 0)

        @pl.when(i == 0)
        def _():
            o_ref[...] = part

        @pl.when(i > 0)
        def _():
            o_ref[...] += part

    return _pallas(
        body, name=name, grid=(nb, L // tm),
        in_specs=[pl.BlockSpec((tm, ka), lambda j, i: (i, j)), pl.BlockSpec((tm, kb), lambda j, i: (i, j))],
        out_specs=pl.BlockSpec((None, ka, kb), lambda j, i: (j, 0, 0)),
        out_shape=jax.ShapeDtypeStruct((nb, ka, kb), F32),
        compiler_params=_cparams("parallel", "arbitrary"),
    )(a, g)


SCAN_ROWS = 128
STATE_DTYPE = BF16


def _scan_fwd(br, bi, ar, ai, *, name):
    L, S, _ = br.shape
    tb = SCAN_ROWS

    def body(br_ref, bi_ref, ar_ref, ai_ref, xr_ref, xi_ref, carry_ref):
        @pl.when(pl.program_id(0) == 0)
        def _():
            carry_ref[...] = jnp.zeros_like(carry_ref)

        a_r, a_i = ar_ref[...], ai_ref[...]

        def step(t, c):
            xr, xi = c
            nr = a_r * xr - a_i * xi + br_ref[t].astype(F32)
            ni = a_r * xi + a_i * xr + bi_ref[t].astype(F32)
            xr_ref[t] = nr.astype(xr_ref.dtype)
            xi_ref[t] = ni.astype(xi_ref.dtype)
            return nr, ni

        xr, xi = lax.fori_loop(0, tb, step, (carry_ref[0], carry_ref[1]), unroll=8)
        carry_ref[0] = xr
        carry_ref[1] = xi

    blk = pl.BlockSpec((tb, S, LANES), lambda i: (i, 0, 0))
    par = pl.BlockSpec((S, LANES), lambda i: (0, 0))
    return _pallas(
        body, name=name, grid=(L // tb,), in_specs=[blk, blk, par, par], out_specs=[blk, blk],
        out_shape=[jax.ShapeDtypeStruct(br.shape, STATE_DTYPE)] * 2,
        scratch_shapes=[pltpu.VMEM((2, S, LANES), F32)],
        compiler_params=_cparams("arbitrary"),
    )(br, bi, ar, ai)


def _scan_bwd(gr_in, gi_in, ar, ai, xr, xi, *, name):
    L, S, _ = gr_in.shape
    tb = SCAN_ROWS
    nblk = L // tb

    def body(gr_ref, gi_ref, ar_ref, ai_ref, xr_ref, xi_ref, or_ref, oi_ref, dar_ref, dai_ref, carry_ref):
        @pl.when(pl.program_id(0) == 0)
        def _():
            carry_ref[...] = jnp.zeros_like(carry_ref)
            dar_ref[...] = jnp.zeros_like(dar_ref)
            dai_ref[...] = jnp.zeros_like(dai_ref)

        a_r, a_i = ar_ref[...], ai_ref[...]

        def step(s, c):
            gr, gi, dr, di = c
            t = tb - 1 - s
            x_r, x_i = xr_ref[t].astype(F32), xi_ref[t].astype(F32)
            dr = dr + gr * x_r + gi * x_i
            di = di + gi * x_r - gr * x_i
            nr = a_r * gr + a_i * gi + gr_ref[t].astype(F32)
            ni = a_r * gi - a_i * gr + gi_ref[t].astype(F32)
            or_ref[t] = nr.astype(or_ref.dtype)
            oi_ref[t] = ni.astype(oi_ref.dtype)
            return nr, ni, dr, di

        gr, gi, dr, di = lax.fori_loop(0, tb, step, (carry_ref[0], carry_ref[1], dar_ref[...], dai_ref[...]), unroll=8)
        carry_ref[0] = gr
        carry_ref[1] = gi
        dar_ref[...] = dr
        dai_ref[...] = di

    blk = pl.BlockSpec((tb, S, LANES), lambda i: (nblk - 1 - i, 0, 0))
    par = pl.BlockSpec((S, LANES), lambda i: (0, 0))
    return _pallas(
        body, name=name, grid=(nblk,), in_specs=[blk, blk, par, par, blk, blk], out_specs=[blk, blk, par, par],
        out_shape=[jax.ShapeDtypeStruct(gr_in.shape, STATE_DTYPE)] * 2 + [jax.ShapeDtypeStruct((S, LANES), F32)] * 2,
        scratch_shapes=[pltpu.VMEM((2, S, LANES), F32)],
        compiler_params=_cparams("arbitrary"),
    )(gr_in, gi_in, ar, ai, xr, xi)


def _make_ssm(name):
    nb = GROUPS_PER_BLOCK

    def fwd(u, wr, wi, cr, cin, lam_r, lam_i):
        L = u.shape[0]
        bur, bui = _bdmm([u], [[wr], [wi]], out_dtype=STATE_DTYPE, name=name + "_bu")
        S = bur.shape[1] // LANES
        xr, xi = _scan_fwd(bur.reshape(L, S, LANES), bui.reshape(L, S, LANES), lam_r, lam_i, name=name + "_scan")
        xr, xi = xr.reshape(L, S * LANES), xi.reshape(L, S * LANES)
        (y,) = _bdmm([xr, xi], [[cr, cin]], name=name + "_cx")
        return y, (u, wr, wi, cr, cin, lam_r, lam_i, xr, xi)

    def bwd(res, dy):
        u, wr, wi, cr, cin, lam_r, lam_i, xr, xi = res
        L = u.shape[0]
        S = xr.shape[1] // LANES
        tr = lambda w: jnp.swapaxes(w, 1, 2)
        gin_r, gin_i = _bdmm([dy], [[tr(cr)], [tr(cin)]], out_dtype=STATE_DTYPE, name=name + "_gin")
        gr, gi, dlr, dli = _scan_bwd(gin_r.reshape(L, S, LANES), gin_i.reshape(L, S, LANES), lam_r, lam_i,
                                     xr.reshape(L, S, LANES), xi.reshape(L, S, LANES), name=name + "_rscan")
        gr, gi = gr.reshape(L, S * LANES), gi.reshape(L, S * LANES)
        (du,) = _bdmm([gr, gi], [[tr(wr), tr(wi)]], name=name + "_du")
        dwr = _bdmm_tn(u, gr, nb, name=name + "_dwr")
        dwi = _bdmm_tn(u, gi, nb, name=name + "_dwi")
        dcr = _bdmm_tn(xr, dy, nb, name=name + "_dcr")
        dcin = _bdmm_tn(xi, dy, nb, name=name + "_dci")
        return du, dwr, dwi, dcr, dcin, dlr, dli

    return fwd, bwd


def _s5_discretize(a_re, a_im, log_dt, b_re, b_im, c_re, c_im):
    G, P, C, nb = S5_GROUPS, S5_STATE, S5_GROUP, GROUPS_PER_BLOCK
    dt = jnp.exp(log_dt)[:, None]
    mag = jnp.exp(dt * a_re)
    ang = dt * a_im
    abar_re = mag * jnp.cos(ang)
    abar_im = mag * jnp.sin(ang)
    den = a_re * a_re + a_im * a_im
    coef_re = ((abar_re - 1.0) * a_re + abar_im * a_im) / den
    coef_im = (abar_im * a_re - (abar_re - 1.0) * a_im) / den
    bbar_re = coef_re[..., None] * b_re - coef_im[..., None] * b_im
    bbar_im = coef_re[..., None] * b_im + coef_im[..., None] * b_re
    eye = jnp.eye(nb, dtype=F32)

    def blocks_in(bb):
        return jnp.einsum("jgpc,gh->jgchp", bb.reshape(G // nb, nb, P, C), eye).reshape(G // nb, nb * C, nb * P)

    def blocks_out(cc):
        return jnp.einsum("jgcp,gh->jgphc", cc.reshape(G // nb, nb, C, P), eye).reshape(G // nb, nb * P, nb * C)

    lam_r = abar_re.reshape(G * P // LANES, LANES)
    lam_i = abar_im.reshape(G * P // LANES, LANES)
    return blocks_in(bbar_re), blocks_in(bbar_im), blocks_out(c_re), blocks_out(-c_im), lam_r, lam_i


ATTN_Q_ROWS = 384


HEADS_PER_BLOCK = LANES // HEAD_DIM


def _head_masks(shape, axis):
    idx = lax.broadcasted_iota(jnp.int32, shape, axis) // HEAD_DIM
    return [idx == hh for hh in range(HEADS_PER_BLOCK)]


def _masked_bf16(x, masks):
    return [jnp.where(m, x, 0.0).astype(BF16) for m in masks]


def _tri_sum(x, tri2):
    hi = x.astype(BF16)
    lo = (x - hi.astype(F32)).astype(BF16)
    return _dot(jnp.concatenate([hi, lo], axis=1), tri2)


def _attn_weights(qh, kt, strict, after, c):
    z = _dot(qh, kt)
    lb = jnp.minimum(z, 0.0) - jnp.log(1.0 + jnp.exp(-jnp.abs(z)))
    l1m = lb - z
    if strict is not None:
        l1m = jnp.where(strict, l1m, 0.0)
    rem = _tri_sum(l1m, after)
    w = jnp.exp(lb + rem + c)
    if strict is not None:
        w = jnp.where(strict, w, 0.0)
    return lb, w, rem[:, 0:1] + l1m[:, 0:1]


def _tri(cmp):
    tk = ATTN_BLOCK
    tri = cmp(lax.broadcasted_iota(jnp.int32, (tk, tk), 0), lax.broadcasted_iota(jnp.int32, (tk, tk), 1)).astype(BF16)
    return jnp.concatenate([tri, tri], axis=0)


def _attn_tile_index(i, jb, nsub):
    return nsub * (i * (i + 1) // 2) + jb


def _attn_fwd(q, kt, v, *, name):
    L = q.shape[0]
    tk = ATTN_BLOCK
    tq = _pick(L, (ATTN_Q_ROWS, tk))
    nsub = tq // tk
    n_qt = L // tq
    n_tiles = _attn_tile_index(n_qt, 0, nsub)
    scale = 1.0 / math.sqrt(HEAD_DIM)

    def body(q_ref, kt_ref, v_ref, o_ref, w_hbm, s_hbm, acc_ref, w_stage, s_stage, sems):
        hp, i = pl.program_id(0), pl.program_id(1)
        after = _tri(lambda r, c: r > c)
        rowq = lax.broadcasted_iota(jnp.int32, (tq, tk), 0)
        colq = lax.broadcasted_iota(jnp.int32, (tq, tk), 1)
        qs = _masked_bf16(q_ref[...].astype(F32) * scale, _head_masks((tq, LANES), 1))
        vmasks = _head_masks((tk, LANES), 1)
        acc_ref[...] = jnp.zeros_like(acc_ref)

        def saves(buf, ii):
            cps = []
            for jd in range(nsub):
                idx = _attn_tile_index(i, ii * nsub + jd, nsub)
                cps.append(pltpu.make_async_copy(w_stage.at[buf, jd], w_hbm.at[hp, idx], sems.at[0, buf, jd]))
                cps.append(pltpu.make_async_copy(s_stage.at[buf, jd], s_hbm.at[hp, idx], sems.at[1, buf, jd]))
            return cps

        def tile(jb, buf, jd, strict, cs, first_row=0):
            r0 = pl.multiple_of(jb * tk, tk)
            rows = pl.ds(first_row, tq - first_row)
            kt_j = kt_ref[:, pl.ds(r0, tk)].astype(BF16)
            vs = _masked_bf16(v_ref[pl.ds(r0, tk), :], vmasks)
            out, acc = [], None
            for hh, (qh, vh, c) in enumerate(zip(qs, vs, cs)):
                lb, w, tot = _attn_weights(qh[first_row:], kt_j, None if strict is None else strict[first_row:], after,
                                           c[first_row:])
                w = w.astype(BF16)
                if first_row:
                    w_stage[buf, jd, hh, pl.ds(0, first_row), :] = jnp.zeros((first_row, tk), BF16)
                    s_stage[buf, jd, hh, pl.ds(0, first_row), :] = jnp.zeros((first_row, tk), BF16)
                w_stage[buf, jd, hh, rows, :] = w
                s_stage[buf, jd, hh, rows, :] = lb.astype(BF16)
                part = _dot(w, vh)
                acc = part if acc is None else acc + part
                out.append(jnp.concatenate([c[:first_row], c[first_row:] + tot], axis=0) if first_row else c + tot)
            acc_ref[rows, :] += acc
            return tuple(out)

        cs = tuple(jnp.zeros((tq, 1), F32) for _ in qs)
        for jd in reversed(range(nsub)):
            cs = tile(i * nsub + jd, 0, jd, (colq + jd * tk) < rowq, cs, first_row=jd * tk)
        for cp in saves(0, i):
            cp.start()

        def step(ii, cs):
            buf = (ii + 1) % 2

            @pl.when(ii >= 1)
            def _():
                for cp in saves(buf, 0):
                    cp.wait()

            for jd in reversed(range(nsub)):
                cs = tile((i - 1 - ii) * nsub + jd, buf, jd, None, cs)
            for cp in saves(buf, i - 1 - ii):
                cp.start()
            return cs

        lax.fori_loop(0, i, step, cs)
        o_ref[...] = acc_ref[...].astype(o_ref.dtype)
        for cp in saves(0, 0):
            cp.wait()

        @pl.when(i >= 1)
        def _():
            for cp in saves(1, 0):
                cp.wait()

    blk = pl.BlockSpec((tq, LANES), lambda h, i: (i, h))
    whole = pl.BlockSpec((L, LANES), lambda h, i: (0, h))
    whole_t = pl.BlockSpec((LANES, L), lambda h, i: (h, 0))
    any_spec = pl.BlockSpec(memory_space=pl.ANY)
    nhb = N_HEADS // HEADS_PER_BLOCK
    saved = jax.ShapeDtypeStruct((nhb, n_tiles, HEADS_PER_BLOCK, tq, tk), BF16)
    return _pallas(
        body, name=name, grid=(nhb, n_qt), in_specs=[blk, whole_t, whole], out_specs=[blk, any_spec, any_spec],
        out_shape=[jax.ShapeDtypeStruct(q.shape, BF16), saved, saved],
        scratch_shapes=[pltpu.VMEM((tq, LANES), F32), pltpu.VMEM((2, nsub, HEADS_PER_BLOCK, tq, tk), BF16),
                        pltpu.VMEM((2, nsub, HEADS_PER_BLOCK, tq, tk), BF16), pltpu.SemaphoreType.DMA((2, 2, nsub))],
        compiler_params=_cparams("parallel", "arbitrary"),
    )(q, kt, v)


def _attn_bwd(q, k, vt, do, w_saved, s_saved, *, name):
    L = q.shape[0]
    tk = ATTN_BLOCK
    tq = _pick(L, (ATTN_Q_ROWS, tk))
    nsub = tq // tk
    scale = 1.0 / math.sqrt(HEAD_DIM)

    def body(q_ref, k_ref, vt_ref, do_ref, w_hbm, s_hbm, dq_ref, dkt_ref, dvt_ref, e_scr, sig_scr, dq_acc_ref, w_stage, sems):
        hp, i = pl.program_id(0), pl.program_id(1)

        @pl.when(i == 0)
        def _():
            dkt_ref[...] = jnp.zeros_like(dkt_ref)
            dvt_ref[...] = jnp.zeros_like(dvt_ref)

        before = _tri(lambda r, c: r < c)
        rowq = lax.broadcasted_iota(jnp.int32, (tq, tk), 0)
        colq = lax.broadcasted_iota(jnp.int32, (tq, tk), 1)
        qmasks = _head_masks((tq, LANES), 1)
        tmasks = _head_masks((LANES, tq), 0)
        kmasks = _head_masks((tk, LANES), 1)
        q_scaled = q_ref[...].astype(F32) * scale
        do32 = do_ref[...].astype(F32)
        dos = _masked_bf16(do32, qmasks)
        qts = _masked_bf16(q_scaled.T, tmasks)
        dots = _masked_bf16(do32.T, tmasks)
        dq_acc_ref[...] = jnp.zeros_like(dq_acc_ref)

        def loads(buf, ii):
            cps = []
            for jd in range(nsub):
                jb = ii * nsub + jd
                idx = _attn_tile_index(i, jb, nsub)
                cps.append(pltpu.make_async_copy(w_hbm.at[hp, idx], w_stage.at[buf, jd], sems.at[0, buf, jd]))
                cps.append(pltpu.make_async_copy(s_hbm.at[hp, idx], sig_scr.at[jb], sems.at[1, buf, jd]))
            return cps

        def weigh(buf, ii):
            for jd in range(nsub):
                jb = ii * nsub + jd
                r0 = pl.multiple_of(jb * tk, tk)
                vt_j = vt_ref[:, pl.ds(r0, tk)].astype(BF16)
                acc = None
                for hh, (dob, dot_h) in enumerate(zip(dos, dots)):
                    w = w_stage[buf, jd, hh]
                    e_scr[hh, jb] = _dot(dob, vt_j) * w.astype(F32)
                    part = _dot(dot_h, w)
                    acc = part if acc is None else acc + part
                dvt_ref[:, pl.ds(r0, tk)] += acc

        def tile_right(jb, strict, cs, first_row=0):
            r0 = pl.multiple_of(jb * tk, tk)
            rows = pl.ds(first_row, tq - first_row)
            ks = _masked_bf16(k_ref[pl.ds(r0, tk), :], kmasks)
            out, dq_acc, dk_acc = [], None, None
            for hh, (kh, qt_h, c2) in enumerate(zip(ks, qts, cs)):
                e = e_scr[hh, jb, rows, :]
                sig = jnp.exp(sig_scr[jb, hh, rows, :].astype(F32))
                left = _dot(e.astype(BF16), before[:tk])
                dz = e - (e + left + c2[first_row:]) * sig
                if strict is not None:
                    dz = jnp.where(strict[first_row:], dz, 0.0)
                dz = dz.astype(BF16)
                p1, p2 = _dot(dz, kh), _dot(qt_h[:, first_row:], dz)
                dq_acc = p1 if dq_acc is None else dq_acc + p1
                dk_acc = p2 if dk_acc is None else dk_acc + p2
                tot = left[:, tk - 1:tk] + e[:, tk - 1:tk]
                out.append(jnp.concatenate([c2[:first_row], c2[first_row:] + tot], axis=0) if first_row else c2 + tot)
            dq_acc_ref[rows, :] += dq_acc
            dkt_ref[:, pl.ds(r0, tk)] += dk_acc
            return tuple(out)

        for cp in loads(0, 0):
            cp.start()

        def weigh_step(ii, carry):
            buf = ii % 2
            for cp in loads(1 - buf, ii + 1):
                cp.start()
            for cp in loads(buf, ii):
                cp.wait()
            weigh(buf, ii)
            return carry

        lax.fori_loop(0, i, weigh_step, 0)
        for cp in loads(i % 2, i):
            cp.wait()
        weigh(i % 2, i)

        def right_step(ii, cs):
            for jd in range(nsub):
                cs = tile_right(ii * nsub + jd, None, cs)
            return cs

        cs = lax.fori_loop(0, i, right_step, tuple(jnp.zeros((tq, 1), F32) for _ in dos))
        for jd in range(nsub):
            cs = tile_right(i * nsub + jd, (colq + jd * tk) < rowq, cs, first_row=jd * tk)
        dq_ref[...] = (dq_acc_ref[...] * scale).astype(dq_ref.dtype)

    nhb = N_HEADS // HEADS_PER_BLOCK
    blk = pl.BlockSpec((tq, LANES), lambda h, i: (i, h))
    whole = pl.BlockSpec((L, LANES), lambda h, i: (0, h))
    whole_t = pl.BlockSpec((LANES, L), lambda h, i: (h, 0))
    t_shape = jax.ShapeDtypeStruct((q.shape[1], L), F32)
    any_spec = pl.BlockSpec(memory_space=pl.ANY)
    return _pallas(
        body, name=name, grid=(nhb, L // tq), in_specs=[blk, whole, whole_t, blk, any_spec, any_spec],
        out_specs=[blk, whole_t, whole_t], out_shape=[jax.ShapeDtypeStruct(q.shape, BF16), t_shape, t_shape],
        scratch_shapes=[pltpu.VMEM((HEADS_PER_BLOCK, L // tk, tq, tk), F32), pltpu.VMEM((L // tk, HEADS_PER_BLOCK, tq, tk), BF16),
                        pltpu.VMEM((tq, LANES), F32), pltpu.VMEM((2, nsub, HEADS_PER_BLOCK, tq, tk), BF16),
                        pltpu.SemaphoreType.DMA((2, 2, nsub))],
        compiler_params=_cparams("parallel", "arbitrary"),
    )(q, k, vt, do, w_saved, s_saved)


def _make_loss_head(n_valid, name):
    def run(h, g, target):
        L, d = h.shape
        tm = _pick(L, (384, 128))

        def body(h_ref, t_ref, g_ref, loss_ref, dh_ref, dg_ref):
            i = pl.program_id(0)
            x = h_ref[...]
            gb = g_ref[...]
            rows = lax.broadcasted_iota(jnp.int32, (tm, 1), 0) + i * tm
            valid = jnp.logical_and(rows >= N_META, rows < n_valid)
            r = lax.rsqrt(jnp.mean(x * x, axis=-1, keepdims=True) + RMS_EPS)
            xh = x * r
            err = jnp.where(valid, xh * gb - t_ref[...], 0.0)
            dy = err * (1.0 / d)
            dxh = dy * gb
            dh_ref[...] = r * (dxh - xh * jnp.mean(dxh * xh, axis=-1, keepdims=True))
            part = 0.5 / d * jnp.sum(jnp.sum(err * err, axis=1, keepdims=True), axis=0, keepdims=True)
            dgp = _colsum(dy * xh)

            @pl.when(i == 0)
            def _():
                loss_ref[...] = part
                dg_ref[...] = dgp

            @pl.when(i > 0)
            def _():
                loss_ref[...] += part
                dg_ref[...] += dgp

        rowspec = pl.BlockSpec((tm, d), lambda i: (i, 0))
        return _pallas(
            body, name=name, grid=(L // tm,),
            in_specs=[rowspec, rowspec, pl.BlockSpec((1, d), lambda i: (0, 0))],
            out_specs=[pl.BlockSpec((1, 1), lambda i: (0, 0)), rowspec, pl.BlockSpec((1, d), lambda i: (0, 0))],
            out_shape=[jax.ShapeDtypeStruct((1, 1), F32), jax.ShapeDtypeStruct((L, d), F32), jax.ShapeDtypeStruct((1, d), F32)],
            compiler_params=_cparams("arbitrary"),
        )(h, target, g.reshape(1, d))

    return run


def _mm(a, b, *, ta=False, tb=False, out_dtype=F32, extras=(), epilogue=None, name):
    n = b.shape[0] if tb else b.shape[1]
    return _gmm([(a, ta)], [(b, tb)], [[(0, 0, 0)]], n=n, extras=extras, epilogue=epilogue, out_dtypes=(out_dtype,), name=name)[0]


def _add_epilogue(acc, resid):
    return (acc + resid,)


def _make_ffn_block(name):
    def fwd(h, gain, w_in, w_out):
        dff = w_out.shape[0]
        f = _rmsnorm_fwd(h, gain, out_dtype=BF16, name=name + "_norm")

        def swiglu(g, u):
            return g, u, g * _sigmoid(g) * u

        g, u, a = _gmm([(f, False)], [(w_in, False)], [[(0, 0, 0)], [(0, 0, dff)]], n=dff, epilogue=swiglu,
                       out_dtypes=(BF16, BF16, BF16), name=name + "_in")
        h2 = _mm(a, w_out, extras=[h], epilogue=_add_epilogue, name=name + "_out")
        return h2, (h, gain, f, g, u, a, w_in, w_out)

    def bwd(res, dh2, emit):
        h, gain, f, g, u, a, w_in, w_out = res
        dff, dm = w_out.shape

        def dswiglu(da, gb, ub):
            gb, ub = gb.astype(F32), ub.astype(F32)
            s = _sigmoid(gb)
            return da * ub * (s + gb * s * (1.0 - s)), da * gb * s

        dg, du = _gmm([(dh2, False)], [(w_out, True)], [[(0, 0, 0)]], n=dff, extras=[g, u], epilogue=dswiglu,
                      out_dtypes=(BF16, BF16), name=name + "_da")
        dw_out = _mm(a, dh2, ta=True, name=name + "_dwout")
        (df,) = _gmm([(dg, False), (du, False)], [(w_in, True)], [[(0, 0, 0), (1, 0, dff)]], n=dm, name=name + "_df")
        dw_in = jnp.concatenate([_mm(f, dg, ta=True, name=name + "_dwg"), _mm(f, du, ta=True, name=name + "_dwu")], axis=1)
        tok = emit({"w_ffn_in": dw_in, "w_ffn_out": dw_out})
        dh, dgain = _rmsnorm_bwd(h, gain + tok, df, dh2, name=name + "_dnorm")
        return dh, dgain

    return fwd, bwd


def _make_s5_out_block(name):
    def post(yb, ub, db):
        s = yb + db * ub
        return (0.5 * s * (1.0 + jnp.tanh(GELU_K * (s + GELU_C * s * s * s))),)

    def fwd(y, u, d, h, w_glu):
        L, dm = y.shape
        (z,) = _rowwise(post, [y, u], [d.reshape(1, dm)], [((L, dm), BF16)], [], tm=_pick(L, (384, 128)), name=name + "_gelu")

        def glu(val, gate, resid):
            return resid + val * _sigmoid(gate), val, gate

        h2, val, gate = _gmm([(z, False)], [(w_glu, False)], [[(0, 0, 0)], [(0, 0, dm)]], n=dm, extras=[h], epilogue=glu,
                             out_dtypes=(F32, BF16, BF16), name=name + "_glu")
        return h2, (y, u, d, z, val, gate, w_glu)

    def bwd(res, dh2, emit):
        y, u, d, z, val, gate, w_glu = res
        L, dm = y.shape

        def dglu(dh2b, valb, gateb):
            s = _sigmoid(gateb.astype(F32))
            return dh2b * s, dh2b * valb.astype(F32) * s * (1.0 - s)

        dval, dgate = _rowwise(dglu, [dh2, val, gate], [], [((L, dm), BF16)] * 2, [], tm=_pick(L, (384, 128)), name=name + "_dglu")
        (dz,) = _gmm([(dval, False), (dgate, False)], [(w_glu, True)], [[(0, 0, 0), (1, 0, dm)]], n=dm, name=name + "_dz")

        def dpost(yb, ub, dzb, db):
            s = yb + db * ub
            t = jnp.tanh(GELU_K * (s + GELU_C * s * s * s))
            ds = dzb * (0.5 * (1.0 + t) + 0.5 * s * (1.0 - t * t) * GELU_K * (1.0 + 3.0 * GELU_C * s * s))
            return ds, ds * db, _colsum(ds * ub)

        dw = jnp.concatenate([_mm(z, dval, ta=True, name=name + "_dwv"), _mm(z, dgate, ta=True, name=name + "_dwg")], axis=1)
        tok = emit({"s5_w_glu": dw})
        dy, du, dd = _rowwise(dpost, [y, u, dz], [(d + tok).reshape(1, dm)], [((L, dm), F32)] * 2, [((1, dm), F32)],
                              tm=_pick(L, (384, 128)), name=name + "_dgelu")
        return dy, du, dd.reshape(d.shape)

    return fwd, bwd


def _make_attn_layer(name):
    def fwd(h, g_kv, g_q, w_k, w_v, w_q, w_o):
        dm = h.shape[1]
        kvn = _rmsnorm_fwd(h, g_kv, out_dtype=BF16, name=name + "_kvnorm")
        qn = _rmsnorm_fwd(h, g_q, out_dtype=BF16, name=name + "_qnorm")
        k, v = _gmm([(kvn, False)], [(w_k, False), (w_v, False)], [[(0, 0, 0)], [(0, 1, 0)]], n=dm,
                    out_dtypes=(BF16, BF16), name=name + "_kv")
        q = _mm(qn, w_q, out_dtype=BF16, name=name + "_q")
        o, w_saved, s_saved = _attn_fwd(q, k.T, v, name=name + "_fwd")
        h2 = _mm(o, w_o, extras=[h], epilogue=_add_epilogue, name=name + "_o")
        return h2, (h, g_kv, g_q, kvn, qn, q, k, v.T, o, w_saved, s_saved, w_k, w_v, w_q, w_o)

    def bwd(res, dh2, emit):
        h, g_kv, g_q, kvn, qn, q, k, vt, o, w_saved, s_saved, w_k, w_v, w_q, w_o = res
        dm = h.shape[1]
        do = _mm(dh2, w_o, tb=True, out_dtype=BF16, name=name + "_do")
        dw_o = _mm(o, dh2, ta=True, name=name + "_dwo")
        dq, dkt, dvt = _attn_bwd(q, k, vt, do, w_saved, s_saved, name=name + "_bwd")
        dqn = _mm(dq, w_q, tb=True, name=name + "_dqn")
        dw_q = _mm(qn, dq, ta=True, name=name + "_dwq")
        (dkvn,) = _gmm([(dkt, True), (dvt, True)], [(w_k, True), (w_v, True)], [[(0, 0, 0), (1, 1, 0)]], n=dm, name=name + "_dkvn")
        dw_k = _mm(kvn, dkt, ta=True, tb=True, name=name + "_dwk")
        dw_v = _mm(kvn, dvt, ta=True, tb=True, name=name + "_dwv")
        tok = emit({"w_k": dw_k, "w_v": dw_v, "w_q": dw_q, "w_o": dw_o})
        dh, dg_q = _rmsnorm_bwd(h, g_q + tok, dqn, dh2, name=name + "_dqnorm")
        dh, dg_kv = _rmsnorm_bwd(h, g_kv, dkvn, dh, name=name + "_dkvnorm")
        return dh, dg_kv, dg_q

    return fwd, bwd


S5_PARAMS = ("s5_a_re", "s5_a_im", "s5_log_dt", "s5_b_re", "s5_b_im", "s5_c_re", "s5_c_im")


def _forward_backward(x_pad, target_pad, n_valid, small, get_weights, put_grads):
    gs = {}
    h0 = x_pad.at[:N_META].set(small["meta_tokens"])

    ssm_fwd, ssm_bwd = _make_ssm("ssm")
    s5out_fwd, s5out_bwd = _make_s5_out_block("s5out")
    ffn_fwd, ffn_bwd = zip(_make_ffn_block("ffn0"), _make_ffn_block("ffn1"))
    attn_fwd, attn_bwd = _make_attn_layer("attn")
    u = _rmsnorm_fwd(h0, small["norm_mix"][0], out_dtype=F32, name="mix0_norm")
    disc, disc_vjp = jax.vjp(_s5_discretize, *[small[n][0] for n in S5_PARAMS])
    y, ssm_res = ssm_fwd(u, *disc)
    w = get_weights("s5", y)
    h1, s5out_res = s5out_fwd(y, u, small["s5_d"][0], h0, w["s5_w_glu"])
    w = get_weights("ffn0", h1)
    h2, ffn0_res = ffn_fwd[0](h1, small["norm_ffn"][0], w["w_ffn_in"], w["w_ffn_out"])
    w = get_weights("attn", h2)
    h3, attn_res = attn_fwd(h2, small["norm_kv"], small["norm_mix"][1], w["w_k"], w["w_v"], w["w_q"], w["w_o"])
    w = get_weights("ffn1", h3)
    h4, ffn1_res = ffn_fwd[1](h3, small["norm_ffn"][1], w["w_ffn_in"], w["w_ffn_out"])
    loss, dh, dg_final = _make_loss_head(n_valid, "loss_head")(h4, small["norm_final"], target_pad)
    gs["norm_final"] = dg_final.reshape(small["norm_final"].shape)

    dh, dg_ffn1 = ffn_bwd[1](ffn1_res, dh, functools.partial(put_grads, "ffn1"))
    dh, gs["norm_kv"], dg_mix1 = attn_bwd(attn_res, dh, functools.partial(put_grads, "attn"))
    dh, dg_ffn0 = ffn_bwd[0](ffn0_res, dh, functools.partial(put_grads, "ffn0"))
    dy, du, dd = s5out_bwd(s5out_res, dh, functools.partial(put_grads, "s5"))
    du_ssm, *ddisc = ssm_bwd(ssm_res, dy)
    dh, dg_mix0 = _rmsnorm_bwd(h0, small["norm_mix"][0], du + du_ssm, dh, name="mix0_dnorm")
    for n, g in zip(S5_PARAMS, disc_vjp(tuple(ddisc))):
        gs[n] = g[None]
    gs["s5_d"] = dd.reshape(small["s5_d"].shape)
    gs["norm_mix"] = jnp.stack([dg_mix0, dg_mix1])
    gs["norm_ffn"] = jnp.stack([dg_ffn0, dg_ffn1])
    gs["meta_tokens"] = dh[:N_META]
    return loss, dh, gs


def _coords():
    return lax.axis_index("x"), lax.axis_index("y"), lax.axis_index("c")


def _flip(bits):
    x, y, c = _coords()
    fx, fy, fc = bits
    return (x ^ fx if fx else x, y ^ fy if fy else y, c ^ fc if fc else c)


def _exchange(ins, out_shapes, copies, local_copies, *, aliases=None, name):
    n_in, n_out = len(ins), len(out_shapes)
    n_cp, n_loc = len(copies), len(local_copies)
    aliases = aliases or {}

    def body(*refs):
        in_refs, out_refs = refs[:n_in], refs[n_in:n_in + n_out]
        send_sems, recv_sems, loc_sems = refs[n_in + n_out:]
        me = _coords()
        locs = []
        for n, (ii, oi, dfn) in enumerate(local_copies):
            cp = pltpu.make_async_copy(in_refs[ii], out_refs[oi].at[dfn(*me)], loc_sems.at[n])
            cp.start()
            locs.append(cp)
        sends = []
        for n, (ii, sfn, bits, oi, dfn) in enumerate(copies):
            src = in_refs[ii] if sfn is None else in_refs[ii].at[sfn(*me)]
            cp = pltpu.make_async_remote_copy(
                src_ref=src, dst_ref=out_refs[oi].at[dfn(*me)], send_sem=send_sems.at[n], recv_sem=recv_sems.at[n],
                device_id=_flip(bits), device_id_type=MESH)
            cp.start()
            sends.append(cp)
        for n, (ii, sfn, bits, oi, dfn) in enumerate(copies):
            peer = _flip(bits)
            src = in_refs[ii] if sfn is None else in_refs[ii].at[sfn(*me)]
            pltpu.make_async_remote_copy(
                src_ref=src, dst_ref=out_refs[oi].at[dfn(*peer)], send_sem=send_sems.at[n], recv_sem=recv_sems.at[n],
                device_id=peer, device_id_type=MESH).wait_recv()
        for cp in sends:
            cp.wait_send()
        for cp in locs:
            cp.wait()

    any_spec = pl.BlockSpec(memory_space=pl.ANY)
    return _pallas(
        body, name=name, in_specs=[any_spec] * n_in, out_specs=[any_spec] * n_out,
        out_shape=[jax.ShapeDtypeStruct(s, d) for s, d in out_shapes],
        scratch_shapes=[pltpu.SemaphoreType.DMA((max(n_cp, 1),)), pltpu.SemaphoreType.DMA((max(n_cp, 1),)),
                        pltpu.SemaphoreType.DMA((max(n_loc, 1),))],
        input_output_aliases=aliases,
        compiler_params=pltpu.CompilerParams(has_side_effects=True),
    )(*ins)


ICI_FLIPS = ((1, 0, 0), (0, 1, 0), (1, 1, 0))
D2D_FLIP = (0, 0, 1)


def _slot_of(x, y, c):
    return 4 * x + 2 * y + c


def _all_gather(shards, *, name):
    n = len(shards)
    outs = [((N_DEV,) + s.shape, s.dtype) for s in shards]
    copies = [(t, None, bits, t, _slot_of) for t in range(n) for bits in ICI_FLIPS]
    local = [(t, t, _slot_of) for t in range(n)]
    bufs = _exchange(shards, outs, copies, local, name=name + "_ici")
    copies2 = [(t, (lambda x, y, c, q=q: 2 * q + c), D2D_FLIP, t, (lambda x, y, c, q=q: 2 * q + c))
               for t in range(n) for q in range(4)]
    return _exchange(bufs, outs, copies2, [], aliases={t: t for t in range(n)}, name=name + "_d2d")


ALL_FLIPS = tuple((m >> 2 & 1, m >> 1 & 1, m & 1) for m in range(1, N_DEV))
HBM_SPEC = pl.BlockSpec(memory_space=pltpu.HBM)
SEM_SPEC = pl.BlockSpec(memory_space=pltpu.SEMAPHORE)
SIDE_EFFECT = pltpu.SideEffectType.DATAFLOW_SIDE_EFFECTING


def _copy_desc(copies, n, src_refs, land_refs, send_sems, recv_sems, sender):
    si, sfn, bits, li, dfn = copies[n]
    src = src_refs[si] if sfn is None else src_refs[si].at[sfn(*sender)]
    return pltpu.make_async_remote_copy(
        src_ref=src, dst_ref=land_refs[li].at[dfn(*sender)], send_sem=send_sems.at[n], recv_sem=recv_sems.at[n],
        device_id=_flip(bits), device_id_type=MESH)


def _start_copies(srcs, lands, copies, *, name):
    n_s, n_l, n_c = len(srcs), len(lands), len(copies)

    def body(*refs):
        src_refs = refs[:n_s]
        send_sems, recv_sems = refs[n_s], refs[n_s + 1]
        land_refs = refs[2 * n_s + 2:2 * n_s + 2 + n_l]
        token = refs[-1]
        me = _coords()
        for n in range(n_c):
            _copy_desc(copies, n, src_refs, land_refs, send_sems, recv_sems, me).start()
        token[...] = jnp.zeros_like(token)

    res = _pallas(
        body, name=name,
        out_shape=(pltpu.SemaphoreType.DMA((n_c,)), pltpu.SemaphoreType.DMA((n_c,)),
                   *[pltpu.HBM(s.shape, s.dtype) for s in srcs], *[pltpu.HBM(shape, dtype) for shape, dtype in lands],
                   jax.ShapeDtypeStruct((8, LANES), F32)),
        in_specs=[HBM_SPEC] * n_s,
        out_specs=(SEM_SPEC, SEM_SPEC, *[HBM_SPEC] * (n_s + n_l), pl.BlockSpec(memory_space=pltpu.VMEM)),
        input_output_aliases={i: 2 + i for i in range(n_s)},
        compiler_params=pltpu.CompilerParams(has_side_effects=SIDE_EFFECT),
    )(*[pltpu.with_memory_space_constraint(s, pltpu.HBM) for s in srcs])
    return res[0], res[1], list(res[2:2 + n_s]), list(res[2 + n_s:2 + n_s + n_l]), res[-1]


def _wait_copies(send_sems, recv_sems, srcs, lands, copies, which, after, *, name):
    src_ids, land_ids = sorted(srcs), sorted(lands)
    n_s, n_l = len(src_ids), len(land_ids)
    srcs, lands = [srcs[i] for i in src_ids], [lands[i] for i in land_ids]

    def body(*refs):
        src_refs, land_refs = dict(zip(src_ids, refs[:n_s])), dict(zip(land_ids, refs[n_s:n_s + n_l]))
        s_sems, r_sems = refs[n_s + n_l], refs[n_s + n_l + 1]
        me = _coords()
        for n in which:
            cp = _copy_desc(copies, n, src_refs, land_refs, s_sems, r_sems, me)
            cp.wait_send()
            _copy_desc(copies, n, src_refs, land_refs, s_sems, r_sems, _flip(copies[n][2])).wait_recv()

    res = _pallas(
        body, name=name,
        out_shape=tuple(pltpu.HBM(b.shape, b.dtype) for b in list(srcs) + list(lands)),
        in_specs=[HBM_SPEC] * (n_s + n_l) + [SEM_SPEC, SEM_SPEC, pl.BlockSpec(memory_space=pl.ANY)],
        out_specs=tuple([HBM_SPEC] * (n_s + n_l)),
        input_output_aliases={i: i for i in range(n_s + n_l)},
        compiler_params=pltpu.CompilerParams(has_side_effects=SIDE_EFFECT),
    )(*srcs, *lands, send_sems, recv_sems, after)
    return dict(zip(src_ids, res[:n_s])), dict(zip(land_ids, res[n_s:]))


def _adam_math(w, g, m, v):
    m = ADAM_B1 * m + (1.0 - ADAM_B1) * g
    v = ADAM_B2 * v + (1.0 - ADAM_B2) * (g * g)
    m_hat = m / (1.0 - ADAM_B1 ** ADAM_STEP)
    v_hat = v / (1.0 - ADAM_B2 ** ADAM_STEP)
    delta = -ADAM_LR * (m_hat / (jnp.sqrt(v_hat) + ADAM_EPS) + ADAM_WD * w)
    return delta, m, v


def _shard_adamw(own, landing, w, m, v, *, name):
    r, cdim = w.shape
    tr = _pick(r, (256, 128, 64, 32, 16))

    def body(g_ref, l_ref, w_ref, m_ref, v_ref, g_out, d_out, m_out, v_out):
        g = g_ref[...]
        for n in range(N_DEV - 1):
            g = g + l_ref[n].astype(F32)
        d, mn, vn = _adam_math(w_ref[...], g, m_ref[...], v_ref[...])
        g_out[...] = g
        d_out[...] = d
        m_out[...] = mn
        v_out[...] = vn

    blk = pl.BlockSpec((tr, cdim), lambda i: (i, 0))
    return _pallas(
        body, name=name, grid=(r // tr,),
        in_specs=[blk, pl.BlockSpec((N_DEV - 1, tr, cdim), lambda i: (0, i, 0)), blk, blk, blk], out_specs=[blk] * 4,
        out_shape=[jax.ShapeDtypeStruct((r, cdim), F32)] * 4,
        compiler_params=_cparams("parallel"),
    )(own, landing, w, m, v)


def _small_adamw(slots, w, m, v, *, name):
    r, cdim = w.shape

    def body(s_ref, w_ref, m_ref, v_ref, g_out, d_out, m_out, v_out):
        g = s_ref[0]
        for n in range(1, N_DEV):
            g = g + s_ref[n]
        d, mn, vn = _adam_math(w_ref[...], g, m_ref[...], v_ref[...])
        g_out[...] = g
        d_out[...] = d
        m_out[...] = mn
        v_out[...] = vn

    tr = _pick(r, (256, 128, 64, 32, 16, 8))
    blk = pl.BlockSpec((tr, cdim), lambda i: (i, 0))
    return _pallas(
        body, name=name, grid=(r // tr,),
        in_specs=[pl.BlockSpec((N_DEV, tr, cdim), lambda i: (0, i, 0)), blk, blk, blk], out_specs=[blk] * 4,
        out_shape=[jax.ShapeDtypeStruct((r, cdim), F32)] * 4,
        compiler_params=_cparams("parallel"),
    )(slots, w, m, v)


def _plain_adamw(g, w, m, v, *, name):
    def body(g_ref, w_ref, m_ref, v_ref, d_out, m_out, v_out):
        d, mn, vn = _adam_math(w_ref[...], g_ref[...], m_ref[...], v_ref[...])
        d_out[...] = d
        m_out[...] = mn
        v_out[...] = vn

    spec = pl.BlockSpec(g.shape, lambda: (0,) * g.ndim)
    return _pallas(body, name=name, in_specs=[spec] * 4, out_specs=[spec] * 3,
                   out_shape=[jax.ShapeDtypeStruct(g.shape, F32)] * 3)(g, w, m, v)


def _cast_bf16(x, *, name):
    r, cdim = x.shape
    tr = _pick(r, (256, 128, 64, 32, 16))

    def body(x_ref, o_ref):
        o_ref[...] = x_ref[...].astype(BF16)

    return _pallas(body, name=name, grid=(r // tr,), in_specs=[pl.BlockSpec((tr, cdim), lambda i: (i, 0))],
                   out_specs=pl.BlockSpec((tr, cdim), lambda i: (i, 0)), out_shape=jax.ShapeDtypeStruct(x.shape, BF16),
                   compiler_params=_cparams("parallel"))(x)


SMALL_NAMES = ("norm_mix", "norm_ffn", "s5_a_re", "s5_a_im", "s5_log_dt", "s5_b_re", "s5_b_im", "s5_c_re", "s5_c_im",
               "norm_kv", "norm_final")


def _pack_rows(arrs):
    rows = []
    for a in arrs:
        flat = a.reshape(-1)
        pad = (-flat.shape[0]) % (8 * LANES)
        rows.append(jnp.pad(flat, (0, pad)).reshape(-1, LANES))
    return jnp.concatenate(rows, axis=0)


def _unpack_rows(packed, like):
    out, r0 = [], 0
    for a in like:
        n = math.prod(a.shape)
        nr = (n + 8 * LANES - 1) // (8 * LANES) * 8
        out.append(packed[r0:r0 + nr].reshape(-1)[:n].reshape(a.shape))
        r0 += nr
    return out


def kernel(x, meta_tokens, norm_mix, norm_ffn, s5_a_re, s5_a_im, s5_log_dt, s5_b_re, s5_b_im, s5_c_re, s5_c_im, s5_d, s5_w_glu, norm_kv, w_kv, w_q, w_o, w_ffn_in, w_ffn_out, norm_final, loss_target, m_meta_tokens, m_norm_mix, m_norm_ffn, m_s5_a_re, m_s5_a_im, m_s5_log_dt, m_s5_b_re, m_s5_b_im, m_s5_c_re, m_s5_c_im, m_s5_d, m_s5_w_glu, m_norm_kv, m_w_kv, m_w_q, m_w_o, m_w_ffn_in, m_w_ffn_out, m_norm_final, v_meta_tokens, v_norm_mix, v_norm_ffn, v_s5_a_re, v_s5_a_im, v_s5_log_dt, v_s5_b_re, v_s5_b_im, v_s5_c_re, v_s5_c_im, v_s5_d, v_s5_w_glu, v_norm_kv, v_w_kv, v_w_q, v_w_o, v_w_ffn_in, v_w_ffn_out, v_norm_final):
    args = dict(locals())
    seq = x.shape[1]
    n_valid = N_META + seq
    Lp = (n_valid + ATTN_BLOCK - 1) // ATTN_BLOCK * ATTN_BLOCK
    dm = D_MODEL
    my_slot = _slot_of(*_coords())

    n_ffn = w_ffn_in.shape[0]
    groups = {"s5": ["s5_w_glu"], "ffn0": ["w_ffn_in0", "w_ffn_out0"], "attn": ["w_kv", "w_q", "w_o"],
              "ffn1": ["w_ffn_in1", "w_ffn_out1"]}
    shards = {"s5_w_glu": s5_w_glu[0], "w_kv": w_kv, "w_q": w_q[0], "w_o": w_o[0]}
    for l in range(n_ffn):
        shards[f"w_ffn_in{l}"], shards[f"w_ffn_out{l}"] = w_ffn_in[l], w_ffn_out[l]
    by_columns = {"s5_w_glu", "w_kv", "w_ffn_in0", "w_ffn_in1"}
    names = [n for g in groups.values() for n in g]

    def gathered_full(n, g8):
        if n in by_columns:
            return jnp.swapaxes(g8, 0, 1).reshape(g8.shape[1], -1)
        return g8.reshape(-1, g8.shape[2])

    col_shard = jnp.concatenate([meta_tokens, s5_d, jnp.zeros((7, LANES), F32)], axis=0)
    ag_srcs = [col_shard] + [_cast_bf16(shards[n], name="cast_" + n) for n in names]
    ag_copies = [(t, None, bits, t, _slot_of) for t in range(len(ag_srcs)) for bits in ALL_FLIPS]
    ag_send, ag_recv, ag_srcs, ag_lands, ag_token = _start_copies(
        ag_srcs, [((N_DEV,) + s.shape, s.dtype) for s in ag_srcs], ag_copies, name="ag_start")

    def gather_wait(tag, ids, after):
        which = [t * len(ALL_FLIPS) + r for t in ids for r in range(len(ALL_FLIPS))]
        srcs, lands = _wait_copies(ag_send, ag_recv, {t: ag_srcs[t] for t in ids}, {t: ag_lands[t] for t in ids},
                                   ag_copies, which, after, name="ag_wait_" + tag)
        return {t: lax.dynamic_update_index_in_dim(lands[t], srcs[t], my_slot, 0) for t in ids}

    def get_weights(group, after):
        ids = [1 + names.index(n) for n in groups[group]]
        full = {names[t - 1]: gathered_full(names[t - 1], g8) for t, g8 in gather_wait(group, ids, after).items()}
        if group == "attn":
            w_kv_full = full.pop("w_kv")
            full["w_k"], full["w_v"] = w_kv_full[:, :dm], w_kv_full[:, dm:]
        return {n.rstrip("01") if n.startswith("w_ffn") else n: w for n, w in full.items()}

    gcol = gather_wait("cols", [0], ag_token)[0]
    small = {n: args[n] for n in SMALL_NAMES}
    small["meta_tokens"] = jnp.swapaxes(gcol[:, :N_META], 0, 1).reshape(N_META, dm)
    small["s5_d"] = gcol[:, N_META].reshape(1, dm)

    scatters = {}

    def put_grads(group, grads):
        if group == "attn":
            grads = {"w_kv": jnp.concatenate([grads.pop("w_k"), grads.pop("w_v")], axis=1), **grads}
        own, send = {}, []
        for n in groups[group]:
            g = grads[n.rstrip("01") if n.startswith("w_ffn") else n]
            r, c = shards[n].shape
            if n in by_columns:
                own[n] = lax.dynamic_slice_in_dim(g, my_slot * c, c, axis=1)
                send.append(jnp.swapaxes(g.reshape(r, N_DEV, c), 0, 1).astype(BF16))
            else:
                own[n] = lax.dynamic_slice_in_dim(g, my_slot * r, r, axis=0)
                send.append(g.reshape(N_DEV, r, c).astype(BF16))
        copies = [(t, (lambda x, y, c, b=bits: _slot_of(x ^ b[0], y ^ b[1], c ^ b[2])), bits, t, (lambda x, y, c, k=k: k))
                  for t in range(len(send)) for k, bits in enumerate(ALL_FLIPS)]
        lands = [((len(ALL_FLIPS),) + s.shape[1:], BF16) for s in send]
        s_sem, r_sem, srcs, lands, token = _start_copies(send, lands, copies, name="rs_start_" + group)
        scatters[group] = (s_sem, r_sem, srcs, lands, copies, own)
        return token[0, 0]

    x_pad = jnp.pad(x[0], ((N_META, Lp - n_valid), (0, 0)))
    t_pad = jnp.pad(loss_target[0], ((N_META, Lp - n_valid), (0, 0)))
    loss_local, g_xpad, g_small = _forward_backward(x_pad, t_pad, n_valid, small, get_weights, put_grads)
    loss = lax.psum(loss_local[0, 0], ("x", "y", "c"))
    grad_x = g_xpad[N_META:n_valid][None]

    small_list = [g_small[n] for n in SMALL_NAMES] + [g_small["meta_tokens"], g_small["s5_d"]]
    small_bufs = _all_gather([_pack_rows(small_list)], name="small_ag")

    out, per_tensor = {}, {}
    for group, (s_sem, r_sem, srcs, lands, copies, own) in scatters.items():
        ids = list(range(len(srcs)))
        _, landed = _wait_copies(s_sem, r_sem, dict(zip(ids, srcs)), dict(zip(ids, lands)), copies, list(range(len(copies))),
                                 small_bufs[0], name="rs_wait_" + group)
        for t, n in enumerate(groups[group]):
            base, layer = (n[:-1], int(n[-1])) if n.startswith("w_ffn") else (n, None)
            pick = (lambda a: a[layer]) if layer is not None else (lambda a: a.reshape(shards[n].shape))
            per_tensor[n] = _shard_adamw(own[n], landed[t], shards[n], pick(args["m_" + base]), pick(args["v_" + base]),
                                         name="adamw_" + n)
    for base in ("s5_w_glu", "w_kv", "w_q", "w_o"):
        out[base] = tuple(a.reshape(args[base].shape) for a in per_tensor[base])
    for base in ("w_ffn_in", "w_ffn_out"):
        out[base] = tuple(jnp.stack([per_tensor[f"{base}{l}"][k] for l in range(n_ffn)]) for k in range(4))

    zeros_tail = [jnp.zeros_like(g_small["meta_tokens"]), jnp.zeros_like(g_small["s5_d"])]
    pw = _pack_rows([args[n] for n in SMALL_NAMES] + zeros_tail)
    pm = _pack_rows([args["m_" + n] for n in SMALL_NAMES] + zeros_tail)
    pv = _pack_rows([args["v_" + n] for n in SMALL_NAMES] + zeros_tail)
    sg, sd, sm, sv = _small_adamw(small_bufs[0], pw, pm, pv, name="adamw_small")
    like = small_list
    ug, ud, um, uv = (_unpack_rows(a, like) for a in (sg, sd, sm, sv))
    for i, n in enumerate(SMALL_NAMES):
        out[n] = (ug[i], ud[i], um[i], uv[i])
    g_meta = lax.dynamic_slice_in_dim(ug[-2], my_slot * LANES, LANES, axis=1)
    g_d = lax.dynamic_slice_in_dim(ug[-1].reshape(1, dm), my_slot * LANES, LANES, axis=1)
    pad7 = jnp.zeros((7, LANES), F32)
    gc = jnp.concatenate([g_meta, g_d, pad7], axis=0)
    wc = jnp.concatenate([meta_tokens, s5_d, pad7], axis=0)
    mc = jnp.concatenate([m_meta_tokens, m_s5_d, pad7], axis=0)
    vc = jnp.concatenate([v_meta_tokens, v_s5_d, pad7], axis=0)
    dc, mcn, vcn = _plain_adamw(gc, wc, mc, vc, name="adamw_cols")
    out["meta_tokens"] = (g_meta, dc[:N_META], mcn[:N_META], vcn[:N_META])
    out["s5_d"] = (g_d, dc[N_META:N_META + 1], mcn[N_META:N_META + 1], vcn[N_META:N_META + 1])

    order = ["meta_tokens", "norm_mix", "norm_ffn", "s5_a_re", "s5_a_im", "s5_log_dt", "s5_b_re", "s5_b_im", "s5_c_re",
             "s5_c_im", "s5_d", "s5_w_glu", "norm_kv", "w_kv", "w_q", "w_o", "w_ffn_in", "w_ffn_out", "norm_final"]
    res = [loss, grad_x]
    for k in range(4):
        res += [out[n][k] for n in order]
    return tuple(res)
```

```python
import functools
import math

import jax
import jax.numpy as jnp
from jax import lax
from jax.experimental import pallas as pl
from jax.experimental.pallas import tpu as pltpu

F32 = jnp.float32
BF16 = jnp.bfloat16

N_META = 16
D_MODEL = 1024
S5_GROUPS = 64
S5_GROUP = 16
S5_STATE = 64
N_HEADS = 16
HEAD_DIM = 64
RMS_EPS = 1e-6
ADAM_LR, ADAM_B1, ADAM_B2, ADAM_EPS, ADAM_WD, ADAM_STEP = 0.001, 0.9, 0.999, 1e-08, 0.01, 10

LANES = 128
ATTN_BLOCK = 128
GROUPS_PER_BLOCK = 8
N_DEV = 8
MESH = pl.DeviceIdType.MESH
VMEM_LIMIT = 56 * 1024 * 1024


def _pallas(body, **kw):
    return pl.pallas_call(body, **kw)


def _cparams(*sem):
    return pltpu.CompilerParams(dimension_semantics=sem, vmem_limit_bytes=VMEM_LIMIT)


def _pick(n, prefs):
    for p in prefs:
        if n % p == 0:
            return p
    return n


def _dot(a, b, ca=1, cb=0):
    return lax.dot_general(a, b, (((ca,), (cb,)), ((), ())), preferred_element_type=F32)


MATMUL_VMEM_BUDGET = 36 * 1024 * 1024
MATMUL_TILES = (1408, 1024, 512, 384, 256, 128)


def _matmul_tiles(M, N, K, a_bytes, b_bytes, tile_bytes, n_acc, offsets):
    best = None
    for tm in (t for t in MATMUL_TILES if M % t == 0):
        for tn in (t for t in MATMUL_TILES if N % t == 0):
            for tk in (t for t in MATMUL_TILES if K % t == 0):
                if any(off % (tn if along_n else tk) for off, along_n in offsets):
                    continue
                vmem = 2 * (tm * tk * a_bytes + tk * tn * b_bytes + tm * tn * tile_bytes) + n_acc * tm * tn * 4
                if vmem > MATMUL_VMEM_BUDGET:
                    continue
                traffic = M * K * a_bytes * (N // tn) + K * N * b_bytes * (M // tm) + M * N * tile_bytes
                traffic += (K // tk - 1) * n_acc * M * N * 8 // 3
                key = (traffic, -tk, -tm)
                if best is None or key < best[0]:
                    best = (key, (tm, tn, tk))
    return best[1]


def _gmm(a_list, b_list, accs, *, n, extras=(), epilogue=None, out_dtypes=(F32,), name):
    a0, ta0 = a_list[0]
    K, M = a0.shape if ta0 else a0.shape[::-1]
    N = n
    b_keys = sorted({(bi, off) for terms in accs for _, bi, off in terms})
    a_bytes = sum(a.dtype.itemsize for a, _ in a_list)
    b_bytes = sum(b_list[bi][0].dtype.itemsize for bi, _ in b_keys)
    tile_bytes = sum(e.dtype.itemsize for e in extras) + sum(jnp.dtype(d).itemsize for d in out_dtypes)
    offsets = [(off, not b_list[bi][1]) for bi, off in b_keys]
    tm, tn, tk = _matmul_tiles(M, N, K, a_bytes, b_bytes, tile_bytes, len(accs), offsets)
    nk = K // tk
    n_a, n_b, n_e, n_o, n_acc = len(a_list), len(b_keys), len(extras), len(out_dtypes), len(accs)
    if epilogue is None:
        epilogue = lambda *vals: vals

    def body(*refs):
        a_refs, b_refs = refs[:n_a], refs[n_a:n_a + n_b]
        e_refs = refs[n_a + n_b:n_a + n_b + n_e]
        o_refs = refs[n_a + n_b + n_e:n_a + n_b + n_e + n_o]
        acc_refs = refs[n_a + n_b + n_e + n_o:]
        a_vals = [r[...].astype(BF16) for r in a_refs]
        b_vals = {key: r[...].astype(BF16) for key, r in zip(b_keys, b_refs)}
        parts = []
        for terms in accs:
            p = None
            for ai, bi, off in terms:
                d = _dot(a_vals[ai], b_vals[(bi, off)], 0 if a_list[ai][1] else 1, 1 if b_list[bi][1] else 0)
                p = d if p is None else p + d
            parts.append(p)

        def finish(vals):
            outs = epilogue(*vals, *[r[...] for r in e_refs])
            for r, val in zip(o_refs, outs):
                r[...] = val.astype(r.dtype)

        if nk == 1:
            finish(parts)
        else:
            k = pl.program_id(2)

            @pl.when(k == 0)
            def _():
                for r, p in zip(acc_refs, parts):
                    r[...] = p

            @pl.when(k > 0)
            def _():
                for r, p in zip(acc_refs, parts):
                    r[...] += p

            @pl.when(k == nk - 1)
            def _():
                finish([r[...] for r in acc_refs])

    in_specs = [pl.BlockSpec((tk, tm), lambda i, j, k: (k, i)) if ta else pl.BlockSpec((tm, tk), lambda i, j, k: (i, k))
                for _, ta in a_list]
    for bi, off in b_keys:
        if b_list[bi][1]:
            in_specs.append(pl.BlockSpec((tn, tk), lambda i, j, k, o=off // tk: (j, k + o)))
        else:
            in_specs.append(pl.BlockSpec((tk, tn), lambda i, j, k, o=off // tn: (k, j + o)))
    tile = pl.BlockSpec((tm, tn), lambda i, j, k: (i, j))
    in_specs += [tile] * n_e
    return _pallas(
        body, name=name, grid=(M // tm, N // tn, nk), in_specs=in_specs, out_specs=[tile] * n_o,
        out_shape=[jax.ShapeDtypeStruct((M, N), d) for d in out_dtypes],
        scratch_shapes=[pltpu.VMEM((tm, tn), F32)] * (n_acc if nk > 1 else 0),
        compiler_params=_cparams("parallel", "parallel", "arbitrary"),
    )(*[a for a, _ in a_list], *[b_list[bi][0] for bi, _ in b_keys], *extras)


def _rowwise(fn, row_ins, full_ins, row_outs, acc_outs, *, tm, name):
    L = row_ins[0].shape[0]
    n_row, n_full, n_ro = len(row_ins), len(full_ins), len(row_outs)

    def body(*refs):
        ins = [r[...] for r in refs[:n_row + n_full]]
        outs = refs[n_row + n_full:]
        res = fn(*ins)
        for r, val in zip(outs[:n_ro], res[:n_ro]):
            r[...] = val.astype(r.dtype)
        if acc_outs:
            first = pl.program_id(0) == 0
            for r, val in zip(outs[n_ro:], res[n_ro:]):
                @pl.when(first)
                def _(r=r, val=val):
                    r[...] = val

                @pl.when(jnp.logical_not(first))
                def _(r=r, val=val):
                    r[...] += val

    in_specs = [pl.BlockSpec((tm, a.shape[1]), lambda i: (i, 0)) for a in row_ins]
    in_specs += [pl.BlockSpec(a.shape, lambda i, nd=a.ndim: (0,) * nd) for a in full_ins]
    out_specs = [pl.BlockSpec((tm, s[1]), lambda i: (i, 0)) for s, _ in row_outs]
    out_specs += [pl.BlockSpec(s, lambda i, nd=len(s): (0,) * nd) for s, _ in acc_outs]
    out_shape = [jax.ShapeDtypeStruct(s, d) for s, d in list(row_outs) + list(acc_outs)]
    return _pallas(
        body, name=name, grid=(L // tm,), in_specs=in_specs, out_specs=out_specs, out_shape=out_shape,
        compiler_params=_cparams("arbitrary" if acc_outs else "parallel"),
    )(*row_ins, *full_ins)


def _colsum(x):
    return jnp.sum(x, axis=0, keepdims=True)


def _rmsnorm_fwd(x, g, *, out_dtype, name):
    L, d = x.shape

    def f(xb, gb):
        r = lax.rsqrt(jnp.mean(xb * xb, axis=-1, keepdims=True) + RMS_EPS)
        return (xb * r * gb,)

    return _rowwise(f, [x], [g.reshape(1, d)], [((L, d), out_dtype)], [], tm=_pick(L, (384, 128)), name=name)[0]


def _rmsnorm_bwd(x, g, dy, resid, *, name):
    L, d = x.shape

    def f(xb, dyb, *rest):
        gb = rest[-1]
        r = lax.rsqrt(jnp.mean(xb * xb, axis=-1, keepdims=True) + RMS_EPS)
        xh = xb * r
        dxh = dyb * gb
        dx = r * (dxh - xh * jnp.mean(dxh * xh, axis=-1, keepdims=True))
        if resid is not None:
            dx = dx + rest[0]
        return dx, _colsum(dyb * xh)

    rows = [x, dy] + ([resid] if resid is not None else [])
    dx, dg = _rowwise(f, rows, [g.reshape(1, d)], [((L, d), F32)], [((1, d), F32)], tm=_pick(L, (384, 128)), name=name)
    return dx, dg.reshape(g.shape)


def _sigmoid(x):
    return 1.0 / (1.0 + jnp.exp(-x))


GELU_K = math.sqrt(2.0 / math.pi)
GELU_C = 0.044715


def _bdmm(a_list, w_rows, *, out_dtype=F32, name):
    L = a_list[0].shape[0]
    nb, ka, kb = w_rows[0][0].shape
    tm = _pick(L, (1408, 384, 128))
    n_a, n_o = len(a_list), len(w_rows)
    ws = [w for row in w_rows for w in row]

    def body(*refs):
        a_vals = [r[...].astype(BF16) for r in refs[:n_a]]
        w_refs = refs[n_a:n_a + n_a * n_o]
        outs = refs[n_a + n_a * n_o:]
        for o in range(n_o):
            acc = None
            for i in range(n_a):
                p = _dot(a_vals[i], w_refs[o * n_a + i][...].astype(BF16))
                acc = p if acc is None else acc + p
            outs[o][...] = acc.astype(out_dtype)

    return _pallas(
        body, name=name, grid=(L // tm, nb),
        in_specs=[pl.BlockSpec((tm, ka), lambda i, j: (i, j)) for _ in a_list]
        + [pl.BlockSpec((None, ka, kb), lambda i, j: (j, 0, 0)) for _ in ws],
        out_specs=[pl.BlockSpec((tm, kb), lambda i, j: (i, j)) for _ in range(n_o)],
        out_shape=[jax.ShapeDtypeStruct((L, nb * kb), out_dtype) for _ in range(n_o)],
        compiler_params=_cparams("parallel", "parallel"),
    )(*a_list, *ws)


def _bdmm_tn(a_list, g_list, nb, *, name):
    n_a, n_g = len(a_list), len(g_list)
    n_o = max(n_a, n_g)
    L = a_list[0].shape[0]
    ka, kb = a_list[0].shape[1] // nb, g_list[0].shape[1] // nb
    tm = _pick(L, (1408, 384, 128))

    def body(*refs):
        i = pl.program_id(1)
        a_vals = [r[...].astype(BF16) for r in refs[:n_a]]
        g_vals = [r[...].astype(BF16) for r in refs[n_a:n_a + n_g]]
        outs = refs[n_a + n_g:]
        parts = [_dot(a_vals[k % n_a], g_vals[k % n_g], 0, 0) for k in range(n_o)]

        @pl.when(i == 0)
        def _():
            for r, p in zip(outs, parts):
                r[...] = p

        @pl.when(i > 0)
        def _():
            for r, p in zip(outs, parts):
                r[...] += p

    return _pallas(
        body, name=name, grid=(nb, L // tm),
        in_specs=[pl.BlockSpec((tm, ka), lambda j, i: (i, j))] * n_a + [pl.BlockSpec((tm, kb), lambda j, i: (i, j))] * n_g,
        out_specs=[pl.BlockSpec((None, ka, kb), lambda j, i: (j, 0, 0))] * n_o,
        out_shape=[jax.ShapeDtypeStruct((nb, ka, kb), F32)] * n_o,
        compiler_params=_cparams("parallel", "arbitrary"),
    )(*a_list, *g_list)


SCAN_ROWS = 128
STATE_DTYPE = BF16


def _scan_fwd(br, bi, ar, ai, *, name):
    L, S, _ = br.shape
    tb = SCAN_ROWS

    def body(br_ref, bi_ref, ar_ref, ai_ref, xr_ref, xi_ref, carry_ref):
        @pl.when(pl.program_id(0) == 0)
        def _():
            carry_ref[...] = jnp.zeros_like(carry_ref)

        a_r, a_i = ar_ref[...], ai_ref[...]

        def step(t, c):
            xr, xi = c
            nr = a_r * xr - a_i * xi + br_ref[t].astype(F32)
            ni = a_r * xi + a_i * xr + bi_ref[t].astype(F32)
            xr_ref[t] = nr.astype(xr_ref.dtype)
            xi_ref[t] = ni.astype(xi_ref.dtype)
            return nr, ni

        xr, xi = lax.fori_loop(0, tb, step, (carry_ref[0], carry_ref[1]), unroll=8)
        carry_ref[0] = xr
        carry_ref[1] = xi

    blk = pl.BlockSpec((tb, S, LANES), lambda i: (i, 0, 0))
    par = pl.BlockSpec((S, LANES), lambda i: (0, 0))
    return _pallas(
        body, name=name, grid=(L // tb,), in_specs=[blk, blk, par, par], out_specs=[blk, blk],
        out_shape=[jax.ShapeDtypeStruct(br.shape, STATE_DTYPE)] * 2,
        scratch_shapes=[pltpu.VMEM((2, S, LANES), F32)],
        compiler_params=_cparams("arbitrary"),
    )(br, bi, ar, ai)


def _scan_bwd(gr_in, gi_in, ar, ai, xr, xi, *, name):
    L, S, _ = gr_in.shape
    tb = SCAN_ROWS
    nblk = L // tb

    def body(gr_ref, gi_ref, ar_ref, ai_ref, xr_ref, xi_ref, or_ref, oi_ref, dar_ref, dai_ref, carry_ref):
        @pl.when(pl.program_id(0) == 0)
        def _():
            carry_ref[...] = jnp.zeros_like(carry_ref)
            dar_ref[...] = jnp.zeros_like(dar_ref)
            dai_ref[...] = jnp.zeros_like(dai_ref)

        a_r, a_i = ar_ref[...], ai_ref[...]

        def step(s, c):
            gr, gi, dr, di = c
            t = tb - 1 - s
            x_r, x_i = xr_ref[t].astype(F32), xi_ref[t].astype(F32)
            dr = dr + gr * x_r + gi * x_i
            di = di + gi * x_r - gr * x_i
            nr = a_r * gr + a_i * gi + gr_ref[t].astype(F32)
            ni = a_r * gi - a_i * gr + gi_ref[t].astype(F32)
            or_ref[t] = nr.astype(or_ref.dtype)
            oi_ref[t] = ni.astype(oi_ref.dtype)
            return nr, ni, dr, di

        gr, gi, dr, di = lax.fori_loop(0, tb, step, (carry_ref[0], carry_ref[1], dar_ref[...], dai_ref[...]), unroll=8)
        carry_ref[0] = gr
        carry_ref[1] = gi
        dar_ref[...] = dr
        dai_ref[...] = di

    blk = pl.BlockSpec((tb, S, LANES), lambda i: (nblk - 1 - i, 0, 0))
    par = pl.BlockSpec((S, LANES), lambda i: (0, 0))
    return _pallas(
        body, name=name, grid=(nblk,), in_specs=[blk, blk, par, par, blk, blk], out_specs=[blk, blk, par, par],
        out_shape=[jax.ShapeDtypeStruct(gr_in.shape, STATE_DTYPE)] * 2 + [jax.ShapeDtypeStruct((S, LANES), F32)] * 2,
        scratch_shapes=[pltpu.VMEM((2, S, LANES), F32)],
        compiler_params=_cparams("arbitrary"),
    )(gr_in, gi_in, ar, ai, xr, xi)


def _make_ssm(name):
    nb = GROUPS_PER_BLOCK

    def fwd(u, wr, wi, cr, cin, lam_r, lam_i):
        L = u.shape[0]
        bur, bui = _bdmm([u], [[wr], [wi]], out_dtype=STATE_DTYPE, name=name + "_bu")
        S = bur.shape[1] // LANES
        xr, xi = _scan_fwd(bur.reshape(L, S, LANES), bui.reshape(L, S, LANES), lam_r, lam_i, name=name + "_scan")
        xr, xi = xr.reshape(L, S * LANES), xi.reshape(L, S * LANES)
        (y,) = _bdmm([xr, xi], [[cr, cin]], name=name + "_cx")
        return y, (u, wr, wi, cr, cin, lam_r, lam_i, xr, xi)

    def bwd(res, dy):
        u, wr, wi, cr, cin, lam_r, lam_i, xr, xi = res
        L = u.shape[0]
        S = xr.shape[1] // LANES
        tr = lambda w: jnp.swapaxes(w, 1, 2)
        gin_r, gin_i = _bdmm([dy], [[tr(cr)], [tr(cin)]], out_dtype=STATE_DTYPE, name=name + "_gin")
        gr, gi, dlr, dli = _scan_bwd(gin_r.reshape(L, S, LANES), gin_i.reshape(L, S, LANES), lam_r, lam_i,
                                     xr.reshape(L, S, LANES), xi.reshape(L, S, LANES), name=name + "_rscan")
        gr, gi = gr.reshape(L, S * LANES), gi.reshape(L, S * LANES)
        (du,) = _bdmm([gr, gi], [[tr(wr), tr(wi)]], name=name + "_du")
        dwr, dwi = _bdmm_tn([u], [gr, gi], nb, name=name + "_dw")
        dcr, dcin = _bdmm_tn([xr, xi], [dy], nb, name=name + "_dc")
        return du, dwr, dwi, dcr, dcin, dlr, dli

    return fwd, bwd


def _s5_discretize(a_re, a_im, log_dt, b_re, b_im, c_re, c_im):
    G, P, C, nb = S5_GROUPS, S5_STATE, S5_GROUP, GROUPS_PER_BLOCK
    dt = jnp.exp(log_dt)[:, None]
    mag = jnp.exp(dt * a_re)
    ang = dt * a_im
    abar_re = mag * jnp.cos(ang)
    abar_im = mag * jnp.sin(ang)
    den = a_re * a_re + a_im * a_im
    coef_re = ((abar_re - 1.0) * a_re + abar_im * a_im) / den
    coef_im = (abar_im * a_re - (abar_re - 1.0) * a_im) / den
    bbar_re = coef_re[..., None] * b_re - coef_im[..., None] * b_im
    bbar_im = coef_re[..., None] * b_im + coef_im[..., None] * b_re
    eye = jnp.eye(nb, dtype=F32)

    def blocks_in(bb):
        return jnp.einsum("jgpc,gh->jgchp", bb.reshape(G // nb, nb, P, C), eye).reshape(G // nb, nb * C, nb * P)

    def blocks_out(cc):
        return jnp.einsum("jgcp,gh->jgphc", cc.reshape(G // nb, nb, C, P), eye).reshape(G // nb, nb * P, nb * C)

    lam_r = abar_re.reshape(G * P // LANES, LANES)
    lam_i = abar_im.reshape(G * P // LANES, LANES)
    return blocks_in(bbar_re), blocks_in(bbar_im), blocks_out(c_re), blocks_out(-c_im), lam_r, lam_i


ATTN_Q_ROWS = 384


HEADS_PER_BLOCK = LANES // HEAD_DIM


def _head_masks(shape, axis):
    idx = lax.broadcasted_iota(jnp.int32, shape, axis) // HEAD_DIM
    return [idx == hh for hh in range(HEADS_PER_BLOCK)]


def _masked_bf16(x, masks):
    return [jnp.where(m, x, 0.0).astype(BF16) for m in masks]


def _tri_sum(x, tri2):
    hi = x.astype(BF16)
    lo = (x - hi.astype(F32)).astype(BF16)
    return _dot(jnp.concatenate([hi, lo], axis=1), tri2)


def _attn_weights(qh, kt, strict, after, c):
    z = _dot(qh, kt)
    lb = jnp.minimum(z, 0.0) - jnp.log(1.0 + jnp.exp(-jnp.abs(z)))
    l1m = lb - z
    if strict is not None:
        l1m = jnp.where(strict, l1m, 0.0)
    rem = _tri_sum(l1m, after)
    w = jnp.exp(lb + rem + c)
    if strict is not None:
        w = jnp.where(strict, w, 0.0)
    return lb, w, rem[:, 0:1] + l1m[:, 0:1]


def _tri(cmp):
    tk = ATTN_BLOCK
    tri = cmp(lax.broadcasted_iota(jnp.int32, (tk, tk), 0), lax.broadcasted_iota(jnp.int32, (tk, tk), 1)).astype(BF16)
    return jnp.concatenate([tri, tri], axis=0)


def _attn_tile_index(i, jb, nsub):
    return nsub * (i * (i + 1) // 2) + jb


def _attn_fwd(q, kt, v, *, name):
    L = q.shape[0]
    tk = ATTN_BLOCK
    tq = _pick(L, (ATTN_Q_ROWS, tk))
    nsub = tq // tk
    n_qt = L // tq
    n_tiles = _attn_tile_index(n_qt, 0, nsub)
    scale = 1.0 / math.sqrt(HEAD_DIM)

    def body(q_ref, kt_ref, v_ref, o_ref, w_hbm, s_hbm, acc_ref, w_stage, s_stage, sems):
        hp, i = pl.program_id(0), pl.program_id(1)
        after = _tri(lambda r, c: r > c)
        rowq = lax.broadcasted_iota(jnp.int32, (tq, tk), 0)
        colq = lax.broadcasted_iota(jnp.int32, (tq, tk), 1)
        qs = _masked_bf16(q_ref[...].astype(F32) * scale, _head_masks((tq, LANES), 1))
        vmasks = _head_masks((tk, LANES), 1)
        acc_ref[...] = jnp.zeros_like(acc_ref)

        def saves(buf, ii):
            cps = []
            for jd in range(nsub):
                idx = _attn_tile_index(i, ii * nsub + jd, nsub)
                cps.append(pltpu.make_async_copy(w_stage.at[buf, jd], w_hbm.at[hp, idx], sems.at[0, buf, jd]))
                cps.append(pltpu.make_async_copy(s_stage.at[buf, jd], s_hbm.at[hp, idx], sems.at[1, buf, jd]))
            return cps

        def tile(jb, buf, jd, strict, cs, first_row=0):
            r0 = pl.multiple_of(jb * tk, tk)
            rows = pl.ds(first_row, tq - first_row)
            kt_j = kt_ref[:, pl.ds(r0, tk)].astype(BF16)
            vs = _masked_bf16(v_ref[pl.ds(r0, tk), :], vmasks)
            out, acc = [], None
            for hh, (qh, vh, c) in enumerate(zip(qs, vs, cs)):
                lb, w, tot = _attn_weights(qh[first_row:], kt_j, None if strict is None else strict[first_row:], after,
                                           c[first_row:])
                w = w.astype(BF16)
                if first_row:
                    w_stage[buf, jd, hh, pl.ds(0, first_row), :] = jnp.zeros((first_row, tk), BF16)
                    s_stage[buf, jd, hh, pl.ds(0, first_row), :] = jnp.zeros((first_row, tk), BF16)
                w_stage[buf, jd, hh, rows, :] = w
                s_stage[buf, jd, hh, rows, :] = lb.astype(BF16)
                part = _dot(w, vh)
                acc = part if acc is None else acc + part
                out.append(jnp.concatenate([c[:first_row], c[first_row:] + tot], axis=0) if first_row else c + tot)
            acc_ref[rows, :] += acc
            return tuple(out)

        cs = tuple(jnp.zeros((tq, 1), F32) for _ in qs)
        for jd in reversed(range(nsub)):
            cs = tile(i * nsub + jd, 0, jd, (colq + jd * tk) < rowq, cs, first_row=jd * tk)
        for cp in saves(0, i):
            cp.start()

        def step(ii, cs):
            buf = (ii + 1) % 2

            @pl.when(ii >= 1)
            def _():
                for cp in saves(buf, 0):
                    cp.wait()

            for jd in reversed(range(nsub)):
                cs = tile((i - 1 - ii) * nsub + jd, buf, jd, None, cs)
            for cp in saves(buf, i - 1 - ii):
                cp.start()
            return cs

        lax.fori_loop(0, i, step, cs)
        o_ref[...] = acc_ref[...].astype(o_ref.dtype)
        for cp in saves(0, 0):
            cp.wait()

        @pl.when(i >= 1)
        def _():
            for cp in saves(1, 0):
                cp.wait()

    blk = pl.BlockSpec((tq, LANES), lambda h, i: (i, h))
    whole = pl.BlockSpec((L, LANES), lambda h, i: (0, h))
    whole_t = pl.BlockSpec((LANES, L), lambda h, i: (h, 0))
    any_spec = pl.BlockSpec(memory_space=pl.ANY)
    nhb = N_HEADS // HEADS_PER_BLOCK
    saved = jax.ShapeDtypeStruct((nhb, n_tiles, HEADS_PER_BLOCK, tq, tk), BF16)
    return _pallas(
        body, name=name, grid=(nhb, n_qt), in_specs=[blk, whole_t, whole], out_specs=[blk, any_spec, any_spec],
        out_shape=[jax.ShapeDtypeStruct(q.shape, BF16), saved, saved],
        scratch_shapes=[pltpu.VMEM((tq, LANES), F32), pltpu.VMEM((2, nsub, HEADS_PER_BLOCK, tq, tk), BF16),
                        pltpu.VMEM((2, nsub, HEADS_PER_BLOCK, tq, tk), BF16), pltpu.SemaphoreType.DMA((2, 2, nsub))],
        compiler_params=_cparams("parallel", "arbitrary"),
    )(q, kt, v)


def _attn_bwd(q, k, vt, do, w_saved, s_saved, *, name):
    L = q.shape[0]
    tk = ATTN_BLOCK
    tq = _pick(L, (ATTN_Q_ROWS, tk))
    nsub = tq // tk
    scale = 1.0 / math.sqrt(HEAD_DIM)

    def body(q_ref, k_ref, vt_ref, do_ref, w_hbm, s_hbm, dq_ref, dkt_ref, dvt_ref, e_scr, sig_scr, dq_acc_ref, w_stage, sems):
        hp, i = pl.program_id(0), pl.program_id(1)

        @pl.when(i == 0)
        def _():
            dkt_ref[...] = jnp.zeros_like(dkt_ref)
            dvt_ref[...] = jnp.zeros_like(dvt_ref)

        before = _tri(lambda r, c: r < c)
        rowq = lax.broadcasted_iota(jnp.int32, (tq, tk), 0)
        colq = lax.broadcasted_iota(jnp.int32, (tq, tk), 1)
        qmasks = _head_masks((tq, LANES), 1)
        tmasks = _head_masks((LANES, tq), 0)
        kmasks = _head_masks((tk, LANES), 1)
        q_scaled = q_ref[...].astype(F32) * scale
        do32 = do_ref[...].astype(F32)
        dos = _masked_bf16(do32, qmasks)
        qts = _masked_bf16(q_scaled.T, tmasks)
        dots = _masked_bf16(do32.T, tmasks)
        dq_acc_ref[...] = jnp.zeros_like(dq_acc_ref)

        def loads(buf, ii):
            cps = []
            for jd in range(nsub):
                jb = ii * nsub + jd
                idx = _attn_tile_index(i, jb, nsub)
                cps.append(pltpu.make_async_copy(w_hbm.at[hp, idx], w_stage.at[buf, jd], sems.at[0, buf, jd]))
                cps.append(pltpu.make_async_copy(s_hbm.at[hp, idx], sig_scr.at[jb], sems.at[1, buf, jd]))
            return cps

        def weigh(buf, ii):
            for jd in range(nsub):
                jb = ii * nsub + jd
                r0 = pl.multiple_of(jb * tk, tk)
                vt_j = vt_ref[:, pl.ds(r0, tk)].astype(BF16)
                acc = None
                for hh, (dob, dot_h) in enumerate(zip(dos, dots)):
                    w = w_stage[buf, jd, hh]
                    e_scr[hh, jb] = _dot(dob, vt_j) * w.astype(F32)
                    part = _dot(dot_h, w)
                    acc = part if acc is None else acc + part
                dvt_ref[:, pl.ds(r0, tk)] += acc

        def tile_right(jb, strict, cs, first_row=0):
            r0 = pl.multiple_of(jb * tk, tk)
            rows = pl.ds(first_row, tq - first_row)
            ks = _masked_bf16(k_ref[pl.ds(r0, tk), :], kmasks)
            out, dq_acc, dk_acc = [], None, None
            for hh, (kh, qt_h, c2) in enumerate(zip(ks, qts, cs)):
                e = e_scr[hh, jb, rows, :]
                sig = jnp.exp(sig_scr[jb, hh, rows, :].astype(F32))
                left = _dot(e.astype(BF16), before[:tk])
                dz = e - (e + left + c2[first_row:]) * sig
                if strict is not None:
                    dz = jnp.where(strict[first_row:], dz, 0.0)
                dz = dz.astype(BF16)
                p1, p2 = _dot(dz, kh), _dot(qt_h[:, first_row:], dz)
                dq_acc = p1 if dq_acc is None else dq_acc + p1
                dk_acc = p2 if dk_acc is None else dk_acc + p2
                tot = left[:, tk - 1:tk] + e[:, tk - 1:tk]
                out.append(jnp.concatenate([c2[:first_row], c2[first_row:] + tot], axis=0) if first_row else c2 + tot)
            dq_acc_ref[rows, :] += dq_acc
            dkt_ref[:, pl.ds(r0, tk)] += dk_acc
            return tuple(out)

        for cp in loads(0, 0):
            cp.start()

        def weigh_step(ii, carry):
            buf = ii % 2
            for cp in loads(1 - buf, ii + 1):
                cp.start()
            for cp in loads(buf, ii):
                cp.wait()
            weigh(buf, ii)
            return carry

        lax.fori_loop(0, i, weigh_step, 0)
        for cp in loads(i % 2, i):
            cp.wait()
        weigh(i % 2, i)

        def right_step(ii, cs):
            for jd in range(nsub):
                cs = tile_right(ii * nsub + jd, None, cs)
            return cs

        cs = lax.fori_loop(0, i, right_step, tuple(jnp.zeros((tq, 1), F32) for _ in dos))
        for jd in range(nsub):
            cs = tile_right(i * nsub + jd, (colq + jd * tk) < rowq, cs, first_row=jd * tk)
        dq_ref[...] = (dq_acc_ref[...] * scale).astype(dq_ref.dtype)

    nhb = N_HEADS // HEADS_PER_BLOCK
    blk = pl.BlockSpec((tq, LANES), lambda h, i: (i, h))
    whole = pl.BlockSpec((L, LANES), lambda h, i: (0, h))
    whole_t = pl.BlockSpec((LANES, L), lambda h, i: (h, 0))
    t_shape = jax.ShapeDtypeStruct((q.shape[1], L), F32)
    any_spec = pl.BlockSpec(memory_space=pl.ANY)
    return _pallas(
        body, name=name, grid=(nhb, L // tq), in_specs=[blk, whole, whole_t, blk, any_spec, any_spec],
        out_specs=[blk, whole_t, whole_t], out_shape=[jax.ShapeDtypeStruct(q.shape, BF16), t_shape, t_shape],
        scratch_shapes=[pltpu.VMEM((HEADS_PER_BLOCK, L // tk, tq, tk), F32), pltpu.VMEM((L // tk, HEADS_PER_BLOCK, tq, tk), BF16),
                        pltpu.VMEM((tq, LANES), F32), pltpu.VMEM((2, nsub, HEADS_PER_BLOCK, tq, tk), BF16),
                        pltpu.SemaphoreType.DMA((2, 2, nsub))],
        compiler_params=_cparams("parallel", "arbitrary"),
    )(q, k, vt, do, w_saved, s_saved)


def _make_loss_head(n_valid, name):
    def run(h, g, target):
        L, d = h.shape
        tm = _pick(L, (384, 128))

        def body(h_ref, t_ref, g_ref, loss_ref, dh_ref, dg_ref):
            i = pl.program_id(0)
            x = h_ref[...]
            gb = g_ref[...]
            rows = lax.broadcasted_iota(jnp.int32, (tm, 1), 0) + i * tm
            valid = jnp.logical_and(rows >= N_META, rows < n_valid)
            r = lax.rsqrt(jnp.mean(x * x, axis=-1, keepdims=True) + RMS_EPS)
            xh = x * r
            err = jnp.where(valid, xh * gb - t_ref[...], 0.0)
            dy = err * (1.0 / d)
            dxh = dy * gb
            dh_ref[...] = r * (dxh - xh * jnp.mean(dxh * xh, axis=-1, keepdims=True))
            part = 0.5 / d * jnp.sum(jnp.sum(err * err, axis=1, keepdims=True), axis=0, keepdims=True)
            dgp = _colsum(dy * xh)

            @pl.when(i == 0)
            def _():
                loss_ref[...] = part
                dg_ref[...] = dgp

            @pl.when(i > 0)
            def _():
                loss_ref[...] += part
                dg_ref[...] += dgp

        rowspec = pl.BlockSpec((tm, d), lambda i: (i, 0))
        return _pallas(
            body, name=name, grid=(L // tm,),
            in_specs=[rowspec, rowspec, pl.BlockSpec((1, d), lambda i: (0, 0))],
            out_specs=[pl.BlockSpec((1, 1), lambda i: (0, 0)), rowspec, pl.BlockSpec((1, d), lambda i: (0, 0))],
            out_shape=[jax.ShapeDtypeStruct((1, 1), F32), jax.ShapeDtypeStruct((L, d), F32), jax.ShapeDtypeStruct((1, d), F32)],
            compiler_params=_cparams("arbitrary"),
        )(h, target, g.reshape(1, d))

    return run


def _mm(a, b, *, ta=False, tb=False, out_dtype=F32, extras=(), epilogue=None, name):
    n = b.shape[0] if tb else b.shape[1]
    return _gmm([(a, ta)], [(b, tb)], [[(0, 0, 0)]], n=n, extras=extras, epilogue=epilogue, out_dtypes=(out_dtype,), name=name)[0]


def _add_epilogue(acc, resid):
    return (acc + resid,)


def _make_ffn_block(name):
    def fwd(h, gain, w_in, w_out):
        dff = w_out.shape[0]
        f = _rmsnorm_fwd(h, gain, out_dtype=BF16, name=name + "_norm")

        def swiglu(g, u):
            return g, u, g * _sigmoid(g) * u

        g, u, a = _gmm([(f, False)], [(w_in, False)], [[(0, 0, 0)], [(0, 0, dff)]], n=dff, epilogue=swiglu,
                       out_dtypes=(BF16, BF16, BF16), name=name + "_in")
        h2 = _mm(a, w_out, extras=[h], epilogue=_add_epilogue, name=name + "_out")
        return h2, (h, gain, f, g, u, a, w_in, w_out)

    def bwd(res, dh2, emit):
        h, gain, f, g, u, a, w_in, w_out = res
        dff, dm = w_out.shape

        def dswiglu(da, gb, ub):
            gb, ub = gb.astype(F32), ub.astype(F32)
            s = _sigmoid(gb)
            return da * ub * (s + gb * s * (1.0 - s)), da * gb * s

        dg, du = _gmm([(dh2, False)], [(w_out, True)], [[(0, 0, 0)]], n=dff, extras=[g, u], epilogue=dswiglu,
                      out_dtypes=(BF16, BF16), name=name + "_da")
        dw_out = _mm(a, dh2, ta=True, name=name + "_dwout")
        (df,) = _gmm([(dg, False), (du, False)], [(w_in, True)], [[(0, 0, 0), (1, 0, dff)]], n=dm, name=name + "_df")
        dw_in = jnp.concatenate([_mm(f, dg, ta=True, name=name + "_dwg"), _mm(f, du, ta=True, name=name + "_dwu")], axis=1)
        tok = emit({"w_ffn_in": dw_in, "w_ffn_out": dw_out})
        dh, dgain = _rmsnorm_bwd(h, gain + tok, df, dh2, name=name + "_dnorm")
        return dh, dgain

    return fwd, bwd


def _make_s5_out_block(name):
    def post(yb, ub, db):
        s = yb + db * ub
        return (0.5 * s * (1.0 + jnp.tanh(GELU_K * (s + GELU_C * s * s * s))),)

    def fwd(y, u, d, h, w_glu):
        L, dm = y.shape
        (z,) = _rowwise(post, [y, u], [d.reshape(1, dm)], [((L, dm), BF16)], [], tm=_pick(L, (384, 128)), name=name + "_gelu")

        def glu(val, gate, resid):
            return resid + val * _sigmoid(gate), val, gate

        h2, val, gate = _gmm([(z, False)], [(w_glu, False)], [[(0, 0, 0)], [(0, 0, dm)]], n=dm, extras=[h], epilogue=glu,
                             out_dtypes=(F32, BF16, BF16), name=name + "_glu")
        return h2, (y, u, d, z, val, gate, w_glu)

    def bwd(res, dh2, emit):
        y, u, d, z, val, gate, w_glu = res
        L, dm = y.shape

        def dglu(dh2b, valb, gateb):
            s = _sigmoid(gateb.astype(F32))
            return dh2b * s, dh2b * valb.astype(F32) * s * (1.0 - s)

        dval, dgate = _rowwise(dglu, [dh2, val, gate], [], [((L, dm), BF16)] * 2, [], tm=_pick(L, (384, 128)), name=name + "_dglu")
        (dz,) = _gmm([(dval, False), (dgate, False)], [(w_glu, True)], [[(0, 0, 0), (1, 0, dm)]], n=dm, name=name + "_dz")

        def dpost(yb, ub, dzb, db):
            s = yb + db * ub
            t = jnp.tanh(GELU_K * (s + GELU_C * s * s * s))
            ds = dzb * (0.5 * (1.0 + t) + 0.5 * s * (1.0 - t * t) * GELU_K * (1.0 + 3.0 * GELU_C * s * s))
            return ds, ds * db, _colsum(ds * ub)

        dw = jnp.concatenate([_mm(z, dval, ta=True, name=name + "_dwv"), _mm(z, dgate, ta=True, name=name + "_dwg")], axis=1)
        tok = emit({"s5_w_glu": dw})
        dy, du, dd = _rowwise(dpost, [y, u, dz], [(d + tok).reshape(1, dm)], [((L, dm), F32)] * 2, [((1, dm), F32)],
                              tm=_pick(L, (384, 128)), name=name + "_dgelu")
        return dy, du, dd.reshape(d.shape)

    return fwd, bwd


def _make_attn_layer(name):
    def fwd(h, g_kv, g_q, w_k, w_v, w_q, w_o):
        dm = h.shape[1]
        kvn = _rmsnorm_fwd(h, g_kv, out_dtype=BF16, name=name + "_kvnorm")
        qn = _rmsnorm_fwd(h, g_q, out_dtype=BF16, name=name + "_qnorm")
        k, v = _gmm([(kvn, False)], [(w_k, False), (w_v, False)], [[(0, 0, 0)], [(0, 1, 0)]], n=dm,
                    out_dtypes=(BF16, BF16), name=name + "_kv")
        q = _mm(qn, w_q, out_dtype=BF16, name=name + "_q")
        kt, vt = _gmm([(w_k, True), (w_v, True)], [(kvn, True)], [[(0, 0, 0)], [(1, 0, 0)]], n=h.shape[0],
                      out_dtypes=(BF16, BF16), name=name + "_kvt")
        o, w_saved, s_saved = _attn_fwd(q, kt, v, name=name + "_fwd")
        h2 = _mm(o, w_o, extras=[h], epilogue=_add_epilogue, name=name + "_o")
        return h2, (h, g_kv, g_q, kvn, qn, q, k, vt, o, w_saved, s_saved, w_k, w_v, w_q, w_o)

    def bwd(res, dh2, emit):
        h, g_kv, g_q, kvn, qn, q, k, vt, o, w_saved, s_saved, w_k, w_v, w_q, w_o = res
        dm = h.shape[1]
        do = _mm(dh2, w_o, tb=True, out_dtype=BF16, name=name + "_do")
        dw_o = _mm(o, dh2, ta=True, name=name + "_dwo")
        dq, dkt, dvt = _attn_bwd(q, k, vt, do, w_saved, s_saved, name=name + "_bwd")
        dqn = _mm(dq, w_q, tb=True, name=name + "_dqn")
        dw_q = _mm(qn, dq, ta=True, name=name + "_dwq")
        (dkvn,) = _gmm([(dkt, True), (dvt, True)], [(w_k, True), (w_v, True)], [[(0, 0, 0), (1, 1, 0)]], n=dm, name=name + "_dkvn")
        dw_k = _mm(kvn, dkt, ta=True, tb=True, name=name + "_dwk")
        dw_v = _mm(kvn, dvt, ta=True, tb=True, name=name + "_dwv")
        tok = emit({"w_k": dw_k, "w_v": dw_v, "w_q": dw_q, "w_o": dw_o})
        dh, dg_q = _rmsnorm_bwd(h, g_q + tok, dqn, dh2, name=name + "_dqnorm")
        dh, dg_kv = _rmsnorm_bwd(h, g_kv, dkvn, dh, name=name + "_dkvnorm")
        return dh, dg_kv, dg_q

    return fwd, bwd


S5_PARAMS = ("s5_a_re", "s5_a_im", "s5_log_dt", "s5_b_re", "s5_b_im", "s5_c_re", "s5_c_im")


def _forward_backward(x_pad, target_pad, n_valid, small, get_weights, put_grads):
    gs = {}
    h0 = x_pad.at[:N_META].set(small["meta_tokens"])

    ssm_fwd, ssm_bwd = _make_ssm("ssm")
    s5out_fwd, s5out_bwd = _make_s5_out_block("s5out")
    ffn_fwd, ffn_bwd = zip(_make_ffn_block("ffn0"), _make_ffn_block("ffn1"))
    attn_fwd, attn_bwd = _make_attn_layer("attn")
    u = _rmsnorm_fwd(h0, small["norm_mix"][0], out_dtype=F32, name="mix0_norm")
    disc, disc_vjp = jax.vjp(_s5_discretize, *[small[n][0] for n in S5_PARAMS])
    y, ssm_res = ssm_fwd(u, *disc)
    w = get_weights("s5", y)
    h1, s5out_res = s5out_fwd(y, u, small["s5_d"][0], h0, w["s5_w_glu"])
    w = get_weights("ffn0", h1)
    h2, ffn0_res = ffn_fwd[0](h1, small["norm_ffn"][0], w["w_ffn_in"], w["w_ffn_out"])
    w = get_weights("attn", h2)
    h3, attn_res = attn_fwd(h2, small["norm_kv"], small["norm_mix"][1], w["w_k"], w["w_v"], w["w_q"], w["w_o"])
    w = get_weights("ffn1", h3)
    h4, ffn1_res = ffn_fwd[1](h3, small["norm_ffn"][1], w["w_ffn_in"], w["w_ffn_out"])
    loss, dh, dg_final = _make_loss_head(n_valid, "loss_head")(h4, small["norm_final"], target_pad)
    gs["norm_final"] = dg_final.reshape(small["norm_final"].shape)

    dh, dg_ffn1 = ffn_bwd[1](ffn1_res, dh, functools.partial(put_grads, "ffn1"))
    dh, gs["norm_kv"], dg_mix1 = attn_bwd(attn_res, dh, functools.partial(put_grads, "attn"))
    dh, dg_ffn0 = ffn_bwd[0](ffn0_res, dh, functools.partial(put_grads, "ffn0"))
    dy, du, dd = s5out_bwd(s5out_res, dh, functools.partial(put_grads, "s5"))
    du_ssm, *ddisc = ssm_bwd(ssm_res, dy)
    dh, dg_mix0 = _rmsnorm_bwd(h0, small["norm_mix"][0], du + du_ssm, dh, name="mix0_dnorm")
    for n, g in zip(S5_PARAMS, disc_vjp(tuple(ddisc))):
        gs[n] = g[None]
    gs["s5_d"] = dd.reshape(small["s5_d"].shape)
    gs["norm_mix"] = jnp.stack([dg_mix0, dg_mix1])
    gs["norm_ffn"] = jnp.stack([dg_ffn0, dg_ffn1])
    gs["meta_tokens"] = dh[:N_META]
    return loss, dh, gs


def _coords():
    return lax.axis_index("x"), lax.axis_index("y"), lax.axis_index("c")


def _flip(bits):
    x, y, c = _coords()
    fx, fy, fc = bits
    return (x ^ fx if fx else x, y ^ fy if fy else y, c ^ fc if fc else c)


def _exchange(ins, out_shapes, copies, local_copies, *, aliases=None, name):
    n_in, n_out = len(ins), len(out_shapes)
    n_cp, n_loc = len(copies), len(local_copies)
    aliases = aliases or {}

    def body(*refs):
        in_refs, out_refs = refs[:n_in], refs[n_in:n_in + n_out]
        send_sems, recv_sems, loc_sems = refs[n_in + n_out:]
        me = _coords()
        locs = []
        for n, (ii, oi, dfn) in enumerate(local_copies):
            cp = pltpu.make_async_copy(in_refs[ii], out_refs[oi].at[dfn(*me)], loc_sems.at[n])
            cp.start()
            locs.append(cp)
        sends = []
        for n, (ii, sfn, bits, oi, dfn) in enumerate(copies):
            src = in_refs[ii] if sfn is None else in_refs[ii].at[sfn(*me)]
            cp = pltpu.make_async_remote_copy(
                src_ref=src, dst_ref=out_refs[oi].at[dfn(*me)], send_sem=send_sems.at[n], recv_sem=recv_sems.at[n],
                device_id=_flip(bits), device_id_type=MESH)
            cp.start()
            sends.append(cp)
        for n, (ii, sfn, bits, oi, dfn) in enumerate(copies):
            peer = _flip(bits)
            src = in_refs[ii] if sfn is None else in_refs[ii].at[sfn(*me)]
            pltpu.make_async_remote_copy(
                src_ref=src, dst_ref=out_refs[oi].at[dfn(*peer)], send_sem=send_sems.at[n], recv_sem=recv_sems.at[n],
                device_id=peer, device_id_type=MESH).wait_recv()
        for cp in sends:
            cp.wait_send()
        for cp in locs:
            cp.wait()

    any_spec = pl.BlockSpec(memory_space=pl.ANY)
    return _pallas(
        body, name=name, in_specs=[any_spec] * n_in, out_specs=[any_spec] * n_out,
        out_shape=[jax.ShapeDtypeStruct(s, d) for s, d in out_shapes],
        scratch_shapes=[pltpu.SemaphoreType.DMA((max(n_cp, 1),)), pltpu.SemaphoreType.DMA((max(n_cp, 1),)),
                        pltpu.SemaphoreType.DMA((max(n_loc, 1),))],
        input_output_aliases=aliases,
        compiler_params=pltpu.CompilerParams(has_side_effects=True),
    )(*ins)


ICI_FLIPS = ((1, 0, 0), (0, 1, 0), (1, 1, 0))
D2D_FLIP = (0, 0, 1)


def _slot_of(x, y, c):
    return 4 * x + 2 * y + c


def _all_gather(shards, *, name):
    n = len(shards)
    outs = [((N_DEV,) + s.shape, s.dtype) for s in shards]
    copies = [(t, None, bits, t, _slot_of) for t in range(n) for bits in ICI_FLIPS]
    local = [(t, t, _slot_of) for t in range(n)]
    bufs = _exchange(shards, outs, copies, local, name=name + "_ici")
    copies2 = [(t, (lambda x, y, c, q=q: 2 * q + c), D2D_FLIP, t, (lambda x, y, c, q=q: 2 * q + c))
               for t in range(n) for q in range(4)]
    return _exchange(bufs, outs, copies2, [], aliases={t: t for t in range(n)}, name=name + "_d2d")


ALL_FLIPS = tuple((m >> 2 & 1, m >> 1 & 1, m & 1) for m in range(1, N_DEV))
HBM_SPEC = pl.BlockSpec(memory_space=pltpu.HBM)
SEM_SPEC = pl.BlockSpec(memory_space=pltpu.SEMAPHORE)
SIDE_EFFECT = pltpu.SideEffectType.DATAFLOW_SIDE_EFFECTING


def _copy_desc(copies, n, src_refs, land_refs, send_sems, recv_sems, sender):
    si, sfn, bits, li, dfn = copies[n]
    src = src_refs[si] if sfn is None else src_refs[si].at[sfn(*sender)]
    return pltpu.make_async_remote_copy(
        src_ref=src, dst_ref=land_refs[li].at[dfn(*sender)], send_sem=send_sems.at[n], recv_sem=recv_sems.at[n],
        device_id=_flip(bits), device_id_type=MESH)


def _start_copies(srcs, lands, copies, *, name):
    n_s, n_l, n_c = len(srcs), len(lands), len(copies)

    def body(*refs):
        src_refs = refs[:n_s]
        send_sems, recv_sems = refs[n_s], refs[n_s + 1]
        land_refs = refs[2 * n_s + 2:2 * n_s + 2 + n_l]
        token = refs[-1]
        me = _coords()
        for n in range(n_c):
            _copy_desc(copies, n, src_refs, land_refs, send_sems, recv_sems, me).start()
        token[...] = jnp.zeros_like(token)

    res = _pallas(
        body, name=name,
        out_shape=(pltpu.SemaphoreType.DMA((n_c,)), pltpu.SemaphoreType.DMA((n_c,)),
                   *[pltpu.HBM(s.shape, s.dtype) for s in srcs], *[pltpu.HBM(shape, dtype) for shape, dtype in lands],
                   jax.ShapeDtypeStruct((8, LANES), F32)),
        in_specs=[HBM_SPEC] * n_s,
        out_specs=(SEM_SPEC, SEM_SPEC, *[HBM_SPEC] * (n_s + n_l), pl.BlockSpec(memory_space=pltpu.VMEM)),
        input_output_aliases={i: 2 + i for i in range(n_s)},
        compiler_params=pltpu.CompilerParams(has_side_effects=SIDE_EFFECT),
    )(*[pltpu.with_memory_space_constraint(s, pltpu.HBM) for s in srcs])
    return res[0], res[1], list(res[2:2 + n_s]), list(res[2 + n_s:2 + n_s + n_l]), res[-1]


def _wait_copies(send_sems, recv_sems, srcs, lands, copies, which, after, *, name):
    src_ids, land_ids = sorted(srcs), sorted(lands)
    n_s, n_l = len(src_ids), len(land_ids)
    srcs, lands = [srcs[i] for i in src_ids], [lands[i] for i in land_ids]

    def body(*refs):
        src_refs, land_refs = dict(zip(src_ids, refs[:n_s])), dict(zip(land_ids, refs[n_s:n_s + n_l]))
        s_sems, r_sems = refs[n_s + n_l], refs[n_s + n_l + 1]
        me = _coords()
        for n in which:
            cp = _copy_desc(copies, n, src_refs, land_refs, s_sems, r_sems, me)
            cp.wait_send()
            _copy_desc(copies, n, src_refs, land_refs, s_sems, r_sems, _flip(copies[n][2])).wait_recv()

    res = _pallas(
        body, name=name,
        out_shape=tuple(pltpu.HBM(b.shape, b.dtype) for b in list(srcs) + list(lands)),
        in_specs=[HBM_SPEC] * (n_s + n_l) + [SEM_SPEC, SEM_SPEC, pl.BlockSpec(memory_space=pl.ANY)],
        out_specs=tuple([HBM_SPEC] * (n_s + n_l)),
        input_output_aliases={i: i for i in range(n_s + n_l)},
        compiler_params=pltpu.CompilerParams(has_side_effects=SIDE_EFFECT),
    )(*srcs, *lands, send_sems, recv_sems, after)
    return dict(zip(src_ids, res[:n_s])), dict(zip(land_ids, res[n_s:]))


def _adam_math(w, g, m, v):
    m = ADAM_B1 * m + (1.0 - ADAM_B1) * g
    v = ADAM_B2 * v + (1.0 - ADAM_B2) * (g * g)
    m_hat = m / (1.0 - ADAM_B1 ** ADAM_STEP)
    v_hat = v / (1.0 - ADAM_B2 ** ADAM_STEP)
    delta = -ADAM_LR * (m_hat / (jnp.sqrt(v_hat) + ADAM_EPS) + ADAM_WD * w)
    return delta, m, v


def _shard_adamw(own, landing, w, m, v, *, name):
    r, cdim = w.shape
    tr = _pick(r, (256, 128, 64, 32, 16))

    def body(g_ref, l_ref, w_ref, m_ref, v_ref, g_out, d_out, m_out, v_out):
        g = g_ref[...]
        for n in range(N_DEV - 1):
            g = g + l_ref[n].astype(F32)
        d, mn, vn = _adam_math(w_ref[...], g, m_ref[...], v_ref[...])
        g_out[...] = g
        d_out[...] = d
        m_out[...] = mn
        v_out[...] = vn

    blk = pl.BlockSpec((tr, cdim), lambda i: (i, 0))
    return _pallas(
        body, name=name, grid=(r // tr,),
        in_specs=[blk, pl.BlockSpec((N_DEV - 1, tr, cdim), lambda i: (0, i, 0)), blk, blk, blk], out_specs=[blk] * 4,
        out_shape=[jax.ShapeDtypeStruct((r, cdim), F32)] * 4,
        compiler_params=_cparams("parallel"),
    )(own, landing, w, m, v)


def _small_adamw(slots, w, m, v, *, name):
    r, cdim = w.shape

    def body(s_ref, w_ref, m_ref, v_ref, g_out, d_out, m_out, v_out):
        g = s_ref[0]
        for n in range(1, N_DEV):
            g = g + s_ref[n]
        d, mn, vn = _adam_math(w_ref[...], g, m_ref[...], v_ref[...])
        g_out[...] = g
        d_out[...] = d
        m_out[...] = mn
        v_out[...] = vn

    tr = _pick(r, (256, 128, 64, 32, 16, 8))
    blk = pl.BlockSpec((tr, cdim), lambda i: (i, 0))
    return _pallas(
        body, name=name, grid=(r // tr,),
        in_specs=[pl.BlockSpec((N_DEV, tr, cdim), lambda i: (0, i, 0)), blk, blk, blk], out_specs=[blk] * 4,
        out_shape=[jax.ShapeDtypeStruct((r, cdim), F32)] * 4,
        compiler_params=_cparams("parallel"),
    )(slots, w, m, v)


def _plain_adamw(g, w, m, v, *, name):
    def body(g_ref, w_ref, m_ref, v_ref, d_out, m_out, v_out):
        d, mn, vn = _adam_math(w_ref[...], g_ref[...], m_ref[...], v_ref[...])
        d_out[...] = d
        m_out[...] = mn
        v_out[...] = vn

    spec = pl.BlockSpec(g.shape, lambda: (0,) * g.ndim)
    return _pallas(body, name=name, in_specs=[spec] * 4, out_specs=[spec] * 3,
                   out_shape=[jax.ShapeDtypeStruct(g.shape, F32)] * 3)(g, w, m, v)


def _cast_bf16(x, *, name):
    r, cdim = x.shape
    tr = _pick(r, (256, 128, 64, 32, 16))

    def body(x_ref, o_ref):
        o_ref[...] = x_ref[...].astype(BF16)

    return _pallas(body, name=name, grid=(r // tr,), in_specs=[pl.BlockSpec((tr, cdim), lambda i: (i, 0))],
                   out_specs=pl.BlockSpec((tr, cdim), lambda i: (i, 0)), out_shape=jax.ShapeDtypeStruct(x.shape, BF16),
                   compiler_params=_cparams("parallel"))(x)


SMALL_NAMES = ("norm_mix", "norm_ffn", "s5_a_re", "s5_a_im", "s5_log_dt", "s5_b_re", "s5_b_im", "s5_c_re", "s5_c_im",
               "norm_kv", "norm_final")


def _pack_rows(arrs):
    rows = []
    for a in arrs:
        flat = a.reshape(-1)
        pad = (-flat.shape[0]) % (8 * LANES)
        rows.append(jnp.pad(flat, (0, pad)).reshape(-1, LANES))
    return jnp.concatenate(rows, axis=0)


def _unpack_rows(packed, like):
    out, r0 = [], 0
    for a in like:
        n = math.prod(a.shape)
        nr = (n + 8 * LANES - 1) // (8 * LANES) * 8
        out.append(packed[r0:r0 + nr].reshape(-1)[:n].reshape(a.shape))
        r0 += nr
    return out


def kernel(x, meta_tokens, norm_mix, norm_ffn, s5_a_re, s5_a_im, s5_log_dt, s5_b_re, s5_b_im, s5_c_re, s5_c_im, s5_d, s5_w_glu, norm_kv, w_kv, w_q, w_o, w_ffn_in, w_ffn_out, norm_final, loss_target, m_meta_tokens, m_norm_mix, m_norm_ffn, m_s5_a_re, m_s5_a_im, m_s5_log_dt, m_s5_b_re, m_s5_b_im, m_s5_c_re, m_s5_c_im, m_s5_d, m_s5_w_glu, m_norm_kv, m_w_kv, m_w_q, m_w_o, m_w_ffn_in, m_w_ffn_out, m_norm_final, v_meta_tokens, v_norm_mix, v_norm_ffn, v_s5_a_re, v_s5_a_im, v_s5_log_dt, v_s5_b_re, v_s5_b_im, v_s5_c_re, v_s5_c_im, v_s5_d, v_s5_w_glu, v_norm_kv, v_w_kv, v_w_q, v_w_o, v_w_ffn_in, v_w_ffn_out, v_norm_final):
    args = dict(locals())
    seq = x.shape[1]
    n_valid = N_META + seq
    Lp = (n_valid + ATTN_BLOCK - 1) // ATTN_BLOCK * ATTN_BLOCK
    dm = D_MODEL
    my_slot = _slot_of(*_coords())

    n_ffn = w_ffn_in.shape[0]
    groups = {"s5": ["s5_w_glu"], "ffn0": ["w_ffn_in0", "w_ffn_out0"], "attn": ["w_kv", "w_q", "w_o"],
              "ffn1": ["w_ffn_in1", "w_ffn_out1"]}
    shards = {"s5_w_glu": s5_w_glu[0], "w_kv": w_kv, "w_q": w_q[0], "w_o": w_o[0]}
    for l in range(n_ffn):
        shards[f"w_ffn_in{l}"], shards[f"w_ffn_out{l}"] = w_ffn_in[l], w_ffn_out[l]
    by_columns = {"s5_w_glu", "w_kv", "w_ffn_in0", "w_ffn_in1"}
    names = [n for g in groups.values() for n in g]

    def gathered_full(n, g8):
        if n in by_columns:
            return jnp.swapaxes(g8, 0, 1).reshape(g8.shape[1], -1)
        return g8.reshape(-1, g8.shape[2])

    col_shard = jnp.concatenate([meta_tokens, s5_d, jnp.zeros((7, LANES), F32)], axis=0)
    ag_srcs = [col_shard] + [_cast_bf16(shards[n], name="cast_" + n) for n in names]
    ag_copies = [(t, None, bits, t, _slot_of) for t in range(len(ag_srcs)) for bits in ALL_FLIPS]
    ag_send, ag_recv, ag_srcs, ag_lands, ag_token = _start_copies(
        ag_srcs, [((N_DEV,) + s.shape, s.dtype) for s in ag_srcs], ag_copies, name="ag_start")

    def gather_wait(tag, ids, after):
        which = [t * len(ALL_FLIPS) + r for t in ids for r in range(len(ALL_FLIPS))]
        srcs, lands = _wait_copies(ag_send, ag_recv, {t: ag_srcs[t] for t in ids}, {t: ag_lands[t] for t in ids},
                                   ag_copies, which, after, name="ag_wait_" + tag)
        return {t: lax.dynamic_update_index_in_dim(lands[t], srcs[t], my_slot, 0) for t in ids}

    def get_weights(group, after):
        ids = [1 + names.index(n) for n in groups[group]]
        full = {names[t - 1]: gathered_full(names[t - 1], g8) for t, g8 in gather_wait(group, ids, after).items()}
        if group == "attn":
            w_kv_full = full.pop("w_kv")
            full["w_k"], full["w_v"] = w_kv_full[:, :dm], w_kv_full[:, dm:]
        return {n.rstrip("01") if n.startswith("w_ffn") else n: w for n, w in full.items()}

    gcol = gather_wait("cols", [0], ag_token)[0]
    small = {n: args[n] for n in SMALL_NAMES}
    small["meta_tokens"] = jnp.swapaxes(gcol[:, :N_META], 0, 1).reshape(N_META, dm)
    small["s5_d"] = gcol[:, N_META].reshape(1, dm)

    scatters = {}

    def put_grads(group, grads):
        if group == "attn":
            grads = {"w_kv": jnp.concatenate([grads.pop("w_k"), grads.pop("w_v")], axis=1), **grads}
        own, send = {}, []
        for n in groups[group]:
            g = grads[n.rstrip("01") if n.startswith("w_ffn") else n]
            r, c = shards[n].shape
            if n in by_columns:
                own[n] = lax.dynamic_slice_in_dim(g, my_slot * c, c, axis=1)
                send.append(jnp.swapaxes(g.reshape(r, N_DEV, c), 0, 1).astype(BF16))
            else:
                own[n] = lax.dynamic_slice_in_dim(g, my_slot * r, r, axis=0)
                send.append(g.reshape(N_DEV, r, c).astype(BF16))
        copies = [(t, (lambda x, y, c, b=bits: _slot_of(x ^ b[0], y ^ b[1], c ^ b[2])), bits, t, (lambda x, y, c, k=k: k))
                  for t in range(len(send)) for k, bits in enumerate(ALL_FLIPS)]
        lands = [((len(ALL_FLIPS),) + s.shape[1:], BF16) for s in send]
        s_sem, r_sem, srcs, lands, token = _start_copies(send, lands, copies, name="rs_start_" + group)
        scatters[group] = (s_sem, r_sem, srcs, lands, copies, own)
        return token[0, 0]

    x_pad = jnp.pad(x[0], ((N_META, Lp - n_valid), (0, 0)))
    t_pad = jnp.pad(loss_target[0], ((N_META, Lp - n_valid), (0, 0)))
    loss_local, g_xpad, g_small = _forward_backward(x_pad, t_pad, n_valid, small, get_weights, put_grads)
    loss = lax.psum(loss_local[0, 0], ("x", "y", "c"))
    grad_x = g_xpad[N_META:n_valid][None]

    small_list = [g_small[n] for n in SMALL_NAMES] + [g_small["meta_tokens"], g_small["s5_d"]]
    small_bufs = _all_gather([_pack_rows(small_list)], name="small_ag")

    out, per_tensor = {}, {}
    for group, (s_sem, r_sem, srcs, lands, copies, own) in scatters.items():
        ids = list(range(len(srcs)))
        _, landed = _wait_copies(s_sem, r_sem, dict(zip(ids, srcs)), dict(zip(ids, lands)), copies, list(range(len(copies))),
                                 small_bufs[0], name="rs_wait_" + group)
        for t, n in enumerate(groups[group]):
            base, layer = (n[:-1], int(n[-1])) if n.startswith("w_ffn") else (n, None)
            pick = (lambda a: a[layer]) if layer is not None else (lambda a: a.reshape(shards[n].shape))
            per_tensor[n] = _shard_adamw(own[n], landed[t], shards[n], pick(args["m_" + base]), pick(args["v_" + base]),
                                         name="adamw_" + n)
    for base in ("s5_w_glu", "w_kv", "w_q", "w_o"):
        out[base] = tuple(a.reshape(args[base].shape) for a in per_tensor[base])
    for base in ("w_ffn_in", "w_ffn_out"):
        out[base] = tuple(jnp.stack([per_tensor[f"{base}{l}"][k] for l in range(n_ffn)]) for k in range(4))

    zeros_tail = [jnp.zeros_like(g_small["meta_tokens"]), jnp.zeros_like(g_small["s5_d"])]
    pw = _pack_rows([args[n] for n in SMALL_NAMES] + zeros_tail)
    pm = _pack_rows([args["m_" + n] for n in SMALL_NAMES] + zeros_tail)
    pv = _pack_rows([args["v_" + n] for n in SMALL_NAMES] + zeros_tail)
    sg, sd, sm, sv = _small_adamw(small_bufs[0], pw, pm, pv, name="adamw_small")
    ug, ud, um, uv = (_unpack_rows(a, small_list) for a in (sg, sd, sm, sv))
    for k, n in enumerate(SMALL_NAMES):
        out[n] = (ug[k], ud[k], um[k], uv[k])
    g_meta = lax.dynamic_slice_in_dim(ug[-2], my_slot * LANES, LANES, axis=1)
    g_d = lax.dynamic_slice_in_dim(ug[-1].reshape(1, dm), my_slot * LANES, LANES, axis=1)
    pad7 = jnp.zeros((7, LANES), F32)
    gc = jnp.concatenate([g_meta, g_d, pad7], axis=0)
    wc = jnp.concatenate([meta_tokens, s5_d, pad7], axis=0)
    mc = jnp.concatenate([m_meta_tokens, m_s5_d, pad7], axis=0)
    vc = jnp.concatenate([v_meta_tokens, v_s5_d, pad7], axis=0)
    dc, mcn, vcn = _plain_adamw(gc, wc, mc, vc, name="adamw_cols")
    out["meta_tokens"] = (g_meta, dc[:N_META], mcn[:N_META], vcn[:N_META])
    out["s5_d"] = (g_d, dc[N_META:N_META + 1], mcn[N_META:N_META + 1], vcn[N_META:N_META + 1])

    order = ["meta_tokens", "norm_mix", "norm_ffn", "s5_a_re", "s5_a_im", "s5_log_dt", "s5_b_re", "s5_b_im", "s5_c_re",
             "s5_c_im", "s5_d", "s5_w_glu", "norm_kv", "w_kv", "w_q", "w_o", "w_ffn_in", "w_ffn_out", "norm_final"]
    res = [loss, grad_x]
    for k in range(4):
        res += [out[n][k] for n in order]
    return tuple(res)
```

```python
import functools
import math

import jax
import jax.numpy as jnp
from jax import lax
from jax.experimental import pallas as pl
from jax.experimental.pallas import tpu as pltpu

F32 = jnp.float32
BF16 = jnp.bfloat16

N_META = 16
D_MODEL = 1024
S5_GROUPS = 64
S5_GROUP = 16
S5_STATE = 64
N_HEADS = 16
HEAD_DIM = 64
RMS_EPS = 1e-6
ADAM_LR, ADAM_B1, ADAM_B2, ADAM_EPS, ADAM_WD, ADAM_STEP = 0.001, 0.9, 0.999, 1e-08, 0.01, 10

LANES = 128
ATTN_BLOCK = 128
GROUPS_PER_BLOCK = 8
N_DEV = 8
MESH = pl.DeviceIdType.MESH
VMEM_LIMIT = 56 * 1024 * 1024


def _pallas(body, **kw):
    return pl.pallas_call(body, **kw)


def _cparams(*sem):
    return pltpu.CompilerParams(dimension_semantics=sem, vmem_limit_bytes=VMEM_LIMIT)


def _pick(n, prefs):
    for p in prefs:
        if n % p == 0:
            return p
    return n


def _dot(a, b, ca=1, cb=0):
    return lax.dot_general(a, b, (((ca,), (cb,)), ((), ())), preferred_element_type=F32)


MATMUL_VMEM_BUDGET = 36 * 1024 * 1024
MATMUL_TILES = (1408, 1024, 512, 384, 256, 128)


def _matmul_tiles(M, N, K, a_bytes, b_bytes, tile_bytes, n_acc, offsets):
    best = None
    for tm in (t for t in MATMUL_TILES if M % t == 0):
        for tn in (t for t in MATMUL_TILES if N % t == 0):
            for tk in (t for t in MATMUL_TILES if K % t == 0):
                if any(off % (tn if along_n else tk) for off, along_n in offsets):
                    continue
                vmem = 2 * (tm * tk * a_bytes + tk * tn * b_bytes + tm * tn * tile_bytes) + n_acc * tm * tn * 4
                if vmem > MATMUL_VMEM_BUDGET:
                    continue
                traffic = M * K * a_bytes * (N // tn) + K * N * b_bytes * (M // tm) + M * N * tile_bytes
                traffic += (K // tk - 1) * n_acc * M * N * 8 // 3
                key = (traffic, -tk, -tm)
                if best is None or key < best[0]:
                    best = (key, (tm, tn, tk))
    return best[1]


def _gmm(a_list, b_list, accs, *, n, extras=(), epilogue=None, out_dtypes=(F32,), name):
    a0, ta0 = a_list[0]
    K, M = a0.shape if ta0 else a0.shape[::-1]
    N = n
    b_keys = sorted({(bi, off) for terms in accs for _, bi, off in terms})
    a_bytes = sum(a.dtype.itemsize for a, _ in a_list)
    b_bytes = sum(b_list[bi][0].dtype.itemsize for bi, _ in b_keys)
    tile_bytes = sum(e.dtype.itemsize for e in extras) + sum(jnp.dtype(d).itemsize for d in out_dtypes)
    offsets = [(off, not b_list[bi][1]) for bi, off in b_keys]
    tm, tn, tk = _matmul_tiles(M, N, K, a_bytes, b_bytes, tile_bytes, len(accs), offsets)
    nk = K // tk
    n_a, n_b, n_e, n_o, n_acc = len(a_list), len(b_keys), len(extras), len(out_dtypes), len(accs)
    if epilogue is None:
        epilogue = lambda *vals: vals

    def body(*refs):
        a_refs, b_refs = refs[:n_a], refs[n_a:n_a + n_b]
        e_refs = refs[n_a + n_b:n_a + n_b + n_e]
        o_refs = refs[n_a + n_b + n_e:n_a + n_b + n_e + n_o]
        acc_refs = refs[n_a + n_b + n_e + n_o:]
        a_vals = [r[...].astype(BF16) for r in a_refs]
        b_vals = {key: r[...].astype(BF16) for key, r in zip(b_keys, b_refs)}
        parts = []
        for terms in accs:
            p = None
            for ai, bi, off in terms:
                d = _dot(a_vals[ai], b_vals[(bi, off)], 0 if a_list[ai][1] else 1, 1 if b_list[bi][1] else 0)
                p = d if p is None else p + d
            parts.append(p)

        def finish(vals):
            outs = epilogue(*vals, *[r[...] for r in e_refs])
            for r, val in zip(o_refs, outs):
                r[...] = val.astype(r.dtype)

        if nk == 1:
            finish(parts)
        else:
            k = pl.program_id(2)

            @pl.when(k == 0)
            def _():
                for r, p in zip(acc_refs, parts):
                    r[...] = p

            @pl.when(k > 0)
            def _():
                for r, p in zip(acc_refs, parts):
                    r[...] += p

            @pl.when(k == nk - 1)
            def _():
                finish([r[...] for r in acc_refs])

    in_specs = [pl.BlockSpec((tk, tm), lambda i, j, k: (k, i)) if ta else pl.BlockSpec((tm, tk), lambda i, j, k: (i, k))
                for _, ta in a_list]
    for bi, off in b_keys:
        if b_list[bi][1]:
            in_specs.append(pl.BlockSpec((tn, tk), lambda i, j, k, o=off // tk: (j, k + o)))
        else:
            in_specs.append(pl.BlockSpec((tk, tn), lambda i, j, k, o=off // tn: (k, j + o)))
    tile = pl.BlockSpec((tm, tn), lambda i, j, k: (i, j))
    in_specs += [tile] * n_e
    return _pallas(
        body, name=name, grid=(M // tm, N // tn, nk), in_specs=in_specs, out_specs=[tile] * n_o,
        out_shape=[jax.ShapeDtypeStruct((M, N), d) for d in out_dtypes],
        scratch_shapes=[pltpu.VMEM((tm, tn), F32)] * (n_acc if nk > 1 else 0),
        compiler_params=_cparams("parallel", "parallel", "arbitrary"),
    )(*[a for a, _ in a_list], *[b_list[bi][0] for bi, _ in b_keys], *extras)


def _rowwise(fn, row_ins, full_ins, row_outs, acc_outs, *, tm, name):
    L = row_ins[0].shape[0]
    n_row, n_full, n_ro = len(row_ins), len(full_ins), len(row_outs)

    def body(*refs):
        ins = [r[...] for r in refs[:n_row + n_full]]
        outs = refs[n_row + n_full:]
        res = fn(*ins)
        for r, val in zip(outs[:n_ro], res[:n_ro]):
            r[...] = val.astype(r.dtype)
        if acc_outs:
            first = pl.program_id(0) == 0
            for r, val in zip(outs[n_ro:], res[n_ro:]):
                @pl.when(first)
                def _(r=r, val=val):
                    r[...] = val

                @pl.when(jnp.logical_not(first))
                def _(r=r, val=val):
                    r[...] += val

    in_specs = [pl.BlockSpec((tm, a.shape[1]), lambda i: (i, 0)) for a in row_ins]
    in_specs += [pl.BlockSpec(a.shape, lambda i, nd=a.ndim: (0,) * nd) for a in full_ins]
    out_specs = [pl.BlockSpec((tm, s[1]), lambda i: (i, 0)) for s, _ in row_outs]
    out_specs += [pl.BlockSpec(s, lambda i, nd=len(s): (0,) * nd) for s, _ in acc_outs]
    out_shape = [jax.ShapeDtypeStruct(s, d) for s, d in list(row_outs) + list(acc_outs)]
    return _pallas(
        body, name=name, grid=(L // tm,), in_specs=in_specs, out_specs=out_specs, out_shape=out_shape,
        compiler_params=_cparams("arbitrary" if acc_outs else "parallel"),
    )(*row_ins, *full_ins)


def _colsum(x):
    return jnp.sum(x, axis=0, keepdims=True)


def _rmsnorm_fwd(x, g, *, out_dtype, name):
    L, d = x.shape

    def f(xb, gb):
        r = lax.rsqrt(jnp.mean(xb * xb, axis=-1, keepdims=True) + RMS_EPS)
        return (xb * r * gb,)

    return _rowwise(f, [x], [g.reshape(1, d)], [((L, d), out_dtype)], [], tm=_pick(L, (384, 128)), name=name)[0]


def _rmsnorm_bwd(x, g, dy, resid, *, name):
    L, d = x.shape

    def f(xb, dyb, *rest):
        gb = rest[-1]
        r = lax.rsqrt(jnp.mean(xb * xb, axis=-1, keepdims=True) + RMS_EPS)
        xh = xb * r
        dxh = dyb * gb
        dx = r * (dxh - xh * jnp.mean(dxh * xh, axis=-1, keepdims=True))
        if resid is not None:
            dx = dx + rest[0]
        return dx, _colsum(dyb * xh)

    rows = [x, dy] + ([resid] if resid is not None else [])
    dx, dg = _rowwise(f, rows, [g.reshape(1, d)], [((L, d), F32)], [((1, d), F32)], tm=_pick(L, (384, 128)), name=name)
    return dx, dg.reshape(g.shape)


def _sigmoid(x):
    return 1.0 / (1.0 + jnp.exp(-x))


GELU_K = math.sqrt(2.0 / math.pi)
GELU_C = 0.044715


def _bdmm(a_list, w_rows, *, out_dtype=F32, name):
    L = a_list[0].shape[0]
    nb, ka, kb = w_rows[0][0].shape
    tm = _pick(L, (1408, 384, 128))
    n_a, n_o = len(a_list), len(w_rows)
    ws = [w for row in w_rows for w in row]

    def body(*refs):
        a_vals = [r[...].astype(BF16) for r in refs[:n_a]]
        w_refs = refs[n_a:n_a + n_a * n_o]
        outs = refs[n_a + n_a * n_o:]
        for o in range(n_o):
            acc = None
            for i in range(n_a):
                p = _dot(a_vals[i], w_refs[o * n_a + i][...].astype(BF16))
                acc = p if acc is None else acc + p
            outs[o][...] = acc.astype(out_dtype)

    return _pallas(
        body, name=name, grid=(L // tm, nb),
        in_specs=[pl.BlockSpec((tm, ka), lambda i, j: (i, j)) for _ in a_list]
        + [pl.BlockSpec((None, ka, kb), lambda i, j: (j, 0, 0)) for _ in ws],
        out_specs=[pl.BlockSpec((tm, kb), lambda i, j: (i, j)) for _ in range(n_o)],
        out_shape=[jax.ShapeDtypeStruct((L, nb * kb), out_dtype) for _ in range(n_o)],
        compiler_params=_cparams("parallel", "parallel"),
    )(*a_list, *ws)


def _bdmm_tn(a_list, g_list, nb, *, name):
    n_a, n_g = len(a_list), len(g_list)
    n_o = max(n_a, n_g)
    L = a_list[0].shape[0]
    ka, kb = a_list[0].shape[1] // nb, g_list[0].shape[1] // nb
    tm = _pick(L, (1408, 384, 128))

    def body(*refs):
        i = pl.program_id(1)
        a_vals = [r[...].astype(BF16) for r in refs[:n_a]]
        g_vals = [r[...].astype(BF16) for r in refs[n_a:n_a + n_g]]
        outs = refs[n_a + n_g:]
        parts = [_dot(a_vals[k % n_a], g_vals[k % n_g], 0, 0) for k in range(n_o)]

        @pl.when(i == 0)
        def _():
            for r, p in zip(outs, parts):
                r[...] = p

        @pl.when(i > 0)
        def _():
            for r, p in zip(outs, parts):
                r[...] += p

    return _pallas(
        body, name=name, grid=(nb, L // tm),
        in_specs=[pl.BlockSpec((tm, ka), lambda j, i: (i, j))] * n_a + [pl.BlockSpec((tm, kb), lambda j, i: (i, j))] * n_g,
        out_specs=[pl.BlockSpec((None, ka, kb), lambda j, i: (j, 0, 0))] * n_o,
        out_shape=[jax.ShapeDtypeStruct((nb, ka, kb), F32)] * n_o,
        compiler_params=_cparams("parallel", "arbitrary"),
    )(*a_list, *g_list)


SCAN_ROWS = 128
STATE_DTYPE = BF16


def _scan_fwd(br, bi, ar, ai, *, name):
    L, S, _ = br.shape
    tb = SCAN_ROWS

    def body(br_ref, bi_ref, ar_ref, ai_ref, xr_ref, xi_ref, carry_ref):
        @pl.when(pl.program_id(0) == 0)
        def _():
            carry_ref[...] = jnp.zeros_like(carry_ref)

        a_r, a_i = ar_ref[...], ai_ref[...]

        def step(t, c):
            xr, xi = c
            nr = a_r * xr - a_i * xi + br_ref[t].astype(F32)
            ni = a_r * xi + a_i * xr + bi_ref[t].astype(F32)
            xr_ref[t] = nr.astype(xr_ref.dtype)
            xi_ref[t] = ni.astype(xi_ref.dtype)
            return nr, ni

        xr, xi = lax.fori_loop(0, tb, step, (carry_ref[0], carry_ref[1]), unroll=8)
        carry_ref[0] = xr
        carry_ref[1] = xi

    blk = pl.BlockSpec((tb, S, LANES), lambda i: (i, 0, 0))
    par = pl.BlockSpec((S, LANES), lambda i: (0, 0))
    return _pallas(
        body, name=name, grid=(L // tb,), in_specs=[blk, blk, par, par], out_specs=[blk, blk],
        out_shape=[jax.ShapeDtypeStruct(br.shape, STATE_DTYPE)] * 2,
        scratch_shapes=[pltpu.VMEM((2, S, LANES), F32)],
        compiler_params=_cparams("arbitrary"),
    )(br, bi, ar, ai)


def _scan_bwd(gr_in, gi_in, ar, ai, xr, xi, *, name):
    L, S, _ = gr_in.shape
    tb = SCAN_ROWS
    nblk = L // tb

    def body(gr_ref, gi_ref, ar_ref, ai_ref, xr_ref, xi_ref, or_ref, oi_ref, dar_ref, dai_ref, carry_ref):
        @pl.when(pl.program_id(0) == 0)
        def _():
            carry_ref[...] = jnp.zeros_like(carry_ref)
            dar_ref[...] = jnp.zeros_like(dar_ref)
            dai_ref[...] = jnp.zeros_like(dai_ref)

        a_r, a_i = ar_ref[...], ai_ref[...]

        def step(s, c):
            gr, gi, dr, di = c
            t = tb - 1 - s
            x_r, x_i = xr_ref[t].astype(F32), xi_ref[t].astype(F32)
            dr = dr + gr * x_r + gi * x_i
            di = di + gi * x_r - gr * x_i
            nr = a_r * gr + a_i * gi + gr_ref[t].astype(F32)
            ni = a_r * gi - a_i * gr + gi_ref[t].astype(F32)
            or_ref[t] = nr.astype(or_ref.dtype)
            oi_ref[t] = ni.astype(oi_ref.dtype)
            return nr, ni, dr, di

        gr, gi, dr, di = lax.fori_loop(0, tb, step, (carry_ref[0], carry_ref[1], dar_ref[...], dai_ref[...]), unroll=8)
        carry_ref[0] = gr
        carry_ref[1] = gi
        dar_ref[...] = dr
        dai_ref[...] = di

    blk = pl.BlockSpec((tb, S, LANES), lambda i: (nblk - 1 - i, 0, 0))
    par = pl.BlockSpec((S, LANES), lambda i: (0, 0))
    return _pallas(
        body, name=name, grid=(nblk,), in_specs=[blk, blk, par, par, blk, blk], out_specs=[blk, blk, par, par],
        out_shape=[jax.ShapeDtypeStruct(gr_in.shape, STATE_DTYPE)] * 2 + [jax.ShapeDtypeStruct((S, LANES), F32)] * 2,
        scratch_shapes=[pltpu.VMEM((2, S, LANES), F32)],
        compiler_params=_cparams("arbitrary"),
    )(gr_in, gi_in, ar, ai, xr, xi)


def _make_ssm(name):
    nb = GROUPS_PER_BLOCK

    def fwd(u, wr, wi, cr, cin, lam_r, lam_i):
        L = u.shape[0]
        bur, bui = _bdmm([u], [[wr], [wi]], out_dtype=STATE_DTYPE, name=name + "_bu")
        S = bur.shape[1] // LANES
        xr, xi = _scan_fwd(bur.reshape(L, S, LANES), bui.reshape(L, S, LANES), lam_r, lam_i, name=name + "_scan")
        xr, xi = xr.reshape(L, S * LANES), xi.reshape(L, S * LANES)
        (y,) = _bdmm([xr, xi], [[cr, cin]], name=name + "_cx")
        return y, (u, wr, wi, cr, cin, lam_r, lam_i, xr, xi)

    def bwd(res, dy):
        u, wr, wi, cr, cin, lam_r, lam_i, xr, xi = res
        L = u.shape[0]
        S = xr.shape[1] // LANES
        tr = lambda w: jnp.swapaxes(w, 1, 2)
        gin_r, gin_i = _bdmm([dy], [[tr(cr)], [tr(cin)]], out_dtype=STATE_DTYPE, name=name + "_gin")
        gr, gi, dlr, dli = _scan_bwd(gin_r.reshape(L, S, LANES), gin_i.reshape(L, S, LANES), lam_r, lam_i,
                                     xr.reshape(L, S, LANES), xi.reshape(L, S, LANES), name=name + "_rscan")
        gr, gi = gr.reshape(L, S * LANES), gi.reshape(L, S * LANES)
        (du,) = _bdmm([gr, gi], [[tr(wr), tr(wi)]], name=name + "_du")
        dwr, dwi = _bdmm_tn([u], [gr, gi], nb, name=name + "_dw")
        dcr, dcin = _bdmm_tn([xr, xi], [dy], nb, name=name + "_dc")
        return du, dwr, dwi, dcr, dcin, dlr, dli

    return fwd, bwd


def _s5_discretize(a_re, a_im, log_dt, b_re, b_im, c_re, c_im):
    G, P, C, nb = S5_GROUPS, S5_STATE, S5_GROUP, GROUPS_PER_BLOCK
    dt = jnp.exp(log_dt)[:, None]
    mag = jnp.exp(dt * a_re)
    ang = dt * a_im
    abar_re = mag * jnp.cos(ang)
    abar_im = mag * jnp.sin(ang)
    den = a_re * a_re + a_im * a_im
    coef_re = ((abar_re - 1.0) * a_re + abar_im * a_im) / den
    coef_im = (abar_im * a_re - (abar_re - 1.0) * a_im) / den
    bbar_re = coef_re[..., None] * b_re - coef_im[..., None] * b_im
    bbar_im = coef_re[..., None] * b_im + coef_im[..., None] * b_re
    eye = jnp.eye(nb, dtype=F32)

    def blocks_in(bb):
        return jnp.einsum("jgpc,gh->jgchp", bb.reshape(G // nb, nb, P, C), eye).reshape(G // nb, nb * C, nb * P)

    def blocks_out(cc):
        return jnp.einsum("jgcp,gh->jgphc", cc.reshape(G // nb, nb, C, P), eye).reshape(G // nb, nb * P, nb * C)

    lam_r = abar_re.reshape(G * P // LANES, LANES)
    lam_i = abar_im.reshape(G * P // LANES, LANES)
    return blocks_in(bbar_re), blocks_in(bbar_im), blocks_out(c_re), blocks_out(-c_im), lam_r, lam_i


ATTN_Q_ROWS = 384


HEADS_PER_BLOCK = LANES // HEAD_DIM


def _head_masks(shape, axis):
    idx = lax.broadcasted_iota(jnp.int32, shape, axis) // HEAD_DIM
    return [idx == hh for hh in range(HEADS_PER_BLOCK)]


def _masked_bf16(x, masks):
    return [jnp.where(m, x, 0.0).astype(BF16) for m in masks]


def _tri_sum(x, tri2):
    hi = x.astype(BF16)
    lo = (x - hi.astype(F32)).astype(BF16)
    return _dot(jnp.concatenate([hi, lo], axis=1), tri2)


def _attn_weights(qh, kt, strict, after, c):
    z = _dot(qh, kt)
    lb = jnp.minimum(z, 0.0) - jnp.log(1.0 + jnp.exp(-jnp.abs(z)))
    l1m = lb - z
    if strict is not None:
        l1m = jnp.where(strict, l1m, 0.0)
    rem = _tri_sum(l1m, after)
    w = jnp.exp(lb + rem + c)
    if strict is not None:
        w = jnp.where(strict, w, 0.0)
    return lb, w, rem[:, 0:1] + l1m[:, 0:1]


def _tri(cmp):
    tk = ATTN_BLOCK
    tri = cmp(lax.broadcasted_iota(jnp.int32, (tk, tk), 0), lax.broadcasted_iota(jnp.int32, (tk, tk), 1)).astype(BF16)
    return jnp.concatenate([tri, tri], axis=0)


def _attn_tile_index(i, jb, nsub):
    return nsub * (i * (i + 1) // 2) + jb


def _attn_fwd(q, kt, v, *, name):
    L = q.shape[0]
    tk = ATTN_BLOCK
    tq = _pick(L, (ATTN_Q_ROWS, tk))
    nsub = tq // tk
    n_qt = L // tq
    n_tiles = _attn_tile_index(n_qt, 0, nsub)
    scale = 1.0 / math.sqrt(HEAD_DIM)

    def body(q_ref, kt_ref, v_ref, o_ref, w_hbm, s_hbm, acc_ref, w_stage, s_stage, sems):
        hp, i = pl.program_id(0), pl.program_id(1)
        after = _tri(lambda r, c: r > c)
        rowq = lax.broadcasted_iota(jnp.int32, (tq, tk), 0)
        colq = lax.broadcasted_iota(jnp.int32, (tq, tk), 1)
        qs = _masked_bf16(q_ref[...].astype(F32) * scale, _head_masks((tq, LANES), 1))
        vmasks = _head_masks((tk, LANES), 1)
        acc_ref[...] = jnp.zeros_like(acc_ref)

        def saves(buf, ii):
            cps = []
            for jd in range(nsub):
                idx = _attn_tile_index(i, ii * nsub + jd, nsub)
                cps.append(pltpu.make_async_copy(w_stage.at[buf, jd], w_hbm.at[hp, idx], sems.at[0, buf, jd]))
                cps.append(pltpu.make_async_copy(s_stage.at[buf, jd], s_hbm.at[hp, idx], sems.at[1, buf, jd]))
            return cps

        def tile(jb, buf, jd, strict, cs, first_row=0):
            r0 = pl.multiple_of(jb * tk, tk)
            rows = pl.ds(first_row, tq - first_row)
            kt_j = kt_ref[:, pl.ds(r0, tk)].astype(BF16)
            vs = _masked_bf16(v_ref[pl.ds(r0, tk), :], vmasks)
            out, acc = [], None
            for hh, (qh, vh, c) in enumerate(zip(qs, vs, cs)):
                lb, w, tot = _attn_weights(qh[first_row:], kt_j, None if strict is None else strict[first_row:], after,
                                           c[first_row:])
                w = w.astype(BF16)
                if first_row:
                    w_stage[buf, jd, hh, pl.ds(0, first_row), :] = jnp.zeros((first_row, tk), BF16)
                    s_stage[buf, jd, hh, pl.ds(0, first_row), :] = jnp.zeros((first_row, tk), BF16)
                w_stage[buf, jd, hh, rows, :] = w
                s_stage[buf, jd, hh, rows, :] = lb.astype(BF16)
                part = _dot(w, vh)
                acc = part if acc is None else acc + part
                out.append(jnp.concatenate([c[:first_row], c[first_row:] + tot], axis=0) if first_row else c + tot)
            acc_ref[rows, :] += acc
            return tuple(out)

        cs = tuple(jnp.zeros((tq, 1), F32) for _ in qs)
        for jd in reversed(range(nsub)):
            cs = tile(i * nsub + jd, 0, jd, (colq + jd * tk) < rowq, cs, first_row=jd * tk)
        for cp in saves(0, i):
            cp.start()

        def step(ii, cs):
            buf = (ii + 1) % 2

            @pl.when(ii >= 1)
            def _():
                for cp in saves(buf, 0):
                    cp.wait()

            for jd in reversed(range(nsub)):
                cs = tile((i - 1 - ii) * nsub + jd, buf, jd, None, cs)
            for cp in saves(buf, i - 1 - ii):
                cp.start()
            return cs

        lax.fori_loop(0, i, step, cs)
        o_ref[...] = acc_ref[...].astype(o_ref.dtype)
        for cp in saves(0, 0):
            cp.wait()

        @pl.when(i >= 1)
        def _():
            for cp in saves(1, 0):
                cp.wait()

    blk = pl.BlockSpec((tq, LANES), lambda h, i: (i, h))
    whole = pl.BlockSpec((L, LANES), lambda h, i: (0, h))
    whole_t = pl.BlockSpec((LANES, L), lambda h, i: (h, 0))
    any_spec = pl.BlockSpec(memory_space=pl.ANY)
    nhb = N_HEADS // HEADS_PER_BLOCK
    saved = jax.ShapeDtypeStruct((nhb, n_tiles, HEADS_PER_BLOCK, tq, tk), BF16)
    return _pallas(
        body, name=name, grid=(nhb, n_qt), in_specs=[blk, whole_t, whole], out_specs=[blk, any_spec, any_spec],
        out_shape=[jax.ShapeDtypeStruct(q.shape, BF16), saved, saved],
        scratch_shapes=[pltpu.VMEM((tq, LANES), F32), pltpu.VMEM((2, nsub, HEADS_PER_BLOCK, tq, tk), BF16),
                        pltpu.VMEM((2, nsub, HEADS_PER_BLOCK, tq, tk), BF16), pltpu.SemaphoreType.DMA((2, 2, nsub))],
        compiler_params=_cparams("parallel", "arbitrary"),
    )(q, kt, v)


def _attn_bwd(q, k, vt, do, w_saved, s_saved, *, name):
    L = q.shape[0]
    tk = ATTN_BLOCK
    tq = _pick(L, (ATTN_Q_ROWS, tk))
    nsub = tq // tk
    scale = 1.0 / math.sqrt(HEAD_DIM)

    def body(q_ref, k_ref, vt_ref, do_ref, w_hbm, s_hbm, dq_ref, dkt_ref, dvt_ref, e_scr, sig_scr, dq_acc_ref, w_stage, sems):
        hp, i = pl.program_id(0), pl.program_id(1)

        @pl.when(i == 0)
        def _():
            dkt_ref[...] = jnp.zeros_like(dkt_ref)
            dvt_ref[...] = jnp.zeros_like(dvt_ref)

        before = _tri(lambda r, c: r < c)
        rowq = lax.broadcasted_iota(jnp.int32, (tq, tk), 0)
        colq = lax.broadcasted_iota(jnp.int32, (tq, tk), 1)
        qmasks = _head_masks((tq, LANES), 1)
        tmasks = _head_masks((LANES, tq), 0)
        kmasks = _head_masks((tk, LANES), 1)
        q_scaled = q_ref[...].astype(F32) * scale
        do32 = do_ref[...].astype(F32)
        dos = _masked_bf16(do32, qmasks)
        qts = _masked_bf16(q_scaled.T, tmasks)
        dots = _masked_bf16(do32.T, tmasks)
        dq_acc_ref[...] = jnp.zeros_like(dq_acc_ref)

        def loads(buf, ii, of=None):
            hp_, i_ = (hp, i) if of is None else of
            cps = []
            for jd in range(nsub):
                jb = ii * nsub + jd
                idx = _attn_tile_index(i_, jb, nsub)
                cps.append(pltpu.make_async_copy(w_hbm.at[hp_, idx], w_stage.at[buf, jd], sems.at[0, buf, jd]))
                cps.append(pltpu.make_async_copy(s_hbm.at[hp_, idx], sig_scr.at[jb], sems.at[1, buf, jd]))
            return cps

        def weigh(buf, ii):
            for jd in range(nsub):
                jb = ii * nsub + jd
                r0 = pl.multiple_of(jb * tk, tk)
                vt_j = vt_ref[:, pl.ds(r0, tk)].astype(BF16)
                acc = None
                for hh, (dob, dot_h) in enumerate(zip(dos, dots)):
                    w = w_stage[buf, jd, hh]
                    e_scr[hh, jb] = _dot(dob, vt_j) * w.astype(F32)
                    part = _dot(dot_h, w)
                    acc = part if acc is None else acc + part
                dvt_ref[:, pl.ds(r0, tk)] += acc

        def tile_right(jb, strict, cs, first_row=0):
            r0 = pl.multiple_of(jb * tk, tk)
            rows = pl.ds(first_row, tq - first_row)
            ks = _masked_bf16(k_ref[pl.ds(r0, tk), :], kmasks)
            out, dq_acc, dk_acc = [], None, None
            for hh, (kh, qt_h, c2) in enumerate(zip(ks, qts, cs)):
                e = e_scr[hh, jb, rows, :]
                sig = jnp.exp(sig_scr[jb, hh, rows, :].astype(F32))
                left = _dot(e.astype(BF16), before[:tk])
                dz = e - (e + left + c2[first_row:]) * sig
                if strict is not None:
                    dz = jnp.where(strict[first_row:], dz, 0.0)
                dz = dz.astype(BF16)
                p1, p2 = _dot(dz, kh), _dot(qt_h[:, first_row:], dz)
                dq_acc = p1 if dq_acc is None else dq_acc + p1
                dk_acc = p2 if dk_acc is None else dk_acc + p2
                tot = left[:, tk - 1:tk] + e[:, tk - 1:tk]
                out.append(jnp.concatenate([c2[:first_row], c2[first_row:] + tot], axis=0) if first_row else c2 + tot)
            dq_acc_ref[rows, :] += dq_acc
            dkt_ref[:, pl.ds(r0, tk)] += dk_acc
            return tuple(out)

        @pl.when(jnp.logical_and(hp == 0, i == 0))
        def _():
            for cp in loads(0, 0):
                cp.start()

        def weigh_step(ii, carry):
            buf = ii % 2
            for cp in loads(1 - buf, ii + 1):
                cp.start()
            for cp in loads(buf, ii):
                cp.wait()
            weigh(buf, ii)
            return carry

        lax.fori_loop(0, i, weigh_step, 0)
        for cp in loads(i % 2, i):
            cp.wait()
        weigh(i % 2, i)

        def right_step(ii, cs):
            for jd in range(nsub):
                cs = tile_right(ii * nsub + jd, None, cs)
            return cs

        cs = lax.fori_loop(0, i, right_step, tuple(jnp.zeros((tq, 1), F32) for _ in dos))
        for jd in range(nsub):
            cs = tile_right(i * nsub + jd, (colq + jd * tk) < rowq, cs, first_row=jd * tk)
        dq_ref[...] = (dq_acc_ref[...] * scale).astype(dq_ref.dtype)

        wraps = i == n_qt - 1

        @pl.when(jnp.logical_not(jnp.logical_and(wraps, hp == nhb - 1)))
        def _():
            nxt = (jnp.where(wraps, hp + 1, hp), jnp.where(wraps, 0, i + 1))
            for cp in loads(0, 0, of=nxt):
                cp.start()

    n_qt = L // tq
    nhb = N_HEADS // HEADS_PER_BLOCK
    blk = pl.BlockSpec((tq, LANES), lambda h, i: (i, h))
    whole = pl.BlockSpec((L, LANES), lambda h, i: (0, h))
    whole_t = pl.BlockSpec((LANES, L), lambda h, i: (h, 0))
    t_shape = jax.ShapeDtypeStruct((q.shape[1], L), F32)
    any_spec = pl.BlockSpec(memory_space=pl.ANY)
    return _pallas(
        body, name=name, grid=(nhb, L // tq), in_specs=[blk, whole, whole_t, blk, any_spec, any_spec],
        out_specs=[blk, whole_t, whole_t], out_shape=[jax.ShapeDtypeStruct(q.shape, BF16), t_shape, t_shape],
        scratch_shapes=[pltpu.VMEM((HEADS_PER_BLOCK, L // tk, tq, tk), F32), pltpu.VMEM((L // tk, HEADS_PER_BLOCK, tq, tk), BF16),
                        pltpu.VMEM((tq, LANES), F32), pltpu.VMEM((2, nsub, HEADS_PER_BLOCK, tq, tk), BF16),
                        pltpu.SemaphoreType.DMA((2, 2, nsub))],
        compiler_params=_cparams("arbitrary", "arbitrary"),
    )(q, k, vt, do, w_saved, s_saved)


def _make_loss_head(n_valid, name):
    def run(h, g, target):
        L, d = h.shape
        tm = _pick(L, (384, 128))

        def body(h_ref, t_ref, g_ref, loss_ref, dh_ref, dg_ref):
            i = pl.program_id(0)
            x = h_ref[...]
            gb = g_ref[...]
            rows = lax.broadcasted_iota(jnp.int32, (tm, 1), 0) + i * tm
            valid = jnp.logical_and(rows >= N_META, rows < n_valid)
            r = lax.rsqrt(jnp.mean(x * x, axis=-1, keepdims=True) + RMS_EPS)
            xh = x * r
            err = jnp.where(valid, xh * gb - t_ref[...], 0.0)
            dy = err * (1.0 / d)
            dxh = dy * gb
            dh_ref[...] = r * (dxh - xh * jnp.mean(dxh * xh, axis=-1, keepdims=True))
            part = 0.5 / d * jnp.sum(jnp.sum(err * err, axis=1, keepdims=True), axis=0, keepdims=True)
            dgp = _colsum(dy * xh)

            @pl.when(i == 0)
            def _():
                loss_ref[...] = part
                dg_ref[...] = dgp

            @pl.when(i > 0)
            def _():
                loss_ref[...] += part
                dg_ref[...] += dgp

        rowspec = pl.BlockSpec((tm, d), lambda i: (i, 0))
        return _pallas(
            body, name=name, grid=(L // tm,),
            in_specs=[rowspec, rowspec, pl.BlockSpec((1, d), lambda i: (0, 0))],
            out_specs=[pl.BlockSpec((1, 1), lambda i: (0, 0)), rowspec, pl.BlockSpec((1, d), lambda i: (0, 0))],
            out_shape=[jax.ShapeDtypeStruct((1, 1), F32), jax.ShapeDtypeStruct((L, d), F32), jax.ShapeDtypeStruct((1, d), F32)],
            compiler_params=_cparams("arbitrary"),
        )(h, target, g.reshape(1, d))

    return run


def _mm(a, b, *, ta=False, tb=False, out_dtype=F32, extras=(), epilogue=None, name):
    n = b.shape[0] if tb else b.shape[1]
    return _gmm([(a, ta)], [(b, tb)], [[(0, 0, 0)]], n=n, extras=extras, epilogue=epilogue, out_dtypes=(out_dtype,), name=name)[0]


def _add_epilogue(acc, resid):
    return (acc + resid,)


def _make_ffn_block(name):
    def fwd(h, gain, w_in, w_out):
        dff = w_out.shape[0]
        f = _rmsnorm_fwd(h, gain, out_dtype=BF16, name=name + "_norm")

        def swiglu(g, u):
            return g, u, g * _sigmoid(g) * u

        g, u, a = _gmm([(f, False)], [(w_in, False)], [[(0, 0, 0)], [(0, 0, dff)]], n=dff, epilogue=swiglu,
                       out_dtypes=(BF16, BF16, BF16), name=name + "_in")
        h2 = _mm(a, w_out, extras=[h], epilogue=_add_epilogue, name=name + "_out")
        return h2, (h, gain, f, g, u, a, w_in, w_out)

    def bwd(res, dh2, emit):
        h, gain, f, g, u, a, w_in, w_out = res
        dff, dm = w_out.shape

        def dswiglu(da, gb, ub):
            gb, ub = gb.astype(F32), ub.astype(F32)
            s = _sigmoid(gb)
            return da * ub * (s + gb * s * (1.0 - s)), da * gb * s

        dg, du = _gmm([(dh2, False)], [(w_out, True)], [[(0, 0, 0)]], n=dff, extras=[g, u], epilogue=dswiglu,
                      out_dtypes=(BF16, BF16), name=name + "_da")
        dw_out = _mm(a, dh2, ta=True, name=name + "_dwout")
        (df,) = _gmm([(dg, False), (du, False)], [(w_in, True)], [[(0, 0, 0), (1, 0, dff)]], n=dm, name=name + "_df")
        dw_in = jnp.concatenate([_mm(f, dg, ta=True, name=name + "_dwg"), _mm(f, du, ta=True, name=name + "_dwu")], axis=1)
        tok = emit({"w_ffn_in": dw_in, "w_ffn_out": dw_out})
        dh, dgain = _rmsnorm_bwd(h, gain + tok, df, dh2, name=name + "_dnorm")
        return dh, dgain

    return fwd, bwd


def _make_s5_out_block(name):
    def post(yb, ub, db):
        s = yb + db * ub
        return (0.5 * s * (1.0 + jnp.tanh(GELU_K * (s + GELU_C * s * s * s))),)

    def fwd(y, u, d, h, w_glu):
        L, dm = y.shape
        (z,) = _rowwise(post, [y, u], [d.reshape(1, dm)], [((L, dm), BF16)], [], tm=_pick(L, (384, 128)), name=name + "_gelu")

        def glu(val, gate, resid):
            return resid + val * _sigmoid(gate), val, gate

        h2, val, gate = _gmm([(z, False)], [(w_glu, False)], [[(0, 0, 0)], [(0, 0, dm)]], n=dm, extras=[h], epilogue=glu,
                             out_dtypes=(F32, BF16, BF16), name=name + "_glu")
        return h2, (y, u, d, z, val, gate, w_glu)

    def bwd(res, dh2, emit):
        y, u, d, z, val, gate, w_glu = res
        L, dm = y.shape

        def dglu(dh2b, valb, gateb):
            s = _sigmoid(gateb.astype(F32))
            return dh2b * s, dh2b * valb.astype(F32) * s * (1.0 - s)

        dval, dgate = _rowwise(dglu, [dh2, val, gate], [], [((L, dm), BF16)] * 2, [], tm=_pick(L, (384, 128)), name=name + "_dglu")
        (dz,) = _gmm([(dval, False), (dgate, False)], [(w_glu, True)], [[(0, 0, 0), (1, 0, dm)]], n=dm, name=name + "_dz")

        def dpost(yb, ub, dzb, db):
            s = yb + db * ub
            t = jnp.tanh(GELU_K * (s + GELU_C * s * s * s))
            ds = dzb * (0.5 * (1.0 + t) + 0.5 * s * (1.0 - t * t) * GELU_K * (1.0 + 3.0 * GELU_C * s * s))
            return ds, ds * db, _colsum(ds * ub)

        dw = jnp.concatenate([_mm(z, dval, ta=True, name=name + "_dwv"), _mm(z, dgate, ta=True, name=name + "_dwg")], axis=1)
        tok = emit({"s5_w_glu": dw})
        dy, du, dd = _rowwise(dpost, [y, u, dz], [(d + tok).reshape(1, dm)], [((L, dm), F32)] * 2, [((1, dm), F32)],
                              tm=_pick(L, (384, 128)), name=name + "_dgelu")
        return dy, du, dd.reshape(d.shape)

    return fwd, bwd


def _make_attn_layer(name):
    def fwd(h, g_kv, g_q, w_k, w_v, w_q, w_o):
        dm = h.shape[1]
        kvn = _rmsnorm_fwd(h, g_kv, out_dtype=BF16, name=name + "_kvnorm")
        qn = _rmsnorm_fwd(h, g_q, out_dtype=BF16, name=name + "_qnorm")
        k, v = _gmm([(kvn, False)], [(w_k, False), (w_v, False)], [[(0, 0, 0)], [(0, 1, 0)]], n=dm,
                    out_dtypes=(BF16, BF16), name=name + "_kv")
        q = _mm(qn, w_q, out_dtype=BF16, name=name + "_q")
        kt, vt = _gmm([(w_k, True), (w_v, True)], [(kvn, True)], [[(0, 0, 0)], [(1, 0, 0)]], n=h.shape[0],
                      out_dtypes=(BF16, BF16), name=name + "_kvt")
        o, w_saved, s_saved = _attn_fwd(q, kt, v, name=name + "_fwd")
        h2 = _mm(o, w_o, extras=[h], epilogue=_add_epilogue, name=name + "_o")
        return h2, (h, g_kv, g_q, kvn, qn, q, k, vt, o, w_saved, s_saved, w_k, w_v, w_q, w_o)

    def bwd(res, dh2, emit):
        h, g_kv, g_q, kvn, qn, q, k, vt, o, w_saved, s_saved, w_k, w_v, w_q, w_o = res
        dm = h.shape[1]
        do = _mm(dh2, w_o, tb=True, out_dtype=BF16, name=name + "_do")
        dw_o = _mm(o, dh2, ta=True, name=name + "_dwo")
        dq, dkt, dvt = _attn_bwd(q, k, vt, do, w_saved, s_saved, name=name + "_bwd")
        dqn = _mm(dq, w_q, tb=True, name=name + "_dqn")
        dw_q = _mm(qn, dq, ta=True, name=name + "_dwq")
        (dkvn,) = _gmm([(dkt, True), (dvt, True)], [(w_k, True), (w_v, True)], [[(0, 0, 0), (1, 1, 0)]], n=dm, name=name + "_dkvn")
        dw_k = _mm(kvn, dkt, ta=True, tb=True, name=name + "_dwk")
        dw_v = _mm(kvn, dvt, ta=True, tb=True, name=name + "_dwv")
        tok = emit({"w_k": dw_k, "w_v": dw_v, "w_q": dw_q, "w_o": dw_o})
        dh, dg_q = _rmsnorm_bwd(h, g_q + tok, dqn, dh2, name=name + "_dqnorm")
        dh, dg_kv = _rmsnorm_bwd(h, g_kv, dkvn, dh, name=name + "_dkvnorm")
        return dh, dg_kv, dg_q

    return fwd, bwd


S5_PARAMS = ("s5_a_re", "s5_a_im", "s5_log_dt", "s5_b_re", "s5_b_im", "s5_c_re", "s5_c_im")


def _forward_backward(x_pad, target_pad, n_valid, small, get_weights, put_grads):
    gs = {}
    h0 = x_pad.at[:N_META].set(small["meta_tokens"])

    ssm_fwd, ssm_bwd = _make_ssm("ssm")
    s5out_fwd, s5out_bwd = _make_s5_out_block("s5out")
    ffn_fwd, ffn_bwd = zip(_make_ffn_block("ffn0"), _make_ffn_block("ffn1"))
    attn_fwd, attn_bwd = _make_attn_layer("attn")
    u = _rmsnorm_fwd(h0, small["norm_mix"][0], out_dtype=F32, name="mix0_norm")
    disc, disc_vjp = jax.vjp(_s5_discretize, *[small[n][0] for n in S5_PARAMS])
    y, ssm_res = ssm_fwd(u, *disc)
    w = get_weights("s5", y)
    h1, s5out_res = s5out_fwd(y, u, small["s5_d"][0], h0, w["s5_w_glu"])
    w = get_weights("ffn0", h1)
    h2, ffn0_res = ffn_fwd[0](h1, small["norm_ffn"][0], w["w_ffn_in"], w["w_ffn_out"])
    w = get_weights("attn", h2)
    h3, attn_res = attn_fwd(h2, small["norm_kv"], small["norm_mix"][1], w["w_k"], w["w_v"], w["w_q"], w["w_o"])
    w = get_weights("ffn1", h3)
    h4, ffn1_res = ffn_fwd[1](h3, small["norm_ffn"][1], w["w_ffn_in"], w["w_ffn_out"])
    loss, dh, dg_final = _make_loss_head(n_valid, "loss_head")(h4, small["norm_final"], target_pad)
    gs["norm_final"] = dg_final.reshape(small["norm_final"].shape)

    dh, dg_ffn1 = ffn_bwd[1](ffn1_res, dh, functools.partial(put_grads, "ffn1"))
    dh, gs["norm_kv"], dg_mix1 = attn_bwd(attn_res, dh, functools.partial(put_grads, "attn"))
    dh, dg_ffn0 = ffn_bwd[0](ffn0_res, dh, functools.partial(put_grads, "ffn0"))
    dy, du, dd = s5out_bwd(s5out_res, dh, functools.partial(put_grads, "s5"))
    du_ssm, *ddisc = ssm_bwd(ssm_res, dy)
    dh, dg_mix0 = _rmsnorm_bwd(h0, small["norm_mix"][0], du + du_ssm, dh, name="mix0_dnorm")
    for n, g in zip(S5_PARAMS, disc_vjp(tuple(ddisc))):
        gs[n] = g[None]
    gs["s5_d"] = dd.reshape(small["s5_d"].shape)
    gs["norm_mix"] = jnp.stack([dg_mix0, dg_mix1])
    gs["norm_ffn"] = jnp.stack([dg_ffn0, dg_ffn1])
    gs["meta_tokens"] = dh[:N_META]
    return loss, dh, gs


def _coords():
    return lax.axis_index("x"), lax.axis_index("y"), lax.axis_index("c")


def _flip(bits):
    x, y, c = _coords()
    fx, fy, fc = bits
    return (x ^ fx if fx else x, y ^ fy if fy else y, c ^ fc if fc else c)


def _exchange(ins, out_shapes, copies, local_copies, *, aliases=None, name):
    n_in, n_out = len(ins), len(out_shapes)
    n_cp, n_loc = len(copies), len(local_copies)
    aliases = aliases or {}

    def body(*refs):
        in_refs, out_refs = refs[:n_in], refs[n_in:n_in + n_out]
        send_sems, recv_sems, loc_sems = refs[n_in + n_out:]
        me = _coords()
        locs = []
        for n, (ii, oi, dfn) in enumerate(local_copies):
            cp = pltpu.make_async_copy(in_refs[ii], out_refs[oi].at[dfn(*me)], loc_sems.at[n])
            cp.start()
            locs.append(cp)
        sends = []
        for n, (ii, sfn, bits, oi, dfn) in enumerate(copies):
            src = in_refs[ii] if sfn is None else in_refs[ii].at[sfn(*me)]
            cp = pltpu.make_async_remote_copy(
                src_ref=src, dst_ref=out_refs[oi].at[dfn(*me)], send_sem=send_sems.at[n], recv_sem=recv_sems.at[n],
                device_id=_flip(bits), device_id_type=MESH)
            cp.start()
            sends.append(cp)
        for n, (ii, sfn, bits, oi, dfn) in enumerate(copies):
            peer = _flip(bits)
            src = in_refs[ii] if sfn is None else in_refs[ii].at[sfn(*me)]
            pltpu.make_async_remote_copy(
                src_ref=src, dst_ref=out_refs[oi].at[dfn(*peer)], send_sem=send_sems.at[n], recv_sem=recv_sems.at[n],
                device_id=peer, device_id_type=MESH).wait_recv()
        for cp in sends:
            cp.wait_send()
        for cp in locs:
            cp.wait()

    any_spec = pl.BlockSpec(memory_space=pl.ANY)
    return _pallas(
        body, name=name, in_specs=[any_spec] * n_in, out_specs=[any_spec] * n_out,
        out_shape=[jax.ShapeDtypeStruct(s, d) for s, d in out_shapes],
        scratch_shapes=[pltpu.SemaphoreType.DMA((max(n_cp, 1),)), pltpu.SemaphoreType.DMA((max(n_cp, 1),)),
                        pltpu.SemaphoreType.DMA((max(n_loc, 1),))],
        input_output_aliases=aliases,
        compiler_params=pltpu.CompilerParams(has_side_effects=True),
    )(*ins)


ICI_FLIPS = ((1, 0, 0), (0, 1, 0), (1, 1, 0))
D2D_FLIP = (0, 0, 1)


def _slot_of(x, y, c):
    return 4 * x + 2 * y + c


def _all_gather(shards, *, name):
    n = len(shards)
    outs = [((N_DEV,) + s.shape, s.dtype) for s in shards]
    copies = [(t, None, bits, t, _slot_of) for t in range(n) for bits in ICI_FLIPS]
    local = [(t, t, _slot_of) for t in range(n)]
    bufs = _exchange(shards, outs, copies, local, name=name + "_ici")
    copies2 = [(t, (lambda x, y, c, q=q: 2 * q + c), D2D_FLIP, t, (lambda x, y, c, q=q: 2 * q + c))
               for t in range(n) for q in range(4)]
    return _exchange(bufs, outs, copies2, [], aliases={t: t for t in range(n)}, name=name + "_d2d")


ALL_FLIPS = tuple((m >> 2 & 1, m >> 1 & 1, m & 1) for m in range(1, N_DEV))
HBM_SPEC = pl.BlockSpec(memory_space=pltpu.HBM)
SEM_SPEC = pl.BlockSpec(memory_space=pltpu.SEMAPHORE)
SIDE_EFFECT = pltpu.SideEffectType.DATAFLOW_SIDE_EFFECTING


def _copy_desc(copies, n, src_refs, land_refs, send_sems, recv_sems, sender):
    si, sfn, bits, li, dfn = copies[n]
    src = src_refs[si] if sfn is None else src_refs[si].at[sfn(*sender)]
    return pltpu.make_async_remote_copy(
        src_ref=src, dst_ref=land_refs[li].at[dfn(*sender)], send_sem=send_sems.at[n], recv_sem=recv_sems.at[n],
        device_id=_flip(bits), device_id_type=MESH)


def _start_copies(srcs, lands, copies, *, name):
    n_s, n_l, n_c = len(srcs), len(lands), len(copies)

    def body(*refs):
        src_refs = refs[:n_s]
        send_sems, recv_sems = refs[n_s], refs[n_s + 1]
        land_refs = refs[2 * n_s + 2:2 * n_s + 2 + n_l]
        token = refs[-1]
        me = _coords()
        for n in range(n_c):
            _copy_desc(copies, n, src_refs, land_refs, send_sems, recv_sems, me).start()
        token[...] = jnp.zeros_like(token)

    res = _pallas(
        body, name=name,
        out_shape=(pltpu.SemaphoreType.DMA((n_c,)), pltpu.SemaphoreType.DMA((n_c,)),
                   *[pltpu.HBM(s.shape, s.dtype) for s in srcs], *[pltpu.HBM(shape, dtype) for shape, dtype in lands],
                   jax.ShapeDtypeStruct((8, LANES), F32)),
        in_specs=[HBM_SPEC] * n_s,
        out_specs=(SEM_SPEC, SEM_SPEC, *[HBM_SPEC] * (n_s + n_l), pl.BlockSpec(memory_space=pltpu.VMEM)),
        input_output_aliases={i: 2 + i for i in range(n_s)},
        compiler_params=pltpu.CompilerParams(has_side_effects=SIDE_EFFECT),
    )(*[pltpu.with_memory_space_constraint(s, pltpu.HBM) for s in srcs])
    return res[0], res[1], list(res[2:2 + n_s]), list(res[2 + n_s:2 + n_s + n_l]), res[-1]


def _wait_copies(send_sems, recv_sems, srcs, lands, copies, which, after, *, name):
    src_ids, land_ids = sorted(srcs), sorted(lands)
    n_s, n_l = len(src_ids), len(land_ids)
    srcs, lands = [srcs[i] for i in src_ids], [lands[i] for i in land_ids]

    def body(*refs):
        src_refs, land_refs = dict(zip(src_ids, refs[:n_s])), dict(zip(land_ids, refs[n_s:n_s + n_l]))
        s_sems, r_sems = refs[n_s + n_l], refs[n_s + n_l + 1]
        me = _coords()
        for n in which:
            cp = _copy_desc(copies, n, src_refs, land_refs, s_sems, r_sems, me)
            cp.wait_send()
            _copy_desc(copies, n, src_refs, land_refs, s_sems, r_sems, _flip(copies[n][2])).wait_recv()

    res = _pallas(
        body, name=name,
        out_shape=tuple(pltpu.HBM(b.shape, b.dtype) for b in list(srcs) + list(lands)),
        in_specs=[HBM_SPEC] * (n_s + n_l) + [SEM_SPEC, SEM_SPEC, pl.BlockSpec(memory_space=pl.ANY)],
        out_specs=tuple([HBM_SPEC] * (n_s + n_l)),
        input_output_aliases={i: i for i in range(n_s + n_l)},
        compiler_params=pltpu.CompilerParams(has_side_effects=SIDE_EFFECT),
    )(*srcs, *lands, send_sems, recv_sems, after)
    return dict(zip(src_ids, res[:n_s])), dict(zip(land_ids, res[n_s:]))


def _adam_math(w, g, m, v):
    m = ADAM_B1 * m + (1.0 - ADAM_B1) * g
    v = ADAM_B2 * v + (1.0 - ADAM_B2) * (g * g)
    m_hat = m / (1.0 - ADAM_B1 ** ADAM_STEP)
    v_hat = v / (1.0 - ADAM_B2 ** ADAM_STEP)
    delta = -ADAM_LR * (m_hat / (jnp.sqrt(v_hat) + ADAM_EPS) + ADAM_WD * w)
    return delta, m, v


def _shard_adamw(own, landing, w, m, v, *, name):
    r, cdim = w.shape
    tr = _pick(r, (256, 128, 64, 32, 16))

    def body(g_ref, l_ref, w_ref, m_ref, v_ref, g_out, d_out, m_out, v_out):
        g = g_ref[...]
        for n in range(N_DEV - 1):
            g = g + l_ref[n].astype(F32)
        d, mn, vn = _adam_math(w_ref[...], g, m_ref[...], v_ref[...])
        g_out[...] = g
        d_out[...] = d
        m_out[...] = mn
        v_out[...] = vn

    blk = pl.BlockSpec((tr, cdim), lambda i: (i, 0))
    return _pallas(
        body, name=name, grid=(r // tr,),
        in_specs=[blk, pl.BlockSpec((N_DEV - 1, tr, cdim), lambda i: (0, i, 0)), blk, blk, blk], out_specs=[blk] * 4,
        out_shape=[jax.ShapeDtypeStruct((r, cdim), F32)] * 4,
        compiler_params=_cparams("parallel"),
    )(own, landing, w, m, v)


def _small_adamw(slots, w, m, v, *, name):
    r, cdim = w.shape

    def body(s_ref, w_ref, m_ref, v_ref, g_out, d_out, m_out, v_out):
        g = s_ref[0]
        for n in range(1, N_DEV):
            g = g + s_ref[n]
        d, mn, vn = _adam_math(w_ref[...], g, m_ref[...], v_ref[...])
        g_out[...] = g
        d_out[...] = d
        m_out[...] = mn
        v_out[...] = vn

    tr = _pick(r, (256, 128, 64, 32, 16, 8))
    blk = pl.BlockSpec((tr, cdim), lambda i: (i, 0))
    return _pallas(
        body, name=name, grid=(r // tr,),
        in_specs=[pl.BlockSpec((N_DEV, tr, cdim), lambda i: (0, i, 0)), blk, blk, blk], out_specs=[blk] * 4,
        out_shape=[jax.ShapeDtypeStruct((r, cdim), F32)] * 4,
        compiler_params=_cparams("parallel"),
    )(slots, w, m, v)


def _plain_adamw(g, w, m, v, *, name):
    def body(g_ref, w_ref, m_ref, v_ref, d_out, m_out, v_out):
        d, mn, vn = _adam_math(w_ref[...], g_ref[...], m_ref[...], v_ref[...])
        d_out[...] = d
        m_out[...] = mn
        v_out[...] = vn

    spec = pl.BlockSpec(g.shape, lambda: (0,) * g.ndim)
    return _pallas(body, name=name, in_specs=[spec] * 4, out_specs=[spec] * 3,
                   out_shape=[jax.ShapeDtypeStruct(g.shape, F32)] * 3)(g, w, m, v)


def _cast_bf16(x, *, name):
    r, cdim = x.shape
    tr = _pick(r, (256, 128, 64, 32, 16))

    def body(x_ref, o_ref):
        o_ref[...] = x_ref[...].astype(BF16)

    return _pallas(body, name=name, grid=(r // tr,), in_specs=[pl.BlockSpec((tr, cdim), lambda i: (i, 0))],
                   out_specs=pl.BlockSpec((tr, cdim), lambda i: (i, 0)), out_shape=jax.ShapeDtypeStruct(x.shape, BF16),
                   compiler_params=_cparams("parallel"))(x)


SMALL_NAMES = ("norm_mix", "norm_ffn", "s5_a_re", "s5_a_im", "s5_log_dt", "s5_b_re", "s5_b_im", "s5_c_re", "s5_c_im",
               "norm_kv", "norm_final")


def _pack_rows(arrs):
    rows = []
    for a in arrs:
        flat = a.reshape(-1)
        pad = (-flat.shape[0]) % (8 * LANES)
        rows.append(jnp.pad(flat, (0, pad)).reshape(-1, LANES))
    return jnp.concatenate(rows, axis=0)


def _unpack_rows(packed, like):
    out, r0 = [], 0
    for a in like:
        n = math.prod(a.shape)
        nr = (n + 8 * LANES - 1) // (8 * LANES) * 8
        out.append(packed[r0:r0 + nr].reshape(-1)[:n].reshape(a.shape))
        r0 += nr
    return out


def kernel(x, meta_tokens, norm_mix, norm_ffn, s5_a_re, s5_a_im, s5_log_dt, s5_b_re, s5_b_im, s5_c_re, s5_c_im, s5_d, s5_w_glu, norm_kv, w_kv, w_q, w_o, w_ffn_in, w_ffn_out, norm_final, loss_target, m_meta_tokens, m_norm_mix, m_norm_ffn, m_s5_a_re, m_s5_a_im, m_s5_log_dt, m_s5_b_re, m_s5_b_im, m_s5_c_re, m_s5_c_im, m_s5_d, m_s5_w_glu, m_norm_kv, m_w_kv, m_w_q, m_w_o, m_w_ffn_in, m_w_ffn_out, m_norm_final, v_meta_tokens, v_norm_mix, v_norm_ffn, v_s5_a_re, v_s5_a_im, v_s5_log_dt, v_s5_b_re, v_s5_b_im, v_s5_c_re, v_s5_c_im, v_s5_d, v_s5_w_glu, v_norm_kv, v_w_kv, v_w_q, v_w_o, v_w_ffn_in, v_w_ffn_out, v_norm_final):
    args = dict(locals())
    seq = x.shape[1]
    n_valid = N_META + seq
    Lp = (n_valid + ATTN_BLOCK - 1) // ATTN_BLOCK * ATTN_BLOCK
    dm = D_MODEL
    my_slot = _slot_of(*_coords())

    n_ffn = w_ffn_in.shape[0]
    groups = {"s5": ["s5_w_glu"], "ffn0": ["w_ffn_in0", "w_ffn_out0"], "attn": ["w_kv", "w_q", "w_o"],
              "ffn1": ["w_ffn_in1", "w_ffn_out1"]}
    shards = {"s5_w_glu": s5_w_glu[0], "w_kv": w_kv, "w_q": w_q[0], "w_o": w_o[0]}
    for l in range(n_ffn):
        shards[f"w_ffn_in{l}"], shards[f"w_ffn_out{l}"] = w_ffn_in[l], w_ffn_out[l]
    by_columns = {"s5_w_glu", "w_kv", "w_ffn_in0", "w_ffn_in1"}
    names = [n for g in groups.values() for n in g]

    def gathered_full(n, g8):
        if n in by_columns:
            return jnp.swapaxes(g8, 0, 1).reshape(g8.shape[1], -1)
        return g8.reshape(-1, g8.shape[2])

    col_shard = jnp.concatenate([meta_tokens, s5_d, jnp.zeros((7, LANES), F32)], axis=0)
    ag_srcs = [col_shard] + [_cast_bf16(shards[n], name="cast_" + n) for n in names]
    ag_copies = [(t, None, bits, t, _slot_of) for t in range(len(ag_srcs)) for bits in ALL_FLIPS]
    ag_send, ag_recv, ag_srcs, ag_lands, ag_token = _start_copies(
        ag_srcs, [((N_DEV,) + s.shape, s.dtype) for s in ag_srcs], ag_copies, name="ag_start")

    def gather_wait(tag, ids, after):
        which = [t * len(ALL_FLIPS) + r for t in ids for r in range(len(ALL_FLIPS))]
        srcs, lands = _wait_copies(ag_send, ag_recv, {t: ag_srcs[t] for t in ids}, {t: ag_lands[t] for t in ids},
                                   ag_copies, which, after, name="ag_wait_" + tag)
        return {t: lax.dynamic_update_index_in_dim(lands[t], srcs[t], my_slot, 0) for t in ids}

    def get_weights(group, after):
        ids = [1 + names.index(n) for n in groups[group]]
        full = {names[t - 1]: gathered_full(names[t - 1], g8) for t, g8 in gather_wait(group, ids, after).items()}
        if group == "attn":
            w_kv_full = full.pop("w_kv")
            full["w_k"], full["w_v"] = w_kv_full[:, :dm], w_kv_full[:, dm:]
        return {n.rstrip("01") if n.startswith("w_ffn") else n: w for n, w in full.items()}

    gcol = gather_wait("cols", [0], ag_token)[0]
    small = {n: args[n] for n in SMALL_NAMES}
    small["meta_tokens"] = jnp.swapaxes(gcol[:, :N_META], 0, 1).reshape(N_META, dm)
    small["s5_d"] = gcol[:, N_META].reshape(1, dm)

    scatters = {}

    def put_grads(group, grads):
        if group == "attn":
            grads = {"w_kv": jnp.concatenate([grads.pop("w_k"), grads.pop("w_v")], axis=1), **grads}
        own, send = {}, []
        for n in groups[group]:
            g = grads[n.rstrip("01") if n.startswith("w_ffn") else n]
            r, c = shards[n].shape
            if n in by_columns:
                own[n] = lax.dynamic_slice_in_dim(g, my_slot * c, c, axis=1)
                send.append(jnp.swapaxes(g.reshape(r, N_DEV, c), 0, 1).astype(BF16))
            else:
                own[n] = lax.dynamic_slice_in_dim(g, my_slot * r, r, axis=0)
                send.append(g.reshape(N_DEV, r, c).astype(BF16))
        copies = [(t, (lambda x, y, c, b=bits: _slot_of(x ^ b[0], y ^ b[1], c ^ b[2])), bits, t, (lambda x, y, c, k=k: k))
                  for t in range(len(send)) for k, bits in enumerate(ALL_FLIPS)]
        lands = [((len(ALL_FLIPS),) + s.shape[1:], BF16) for s in send]
        s_sem, r_sem, srcs, lands, token = _start_copies(send, lands, copies, name="rs_start_" + group)
        scatters[group] = (s_sem, r_sem, srcs, lands, copies, own)
        return token[0, 0]

    x_pad = jnp.pad(x[0], ((N_META, Lp - n_valid), (0, 0)))
    t_pad = jnp.pad(loss_target[0], ((N_META, Lp - n_valid), (0, 0)))
    loss_local, g_xpad, g_small = _forward_backward(x_pad, t_pad, n_valid, small, get_weights, put_grads)
    loss = lax.psum(loss_local[0, 0], ("x", "y", "c"))
    grad_x = g_xpad[N_META:n_valid][None]

    small_list = [g_small[n] for n in SMALL_NAMES] + [g_small["meta_tokens"], g_small["s5_d"]]
    small_bufs = _all_gather([_pack_rows(small_list)], name="small_ag")

    out, per_tensor = {}, {}
    for group, (s_sem, r_sem, srcs, lands, copies, own) in scatters.items():
        ids = list(range(len(srcs)))
        _, landed = _wait_copies(s_sem, r_sem, dict(zip(ids, srcs)), dict(zip(ids, lands)), copies, list(range(len(copies))),
                                 small_bufs[0], name="rs_wait_" + group)
        for t, n in enumerate(groups[group]):
            base, layer = (n[:-1], int(n[-1])) if n.startswith("w_ffn") else (n, None)
            pick = (lambda a: a[layer]) if layer is not None else (lambda a: a.reshape(shards[n].shape))
            per_tensor[n] = _shard_adamw(own[n], landed[t], shards[n], pick(args["m_" + base]), pick(args["v_" + base]),
                                         name="adamw_" + n)
    for base in ("s5_w_glu", "w_kv", "w_q", "w_o"):
        out[base] = tuple(a.reshape(args[base].shape) for a in per_tensor[base])
    for base in ("w_ffn_in", "w_ffn_out"):
        out[base] = tuple(jnp.stack([per_tensor[f"{base}{l}"][k] for l in range(n_ffn)]) for k in range(4))

    zeros_tail = [jnp.zeros_like(g_small["meta_tokens"]), jnp.zeros_like(g_small["s5_d"])]
    pw = _pack_rows([args[n] for n in SMALL_NAMES] + zeros_tail)
    pm = _pack_rows([args["m_" + n] for n in SMALL_NAMES] + zeros_tail)
    pv = _pack_rows([args["v_" + n] for n in SMALL_NAMES] + zeros_tail)
    sg, sd, sm, sv = _small_adamw(small_bufs[0], pw, pm, pv, name="adamw_small")
    ug, ud, um, uv = (_unpack_rows(a, small_list) for a in (sg, sd, sm, sv))
    for k, n in enumerate(SMALL_NAMES):
        out[n] = (ug[k], ud[k], um[k], uv[k])
    g_meta = lax.dynamic_slice_in_dim(ug[-2], my_slot * LANES, LANES, axis=1)
    g_d = lax.dynamic_slice_in_dim(ug[-1].reshape(1, dm), my_slot * LANES, LANES, axis=1)
    pad7 = jnp.zeros((7, LANES), F32)
    gc = jnp.concatenate([g_meta, g_d, pad7], axis=0)
    wc = jnp.concatenate([meta_tokens, s5_d, pad7], axis=0)
    mc = jnp.concatenate([m_meta_tokens, m_s5_d, pad7], axis=0)
    vc = jnp.concatenate([v_meta_tokens, v_s5_d, pad7], axis=0)
    dc, mcn, vcn = _plain_adamw(gc, wc, mc, vc, name="adamw_cols")
    out["meta_tokens"] = (g_meta, dc[:N_META], mcn[:N_META], vcn[:N_META])
    out["s5_d"] = (g_d, dc[N_META:N_META + 1], mcn[N_META:N_META + 1], vcn[N_META:N_META + 1])

    order = ["meta_tokens", "norm_mix", "norm_ffn", "s5_a_re", "s5_a_im", "s5_log_dt", "s5_b_re", "s5_b_im", "s5_c_re",
             "s5_c_im", "s5_d", "s5_w_glu", "norm_kv", "w_kv", "w_q", "w_o", "w_ffn_in", "w_ffn_out", "norm_final"]
    res = [loss, grad_x]
    for k in range(4):
        res += [out[n][k] for n in order]
    return tuple(res)
```

```python
import functools
import math

import jax
import jax.numpy as jnp
from jax import lax
from jax.experimental import pallas as pl
from jax.experimental.pallas import tpu as pltpu

F32 = jnp.float32
BF16 = jnp.bfloat16

N_META = 16
D_MODEL = 1024
S5_GROUPS = 64
S5_GROUP = 16
S5_STATE = 64
N_HEADS = 16
HEAD_DIM = 64
RMS_EPS = 1e-6
ADAM_LR, ADAM_B1, ADAM_B2, ADAM_EPS, ADAM_WD, ADAM_STEP = 0.001, 0.9, 0.999, 1e-08, 0.01, 10

LANES = 128
ATTN_BLOCK = 128
GROUPS_PER_BLOCK = 8
N_DEV = 8
MESH = pl.DeviceIdType.MESH
VMEM_LIMIT = 56 * 1024 * 1024


def _pallas(body, **kw):
    return pl.pallas_call(body, **kw)


def _cparams(*sem):
    return pltpu.CompilerParams(dimension_semantics=sem, vmem_limit_bytes=VMEM_LIMIT)


def _pick(n, prefs):
    for p in prefs:
        if n % p == 0:
            return p
    return n


def _dot(a, b, ca=1, cb=0):
    return lax.dot_general(a, b, (((ca,), (cb,)), ((), ())), preferred_element_type=F32)


MATMUL_VMEM_BUDGET = 36 * 1024 * 1024
MATMUL_TILES = (1408, 1024, 512, 384, 256, 128)


def _matmul_tiles(M, N, K, a_bytes, b_bytes, tile_bytes, n_acc, offsets):
    best = None
    for tm in (t for t in MATMUL_TILES if M % t == 0):
        for tn in (t for t in MATMUL_TILES if N % t == 0):
            for tk in (t for t in MATMUL_TILES if K % t == 0):
                if any(off % (tn if along_n else tk) for off, along_n in offsets):
                    continue
                vmem = 2 * (tm * tk * a_bytes + tk * tn * b_bytes + tm * tn * tile_bytes) + n_acc * tm * tn * 4
                if vmem > MATMUL_VMEM_BUDGET:
                    continue
                traffic = M * K * a_bytes * (N // tn) + K * N * b_bytes * (M // tm) + M * N * tile_bytes
                traffic += (K // tk - 1) * n_acc * M * N * 8 // 3
                key = (traffic, -tk, -tm)
                if best is None or key < best[0]:
                    best = (key, (tm, tn, tk))
    return best[1]


def _gmm(a_list, b_list, accs, *, n, extras=(), epilogue=None, out_dtypes=(F32,), name):
    a0, ta0 = a_list[0]
    K, M = a0.shape if ta0 else a0.shape[::-1]
    N = n
    b_keys = sorted({(bi, off) for terms in accs for _, bi, off in terms})
    a_bytes = sum(a.dtype.itemsize for a, _ in a_list)
    b_bytes = sum(b_list[bi][0].dtype.itemsize for bi, _ in b_keys)
    tile_bytes = sum(e.dtype.itemsize for e in extras) + sum(jnp.dtype(d).itemsize for d in out_dtypes)
    offsets = [(off, not b_list[bi][1]) for bi, off in b_keys]
    tm, tn, tk = _matmul_tiles(M, N, K, a_bytes, b_bytes, tile_bytes, len(accs), offsets)
    nk = K // tk
    n_a, n_b, n_e, n_o, n_acc = len(a_list), len(b_keys), len(extras), len(out_dtypes), len(accs)
    if epilogue is None:
        epilogue = lambda *vals: vals

    def body(*refs):
        a_refs, b_refs = refs[:n_a], refs[n_a:n_a + n_b]
        e_refs = refs[n_a + n_b:n_a + n_b + n_e]
        o_refs = refs[n_a + n_b + n_e:n_a + n_b + n_e + n_o]
        acc_refs = refs[n_a + n_b + n_e + n_o:]
        a_vals = [r[...].astype(BF16) for r in a_refs]
        b_vals = {key: r[...].astype(BF16) for key, r in zip(b_keys, b_refs)}
        parts = []
        for terms in accs:
            p = None
            for ai, bi, off in terms:
                d = _dot(a_vals[ai], b_vals[(bi, off)], 0 if a_list[ai][1] else 1, 1 if b_list[bi][1] else 0)
                p = d if p is None else p + d
            parts.append(p)

        def finish(vals):
            outs = epilogue(*vals, *[r[...] for r in e_refs])
            for r, val in zip(o_refs, outs):
                r[...] = val.astype(r.dtype)

        if nk == 1:
            finish(parts)
        else:
            k = pl.program_id(2)

            @pl.when(k == 0)
            def _():
                for r, p in zip(acc_refs, parts):
                    r[...] = p

            @pl.when(k > 0)
            def _():
                for r, p in zip(acc_refs, parts):
                    r[...] += p

            @pl.when(k == nk - 1)
            def _():
                finish([r[...] for r in acc_refs])

    in_specs = [pl.BlockSpec((tk, tm), lambda i, j, k: (k, i)) if ta else pl.BlockSpec((tm, tk), lambda i, j, k: (i, k))
                for _, ta in a_list]
    for bi, off in b_keys:
        if b_list[bi][1]:
            in_specs.append(pl.BlockSpec((tn, tk), lambda i, j, k, o=off // tk: (j, k + o)))
        else:
            in_specs.append(pl.BlockSpec((tk, tn), lambda i, j, k, o=off // tn: (k, j + o)))
    tile = pl.BlockSpec((tm, tn), lambda i, j, k: (i, j))
    in_specs += [tile] * n_e
    return _pallas(
        body, name=name, grid=(M // tm, N // tn, nk), in_specs=in_specs, out_specs=[tile] * n_o,
        out_shape=[jax.ShapeDtypeStruct((M, N), d) for d in out_dtypes],
        scratch_shapes=[pltpu.VMEM((tm, tn), F32)] * (n_acc if nk > 1 else 0),
        compiler_params=_cparams("parallel", "parallel", "arbitrary"),
    )(*[a for a, _ in a_list], *[b_list[bi][0] for bi, _ in b_keys], *extras)


def _rowwise(fn, row_ins, full_ins, row_outs, acc_outs, *, tm, name):
    L = row_ins[0].shape[0]
    n_row, n_full, n_ro = len(row_ins), len(full_ins), len(row_outs)

    def body(*refs):
        ins = [r[...] for r in refs[:n_row + n_full]]
        outs = refs[n_row + n_full:]
        res = fn(*ins)
        for r, val in zip(outs[:n_ro], res[:n_ro]):
            r[...] = val.astype(r.dtype)
        if acc_outs:
            first = pl.program_id(0) == 0
            for r, val in zip(outs[n_ro:], res[n_ro:]):
                @pl.when(first)
                def _(r=r, val=val):
                    r[...] = val

                @pl.when(jnp.logical_not(first))
                def _(r=r, val=val):
                    r[...] += val

    in_specs = [pl.BlockSpec((tm, a.shape[1]), lambda i: (i, 0)) for a in row_ins]
    in_specs += [pl.BlockSpec(a.shape, lambda i, nd=a.ndim: (0,) * nd) for a in full_ins]
    out_specs = [pl.BlockSpec((tm, s[1]), lambda i: (i, 0)) for s, _ in row_outs]
    out_specs += [pl.BlockSpec(s, lambda i, nd=len(s): (0,) * nd) for s, _ in acc_outs]
    out_shape = [jax.ShapeDtypeStruct(s, d) for s, d in list(row_outs) + list(acc_outs)]
    return _pallas(
        body, name=name, grid=(L // tm,), in_specs=in_specs, out_specs=out_specs, out_shape=out_shape,
        compiler_params=_cparams("arbitrary" if acc_outs else "parallel"),
    )(*row_ins, *full_ins)


def _colsum(x):
    return jnp.sum(x, axis=0, keepdims=True)


def _rmsnorm_fwd(x, g, *, out_dtype, name):
    L, d = x.shape

    def f(xb, gb):
        r = lax.rsqrt(jnp.mean(xb * xb, axis=-1, keepdims=True) + RMS_EPS)
        return (xb * r * gb,)

    return _rowwise(f, [x], [g.reshape(1, d)], [((L, d), out_dtype)], [], tm=_pick(L, (384, 128)), name=name)[0]


def _rmsnorm_bwd(x, g, dy, resid, *, name):
    L, d = x.shape

    def f(xb, dyb, *rest):
        gb = rest[-1]
        r = lax.rsqrt(jnp.mean(xb * xb, axis=-1, keepdims=True) + RMS_EPS)
        xh = xb * r
        dxh = dyb * gb
        dx = r * (dxh - xh * jnp.mean(dxh * xh, axis=-1, keepdims=True))
        if resid is not None:
            dx = dx + rest[0]
        return dx, _colsum(dyb * xh)

    rows = [x, dy] + ([resid] if resid is not None else [])
    dx, dg = _rowwise(f, rows, [g.reshape(1, d)], [((L, d), F32)], [((1, d), F32)], tm=_pick(L, (384, 128)), name=name)
    return dx, dg.reshape(g.shape)


def _sigmoid(x):
    return 1.0 / (1.0 + jnp.exp(-x))


GELU_K = math.sqrt(2.0 / math.pi)
GELU_C = 0.044715


def _bdmm(a_list, w_rows, *, out_dtype=F32, name):
    L = a_list[0].shape[0]
    nb, ka, kb = w_rows[0][0].shape
    tm = _pick(L, (1408, 384, 128))
    n_a, n_o = len(a_list), len(w_rows)
    ws = [w for row in w_rows for w in row]

    def body(*refs):
        a_vals = [r[...].astype(BF16) for r in refs[:n_a]]
        w_refs = refs[n_a:n_a + n_a * n_o]
        outs = refs[n_a + n_a * n_o:]
        for o in range(n_o):
            acc = None
            for i in range(n_a):
                p = _dot(a_vals[i], w_refs[o * n_a + i][...].astype(BF16))
                acc = p if acc is None else acc + p
            outs[o][...] = acc.astype(out_dtype)

    return _pallas(
        body, name=name, grid=(L // tm, nb),
        in_specs=[pl.BlockSpec((tm, ka), lambda i, j: (i, j)) for _ in a_list]
        + [pl.BlockSpec((None, ka, kb), lambda i, j: (j, 0, 0)) for _ in ws],
        out_specs=[pl.BlockSpec((tm, kb), lambda i, j: (i, j)) for _ in range(n_o)],
        out_shape=[jax.ShapeDtypeStruct((L, nb * kb), out_dtype) for _ in range(n_o)],
        compiler_params=_cparams("parallel", "parallel"),
    )(*a_list, *ws)


def _bdmm_tn(a_list, g_list, nb, *, name):
    n_a, n_g = len(a_list), len(g_list)
    n_o = max(n_a, n_g)
    L = a_list[0].shape[0]
    ka, kb = a_list[0].shape[1] // nb, g_list[0].shape[1] // nb
    tm = _pick(L, (1408, 384, 128))

    def body(*refs):
        i = pl.program_id(1)
        a_vals = [r[...].astype(BF16) for r in refs[:n_a]]
        g_vals = [r[...].astype(BF16) for r in refs[n_a:n_a + n_g]]
        outs = refs[n_a + n_g:]
        parts = [_dot(a_vals[k % n_a], g_vals[k % n_g], 0, 0) for k in range(n_o)]

        @pl.when(i == 0)
        def _():
            for r, p in zip(outs, parts):
                r[...] = p

        @pl.when(i > 0)
        def _():
            for r, p in zip(outs, parts):
                r[...] += p

    return _pallas(
        body, name=name, grid=(nb, L // tm),
        in_specs=[pl.BlockSpec((tm, ka), lambda j, i: (i, j))] * n_a + [pl.BlockSpec((tm, kb), lambda j, i: (i, j))] * n_g,
        out_specs=[pl.BlockSpec((None, ka, kb), lambda j, i: (j, 0, 0))] * n_o,
        out_shape=[jax.ShapeDtypeStruct((nb, ka, kb), F32)] * n_o,
        compiler_params=_cparams("parallel", "arbitrary"),
    )(*a_list, *g_list)


SCAN_ROWS = 128
STATE_DTYPE = BF16


def _scan_fwd(br, bi, ar, ai, *, name):
    L, S, _ = br.shape
    tb = SCAN_ROWS

    def body(br_ref, bi_ref, ar_ref, ai_ref, xr_ref, xi_ref, carry_ref):
        @pl.when(pl.program_id(0) == 0)
        def _():
            carry_ref[...] = jnp.zeros_like(carry_ref)

        a_r, a_i = ar_ref[...], ai_ref[...]

        def step(t, c):
            xr, xi = c
            nr = a_r * xr - a_i * xi + br_ref[t].astype(F32)
            ni = a_r * xi + a_i * xr + bi_ref[t].astype(F32)
            xr_ref[t] = nr.astype(xr_ref.dtype)
            xi_ref[t] = ni.astype(xi_ref.dtype)
            return nr, ni

        xr, xi = lax.fori_loop(0, tb, step, (carry_ref[0], carry_ref[1]), unroll=8)
        carry_ref[0] = xr
        carry_ref[1] = xi

    blk = pl.BlockSpec((tb, S, LANES), lambda i: (i, 0, 0))
    par = pl.BlockSpec((S, LANES), lambda i: (0, 0))
    return _pallas(
        body, name=name, grid=(L // tb,), in_specs=[blk, blk, par, par], out_specs=[blk, blk],
        out_shape=[jax.ShapeDtypeStruct(br.shape, STATE_DTYPE)] * 2,
        scratch_shapes=[pltpu.VMEM((2, S, LANES), F32)],
        compiler_params=_cparams("arbitrary"),
    )(br, bi, ar, ai)


def _scan_bwd(gr_in, gi_in, ar, ai, xr, xi, *, name):
    L, S, _ = gr_in.shape
    tb = SCAN_ROWS
    nblk = L // tb

    def body(gr_ref, gi_ref, ar_ref, ai_ref, xr_ref, xi_ref, or_ref, oi_ref, dar_ref, dai_ref, carry_ref):
        @pl.when(pl.program_id(0) == 0)
        def _():
            carry_ref[...] = jnp.zeros_like(carry_ref)
            dar_ref[...] = jnp.zeros_like(dar_ref)
            dai_ref[...] = jnp.zeros_like(dai_ref)

        a_r, a_i = ar_ref[...], ai_ref[...]

        def step(s, c):
            gr, gi, dr, di = c
            t = tb - 1 - s
            x_r, x_i = xr_ref[t].astype(F32), xi_ref[t].astype(F32)
            dr = dr + gr * x_r + gi * x_i
            di = di + gi * x_r - gr * x_i
            nr = a_r * gr + a_i * gi + gr_ref[t].astype(F32)
            ni = a_r * gi - a_i * gr + gi_ref[t].astype(F32)
            or_ref[t] = nr.astype(or_ref.dtype)
            oi_ref[t] = ni.astype(oi_ref.dtype)
            return nr, ni, dr, di

        gr, gi, dr, di = lax.fori_loop(0, tb, step, (carry_ref[0], carry_ref[1], dar_ref[...], dai_ref[...]), unroll=8)
        carry_ref[0] = gr
        carry_ref[1] = gi
        dar_ref[...] = dr
        dai_ref[...] = di

    blk = pl.BlockSpec((tb, S, LANES), lambda i: (nblk - 1 - i, 0, 0))
    par = pl.BlockSpec((S, LANES), lambda i: (0, 0))
    return _pallas(
        body, name=name, grid=(nblk,), in_specs=[blk, blk, par, par, blk, blk], out_specs=[blk, blk, par, par],
        out_shape=[jax.ShapeDtypeStruct(gr_in.shape, STATE_DTYPE)] * 2 + [jax.ShapeDtypeStruct((S, LANES), F32)] * 2,
        scratch_shapes=[pltpu.VMEM((2, S, LANES), F32)],
        compiler_params=_cparams("arbitrary"),
    )(gr_in, gi_in, ar, ai, xr, xi)


def _make_ssm(name):
    nb = GROUPS_PER_BLOCK

    def fwd(u, wr, wi, cr, cin, lam_r, lam_i):
        L = u.shape[0]
        bur, bui = _bdmm([u], [[wr], [wi]], out_dtype=STATE_DTYPE, name=name + "_bu")
        S = bur.shape[1] // LANES
        xr, xi = _scan_fwd(bur.reshape(L, S, LANES), bui.reshape(L, S, LANES), lam_r, lam_i, name=name + "_scan")
        xr, xi = xr.reshape(L, S * LANES), xi.reshape(L, S * LANES)
        (y,) = _bdmm([xr, xi], [[cr, cin]], name=name + "_cx")
        return y, (u, wr, wi, cr, cin, lam_r, lam_i, xr, xi)

    def bwd(res, dy):
        u, wr, wi, cr, cin, lam_r, lam_i, xr, xi = res
        L = u.shape[0]
        S = xr.shape[1] // LANES
        tr = lambda w: jnp.swapaxes(w, 1, 2)
        gin_r, gin_i = _bdmm([dy], [[tr(cr)], [tr(cin)]], out_dtype=STATE_DTYPE, name=name + "_gin")
        gr, gi, dlr, dli = _scan_bwd(gin_r.reshape(L, S, LANES), gin_i.reshape(L, S, LANES), lam_r, lam_i,
                                     xr.reshape(L, S, LANES), xi.reshape(L, S, LANES), name=name + "_rscan")
        gr, gi = gr.reshape(L, S * LANES), gi.reshape(L, S * LANES)
        (du,) = _bdmm([gr, gi], [[tr(wr), tr(wi)]], name=name + "_du")
        dwr, dwi = _bdmm_tn([u], [gr, gi], nb, name=name + "_dw")
        dcr, dcin = _bdmm_tn([xr, xi], [dy], nb, name=name + "_dc")
        return du, dwr, dwi, dcr, dcin, dlr, dli

    return fwd, bwd


def _s5_discretize(a_re, a_im, log_dt, b_re, b_im, c_re, c_im):
    G, P, C, nb = S5_GROUPS, S5_STATE, S5_GROUP, GROUPS_PER_BLOCK
    dt = jnp.exp(log_dt)[:, None]
    mag = jnp.exp(dt * a_re)
    ang = dt * a_im
    abar_re = mag * jnp.cos(ang)
    abar_im = mag * jnp.sin(ang)
    den = a_re * a_re + a_im * a_im
    coef_re = ((abar_re - 1.0) * a_re + abar_im * a_im) / den
    coef_im = (abar_im * a_re - (abar_re - 1.0) * a_im) / den
    bbar_re = coef_re[..., None] * b_re - coef_im[..., None] * b_im
    bbar_im = coef_re[..., None] * b_im + coef_im[..., None] * b_re
    eye = jnp.eye(nb, dtype=F32)

    def blocks_in(bb):
        return jnp.einsum("jgpc,gh->jgchp", bb.reshape(G // nb, nb, P, C), eye).reshape(G // nb, nb * C, nb * P)

    def blocks_out(cc):
        return jnp.einsum("jgcp,gh->jgphc", cc.reshape(G // nb, nb, C, P), eye).reshape(G // nb, nb * P, nb * C)

    lam_r = abar_re.reshape(G * P // LANES, LANES)
    lam_i = abar_im.reshape(G * P // LANES, LANES)
    return blocks_in(bbar_re), blocks_in(bbar_im), blocks_out(c_re), blocks_out(-c_im), lam_r, lam_i


ATTN_Q_ROWS = 384


HEADS_PER_BLOCK = LANES // HEAD_DIM


def _head_masks(shape, axis):
    idx = lax.broadcasted_iota(jnp.int32, shape, axis) // HEAD_DIM
    return [idx == hh for hh in range(HEADS_PER_BLOCK)]


def _masked_bf16(x, masks):
    return [jnp.where(m, x, 0.0).astype(BF16) for m in masks]


def _tri_sum(x, tri2):
    hi = x.astype(BF16)
    lo = (x - hi.astype(F32)).astype(BF16)
    return _dot(jnp.concatenate([hi, lo], axis=1), tri2)


def _attn_weights(qh, kt, strict, after, c):
    z = _dot(qh, kt)
    lb = jnp.minimum(z, 0.0) - jnp.log(1.0 + jnp.exp(-jnp.abs(z)))
    l1m = lb - z
    if strict is not None:
        l1m = jnp.where(strict, l1m, 0.0)
    rem = _tri_sum(l1m, after)
    w = jnp.exp(lb + rem + c)
    if strict is not None:
        w = jnp.where(strict, w, 0.0)
    return lb, w, rem[:, 0:1] + l1m[:, 0:1]


def _tri(cmp):
    tk = ATTN_BLOCK
    tri = cmp(lax.broadcasted_iota(jnp.int32, (tk, tk), 0), lax.broadcasted_iota(jnp.int32, (tk, tk), 1)).astype(BF16)
    return jnp.concatenate([tri, tri], axis=0)


def _attn_tile_index(i, jb, nsub):
    return nsub * (i * (i + 1) // 2) + jb


def _attn_fwd(q, kt, v, *, name):
    L = q.shape[0]
    tk = ATTN_BLOCK
    tq = _pick(L, (ATTN_Q_ROWS, tk))
    nsub = tq // tk
    n_qt = L // tq
    n_tiles = _attn_tile_index(n_qt, 0, nsub)
    scale = 1.0 / math.sqrt(HEAD_DIM)

    def body(q_ref, kt_ref, v_ref, o_ref, w_hbm, s_hbm, acc_ref, w_stage, s_stage, sems):
        hp, i = pl.program_id(0), pl.program_id(1)
        after = _tri(lambda r, c: r > c)
        rowq = lax.broadcasted_iota(jnp.int32, (tq, tk), 0)
        colq = lax.broadcasted_iota(jnp.int32, (tq, tk), 1)
        qs = _masked_bf16(q_ref[...].astype(F32) * scale, _head_masks((tq, LANES), 1))
        vmasks = _head_masks((tk, LANES), 1)
        acc_ref[...] = jnp.zeros_like(acc_ref)

        def saves(buf, ii):
            cps = []
            for jd in range(nsub):
                idx = _attn_tile_index(i, ii * nsub + jd, nsub)
                cps.append(pltpu.make_async_copy(w_stage.at[buf, jd], w_hbm.at[hp, idx], sems.at[0, buf, jd]))
                cps.append(pltpu.make_async_copy(s_stage.at[buf, jd], s_hbm.at[hp, idx], sems.at[1, buf, jd]))
            return cps

        def tile(jb, buf, jd, strict, cs, first_row=0):
            r0 = pl.multiple_of(jb * tk, tk)
            rows = pl.ds(first_row, tq - first_row)
            kt_j = kt_ref[:, pl.ds(r0, tk)].astype(BF16)
            vs = _masked_bf16(v_ref[pl.ds(r0, tk), :], vmasks)
            out, acc = [], None
            for hh, (qh, vh, c) in enumerate(zip(qs, vs, cs)):
                lb, w, tot = _attn_weights(qh[first_row:], kt_j, None if strict is None else strict[first_row:], after,
                                           c[first_row:])
                w = w.astype(BF16)
                if first_row:
                    w_stage[buf, jd, hh, pl.ds(0, first_row), :] = jnp.zeros((first_row, tk), BF16)
                    s_stage[buf, jd, hh, pl.ds(0, first_row), :] = jnp.zeros((first_row, tk), BF16)
                w_stage[buf, jd, hh, rows, :] = w
                s_stage[buf, jd, hh, rows, :] = lb.astype(BF16)
                part = _dot(w, vh)
                acc = part if acc is None else acc + part
                out.append(jnp.concatenate([c[:first_row], c[first_row:] + tot], axis=0) if first_row else c + tot)
            acc_ref[rows, :] += acc
            return tuple(out)

        @pl.when(jnp.logical_or(hp > 0, i > 0))
        def _():
            for cp in saves(0, 0):
                cp.wait()

        if n_qt > 1:
            @pl.when(jnp.logical_or(i >= 2, jnp.logical_and(i == 0, hp > 0)))
            def _():
                for cp in saves(1, 0):
                    cp.wait()

        cs = tuple(jnp.zeros((tq, 1), F32) for _ in qs)
        for jd in reversed(range(nsub)):
            cs = tile(i * nsub + jd, 0, jd, (colq + jd * tk) < rowq, cs, first_row=jd * tk)
        for cp in saves(0, i):
            cp.start()

        def step(ii, cs):
            buf = (ii + 1) % 2

            @pl.when(ii >= 1)
            def _():
                for cp in saves(buf, 0):
                    cp.wait()

            for jd in reversed(range(nsub)):
                cs = tile((i - 1 - ii) * nsub + jd, buf, jd, None, cs)
            for cp in saves(buf, i - 1 - ii):
                cp.start()
            return cs

        lax.fori_loop(0, i, step, cs)
        o_ref[...] = acc_ref[...].astype(o_ref.dtype)

        @pl.when(jnp.logical_and(hp == nhb - 1, i == n_qt - 1))
        def _():
            for cp in saves(0, 0):
                cp.wait()
            if n_qt > 1:
                for cp in saves(1, 0):
                    cp.wait()

    blk = pl.BlockSpec((tq, LANES), lambda h, i: (i, h))
    whole = pl.BlockSpec((L, LANES), lambda h, i: (0, h))
    whole_t = pl.BlockSpec((LANES, L), lambda h, i: (h, 0))
    any_spec = pl.BlockSpec(memory_space=pl.ANY)
    nhb = N_HEADS // HEADS_PER_BLOCK
    saved = jax.ShapeDtypeStruct((nhb, n_tiles, HEADS_PER_BLOCK, tq, tk), BF16)
    return _pallas(
        body, name=name, grid=(nhb, n_qt), in_specs=[blk, whole_t, whole], out_specs=[blk, any_spec, any_spec],
        out_shape=[jax.ShapeDtypeStruct(q.shape, BF16), saved, saved],
        scratch_shapes=[pltpu.VMEM((tq, LANES), F32), pltpu.VMEM((2, nsub, HEADS_PER_BLOCK, tq, tk), BF16),
                        pltpu.VMEM((2, nsub, HEADS_PER_BLOCK, tq, tk), BF16), pltpu.SemaphoreType.DMA((2, 2, nsub))],
        compiler_params=_cparams("arbitrary", "arbitrary"),
    )(q, kt, v)


def _attn_bwd(q, k, vt, do, w_saved, s_saved, *, name):
    L = q.shape[0]
    tk = ATTN_BLOCK
    tq = _pick(L, (ATTN_Q_ROWS, tk))
    nsub = tq // tk
    scale = 1.0 / math.sqrt(HEAD_DIM)

    def body(q_ref, k_ref, vt_ref, do_ref, w_hbm, s_hbm, dq_ref, dkt_ref, dvt_ref, e_scr, sig_scr, dq_acc_ref, w_stage, sems):
        hp, i = pl.program_id(0), pl.program_id(1)

        @pl.when(i == 0)
        def _():
            dkt_ref[...] = jnp.zeros_like(dkt_ref)
            dvt_ref[...] = jnp.zeros_like(dvt_ref)

        before = _tri(lambda r, c: r < c)
        rowq = lax.broadcasted_iota(jnp.int32, (tq, tk), 0)
        colq = lax.broadcasted_iota(jnp.int32, (tq, tk), 1)
        qmasks = _head_masks((tq, LANES), 1)
        tmasks = _head_masks((LANES, tq), 0)
        kmasks = _head_masks((tk, LANES), 1)
        q_scaled = q_ref[...].astype(F32) * scale
        do32 = do_ref[...].astype(F32)
        dos = _masked_bf16(do32, qmasks)
        qts = _masked_bf16(q_scaled.T, tmasks)
        dots = _masked_bf16(do32.T, tmasks)
        dq_acc_ref[...] = jnp.zeros_like(dq_acc_ref)

        def loads(buf, ii, of=None):
            hp_, i_ = (hp, i) if of is None else of
            cps = []
            for jd in range(nsub):
                jb = ii * nsub + jd
                idx = _attn_tile_index(i_, jb, nsub)
                cps.append(pltpu.make_async_copy(w_hbm.at[hp_, idx], w_stage.at[buf, jd], sems.at[0, buf, jd]))
                cps.append(pltpu.make_async_copy(s_hbm.at[hp_, idx], sig_scr.at[jb], sems.at[1, buf, jd]))
            return cps

        def weigh(buf, ii):
            for jd in range(nsub):
                jb = ii * nsub + jd
                r0 = pl.multiple_of(jb * tk, tk)
                vt_j = vt_ref[:, pl.ds(r0, tk)].astype(BF16)
                acc = None
                for hh, (dob, dot_h) in enumerate(zip(dos, dots)):
                    w = w_stage[buf, jd, hh]
                    e_scr[hh, jb] = _dot(dob, vt_j) * w.astype(F32)
                    part = _dot(dot_h, w)
                    acc = part if acc is None else acc + part
                dvt_ref[:, pl.ds(r0, tk)] += acc

        def tile_right(jb, strict, cs, first_row=0):
            r0 = pl.multiple_of(jb * tk, tk)
            rows = pl.ds(first_row, tq - first_row)
            ks = _masked_bf16(k_ref[pl.ds(r0, tk), :], kmasks)
            out, dq_acc, dk_acc = [], None, None
            for hh, (kh, qt_h, c2) in enumerate(zip(ks, qts, cs)):
                e = e_scr[hh, jb, rows, :]
                sig = jnp.exp(sig_scr[jb, hh, rows, :].astype(F32))
                left = _dot(e.astype(BF16), before[:tk])
                dz = e - (e + left + c2[first_row:]) * sig
                if strict is not None:
                    dz = jnp.where(strict[first_row:], dz, 0.0)
                dz = dz.astype(BF16)
                p1, p2 = _dot(dz, kh), _dot(qt_h[:, first_row:], dz)
                dq_acc = p1 if dq_acc is None else dq_acc + p1
                dk_acc = p2 if dk_acc is None else dk_acc + p2
                tot = left[:, tk - 1:tk] + e[:, tk - 1:tk]
                out.append(jnp.concatenate([c2[:first_row], c2[first_row:] + tot], axis=0) if first_row else c2 + tot)
            dq_acc_ref[rows, :] += dq_acc
            dkt_ref[:, pl.ds(r0, tk)] += dk_acc
            return tuple(out)

        @pl.when(jnp.logical_and(hp == 0, i == 0))
        def _():
            for cp in loads(0, 0):
                cp.start()

        def weigh_step(ii, carry):
            buf = ii % 2
            for cp in loads(1 - buf, ii + 1):
                cp.start()
            for cp in loads(buf, ii):
                cp.wait()
            weigh(buf, ii)
            return carry

        lax.fori_loop(0, i, weigh_step, 0)
        for cp in loads(i % 2, i):
            cp.wait()
        weigh(i % 2, i)

        def right_step(ii, cs):
            for jd in range(nsub):
                cs = tile_right(ii * nsub + jd, None, cs)
            return cs

        cs = lax.fori_loop(0, i, right_step, tuple(jnp.zeros((tq, 1), F32) for _ in dos))
        for jd in range(nsub):
            cs = tile_right(i * nsub + jd, (colq + jd * tk) < rowq, cs, first_row=jd * tk)
        dq_ref[...] = (dq_acc_ref[...] * scale).astype(dq_ref.dtype)

        wraps = i == n_qt - 1

        @pl.when(jnp.logical_not(jnp.logical_and(wraps, hp == nhb - 1)))
        def _():
            nxt = (jnp.where(wraps, hp + 1, hp), jnp.where(wraps, 0, i + 1))
            for cp in loads(0, 0, of=nxt):
                cp.start()

    n_qt = L // tq
    nhb = N_HEADS // HEADS_PER_BLOCK
    blk = pl.BlockSpec((tq, LANES), lambda h, i: (i, h))
    whole = pl.BlockSpec((L, LANES), lambda h, i: (0, h))
    whole_t = pl.BlockSpec((LANES, L), lambda h, i: (h, 0))
    t_shape = jax.ShapeDtypeStruct((q.shape[1], L), F32)
    any_spec = pl.BlockSpec(memory_space=pl.ANY)
    return _pallas(
        body, name=name, grid=(nhb, L // tq), in_specs=[blk, whole, whole_t, blk, any_spec, any_spec],
        out_specs=[blk, whole_t, whole_t], out_shape=[jax.ShapeDtypeStruct(q.shape, BF16), t_shape, t_shape],
        scratch_shapes=[pltpu.VMEM((HEADS_PER_BLOCK, L // tk, tq, tk), F32), pltpu.VMEM((L // tk, HEADS_PER_BLOCK, tq, tk), BF16),
                        pltpu.VMEM((tq, LANES), F32), pltpu.VMEM((2, nsub, HEADS_PER_BLOCK, tq, tk), BF16),
                        pltpu.SemaphoreType.DMA((2, 2, nsub))],
        compiler_params=_cparams("arbitrary", "arbitrary"),
    )(q, k, vt, do, w_saved, s_saved)


def _make_loss_head(n_valid, name):
    def run(h, g, target):
        L, d = h.shape
        tm = _pick(L, (384, 128))

        def body(h_ref, t_ref, g_ref, loss_ref, dh_ref, dg_ref):
            i = pl.program_id(0)
            x = h_ref[...]
            gb = g_ref[...]
            rows = lax.broadcasted_iota(jnp.int32, (tm, 1), 0) + i * tm
            valid = jnp.logical_and(rows >= N_META, rows < n_valid)
            r = lax.rsqrt(jnp.mean(x * x, axis=-1, keepdims=True) + RMS_EPS)
            xh = x * r
            err = jnp.where(valid, xh * gb - t_ref[...], 0.0)
            dy = err * (1.0 / d)
            dxh = dy * gb
            dh_ref[...] = r * (dxh - xh * jnp.mean(dxh * xh, axis=-1, keepdims=True))
            part = 0.5 / d * jnp.sum(jnp.sum(err * err, axis=1, keepdims=True), axis=0, keepdims=True)
            dgp = _colsum(dy * xh)

            @pl.when(i == 0)
            def _():
                loss_ref[...] = part
                dg_ref[...] = dgp

            @pl.when(i > 0)
            def _():
                loss_ref[...] += part
                dg_ref[...] += dgp

        rowspec = pl.BlockSpec((tm, d), lambda i: (i, 0))
        return _pallas(
            body, name=name, grid=(L // tm,),
            in_specs=[rowspec, rowspec, pl.BlockSpec((1, d), lambda i: (0, 0))],
            out_specs=[pl.BlockSpec((1, 1), lambda i: (0, 0)), rowspec, pl.BlockSpec((1, d), lambda i: (0, 0))],
            out_shape=[jax.ShapeDtypeStruct((1, 1), F32), jax.ShapeDtypeStruct((L, d), F32), jax.ShapeDtypeStruct((1, d), F32)],
            compiler_params=_cparams("arbitrary"),
        )(h, target, g.reshape(1, d))

    return run


def _mm(a, b, *, ta=False, tb=False, out_dtype=F32, extras=(), epilogue=None, name):
    n = b.shape[0] if tb else b.shape[1]
    return _gmm([(a, ta)], [(b, tb)], [[(0, 0, 0)]], n=n, extras=extras, epilogue=epilogue, out_dtypes=(out_dtype,), name=name)[0]


def _add_epilogue(acc, resid):
    return (acc + resid,)


def _make_ffn_block(name):
    def fwd(h, gain, w_in, w_out):
        dff = w_out.shape[0]
        f = _rmsnorm_fwd(h, gain, out_dtype=BF16, name=name + "_norm")

        def swiglu(g, u):
            return g, u, g * _sigmoid(g) * u

        g, u, a = _gmm([(f, False)], [(w_in, False)], [[(0, 0, 0)], [(0, 0, dff)]], n=dff, epilogue=swiglu,
                       out_dtypes=(BF16, BF16, BF16), name=name + "_in")
        h2 = _mm(a, w_out, extras=[h], epilogue=_add_epilogue, name=name + "_out")
        return h2, (h, gain, f, g, u, a, w_in, w_out)

    def bwd(res, dh2, emit):
        h, gain, f, g, u, a, w_in, w_out = res
        dff, dm = w_out.shape

        def dswiglu(da, gb, ub):
            gb, ub = gb.astype(F32), ub.astype(F32)
            s = _sigmoid(gb)
            return da * ub * (s + gb * s * (1.0 - s)), da * gb * s

        dg, du = _gmm([(dh2, False)], [(w_out, True)], [[(0, 0, 0)]], n=dff, extras=[g, u], epilogue=dswiglu,
                      out_dtypes=(BF16, BF16), name=name + "_da")
        dw_out = _mm(a, dh2, ta=True, name=name + "_dwout")
        (df,) = _gmm([(dg, False), (du, False)], [(w_in, True)], [[(0, 0, 0), (1, 0, dff)]], n=dm, name=name + "_df")
        dw_in = jnp.concatenate([_mm(f, dg, ta=True, name=name + "_dwg"), _mm(f, du, ta=True, name=name + "_dwu")], axis=1)
        tok = emit({"w_ffn_in": dw_in, "w_ffn_out": dw_out})
        dh, dgain = _rmsnorm_bwd(h, gain + tok, df, dh2, name=name + "_dnorm")
        return dh, dgain

    return fwd, bwd


def _make_s5_out_block(name):
    def post(yb, ub, db):
        s = yb + db * ub
        return (0.5 * s * (1.0 + jnp.tanh(GELU_K * (s + GELU_C * s * s * s))),)

    def fwd(y, u, d, h, w_glu):
        L, dm = y.shape
        (z,) = _rowwise(post, [y, u], [d.reshape(1, dm)], [((L, dm), BF16)], [], tm=_pick(L, (384, 128)), name=name + "_gelu")

        def glu(val, gate, resid):
            return resid + val * _sigmoid(gate), val, gate

        h2, val, gate = _gmm([(z, False)], [(w_glu, False)], [[(0, 0, 0)], [(0, 0, dm)]], n=dm, extras=[h], epilogue=glu,
                             out_dtypes=(F32, BF16, BF16), name=name + "_glu")
        return h2, (y, u, d, z, val, gate, w_glu)

    def bwd(res, dh2, emit):
        y, u, d, z, val, gate, w_glu = res
        L, dm = y.shape

        def dglu(dh2b, valb, gateb):
            s = _sigmoid(gateb.astype(F32))
            return dh2b * s, dh2b * valb.astype(F32) * s * (1.0 - s)

        dval, dgate = _rowwise(dglu, [dh2, val, gate], [], [((L, dm), BF16)] * 2, [], tm=_pick(L, (384, 128)), name=name + "_dglu")
        (dz,) = _gmm([(dval, False), (dgate, False)], [(w_glu, True)], [[(0, 0, 0), (1, 0, dm)]], n=dm, name=name + "_dz")

        def dpost(yb, ub, dzb, db):
            s = yb + db * ub
            t = jnp.tanh(GELU_K * (s + GELU_C * s * s * s))
            ds = dzb * (0.5 * (1.0 + t) + 0.5 * s * (1.0 - t * t) * GELU_K * (1.0 + 3.0 * GELU_C * s * s))
            return ds, ds * db, _colsum(ds * ub)

        dw = jnp.concatenate([_mm(z, dval, ta=True, name=name + "_dwv"), _mm(z, dgate, ta=True, name=name + "_dwg")], axis=1)
        tok = emit({"s5_w_glu": dw})
        dy, du, dd = _rowwise(dpost, [y, u, dz], [(d + tok).reshape(1, dm)], [((L, dm), F32)] * 2, [((1, dm), F32)],
                              tm=_pick(L, (384, 128)), name=name + "_dgelu")
        return dy, du, dd.reshape(d.shape)

    return fwd, bwd


def _make_attn_layer(name):
    def fwd(h, g_kv, g_q, w_k, w_v, w_q, w_o):
        dm = h.shape[1]
        kvn = _rmsnorm_fwd(h, g_kv, out_dtype=BF16, name=name + "_kvnorm")
        qn = _rmsnorm_fwd(h, g_q, out_dtype=BF16, name=name + "_qnorm")
        k, v = _gmm([(kvn, False)], [(w_k, False), (w_v, False)], [[(0, 0, 0)], [(0, 1, 0)]], n=dm,
                    out_dtypes=(BF16, BF16), name=name + "_kv")
        q = _mm(qn, w_q, out_dtype=BF16, name=name + "_q")
        kt, vt = _gmm([(w_k, True), (w_v, True)], [(kvn, True)], [[(0, 0, 0)], [(1, 0, 0)]], n=h.shape[0],
                      out_dtypes=(BF16, BF16), name=name + "_kvt")
        o, w_saved, s_saved = _attn_fwd(q, kt, v, name=name + "_fwd")
        h2 = _mm(o, w_o, extras=[h], epilogue=_add_epilogue, name=name + "_o")
        return h2, (h, g_kv, g_q, kvn, qn, q, k, vt, o, w_saved, s_saved, w_k, w_v, w_q, w_o)

    def bwd(res, dh2, emit):
        h, g_kv, g_q, kvn, qn, q, k, vt, o, w_saved, s_saved, w_k, w_v, w_q, w_o = res
        dm = h.shape[1]
        do = _mm(dh2, w_o, tb=True, out_dtype=BF16, name=name + "_do")
        dw_o = _mm(o, dh2, ta=True, name=name + "_dwo")
        dq, dkt, dvt = _attn_bwd(q, k, vt, do, w_saved, s_saved, name=name + "_bwd")
        dqn = _mm(dq, w_q, tb=True, name=name + "_dqn")
        dw_q = _mm(qn, dq, ta=True, name=name + "_dwq")
        (dkvn,) = _gmm([(dkt, True), (dvt, True)], [(w_k, True), (w_v, True)], [[(0, 0, 0), (1, 1, 0)]], n=dm, name=name + "_dkvn")
        dw_k = _mm(kvn, dkt, ta=True, tb=True, name=name + "_dwk")
        dw_v = _mm(kvn, dvt, ta=True, tb=True, name=name + "_dwv")
        tok = emit({"w_k": dw_k, "w_v": dw_v, "w_q": dw_q, "w_o": dw_o})
        dh, dg_q = _rmsnorm_bwd(h, g_q + tok, dqn, dh2, name=name + "_dqnorm")
        dh, dg_kv = _rmsnorm_bwd(h, g_kv, dkvn, dh, name=name + "_dkvnorm")
        return dh, dg_kv, dg_q

    return fwd, bwd


S5_PARAMS = ("s5_a_re", "s5_a_im", "s5_log_dt", "s5_b_re", "s5_b_im", "s5_c_re", "s5_c_im")


def _forward_backward(x_pad, target_pad, n_valid, small, get_weights, put_grads):
    gs = {}
    h0 = x_pad.at[:N_META].set(small["meta_tokens"])

    ssm_fwd, ssm_bwd = _make_ssm("ssm")
    s5out_fwd, s5out_bwd = _make_s5_out_block("s5out")
    ffn_fwd, ffn_bwd = zip(_make_ffn_block("ffn0"), _make_ffn_block("ffn1"))
    attn_fwd, attn_bwd = _make_attn_layer("attn")
    u = _rmsnorm_fwd(h0, small["norm_mix"][0], out_dtype=F32, name="mix0_norm")
    disc, disc_vjp = jax.vjp(_s5_discretize, *[small[n][0] for n in S5_PARAMS])
    y, ssm_res = ssm_fwd(u, *disc)
    w = get_weights("s5", y)
    h1, s5out_res = s5out_fwd(y, u, small["s5_d"][0], h0, w["s5_w_glu"])
    w = get_weights("ffn0", h1)
    h2, ffn0_res = ffn_fwd[0](h1, small["norm_ffn"][0], w["w_ffn_in"], w["w_ffn_out"])
    w = get_weights("attn", h2)
    h3, attn_res = attn_fwd(h2, small["norm_kv"], small["norm_mix"][1], w["w_k"], w["w_v"], w["w_q"], w["w_o"])
    w = get_weights("ffn1", h3)
    h4, ffn1_res = ffn_fwd[1](h3, small["norm_ffn"][1], w["w_ffn_in"], w["w_ffn_out"])
    loss, dh, dg_final = _make_loss_head(n_valid, "loss_head")(h4, small["norm_final"], target_pad)
    gs["norm_final"] = dg_final.reshape(small["norm_final"].shape)

    dh, dg_ffn1 = ffn_bwd[1](ffn1_res, dh, functools.partial(put_grads, "ffn1"))
    dh, gs["norm_kv"], dg_mix1 = attn_bwd(attn_res, dh, functools.partial(put_grads, "attn"))
    dh, dg_ffn0 = ffn_bwd[0](ffn0_res, dh, functools.partial(put_grads, "ffn0"))
    dy, du, dd = s5out_bwd(s5out_res, dh, functools.partial(put_grads, "s5"))
    du_ssm, *ddisc = ssm_bwd(ssm_res, dy)
    dh, dg_mix0 = _rmsnorm_bwd(h0, small["norm_mix"][0], du + du_ssm, dh, name="mix0_dnorm")
    for n, g in zip(S5_PARAMS, disc_vjp(tuple(ddisc))):
        gs[n] = g[None]
    gs["s5_d"] = dd.reshape(small["s5_d"].shape)
    gs["norm_mix"] = jnp.stack([dg_mix0, dg_mix1])
    gs["norm_ffn"] = jnp.stack([dg_ffn0, dg_ffn1])
    gs["meta_tokens"] = dh[:N_META]
    return loss, dh, gs


def _coords():
    return lax.axis_index("x"), lax.axis_index("y"), lax.axis_index("c")


def _flip(bits):
    x, y, c = _coords()
    fx, fy, fc = bits
    return (x ^ fx if fx else x, y ^ fy if fy else y, c ^ fc if fc else c)


def _exchange(ins, out_shapes, copies, local_copies, *, aliases=None, name):
    n_in, n_out = len(ins), len(out_shapes)
    n_cp, n_loc = len(copies), len(local_copies)
    aliases = aliases or {}

    def body(*refs):
        in_refs, out_refs = refs[:n_in], refs[n_in:n_in + n_out]
        send_sems, recv_sems, loc_sems = refs[n_in + n_out:]
        me = _coords()
        locs = []
        for n, (ii, oi, dfn) in enumerate(local_copies):
            cp = pltpu.make_async_copy(in_refs[ii], out_refs[oi].at[dfn(*me)], loc_sems.at[n])
            cp.start()
            locs.append(cp)
        sends = []
        for n, (ii, sfn, bits, oi, dfn) in enumerate(copies):
            src = in_refs[ii] if sfn is None else in_refs[ii].at[sfn(*me)]
            cp = pltpu.make_async_remote_copy(
                src_ref=src, dst_ref=out_refs[oi].at[dfn(*me)], send_sem=send_sems.at[n], recv_sem=recv_sems.at[n],
                device_id=_flip(bits), device_id_type=MESH)
            cp.start()
            sends.append(cp)
        for n, (ii, sfn, bits, oi, dfn) in enumerate(copies):
            peer = _flip(bits)
            src = in_refs[ii] if sfn is None else in_refs[ii].at[sfn(*me)]
            pltpu.make_async_remote_copy(
                src_ref=src, dst_ref=out_refs[oi].at[dfn(*peer)], send_sem=send_sems.at[n], recv_sem=recv_sems.at[n],
                device_id=peer, device_id_type=MESH).wait_recv()
        for cp in sends:
            cp.wait_send()
        for cp in locs:
            cp.wait()

    any_spec = pl.BlockSpec(memory_space=pl.ANY)
    return _pallas(
        body, name=name, in_specs=[any_spec] * n_in, out_specs=[any_spec] * n_out,
        out_shape=[jax.ShapeDtypeStruct(s, d) for s, d in out_shapes],
        scratch_shapes=[pltpu.SemaphoreType.DMA((max(n_cp, 1),)), pltpu.SemaphoreType.DMA((max(n_cp, 1),)),
                        pltpu.SemaphoreType.DMA((max(n_loc, 1),))],
        input_output_aliases=aliases,
        compiler_params=pltpu.CompilerParams(has_side_effects=True),
    )(*ins)


ICI_FLIPS = ((1, 0, 0), (0, 1, 0), (1, 1, 0))
D2D_FLIP = (0, 0, 1)


def _slot_of(x, y, c):
    return 4 * x + 2 * y + c


def _all_gather(shards, *, name):
    n = len(shards)
    outs = [((N_DEV,) + s.shape, s.dtype) for s in shards]
    copies = [(t, None, bits, t, _slot_of) for t in range(n) for bits in ICI_FLIPS]
    local = [(t, t, _slot_of) for t in range(n)]
    bufs = _exchange(shards, outs, copies, local, name=name + "_ici")
    copies2 = [(t, (lambda x, y, c, q=q: 2 * q + c), D2D_FLIP, t, (lambda x, y, c, q=q: 2 * q + c))
               for t in range(n) for q in range(4)]
    return _exchange(bufs, outs, copies2, [], aliases={t: t for t in range(n)}, name=name + "_d2d")


ALL_FLIPS = tuple((m >> 2 & 1, m >> 1 & 1, m & 1) for m in range(1, N_DEV))
HBM_SPEC = pl.BlockSpec(memory_space=pltpu.HBM)
SEM_SPEC = pl.BlockSpec(memory_space=pltpu.SEMAPHORE)
SIDE_EFFECT = pltpu.SideEffectType.DATAFLOW_SIDE_EFFECTING


def _copy_desc(copies, n, src_refs, land_refs, send_sems, recv_sems, sender):
    si, sfn, bits, li, dfn = copies[n]
    src = src_refs[si] if sfn is None else src_refs[si].at[sfn(*sender)]
    return pltpu.make_async_remote_copy(
        src_ref=src, dst_ref=land_refs[li].at[dfn(*sender)], send_sem=send_sems.at[n], recv_sem=recv_sems.at[n],
        device_id=_flip(bits), device_id_type=MESH)


def _start_copies(srcs, lands, copies, *, name):
    n_s, n_l, n_c = len(srcs), len(lands), len(copies)

    def body(*refs):
        src_refs = refs[:n_s]
        send_sems, recv_sems = refs[n_s], refs[n_s + 1]
        land_refs = refs[2 * n_s + 2:2 * n_s + 2 + n_l]
        token = refs[-1]
        me = _coords()
        for n in range(n_c):
            _copy_desc(copies, n, src_refs, land_refs, send_sems, recv_sems, me).start()
        token[...] = jnp.zeros_like(token)

    res = _pallas(
        body, name=name,
        out_shape=(pltpu.SemaphoreType.DMA((n_c,)), pltpu.SemaphoreType.DMA((n_c,)),
                   *[pltpu.HBM(s.shape, s.dtype) for s in srcs], *[pltpu.HBM(shape, dtype) for shape, dtype in lands],
                   jax.ShapeDtypeStruct((8, LANES), F32)),
        in_specs=[HBM_SPEC] * n_s,
        out_specs=(SEM_SPEC, SEM_SPEC, *[HBM_SPEC] * (n_s + n_l), pl.BlockSpec(memory_space=pltpu.VMEM)),
        input_output_aliases={i: 2 + i for i in range(n_s)},
        compiler_params=pltpu.CompilerParams(has_side_effects=SIDE_EFFECT),
    )(*[pltpu.with_memory_space_constraint(s, pltpu.HBM) for s in srcs])
    return res[0], res[1], list(res[2:2 + n_s]), list(res[2 + n_s:2 + n_s + n_l]), res[-1]


def _wait_copies(send_sems, recv_sems, srcs, lands, copies, which, after, *, name):
    src_ids, land_ids = sorted(srcs), sorted(lands)
    n_s, n_l = len(src_ids), len(land_ids)
    srcs, lands = [srcs[i] for i in src_ids], [lands[i] for i in land_ids]

    def body(*refs):
        src_refs, land_refs = dict(zip(src_ids, refs[:n_s])), dict(zip(land_ids, refs[n_s:n_s + n_l]))
        s_sems, r_sems = refs[n_s + n_l], refs[n_s + n_l + 1]
        me = _coords()
        for n in which:
            cp = _copy_desc(copies, n, src_refs, land_refs, s_sems, r_sems, me)
            cp.wait_send()
            _copy_desc(copies, n, src_refs, land_refs, s_sems, r_sems, _flip(copies[n][2])).wait_recv()

    res = _pallas(
        body, name=name,
        out_shape=tuple(pltpu.HBM(b.shape, b.dtype) for b in list(srcs) + list(lands)),
        in_specs=[HBM_SPEC] * (n_s + n_l) + [SEM_SPEC, SEM_SPEC, pl.BlockSpec(memory_space=pl.ANY)],
        out_specs=tuple([HBM_SPEC] * (n_s + n_l)),
        input_output_aliases={i: i for i in range(n_s + n_l)},
        compiler_params=pltpu.CompilerParams(has_side_effects=SIDE_EFFECT),
    )(*srcs, *lands, send_sems, recv_sems, after)
    return dict(zip(src_ids, res[:n_s])), dict(zip(land_ids, res[n_s:]))


def _adam_math(w, g, m, v):
    m = ADAM_B1 * m + (1.0 - ADAM_B1) * g
    v = ADAM_B2 * v + (1.0 - ADAM_B2) * (g * g)
    m_hat = m / (1.0 - ADAM_B1 ** ADAM_STEP)
    v_hat = v / (1.0 - ADAM_B2 ** ADAM_STEP)
    delta = -ADAM_LR * (m_hat / (jnp.sqrt(v_hat) + ADAM_EPS) + ADAM_WD * w)
    return delta, m, v


def _shard_adamw(own, landing, w, m, v, *, name):
    r, cdim = w.shape
    tr = _pick(r, (256, 128, 64, 32, 16))

    def body(g_ref, l_ref, w_ref, m_ref, v_ref, g_out, d_out, m_out, v_out):
        g = g_ref[...]
        for n in range(N_DEV - 1):
            g = g + l_ref[n].astype(F32)
        d, mn, vn = _adam_math(w_ref[...], g, m_ref[...], v_ref[...])
        g_out[...] = g
        d_out[...] = d
        m_out[...] = mn
        v_out[...] = vn

    blk = pl.BlockSpec((tr, cdim), lambda i: (i, 0))
    return _pallas(
        body, name=name, grid=(r // tr,),
        in_specs=[blk, pl.BlockSpec((N_DEV - 1, tr, cdim), lambda i: (0, i, 0)), blk, blk, blk], out_specs=[blk] * 4,
        out_shape=[jax.ShapeDtypeStruct((r, cdim), F32)] * 4,
        compiler_params=_cparams("parallel"),
    )(own, landing, w, m, v)


def _small_adamw(slots, w, m, v, *, name):
    r, cdim = w.shape

    def body(s_ref, w_ref, m_ref, v_ref, g_out, d_out, m_out, v_out):
        g = s_ref[0]
        for n in range(1, N_DEV):
            g = g + s_ref[n]
        d, mn, vn = _adam_math(w_ref[...], g, m_ref[...], v_ref[...])
        g_out[...] = g
        d_out[...] = d
        m_out[...] = mn
        v_out[...] = vn

    tr = _pick(r, (256, 128, 64, 32, 16, 8))
    blk = pl.BlockSpec((tr, cdim), lambda i: (i, 0))
    return _pallas(
        body, name=name, grid=(r // tr,),
        in_specs=[pl.BlockSpec((N_DEV, tr, cdim), lambda i: (0, i, 0)), blk, blk, blk], out_specs=[blk] * 4,
        out_shape=[jax.ShapeDtypeStruct((r, cdim), F32)] * 4,
        compiler_params=_cparams("parallel"),
    )(slots, w, m, v)


def _plain_adamw(g, w, m, v, *, name):
    def body(g_ref, w_ref, m_ref, v_ref, d_out, m_out, v_out):
        d, mn, vn = _adam_math(w_ref[...], g_ref[...], m_ref[...], v_ref[...])
        d_out[...] = d
        m_out[...] = mn
        v_out[...] = vn

    spec = pl.BlockSpec(g.shape, lambda: (0,) * g.ndim)
    return _pallas(body, name=name, in_specs=[spec] * 4, out_specs=[spec] * 3,
                   out_shape=[jax.ShapeDtypeStruct(g.shape, F32)] * 3)(g, w, m, v)


def _cast_bf16(x, *, name):
    r, cdim = x.shape
    tr = _pick(r, (256, 128, 64, 32, 16))

    def body(x_ref, o_ref):
        o_ref[...] = x_ref[...].astype(BF16)

    return _pallas(body, name=name, grid=(r // tr,), in_specs=[pl.BlockSpec((tr, cdim), lambda i: (i, 0))],
                   out_specs=pl.BlockSpec((tr, cdim), lambda i: (i, 0)), out_shape=jax.ShapeDtypeStruct(x.shape, BF16),
                   compiler_params=_cparams("parallel"))(x)


SMALL_NAMES = ("norm_mix", "norm_ffn", "s5_a_re", "s5_a_im", "s5_log_dt", "s5_b_re", "s5_b_im", "s5_c_re", "s5_c_im",
               "norm_kv", "norm_final")


def _pack_rows(arrs):
    rows = []
    for a in arrs:
        flat = a.reshape(-1)
        pad = (-flat.shape[0]) % (8 * LANES)
        rows.append(jnp.pad(flat, (0, pad)).reshape(-1, LANES))
    return jnp.concatenate(rows, axis=0)


def _unpack_rows(packed, like):
    out, r0 = [], 0
    for a in like:
        n = math.prod(a.shape)
        nr = (n + 8 * LANES - 1) // (8 * LANES) * 8
        out.append(packed[r0:r0 + nr].reshape(-1)[:n].reshape(a.shape))
        r0 += nr
    return out


def kernel(x, meta_tokens, norm_mix, norm_ffn, s5_a_re, s5_a_im, s5_log_dt, s5_b_re, s5_b_im, s5_c_re, s5_c_im, s5_d, s5_w_glu, norm_kv, w_kv, w_q, w_o, w_ffn_in, w_ffn_out, norm_final, loss_target, m_meta_tokens, m_norm_mix, m_norm_ffn, m_s5_a_re, m_s5_a_im, m_s5_log_dt, m_s5_b_re, m_s5_b_im, m_s5_c_re, m_s5_c_im, m_s5_d, m_s5_w_glu, m_norm_kv, m_w_kv, m_w_q, m_w_o, m_w_ffn_in, m_w_ffn_out, m_norm_final, v_meta_tokens, v_norm_mix, v_norm_ffn, v_s5_a_re, v_s5_a_im, v_s5_log_dt, v_s5_b_re, v_s5_b_im, v_s5_c_re, v_s5_c_im, v_s5_d, v_s5_w_glu, v_norm_kv, v_w_kv, v_w_q, v_w_o, v_w_ffn_in, v_w_ffn_out, v_norm_final):
    args = dict(locals())
    seq = x.shape[1]
    n_valid = N_META + seq
    Lp = (n_valid + ATTN_BLOCK - 1) // ATTN_BLOCK * ATTN_BLOCK
    dm = D_MODEL
    my_slot = _slot_of(*_coords())

    n_ffn = w_ffn_in.shape[0]
    groups = {"s5": ["s5_w_glu"], "ffn0": ["w_ffn_in0", "w_ffn_out0"], "attn": ["w_kv", "w_q", "w_o"],
              "ffn1": ["w_ffn_in1", "w_ffn_out1"]}
    shards = {"s5_w_glu": s5_w_glu[0], "w_kv": w_kv, "w_q": w_q[0], "w_o": w_o[0]}
    for l in range(n_ffn):
        shards[f"w_ffn_in{l}"], shards[f"w_ffn_out{l}"] = w_ffn_in[l], w_ffn_out[l]
    by_columns = {"s5_w_glu", "w_kv", "w_ffn_in0", "w_ffn_in1"}
    names = [n for g in groups.values() for n in g]

    def gathered_full(n, g8):
        if n in by_columns:
            return jnp.swapaxes(g8, 0, 1).reshape(g8.shape[1], -1)
        return g8.reshape(-1, g8.shape[2])

    col_shard = jnp.concatenate([meta_tokens, s5_d, jnp.zeros((7, LANES), F32)], axis=0)
    ag_srcs = [col_shard] + [_cast_bf16(shards[n], name="cast_" + n) for n in names]
    ag_copies = [(t, None, bits, t, _slot_of) for t in range(len(ag_srcs)) for bits in ALL_FLIPS]
    ag_send, ag_recv, ag_srcs, ag_lands, ag_token = _start_copies(
        ag_srcs, [((N_DEV,) + s.shape, s.dtype) for s in ag_srcs], ag_copies, name="ag_start")

    def gather_wait(tag, ids, after):
        which = [t * len(ALL_FLIPS) + r for t in ids for r in range(len(ALL_FLIPS))]
        srcs, lands = _wait_copies(ag_send, ag_recv, {t: ag_srcs[t] for t in ids}, {t: ag_lands[t] for t in ids},
                                   ag_copies, which, after, name="ag_wait_" + tag)
        return {t: lax.dynamic_update_index_in_dim(lands[t], srcs[t], my_slot, 0) for t in ids}

    def get_weights(group, after):
        ids = [1 + names.index(n) for n in groups[group]]
        full = {names[t - 1]: gathered_full(names[t - 1], g8) for t, g8 in gather_wait(group, ids, after).items()}
        if group == "attn":
            w_kv_full = full.pop("w_kv")
            full["w_k"], full["w_v"] = w_kv_full[:, :dm], w_kv_full[:, dm:]
        return {n.rstrip("01") if n.startswith("w_ffn") else n: w for n, w in full.items()}

    gcol = gather_wait("cols", [0], ag_token)[0]
    small = {n: args[n] for n in SMALL_NAMES}
    small["meta_tokens"] = jnp.swapaxes(gcol[:, :N_META], 0, 1).reshape(N_META, dm)
    small["s5_d"] = gcol[:, N_META].reshape(1, dm)

    scatters = {}

    def put_grads(group, grads):
        if group == "attn":
            grads = {"w_kv": jnp.concatenate([grads.pop("w_k"), grads.pop("w_v")], axis=1), **grads}
        own, send = {}, []
        for n in groups[group]:
            g = grads[n.rstrip("01") if n.startswith("w_ffn") else n]
            r, c = shards[n].shape
            if n in by_columns:
                own[n] = lax.dynamic_slice_in_dim(g, my_slot * c, c, axis=1)
                send.append(jnp.swapaxes(g.reshape(r, N_DEV, c), 0, 1).astype(BF16))
            else:
                own[n] = lax.dynamic_slice_in_dim(g, my_slot * r, r, axis=0)
                send.append(g.reshape(N_DEV, r, c).astype(BF16))
        copies = [(t, (lambda x, y, c, b=bits: _slot_of(x ^ b[0], y ^ b[1], c ^ b[2])), bits, t, (lambda x, y, c, k=k: k))
                  for t in range(len(send)) for k, bits in enumerate(ALL_FLIPS)]
        lands = [((len(ALL_FLIPS),) + s.shape[1:], BF16) for s in send]
        s_sem, r_sem, srcs, lands, token = _start_copies(send, lands, copies, name="rs_start_" + group)
        scatters[group] = (s_sem, r_sem, srcs, lands, copies, own)
        return token[0, 0]

    x_pad = jnp.pad(x[0], ((N_META, Lp - n_valid), (0, 0)))
    t_pad = jnp.pad(loss_target[0], ((N_META, Lp - n_valid), (0, 0)))
    loss_local, g_xpad, g_small = _forward_backward(x_pad, t_pad, n_valid, small, get_weights, put_grads)
    loss = lax.psum(loss_local[0, 0], ("x", "y", "c"))
    grad_x = g_xpad[N_META:n_valid][None]

    small_list = [g_small[n] for n in SMALL_NAMES] + [g_small["meta_tokens"], g_small["s5_d"]]
    small_bufs = _all_gather([_pack_rows(small_list)], name="small_ag")

    out, per_tensor = {}, {}
    for group, (s_sem, r_sem, srcs, lands, copies, own) in scatters.items():
        ids = list(range(len(srcs)))
        _, landed = _wait_copies(s_sem, r_sem, dict(zip(ids, srcs)), dict(zip(ids, lands)), copies, list(range(len(copies))),
                                 small_bufs[0], name="rs_wait_" + group)
        for t, n in enumerate(groups[group]):
            base, layer = (n[:-1], int(n[-1])) if n.startswith("w_ffn") else (n, None)
            pick = (lambda a: a[layer]) if layer is not None else (lambda a: a.reshape(shards[n].shape))
            per_tensor[n] = _shard_adamw(own[n], landed[t], shards[n], pick(args["m_" + base]), pick(args["v_" + base]),
                                         name="adamw_" + n)
    for base in ("s5_w_glu", "w_kv", "w_q", "w_o"):
        out[base] = tuple(a.reshape(args[base].shape) for a in per_tensor[base])
    for base in ("w_ffn_in", "w_ffn_out"):
        out[base] = tuple(jnp.stack([per_tensor[f"{base}{l}"][k] for l in range(n_ffn)]) for k in range(4))

    zeros_tail = [jnp.zeros_like(g_small["meta_tokens"]), jnp.zeros_like(g_small["s5_d"])]
    pw = _pack_rows([args[n] for n in SMALL_NAMES] + zeros_tail)
    pm = _pack_rows([args["m_" + n] for n in SMALL_NAMES] + zeros_tail)
    pv = _pack_rows([args["v_" + n] for n in SMALL_NAMES] + zeros_tail)
    sg, sd, sm, sv = _small_adamw(small_bufs[0], pw, pm, pv, name="adamw_small")
    ug, ud, um, uv = (_unpack_rows(a, small_list) for a in (sg, sd, sm, sv))
    for k, n in enumerate(SMALL_NAMES):
        out[n] = (ug[k], ud[k], um[k], uv[k])
    g_meta = lax.dynamic_slice_in_dim(ug[-2], my_slot * LANES, LANES, axis=1)
    g_d = lax.dynamic_slice_in_dim(ug[-1].reshape(1, dm), my_slot * LANES, LANES, axis=1)
    pad7 = jnp.zeros((7, LANES), F32)
    gc = jnp.concatenate([g_meta, g_d, pad7], axis=0)
    wc = jnp.concatenate([meta_tokens, s5_d, pad7], axis=0)
    mc = jnp.concatenate([m_meta_tokens, m_s5_d, pad7], axis=0)
    vc = jnp.concatenate([v_meta_tokens, v_s5_d, pad7], axis=0)
    dc, mcn, vcn = _plain_adamw(gc, wc, mc, vc, name="adamw_cols")
    out["meta_tokens"] = (g_meta, dc[:N_META], mcn[:N_META], vcn[:N_META])
    out["s5_d"] = (g_d, dc[N_META:N_META + 1], mcn[N_META:N_META + 1], vcn[N_META:N_META + 1])

    order = ["meta_tokens", "norm_mix", "norm_ffn", "s5_a_re", "s5_a_im", "s5_log_dt", "s5_b_re", "s5_b_im", "s5_c_re",
             "s5_c_im", "s5_d", "s5_w_glu", "norm_kv", "w_kv", "w_q", "w_o", "w_ffn_in", "w_ffn_out", "norm_final"]
    res = [loss, grad_x]
    for k in range(4):
        res += [out[n][k] for n in order]
    return tuple(res)
```
